```python
import math
import jax, jax.numpy as jnp
from jax import lax
import numpy as np

D_MODEL = 2048
BATCH = 4
SEQ = 2048
DEPTH = 1

CHUNK = 64
Q_BLOCK = 128
D_MIX = D_MODEL
DIFF_WIDTH = D_MIX // 2
GLA_WIDTH = D_MIX - DIFF_WIDTH
DIFF_QK_DIM = 64
DIFF_V_DIM = 2 * DIFF_QK_DIM
DIFF_HEADS = DIFF_WIDTH // DIFF_V_DIM
ROT_DIM = DIFF_QK_DIM // 4
ROPE_THETA = 500000.0
GLA_HEADS = 4
GLA_V_DIM = GLA_WIDTH // GLA_HEADS
GLA_K_DIM = GLA_V_DIM // 2
GLA_GATE_RANK = 16
GLA_TAU = 16.0
N_GROUPS = 4
EXPERTS_PER_GROUP = 8
TOP_K = 2
D_FF_EXPERT = D_MODEL // 4
RMS_EPS = 1e-6

IN_SPLITS = (DIFF_HEADS * 2 * DIFF_QK_DIM,
             DIFF_HEADS * 2 * DIFF_QK_DIM,
             DIFF_HEADS * DIFF_V_DIM,
             GLA_HEADS * GLA_K_DIM,
             GLA_HEADS * GLA_K_DIM,
             GLA_HEADS * GLA_V_DIM,
             GLA_HEADS * GLA_V_DIM,
             GLA_GATE_RANK)
D_IN = sum(IN_SPLITS)

kernel_name = "hymba_diffattn_gla_hier_moe_block"


def rms_norm(x, g):
    xf = x.astype(jnp.float32)
    y = xf * lax.rsqrt(jnp.mean(xf * xf, axis=-1, keepdims=True) + RMS_EPS)
    return (y * g.astype(jnp.float32)).astype(x.dtype)


def partial_rope(t, cos, sin):
    half = ROT_DIM // 2
    x1 = t[..., :half].astype(jnp.float32)
    x2 = t[..., half:ROT_DIM].astype(jnp.float32)
    r = jnp.concatenate([x1 * cos - x2 * sin, x2 * cos + x1 * sin], axis=-1).astype(t.dtype)
    return jnp.concatenate([r, t[..., ROT_DIM:]], axis=-1)


def diff_attention(q, k, v, positions, q_g, k_g, lq1, lk1, lq2, lk2, out_g, lambda_init):
    B, S = q.shape[0], q.shape[1]
    q = rms_norm(q, q_g)
    k = rms_norm(k, k_g)
    inv = ROPE_THETA ** (-jnp.arange(0, ROT_DIM, 2, dtype=jnp.float32) / ROT_DIM)
    ang = positions.astype(jnp.float32)[..., None] * inv
    cos = jnp.cos(ang)[:, :, None, None, :]
    sin = jnp.sin(ang)[:, :, None, None, :]
    q = partial_rope(q, cos, sin)
    k = partial_rope(k, cos, sin)
    lam = (jnp.exp(jnp.sum(lq1.astype(jnp.float32) * lk1.astype(jnp.float32)))
           - jnp.exp(jnp.sum(lq2.astype(jnp.float32) * lk2.astype(jnp.float32)))
           + lambda_init)
    scale = DIFF_QK_DIM ** -0.5
    n_qb = S // Q_BLOCK
    qb = q.reshape(B, n_qb, Q_BLOCK, DIFF_HEADS, 2, DIFF_QK_DIM).transpose(1, 0, 3, 4, 2, 5)
    kt = k.transpose(0, 2, 3, 1, 4)
    vt = v.transpose(0, 2, 1, 3)
    key_chunk = jnp.arange(S) // CHUNK

    def block(args):
        qi, idx = args
        s = jnp.einsum('bhmqd,bhmkd->bhmqk', qi, kt, preferred_element_type=jnp.float32) * scale
        q_chunk = (idx * Q_BLOCK + jnp.arange(Q_BLOCK)) // CHUNK
        mask = key_chunk[None, :] <= q_chunk[:, None]
        s = jnp.where(mask, s, -jnp.inf)
        p = jax.nn.softmax(s, axis=-1)
        w = p[:, :, 0] - lam * p[:, :, 1]
        return jnp.einsum('bhqk,bhkd->bhqd', w.astype(vt.dtype), vt)

    o = lax.map(block, (qb, jnp.arange(n_qb)))
    o = o.transpose(1, 0, 3, 2, 4).reshape(B, S, DIFF_HEADS, DIFF_V_DIM)
    o = rms_norm(o, out_g) * (1.0 - lambda_init)
    return o.reshape(B, S, DIFF_HEADS * DIFF_V_DIM)


def gla(q, k, v, r, a, w_gate2, b_gate, out_g):
    B, S = q.shape[0], q.shape[1]
    n_c = S // CHUNK
    pre = (a @ w_gate2 + b_gate).astype(jnp.float32)
    log_alpha = (jax.nn.log_sigmoid(pre) / GLA_TAU).reshape(B, S, GLA_HEADS, GLA_K_DIM)

    def chunked(t):
        return t.reshape(B, n_c, CHUNK, GLA_HEADS, t.shape[-1]).transpose(0, 3, 1, 2, 4).astype(jnp.float32)

    bcum = jnp.cumsum(chunked(log_alpha), axis=3)
    b_last = bcum[:, :, :, -1:, :]
    qf = chunked(q) * (GLA_K_DIM ** -0.5)
    kf = chunked(k)
    vf = chunked(v)
    q_in = qf * jnp.exp(bcum)
    k_in = kf * jnp.exp(-bcum)
    k_dec = kf * jnp.exp(b_last - bcum)
    tril = jnp.tril(jnp.ones((CHUNK, CHUNK), dtype=bool))
    att = jnp.where(tril, jnp.einsum('bhcid,bhcjd->bhcij', q_in, k_in), 0.0)
    o_intra = jnp.einsum('bhcij,bhcje->bhcie', att, vf)
    u = jnp.einsum('bhcld,bhcle->bhcde', k_dec, vf)
    decay = jnp.exp(b_last[:, :, :, 0, :])

    def step(state, inp):
        u_c, d_c = inp
        return d_c[..., None] * state + u_c, state

    s0 = jnp.zeros((B, GLA_HEADS, GLA_K_DIM, GLA_V_DIM), jnp.float32)
    _, s_in = lax.scan(step, s0, (jnp.moveaxis(u, 2, 0), jnp.moveaxis(decay, 2, 0)))
    s_in = jnp.moveaxis(s_in, 0, 2)
    o_inter = jnp.einsum('bhcld,bhcde->bhcle', q_in, s_in)
    o = (o_intra + o_inter).transpose(0, 2, 3, 1, 4).reshape(B, S, GLA_HEADS, GLA_V_DIM)
    o = rms_norm(o, out_g).reshape(B, S, GLA_HEADS * GLA_V_DIM)
    return (o * jax.nn.silu(r.astype(jnp.float32))).astype(r.dtype)


def hier_moe(h, w_rg, b_rg, w_re, b_re, w_gate, w_up, w_down):
    B, S, D = h.shape
    t = h.reshape(B * S, D)
    lg = jnp.einsum('td,dg->tg', t, w_rg, preferred_element_type=jnp.float32) + b_rg.astype(jnp.float32)
    pg = jax.nn.softmax(lg, axis=-1)
    g_sel = jnp.argmax(pg, axis=-1)
    pg_sel = jnp.take_along_axis(pg, g_sel[:, None], axis=-1)[:, 0]
    le = jnp.einsum('td,gde->tge', t, w_re, preferred_element_type=jnp.float32) + b_re.astype(jnp.float32)
    le_sel = jnp.take_along_axis(le, g_sel[:, None, None], axis=1)[:, 0]
    pe = jax.nn.softmax(le_sel, axis=-1)
    top_v, top_i = lax.top_k(pe, TOP_K)
    top_v = top_v / jnp.sum(top_v, axis=-1, keepdims=True)
    ew = jnp.sum(jax.nn.one_hot(top_i, EXPERTS_PER_GROUP, dtype=jnp.float32) * top_v[..., None], axis=1)
    combine = (jax.nn.one_hot(g_sel, N_GROUPS, dtype=jnp.float32)[:, :, None]
               * ew[:, None, :] * pg_sel[:, None, None])
    out = jnp.zeros_like(t)
    for g in range(N_GROUPS):
        hg = jax.nn.silu(jnp.einsum('td,edf->tef', t, w_gate[g])) * jnp.einsum('td,edf->tef', t, w_up[g])
        hg = hg * combine[:, g, :, None].astype(hg.dtype)
        out = out + jnp.einsum('tef,efd->td', hg, w_down[g])
    return out.reshape(B, S, D)


def setup_inputs(seed: int = 0) -> dict:
    key = jax.random.key(seed)
    ks = jax.random.split(key, 24)
    nrm = lambda k, s: jax.random.normal(k, s, dtype=jnp.float32)
    gain = lambda k, s: 1.0 + 0.02 * nrm(k, s)
    E, G, F = EXPERTS_PER_GROUP, N_GROUPS, D_FF_EXPERT
    offsets = jax.random.randint(ks[1], (BATCH, 1), 0, 256) * CHUNK
    positions = (offsets + jnp.arange(SEQ)[None, :]).astype(jnp.int32)
    return {
        "x": nrm(ks[0], (BATCH, SEQ, D_MODEL)),
        "positions": positions,
        "norm1_g": gain(ks[2], (DEPTH, D_MODEL)),
        "w_in": nrm(ks[3], (DEPTH, D_MODEL, D_IN)) * D_MODEL ** -0.5,
        "q_norm_g": gain(ks[4], (DEPTH, DIFF_QK_DIM)),
        "k_norm_g": gain(ks[5], (DEPTH, DIFF_QK_DIM)),
        "lambda_q1": 0.1 * nrm(ks[6], (DEPTH, DIFF_QK_DIM)),
        "lambda_k1": 0.1 * nrm(ks[7], (DEPTH, DIFF_QK_DIM)),
        "lambda_q2": 0.1 * nrm(ks[8], (DEPTH, DIFF_QK_DIM)),
        "lambda_k2": 0.1 * nrm(ks[9], (DEPTH, DIFF_QK_DIM)),
        "diff_out_norm_g": gain(ks[10], (DEPTH, DIFF_V_DIM)),
        "gla_w_gate2": nrm(ks[11], (DEPTH, GLA_GATE_RANK, GLA_HEADS * GLA_K_DIM)) * GLA_GATE_RANK ** -0.5,
        "gla_b_gate": 0.1 * nrm(ks[12], (DEPTH, GLA_HEADS * GLA_K_DIM)),
        "gla_out_norm_g": gain(ks[13], (DEPTH, GLA_V_DIM)),
        "w_out": nrm(ks[14], (DEPTH, D_MIX, D_MODEL)) * D_MIX ** -0.5,
        "norm2_g": gain(ks[15], (DEPTH, D_MODEL)),
        "w_router_group": nrm(ks[16], (DEPTH, D_MODEL, G)) * D_MODEL ** -0.5,
        "b_router_group": 0.01 * nrm(ks[17], (DEPTH, G)),
        "w_router_expert": nrm(ks[18], (DEPTH, G, D_MODEL, E)) * D_MODEL ** -0.5,
        "b_router_expert": 0.01 * nrm(ks[19], (DEPTH, G, E)),
        "w_gate_expert": nrm(ks[20], (DEPTH, G, E, D_MODEL, F)) * D_MODEL ** -0.5,
        "w_up_expert": nrm(ks[21], (DEPTH, G, E, D_MODEL, F)) * D_MODEL ** -0.5,
        "w_down_expert": nrm(ks[22], (DEPTH, G, E, F, D_MODEL)) * F ** -0.5,
    }


def reference(x, positions, norm1_g, w_in, q_norm_g, k_norm_g, lambda_q1, lambda_k1,
              lambda_q2, lambda_k2, diff_out_norm_g, gla_w_gate2, gla_b_gate, gla_out_norm_g,
              w_out, norm2_g, w_router_group, b_router_group, w_router_expert, b_router_expert,
              w_gate_expert, w_up_expert, w_down_expert):
    B, S, _ = x.shape
    split_idx = [int(v) for v in np.cumsum(IN_SPLITS)[:-1]]
    for l in range(DEPTH):
        lambda_init = 0.8 - 0.6 * math.exp(-0.3 * l)
        h = rms_norm(x, norm1_g[l])
        proj = h @ w_in[l]
        dq, dk, dv, gq, gk, gv, gr, ga = jnp.split(proj, split_idx, axis=-1)
        a_out = diff_attention(
            dq.reshape(B, S, DIFF_HEADS, 2, DIFF_QK_DIM),
            dk.reshape(B, S, DIFF_HEADS, 2, DIFF_QK_DIM),
            dv.reshape(B, S, DIFF_HEADS, DIFF_V_DIM),
            positions, q_norm_g[l], k_norm_g[l], lambda_q1[l], lambda_k1[l],
            lambda_q2[l], lambda_k2[l], diff_out_norm_g[l], lambda_init)
        g_out = gla(
            gq.reshape(B, S, GLA_HEADS, GLA_K_DIM),
            gk.reshape(B, S, GLA_HEADS, GLA_K_DIM),
            gv.reshape(B, S, GLA_HEADS, GLA_V_DIM),
            gr, ga, gla_w_gate2[l], gla_b_gate[l], gla_out_norm_g[l])
        mixed = jnp.concatenate([a_out, g_out.astype(a_out.dtype)], axis=-1) @ w_out[l]
        x = x + mixed
        x = x + hier_moe(rms_norm(x, norm2_g[l]), w_router_group[l], b_router_group[l],
                         w_router_expert[l], b_router_expert[l], w_gate_expert[l],
                         w_up_expert[l], w_down_expert[l])
    return x
```

```python
import functools
import math

import jax
import jax.numpy as jnp
from jax import lax
from jax.experimental import pallas as pl
from jax.experimental.pallas import tpu as pltpu

D_MODEL = 2048
CHUNK = 64
DIFF_QK_DIM = 64
DIFF_V_DIM = 128
DIFF_HEADS = 8
ROT_DIM = 16
ROPE_THETA = 500000.0
GLA_HEADS = 4
GLA_V_DIM = 256
GLA_K_DIM = 128
GLA_GATE_RANK = 16
GLA_TAU = 16.0
N_GROUPS = 4
EXPERTS_PER_GROUP = 8
N_EXPERTS = N_GROUPS * EXPERTS_PER_GROUP
D_FF = 512
RMS_EPS = 1e-6
LAMBDA_INIT = 0.8 - 0.6 * math.exp(-0.3 * 0)
D_MAIN = 6144

LANES = 128
SUB = D_MODEL // LANES
VMEM_LIMIT = 56 * 1024 * 1024

TM_IN = 1024
TN_IN = 512
TQ = 256
GB = 256
TM_OUT = 512
TB = 256
TM_E = 256

F32 = jnp.float32
BF16 = jnp.bfloat16
HI = lax.Precision.HIGHEST


def _params(n_axes):
    return pltpu.CompilerParams(dimension_semantics=("arbitrary",) * n_axes,
                                vmem_limit_bytes=VMEM_LIMIT)


def _nt_dot(a, b):
    return lax.dot_general(a, b, (((1,), (1,)), ((), ())), preferred_element_type=F32)


def _tn_dot(a, b):
    return lax.dot_general(a, b, (((0,), (0,)), ((), ())), preferred_element_type=F32)


def _silu(x):
    return x * (1.0 / (1.0 + jnp.exp(-x)))


def _store_token_tiles(ref, val):
    n = val.shape[0]
    for s in range(SUB):
        ref[pl.ds(s, n, stride=SUB), :] = val[:, s * LANES:(s + 1) * LANES]


def _load_token_tiles(ref, n):
    return jnp.concatenate([ref[pl.ds(s, n, stride=SUB), :] for s in range(SUB)], axis=1)


def _rope_table_kernel(pos_ref, invf_ref, c_ref, s1_ref, s2_ref):
    ang = pos_ref[...] * invf_ref[...]
    d = lax.broadcasted_iota(jnp.int32, ang.shape, 1) % DIFF_QK_DIM
    cos = jnp.cos(ang)
    sin = jnp.sin(ang)
    half = ROT_DIM // 2
    c_ref[...] = jnp.where(d < ROT_DIM, cos, 1.0)
    s1_ref[...] = jnp.where(d < half, -sin, 0.0)
    s2_ref[...] = jnp.where((d >= half) & (d < ROT_DIM), sin, 0.0)


def _rope_tables(pos_b, invf):
    t = pos_b.shape[0]
    tb = 1024
    spec = pl.BlockSpec((tb, LANES), lambda i: (i, 0))
    return pl.pallas_call(
        _rope_table_kernel,
        grid=(t // tb,),
        in_specs=[spec, pl.BlockSpec((1, LANES), lambda i: (0, 0))],
        out_specs=[spec, spec, spec],
        out_shape=[jax.ShapeDtypeStruct((t, LANES), F32)] * 3,
        compiler_params=_params(1),
        name="rope_tables",
    )(pos_b, invf)


def _in_proj_kernel(x_ref, g_ref, w_ref, wa_ref, proj_ref, ga_ref, h_scr):
    @pl.when(pl.program_id(1) == 0)
    def _():
        def body(c, carry):
            rows = pl.ds(c * 256, 256)
            x = x_ref[rows, :]
            ms = jnp.mean(x * x, axis=-1, keepdims=True)
            h_scr[rows, :] = (x * lax.rsqrt(ms + RMS_EPS) * g_ref[...]).astype(BF16)
            return carry
        lax.fori_loop(0, TM_IN // 256, body, 0)
        ga_ref[...] = jnp.dot(h_scr[...], wa_ref[...].astype(BF16), preferred_element_type=F32)

    proj_ref[...] = jnp.dot(h_scr[...], w_ref[...].astype(BF16),
                            preferred_element_type=F32).astype(BF16)


def _in_proj(x2, g1, w_in, wa):
    t = x2.shape[0]
    return pl.pallas_call(
        _in_proj_kernel,
        grid=(t // TM_IN, D_MAIN // TN_IN),
        in_specs=[
            pl.BlockSpec((TM_IN, D_MODEL), lambda i, j: (i, 0)),
            pl.BlockSpec((1, D_MODEL), lambda i, j: (0, 0)),
            pl.BlockSpec((D_MODEL, TN_IN), lambda i, j: (0, j)),
            pl.BlockSpec((D_MODEL, LANES), lambda i, j: (0, 0)),
        ],
        out_specs=[
            pl.BlockSpec((TM_IN, TN_IN), lambda i, j: (i, j)),
            pl.BlockSpec((TM_IN, LANES), lambda i, j: (i, 0)),
        ],
        out_shape=[jax.ShapeDtypeStruct((t, D_MAIN), BF16),
                   jax.ShapeDtypeStruct((t, LANES), F32)],
        scratch_shapes=[pltpu.VMEM((TM_IN, D_MODEL), BF16)],
        compiler_params=_params(2),
        name="in_proj",
    )(x2, g1, w_in, wa)


def _norm_rope(x, g, c, s1, s2):
    lo = lax.broadcasted_iota(jnp.int32, x.shape, 1) < DIFF_QK_DIM
    x2 = x * x
    s_lo = jnp.sum(jnp.where(lo, x2, 0.0), axis=-1, keepdims=True)
    s_hi = jnp.sum(jnp.where(lo, 0.0, x2), axis=-1, keepdims=True)
    ms = jnp.where(lo, s_lo, s_hi) * (1.0 / DIFF_QK_DIM)
    y = x * lax.rsqrt(ms + RMS_EPS) * g
    half = ROT_DIM // 2
    return y * c + pltpu.roll(y, LANES - half, 1) * s1 + pltpu.roll(y, half, 1) * s2


def _attn_kernel(q_ref, k_ref, v_ref, c_ref, s1_ref, s2_ref, qg_ref, kg_ref,
                 lq1_ref, lk1_ref, lq2_ref, lk2_ref, og_ref, o_ref,
                 q1_scr, q2_scr, k_scr):
    s_len = q_ref.shape[0]
    lam = (jnp.exp(jnp.sum(lq1_ref[...] * lk1_ref[...], axis=-1, keepdims=True))
           - jnp.exp(jnp.sum(lq2_ref[...] * lk2_ref[...], axis=-1, keepdims=True))
           + LAMBDA_INIT)
    c = c_ref[...]
    s1 = s1_ref[...]
    s2 = s2_ref[...]
    lo = lax.broadcasted_iota(jnp.int32, (s_len, LANES), 1) < DIFF_QK_DIM
    qn = _norm_rope(q_ref[...].astype(F32), qg_ref[...], c, s1, s2) * (DIFF_QK_DIM ** -0.5)
    q1_scr[...] = jnp.where(lo, qn, 0.0).astype(BF16)
    q2_scr[...] = jnp.where(lo, 0.0, qn).astype(BF16)
    k_scr[...] = _norm_rope(k_ref[...].astype(F32), kg_ref[...], c, s1, s2).astype(BF16)

    for i in range(s_len // TQ):
        nk = TQ * (i + 1)
        rows = slice(i * TQ, (i + 1) * TQ)
        kk = k_scr[0:nk, :]
        sc1 = _nt_dot(q1_scr[rows, :], kk)
        sc2 = _nt_dot(q2_scr[rows, :], kk)
        row_c = (lax.broadcasted_iota(jnp.int32, (TQ, nk), 0) + i * TQ) // CHUNK
        col_c = lax.broadcasted_iota(jnp.int32, (TQ, nk), 1) // CHUNK
        mask = col_c <= row_c
        sc1 = jnp.where(mask, sc1, -jnp.inf)
        sc2 = jnp.where(mask, sc2, -jnp.inf)
        e1 = jnp.exp(sc1 - jnp.max(sc1, axis=-1, keepdims=True))
        e2 = jnp.exp(sc2 - jnp.max(sc2, axis=-1, keepdims=True))
        r1 = 1.0 / jnp.sum(e1, axis=-1, keepdims=True)
        r2 = lam / jnp.sum(e2, axis=-1, keepdims=True)
        w = (e1 * r1 - e2 * r2).astype(BF16)
        o = jnp.dot(w, v_ref[0:nk, :], preferred_element_type=F32)
        ms = jnp.mean(o * o, axis=-1, keepdims=True)
        y = o * lax.rsqrt(ms + RMS_EPS) * og_ref[...] * (1.0 - LAMBDA_INIT)
        o_ref[rows, :] = y.astype(BF16)


def _diff_attention(proj, tabs, qg, kg, lq1, lk1, lq2, lk2, og, batch, seq):
    c, s1, s2 = tabs
    h = DIFF_HEADS
    blk = lambda off: pl.BlockSpec((seq, LANES), lambda b, hh, off=off: (b, off + hh))
    tab = pl.BlockSpec((seq, LANES), lambda b, hh: (b, 0))
    vec = lambda n: pl.BlockSpec((1, n), lambda b, hh: (0, 0))
    return pl.pallas_call(
        _attn_kernel,
        grid=(batch, h),
        in_specs=[blk(0), blk(h), blk(2 * h), tab, tab, tab,
                  vec(LANES), vec(LANES), vec(DIFF_QK_DIM), vec(DIFF_QK_DIM),
                  vec(DIFF_QK_DIM), vec(DIFF_QK_DIM), vec(LANES)],
        out_specs=pl.BlockSpec((seq, LANES), lambda b, hh: (b, hh)),
        out_shape=jax.ShapeDtypeStruct((batch * seq, h * DIFF_V_DIM), BF16),
        scratch_shapes=[pltpu.VMEM((seq, LANES), BF16)] * 3,
        compiler_params=_params(2),
        name="diff_attention",
    )(proj, proj, proj, c, s1, s2, qg, kg, lq1, lk1, lq2, lk2, og)


def _gla_kernel(q_ref, k_ref, v_ref, r_ref, ga_ref, w2_ref, b2_ref, og_ref, o_ref,
                qin_scr, kdec_scr, dec_scr, acc_scr):
    s_len = q_ref.shape[0]
    ri = lax.broadcasted_iota(jnp.int32, (GB, GB), 0)
    ci = lax.broadcasted_iota(jnp.int32, (GB, GB), 1)
    same = (ri // CHUNK) == (ci // CHUNK)
    blk_ones = jnp.where(same, 1.0, 0.0).astype(F32)
    tril = same & (ci <= ri)
    tri_ones = jnp.where(tril, 1.0, 0.0).astype(F32)

    def intra(g, carry):
        rows = pl.ds(pl.multiple_of(g * GB, GB), GB)
        pre = jnp.dot(ga_ref[rows, :], w2_ref[...], precision=HI,
                      preferred_element_type=F32) + b2_ref[...]
        la = -(jnp.maximum(-pre, 0.0) + jnp.log1p(jnp.exp(-jnp.abs(pre)))) * (1.0 / GLA_TAU)
        bc = jnp.dot(tri_ones, la, precision=HI, preferred_element_type=F32)
        bl = jnp.dot(blk_ones, la, precision=HI, preferred_element_type=F32)
        q = q_ref[rows, :].astype(F32) * (GLA_K_DIM ** -0.5)
        k = k_ref[rows, :].astype(F32)
        q_in = (q * jnp.exp(bc)).astype(BF16)
        k_in = (k * jnp.exp(-bc)).astype(BF16)
        qin_scr[rows, :] = q_in
        kdec_scr[rows, :] = (k * jnp.exp(bl - bc)).astype(BF16)
        dec_scr[rows, :] = jnp.exp(bl)
        att = jnp.where(tril, _nt_dot(q_in, k_in), 0.0).astype(BF16)
        acc_scr[rows, :] = jnp.dot(att, v_ref[rows, :], preferred_element_type=F32)
        return carry

    lax.fori_loop(0, s_len // GB, intra, 0)

    def inter(c, st):
        rows = pl.ds(pl.multiple_of(c * CHUNK, CHUNK), CHUNK)
        acc_scr[rows, :] += _nt_dot(qin_scr[rows, :], st.astype(BF16))
        ut = _tn_dot(v_ref[rows, :], kdec_scr[rows, :])
        dec = dec_scr[pl.ds(pl.multiple_of(c * CHUNK, CHUNK), 1), :]
        return dec * st + ut

    lax.fori_loop(0, s_len // CHUNK, inter, jnp.zeros((GLA_V_DIM, GLA_K_DIM), F32))

    def finish(g, carry):
        rows = pl.ds(pl.multiple_of(g * GB, GB), GB)
        o = acc_scr[rows, :]
        ms = jnp.mean(o * o, axis=-1, keepdims=True)
        y = o * lax.rsqrt(ms + RMS_EPS) * og_ref[...]
        o_ref[rows, :] = (y * _silu(r_ref[rows, :].astype(F32))).astype(BF16)
        return carry

    lax.fori_loop(0, s_len // GB, finish, 0)


def _gla(proj, ga, w2p, b2, og, batch, seq):
    hq = 3 * DIFF_HEADS
    kblk = lambda off: pl.BlockSpec((seq, GLA_K_DIM), lambda b, hh, off=off: (b, off + hh))
    vblk = lambda off: pl.BlockSpec((seq, GLA_V_DIM), lambda b, hh, off=off: (b, off + hh))
    return pl.pallas_call(
        _gla_kernel,
        grid=(batch, GLA_HEADS),
        in_specs=[kblk(hq), kblk(hq + GLA_HEADS), vblk(16), vblk(16 + GLA_HEADS),
                  pl.BlockSpec((seq, LANES), lambda b, hh: (b, 0)),
                  pl.BlockSpec((LANES, GLA_K_DIM), lambda b, hh: (0, hh)),
                  pl.BlockSpec((1, GLA_K_DIM), lambda b, hh: (0, hh)),
                  pl.BlockSpec((1, GLA_V_DIM), lambda b, hh: (0, 0))],
        out_specs=pl.BlockSpec((seq, GLA_V_DIM), lambda b, hh: (b, hh)),
        out_shape=jax.ShapeDtypeStruct((batch * seq, GLA_HEADS * GLA_V_DIM), BF16),
        scratch_shapes=[pltpu.VMEM((seq, GLA_K_DIM), BF16),
                        pltpu.VMEM((seq, GLA_K_DIM), BF16),
                        pltpu.VMEM((seq, GLA_K_DIM), F32),
                        pltpu.VMEM((seq, GLA_V_DIM), F32)],
        compiler_params=_params(2),
        name="gla",
    )(proj, proj, proj, proj, ga, w2p, b2, og)


def _out_proj_kernel(a_ref, g_ref, x_ref, wo_ref, g2_ref, wr_ref, br_ref,
                     x1_ref, h2_ref, lg_ref):
    half = a_ref.shape[1]
    mixed = (jnp.dot(a_ref[...], wo_ref[0:half, :], preferred_element_type=F32)
             + jnp.dot(g_ref[...], wo_ref[half:, :], preferred_element_type=F32))
    x1 = x_ref[...] + mixed
    x1_ref[...] = x1
    ms = jnp.mean(x1 * x1, axis=-1, keepdims=True)
    h2 = x1 * lax.rsqrt(ms + RMS_EPS) * g2_ref[...]
    _store_token_tiles(h2_ref, h2)
    lg_ref[...] = jnp.dot(h2, wr_ref[...], precision=HI,
                          preferred_element_type=F32) + br_ref[...]


def _out_proj(a_out, g_out, x2, wo, g2, wr, br):
    t = x2.shape[0]
    half = a_out.shape[1]
    row = lambda n: pl.BlockSpec((TM_OUT, n), lambda i: (i, 0))
    full = lambda r, n: pl.BlockSpec((r, n), lambda i: (0, 0))
    return pl.pallas_call(
        _out_proj_kernel,
        grid=(t // TM_OUT,),
        in_specs=[row(half), row(half), row(D_MODEL), full(D_MODEL, D_MODEL),
                  full(1, D_MODEL), full(D_MODEL, LANES), full(1, LANES)],
        out_specs=[row(D_MODEL), pl.BlockSpec((TM_OUT * SUB, LANES), lambda i: (i, 0)),
                   row(LANES)],
        out_shape=[jax.ShapeDtypeStruct((t, D_MODEL), F32),
                   jax.ShapeDtypeStruct((t * SUB, LANES), F32),
                   jax.ShapeDtypeStruct((t, LANES), F32)],
        compiler_params=_params(1),
        name="out_proj_router",
    )(a_out, g_out, x2, wo, g2, wr, br)


def _first_argmax(vals, lane_f, valid):
    v = jnp.where(valid, vals, -jnp.inf)
    m = jnp.max(v, axis=-1, keepdims=True)
    idx = jnp.min(jnp.where(valid & (v == m), lane_f, float(LANES)), axis=-1, keepdims=True)
    return m, idx


def _routing_kernel(lg_ref, slots_ref, cw_ref, cnt_ref, e1_scr, e2_scr, r1_scr, r2_scr):
    t = lg_ref.shape[0]
    lane = lax.broadcasted_iota(jnp.int32, (TB, LANES), 1)
    lane_f = lane.astype(F32)
    ri = lax.broadcasted_iota(jnp.int32, (TB, TB), 0)
    ci = lax.broadcasted_iota(jnp.int32, (TB, TB), 1)
    strict_lower = jnp.where(ci < ri, 1.0, 0.0).astype(BF16)

    def softmax_in(lg, valid):
        m = jnp.max(jnp.where(valid, lg, -jnp.inf), axis=-1, keepdims=True)
        ex = jnp.where(valid, jnp.exp(lg - m), 0.0)
        return ex / jnp.sum(ex, axis=-1, keepdims=True)

    def phase1(b, carry):
        rows = pl.ds(pl.multiple_of(b * TB, TB), TB)
        lg = lg_ref[rows, :]
        is_g = lane < N_GROUPS
        pg_sel, g_sel = _first_argmax(softmax_in(lg, is_g), lane_f, is_g)
        e_lo = N_GROUPS + g_sel * EXPERTS_PER_GROUP
        in_grp = (lane_f >= e_lo) & (lane_f < e_lo + EXPERTS_PER_GROUP)
        pe = softmax_in(lg, in_grp)
        v1, l1 = _first_argmax(pe, lane_f, in_grp)
        v2, l2 = _first_argmax(pe, lane_f, in_grp & (lane_f != l1))
        tot = v1 + v2
        c1 = (v1 / tot) * pg_sel
        c2 = (v2 / tot) * pg_sel
        e1 = (l1 - N_GROUPS).astype(jnp.int32)
        e2 = (l2 - N_GROUPS).astype(jnp.int32)
        oh1 = lane == e1
        oh2 = lane == e2
        a = jnp.where(oh1 | oh2, 1.0, 0.0)
        rank = jnp.dot(strict_lower, a.astype(BF16), preferred_element_type=F32) + carry
        e1_scr[rows, :] = jnp.broadcast_to(e1, (TB, LANES))
        e2_scr[rows, :] = jnp.broadcast_to(e2, (TB, LANES))
        r1_scr[rows, :] = jnp.broadcast_to(
            jnp.sum(jnp.where(oh1, rank, 0.0), axis=-1, keepdims=True), (TB, LANES))
        r2_scr[rows, :] = jnp.broadcast_to(
            jnp.sum(jnp.where(oh2, rank, 0.0), axis=-1, keepdims=True), (TB, LANES))
        cw_ref[rows, :] = jnp.where(lane == 0, c1, jnp.where(lane == 1, c2, 0.0))
        return carry + jnp.sum(a, axis=0, keepdims=True)

    counts = lax.fori_loop(0, t // TB, phase1, jnp.zeros((1, LANES), F32))
    cnt_ref[...] = jnp.broadcast_to(counts, cnt_ref.shape)
    n_tiles = jnp.floor((counts + (TM_E - 1)) * (1.0 / TM_E))
    ui = lax.broadcasted_iota(jnp.int32, (LANES, LANES), 0)
    uj = lax.broadcasted_iota(jnp.int32, (LANES, LANES), 1)
    strict_upper = jnp.where(ui < uj, 1.0, 0.0).astype(BF16)
    tile_off = jnp.dot(jnp.broadcast_to(n_tiles, (8, LANES)).astype(BF16), strict_upper,
                       preferred_element_type=F32)[0:1, :]
    row_off = tile_off * TM_E

    def phase2(b, carry):
        rows = pl.ds(pl.multiple_of(b * TB, TB), TB)
        off1 = jnp.sum(jnp.where(lane == e1_scr[rows, :], row_off, 0.0), axis=-1, keepdims=True)
        off2 = jnp.sum(jnp.where(lane == e2_scr[rows, :], row_off, 0.0), axis=-1, keepdims=True)
        sl1 = (off1 + r1_scr[rows, 0:1]).astype(jnp.int32)
        sl2 = (off2 + r2_scr[rows, 0:1]).astype(jnp.int32)
        slots_ref[rows, :] = jnp.where(lane == 0, sl1, jnp.where(lane == 1, sl2, 0))
        return carry

    lax.fori_loop(0, t // TB, phase2, 0)


def _routing(logits):
    t = logits.shape[0]
    return pl.pallas_call(
        _routing_kernel,
        out_shape=[jax.ShapeDtypeStruct((t, LANES), jnp.int32),
                   jax.ShapeDtypeStruct((t, LANES), F32),
                   jax.ShapeDtypeStruct((8, LANES), F32)],
        scratch_shapes=[pltpu.VMEM((t, LANES), jnp.int32), pltpu.VMEM((t, LANES), jnp.int32),
                        pltpu.VMEM((t, LANES), F32), pltpu.VMEM((t, LANES), F32)],
        compiler_params=pltpu.CompilerParams(vmem_limit_bytes=VMEM_LIMIT),
        name="routing",
    )(logits)


def _rows_copy(src, dst, sem, src_tok, dst_tok, n):
    first = lambda tok: tok * SUB if isinstance(tok, int) else pl.multiple_of(tok * SUB, SUB)
    s0 = first(src_tok)
    d0 = first(dst_tok)
    return pltpu.make_async_copy(src.at[pl.ds(s0, n * SUB), :], dst.at[pl.ds(d0, n * SUB), :], sem)


def _dispatch_kernel(sl1_ref, sl2_ref, ends_ref, h2_ref, xs_ref, zero_scr, sem):
    step = pl.program_id(0)

    @pl.when(step == 0)
    def _():
        zero_scr[...] = jnp.zeros_like(zero_scr)
        for e in range(N_EXPERTS):
            start = jnp.maximum(ends_ref[e] - TM_E, 0)
            _rows_copy(zero_scr, xs_ref, sem, 0, start, TM_E).start()
        for e in range(N_EXPERTS):
            _rows_copy(zero_scr, xs_ref, sem, 0, 0, TM_E).wait()

    base = step * TB

    def issue(r, carry):
        _rows_copy(h2_ref, xs_ref, sem, r, sl1_ref[base + r], 1).start()
        _rows_copy(h2_ref, xs_ref, sem, r, sl2_ref[base + r], 1).start()
        return carry

    lax.fori_loop(0, TB, issue, 0, unroll=8)
    _rows_copy(h2_ref, xs_ref, sem, 0, 0, TB).wait()
    _rows_copy(h2_ref, xs_ref, sem, 0, 0, TB).wait()


def _dispatch(h2, sl1, sl2, ends, n_rows):
    t = h2.shape[0] // SUB
    return pl.pallas_call(
        _dispatch_kernel,
        grid_spec=pltpu.PrefetchScalarGridSpec(
            num_scalar_prefetch=3,
            grid=(t // TB,),
            in_specs=[pl.BlockSpec((TB * SUB, LANES), lambda i, *_: (i, 0))],
            out_specs=pl.BlockSpec(memory_space=pl.ANY),
            scratch_shapes=[pltpu.VMEM((TM_E * SUB, LANES), F32), pltpu.SemaphoreType.DMA(())],
        ),
        out_shape=jax.ShapeDtypeStruct((n_rows * SUB, LANES), F32),
        compiler_params=_params(1),
        name="dispatch",
    )(sl1, sl2, ends, h2)


def _expert_kernel(tidx_ref, texp_ref, nt_ref, xs_ref, wg_ref, wu_ref, wd_ref, ys_ref):
    @pl.when(pl.program_id(0) < nt_ref[0])
    def _():
        xb = _load_token_tiles(xs_ref, TM_E).astype(BF16)
        g = jnp.dot(xb, wg_ref[0].astype(BF16), preferred_element_type=F32)
        u = jnp.dot(xb, wu_ref[0].astype(BF16), preferred_element_type=F32)
        h = (_silu(g) * u).astype(BF16)
        _store_token_tiles(ys_ref, jnp.dot(h, wd_ref[0].astype(BF16), preferred_element_type=F32))


def _expert_mlp(xs, tile_idx, tile_exp, n_tiles, wg, wu, wd):
    n_rows = xs.shape[0] // SUB
    tile = pl.BlockSpec((TM_E * SUB, LANES), lambda j, ti, te, nt: (ti[j], 0))
    return pl.pallas_call(
        _expert_kernel,
        grid_spec=pltpu.PrefetchScalarGridSpec(
            num_scalar_prefetch=3,
            grid=(n_rows // TM_E,),
            in_specs=[
                tile,
                pl.BlockSpec((1, D_MODEL, D_FF), lambda j, ti, te, nt: (te[j], 0, 0)),
                pl.BlockSpec((1, D_MODEL, D_FF), lambda j, ti, te, nt: (te[j], 0, 0)),
                pl.BlockSpec((1, D_FF, D_MODEL), lambda j, ti, te, nt: (te[j], 0, 0)),
            ],
            out_specs=tile,
        ),
        out_shape=jax.ShapeDtypeStruct((n_rows * SUB, LANES), F32),
        compiler_params=_params(1),
        name="expert_mlp",
    )(tile_idx, tile_exp, n_tiles, xs, wg, wu, wd)


def _combine_kernel(sl1_ref, sl2_ref, x1_ref, cw_ref, ys_ref, o_ref, y1_scr, y2_scr, sem):
    base = pl.program_id(0) * TB

    def issue(r, carry):
        _rows_copy(ys_ref, y1_scr, sem, sl1_ref[base + r], r, 1).start()
        _rows_copy(ys_ref, y2_scr, sem, sl2_ref[base + r], r, 1).start()
        return carry

    lax.fori_loop(0, TB, issue, 0, unroll=8)
    _rows_copy(ys_ref, y1_scr, sem, 0, 0, TB).wait()
    _rows_copy(ys_ref, y2_scr, sem, 0, 0, TB).wait()
    cw = cw_ref[...]
    o_ref[...] = (x1_ref[...] + cw[:, 0:1] * _load_token_tiles(y1_scr, TB)
                  + cw[:, 1:2] * _load_token_tiles(y2_scr, TB))


def _combine(x1, cw, ys, sl1, sl2):
    t = x1.shape[0]
    return pl.pallas_call(
        _combine_kernel,
        grid_spec=pltpu.PrefetchScalarGridSpec(
            num_scalar_prefetch=2,
            grid=(t // TB,),
            in_specs=[pl.BlockSpec((TB, D_MODEL), lambda i, *_: (i, 0)),
                      pl.BlockSpec((TB, LANES), lambda i, *_: (i, 0)),
                      pl.BlockSpec(memory_space=pl.ANY)],
            out_specs=pl.BlockSpec((TB, D_MODEL), lambda i, *_: (i, 0)),
            scratch_shapes=[pltpu.VMEM((TB * SUB, LANES), F32),
                            pltpu.VMEM((TB * SUB, LANES), F32),
                            pltpu.SemaphoreType.DMA(())],
        ),
        out_shape=jax.ShapeDtypeStruct((t, D_MODEL), F32),
        compiler_params=_params(1),
        name="combine",
    )(sl1, sl2, x1, cw, ys)


def _lane_tile(v, reps):
    return jnp.tile(v.reshape(1, -1), (1, reps))


def kernel(x, positions, norm1_g, w_in, q_norm_g, k_norm_g, lambda_q1, lambda_k1, lambda_q2,
           lambda_k2, diff_out_norm_g, gla_w_gate2, gla_b_gate, gla_out_norm_g, w_out, norm2_g,
           w_router_group, b_router_group, w_router_expert, b_router_expert, w_gate_expert,
           w_up_expert, w_down_expert):
    batch, seq, d = x.shape
    t = batch * seq
    x2 = x.reshape(t, d)

    inv = ROPE_THETA ** (-jnp.arange(0, ROT_DIM, 2, dtype=F32) / ROT_DIM)
    lane_d = jnp.arange(LANES) % DIFF_QK_DIM
    invf = jnp.where(lane_d < ROT_DIM, inv[lane_d % (ROT_DIM // 2)], 0.0).reshape(1, LANES)
    pos_b = jnp.broadcast_to(positions.astype(F32).reshape(t, 1), (t, LANES))
    tabs = _rope_tables(pos_b, invf)

    w_in2 = w_in[0]
    wa = jnp.pad(w_in2[:, D_MAIN:], ((0, 0), (0, LANES - GLA_GATE_RANK)))
    proj, ga = _in_proj(x2, norm1_g, w_in2, wa)

    a_out = _diff_attention(
        proj, tabs, _lane_tile(q_norm_g[0], 2), _lane_tile(k_norm_g[0], 2),
        lambda_q1, lambda_k1, lambda_q2, lambda_k2, diff_out_norm_g, batch, seq)
    w2p = jnp.pad(gla_w_gate2[0], ((0, LANES - GLA_GATE_RANK), (0, 0)))
    g_out = _gla(proj, ga, w2p, gla_b_gate, gla_out_norm_g, batch, seq)

    n_r = N_GROUPS + N_EXPERTS
    wr = jnp.concatenate(
        [w_router_group[0], w_router_expert[0].transpose(1, 0, 2).reshape(d, N_EXPERTS)], axis=1)
    wr = jnp.pad(wr, ((0, 0), (0, LANES - n_r)))
    br = jnp.pad(jnp.concatenate([b_router_group[0], b_router_expert[0].reshape(-1)]),
                 (0, LANES - n_r)).reshape(1, LANES)
    x1, h2, logits = _out_proj(a_out, g_out, x2, w_out[0].astype(BF16), norm2_g, wr, br)

    slots, cw, counts = _routing(logits)
    sl1 = slots[:, 0]
    sl2 = slots[:, 1]
    n_rows = 2 * t + N_EXPERTS * TM_E
    max_tiles = n_rows // TM_E
    cnt = counts[0, :N_EXPERTS].astype(jnp.int32)
    tile_end = jnp.cumsum((cnt + (TM_E - 1)) // TM_E)
    n_tiles = tile_end[-1:]
    tile_idx = jnp.minimum(jnp.arange(max_tiles, dtype=jnp.int32), n_tiles[0] - 1)
    tile_exp = jnp.sum(tile_idx[:, None] >= tile_end[None, :], axis=1).astype(jnp.int32)
    row_end = (tile_end * TM_E).astype(jnp.int32)

    xs = _dispatch(h2, sl1, sl2, row_end, n_rows)
    ys = _expert_mlp(xs, tile_idx, tile_exp, n_tiles.astype(jnp.int32),
                     w_gate_expert[0].reshape(N_EXPERTS, d, D_FF),
                     w_up_expert[0].reshape(N_EXPERTS, d, D_FF),
                     w_down_expert[0].reshape(N_EXPERTS, D_FF, d))
    out = _combine(x1, cw, ys, sl1, sl2)
    return out.reshape(batch, seq, d)
```

```python
import functools
import math

import jax
import jax.numpy as jnp
from jax import lax
from jax.experimental import pallas as pl
from jax.experimental.pallas import tpu as pltpu

D_MODEL = 2048
CHUNK = 64
DIFF_QK_DIM = 64
DIFF_V_DIM = 128
DIFF_HEADS = 8
ROT_DIM = 16
ROPE_THETA = 500000.0
GLA_HEADS = 4
GLA_V_DIM = 256
GLA_K_DIM = 128
GLA_GATE_RANK = 16
GLA_TAU = 16.0
N_GROUPS = 4
EXPERTS_PER_GROUP = 8
N_EXPERTS = N_GROUPS * EXPERTS_PER_GROUP
D_FF = 512
RMS_EPS = 1e-6
LAMBDA_INIT = 0.8 - 0.6 * math.exp(-0.3 * 0)
D_MAIN = 6144

LANES = 128
SUB = D_MODEL // LANES
VMEM_LIMIT = 56 * 1024 * 1024

TM_IN = 1024
TN_IN = 512
TQ = 256
GB = 256
TM_OUT = 512
RB_OUT = 256
TB = 256
TM_E = 256

F32 = jnp.float32
BF16 = jnp.bfloat16
HI = lax.Precision.HIGHEST


def _params(n_axes):
    return pltpu.CompilerParams(dimension_semantics=("arbitrary",) * n_axes,
                                vmem_limit_bytes=VMEM_LIMIT)


def _nt_dot(a, b):
    return lax.dot_general(a, b, (((1,), (1,)), ((), ())), preferred_element_type=F32)


def _tn_dot(a, b):
    return lax.dot_general(a, b, (((0,), (0,)), ((), ())), preferred_element_type=F32)


def _silu(x):
    return x * (1.0 / (1.0 + jnp.exp(-x)))


def _store_token_tiles(ref, val, tok0=0):
    n = val.shape[0]
    for s in range(SUB):
        ref[pl.ds(tok0 * SUB + s, n, stride=SUB), :] = val[:, s * LANES:(s + 1) * LANES]


def _load_token_tiles(ref, n, tok0=0):
    return jnp.concatenate(
        [ref[pl.ds(tok0 * SUB + s, n, stride=SUB), :] for s in range(SUB)], axis=1)


def _rope_table_kernel(pos_ref, invf_ref, c_ref, s1_ref, s2_ref):
    ang = pos_ref[...] * invf_ref[...]
    d = lax.broadcasted_iota(jnp.int32, ang.shape, 1) % DIFF_QK_DIM
    cos = jnp.cos(ang)
    sin = jnp.sin(ang)
    half = ROT_DIM // 2
    c_ref[...] = jnp.where(d < ROT_DIM, cos, 1.0)
    s1_ref[...] = jnp.where(d < half, -sin, 0.0)
    s2_ref[...] = jnp.where((d >= half) & (d < ROT_DIM), sin, 0.0)


def _rope_tables(pos_b, invf):
    t = pos_b.shape[0]
    tb = 1024
    spec = pl.BlockSpec((tb, LANES), lambda i: (i, 0))
    return pl.pallas_call(
        _rope_table_kernel,
        grid=(t // tb,),
        in_specs=[spec, pl.BlockSpec((1, LANES), lambda i: (0, 0))],
        out_specs=[spec, spec, spec],
        out_shape=[jax.ShapeDtypeStruct((t, LANES), F32)] * 3,
        compiler_params=_params(1),
        name="rope_tables",
    )(pos_b, invf)


def _in_proj_kernel(x_ref, g_ref, w_ref, wa_ref, proj_ref, ga_ref, h_scr):
    @pl.when(pl.program_id(1) == 0)
    def _():
        def body(c, carry):
            rows = pl.ds(c * 256, 256)
            x = x_ref[rows, :]
            ms = jnp.mean(x * x, axis=-1, keepdims=True)
            h_scr[rows, :] = (x * lax.rsqrt(ms + RMS_EPS) * g_ref[...]).astype(BF16)
            return carry
        lax.fori_loop(0, TM_IN // 256, body, 0)
        ga_ref[...] = jnp.dot(h_scr[...], wa_ref[...].astype(BF16), preferred_element_type=F32)

    proj_ref[...] = jnp.dot(h_scr[...], w_ref[...].astype(BF16),
                            preferred_element_type=F32).astype(BF16)


def _in_proj(x2, g1, w_in, wa):
    t = x2.shape[0]
    return pl.pallas_call(
        _in_proj_kernel,
        grid=(t // TM_IN, D_MAIN // TN_IN),
        in_specs=[
            pl.BlockSpec((TM_IN, D_MODEL), lambda i, j: (i, 0)),
            pl.BlockSpec((1, D_MODEL), lambda i, j: (0, 0)),
            pl.BlockSpec((D_MODEL, TN_IN), lambda i, j: (0, j)),
            pl.BlockSpec((D_MODEL, LANES), lambda i, j: (0, 0)),
        ],
        out_specs=[
            pl.BlockSpec((TM_IN, TN_IN), lambda i, j: (i, j)),
            pl.BlockSpec((TM_IN, LANES), lambda i, j: (i, 0)),
        ],
        out_shape=[jax.ShapeDtypeStruct((t, D_MAIN), BF16),
                   jax.ShapeDtypeStruct((t, LANES), F32)],
        scratch_shapes=[pltpu.VMEM((TM_IN, D_MODEL), BF16)],
        compiler_params=_params(2),
        name="in_proj",
    )(x2, g1, w_in, wa)


def _norm_rope(x, g, c, s1, s2):
    lo = lax.broadcasted_iota(jnp.int32, x.shape, 1) < DIFF_QK_DIM
    x2 = x * x
    s_lo = jnp.sum(jnp.where(lo, x2, 0.0), axis=-1, keepdims=True)
    s_hi = jnp.sum(jnp.where(lo, 0.0, x2), axis=-1, keepdims=True)
    ms = jnp.where(lo, s_lo, s_hi) * (1.0 / DIFF_QK_DIM)
    y = x * lax.rsqrt(ms + RMS_EPS) * g
    half = ROT_DIM // 2
    return y * c + pltpu.roll(y, LANES - half, 1) * s1 + pltpu.roll(y, half, 1) * s2


def _attn_kernel(q_ref, k_ref, v_ref, c_ref, s1_ref, s2_ref, qg_ref, kg_ref,
                 lq1_ref, lk1_ref, lq2_ref, lk2_ref, og_ref, o_ref,
                 q1_scr, q2_scr, k_scr):
    s_len = q_ref.shape[0]
    lam = (jnp.exp(jnp.sum(lq1_ref[...] * lk1_ref[...], axis=-1, keepdims=True))
           - jnp.exp(jnp.sum(lq2_ref[...] * lk2_ref[...], axis=-1, keepdims=True))
           + LAMBDA_INIT)
    c = c_ref[...]
    s1 = s1_ref[...]
    s2 = s2_ref[...]
    lo = lax.broadcasted_iota(jnp.int32, (s_len, LANES), 1) < DIFF_QK_DIM
    qn = _norm_rope(q_ref[...].astype(F32), qg_ref[...], c, s1, s2) * (DIFF_QK_DIM ** -0.5)
    q1_scr[...] = jnp.where(lo, qn, 0.0).astype(BF16)
    q2_scr[...] = jnp.where(lo, 0.0, qn).astype(BF16)
    k_scr[...] = _norm_rope(k_ref[...].astype(F32), kg_ref[...], c, s1, s2).astype(BF16)

    for i in range(s_len // TQ):
        nk = TQ * (i + 1)
        rows = slice(i * TQ, (i + 1) * TQ)
        kk = k_scr[0:nk, :]
        sc1 = _nt_dot(q1_scr[rows, :], kk)
        sc2 = _nt_dot(q2_scr[rows, :], kk)
        row_c = (lax.broadcasted_iota(jnp.int32, (TQ, nk), 0) + i * TQ) // CHUNK
        col_c = lax.broadcasted_iota(jnp.int32, (TQ, nk), 1) // CHUNK
        mask = col_c <= row_c
        sc1 = jnp.where(mask, sc1, -jnp.inf)
        sc2 = jnp.where(mask, sc2, -jnp.inf)
        e1 = jnp.exp(sc1 - jnp.max(sc1, axis=-1, keepdims=True))
        e2 = jnp.exp(sc2 - jnp.max(sc2, axis=-1, keepdims=True))
        r1 = 1.0 / jnp.sum(e1, axis=-1, keepdims=True)
        r2 = lam / jnp.sum(e2, axis=-1, keepdims=True)
        w = (e1 * r1 - e2 * r2).astype(BF16)
        o = jnp.dot(w, v_ref[0:nk, :], preferred_element_type=F32)
        ms = jnp.mean(o * o, axis=-1, keepdims=True)
        y = o * lax.rsqrt(ms + RMS_EPS) * og_ref[...] * (1.0 - LAMBDA_INIT)
        o_ref[rows, :] = y.astype(BF16)


def _diff_attention(proj, tabs, qg, kg, lq1, lk1, lq2, lk2, og, batch, seq):
    c, s1, s2 = tabs
    h = DIFF_HEADS
    blk = lambda off: pl.BlockSpec((seq, LANES), lambda b, hh, off=off: (b, off + hh))
    tab = pl.BlockSpec((seq, LANES), lambda b, hh: (b, 0))
    vec = lambda n: pl.BlockSpec((1, n), lambda b, hh: (0, 0))
    return pl.pallas_call(
        _attn_kernel,
        grid=(batch, h),
        in_specs=[blk(0), blk(h), blk(2 * h), tab, tab, tab,
                  vec(LANES), vec(LANES), vec(DIFF_QK_DIM), vec(DIFF_QK_DIM),
                  vec(DIFF_QK_DIM), vec(DIFF_QK_DIM), vec(LANES)],
        out_specs=pl.BlockSpec((seq, LANES), lambda b, hh: (b, hh)),
        out_shape=jax.ShapeDtypeStruct((batch * seq, h * DIFF_V_DIM), BF16),
        scratch_shapes=[pltpu.VMEM((seq, LANES), BF16)] * 3,
        compiler_params=_params(2),
        name="diff_attention",
    )(proj, proj, proj, c, s1, s2, qg, kg, lq1, lk1, lq2, lk2, og)


def _gla_kernel(q_ref, k_ref, v_ref, r_ref, ga_ref, w2_ref, b2_ref, og_ref, o_ref,
                qin_scr, kdec_scr, dec_scr, acc_scr):
    s_len = q_ref.shape[0]
    ri = lax.broadcasted_iota(jnp.int32, (GB, GB), 0)
    ci = lax.broadcasted_iota(jnp.int32, (GB, GB), 1)
    same = (ri // CHUNK) == (ci // CHUNK)
    blk_ones = jnp.where(same, 1.0, 0.0).astype(F32)
    tril = same & (ci <= ri)
    tri_ones = jnp.where(tril, 1.0, 0.0).astype(F32)

    def intra(g, carry):
        rows = pl.ds(pl.multiple_of(g * GB, GB), GB)
        pre = jnp.dot(ga_ref[rows, :], w2_ref[...], precision=HI,
                      preferred_element_type=F32) + b2_ref[...]
        la = -(jnp.maximum(-pre, 0.0) + jnp.log1p(jnp.exp(-jnp.abs(pre)))) * (1.0 / GLA_TAU)
        bc = jnp.dot(tri_ones, la, precision=HI, preferred_element_type=F32)
        bl = jnp.dot(blk_ones, la, precision=HI, preferred_element_type=F32)
        q = q_ref[rows, :].astype(F32) * (GLA_K_DIM ** -0.5)
        k = k_ref[rows, :].astype(F32)
        q_in = (q * jnp.exp(bc)).astype(BF16)
        k_in = (k * jnp.exp(-bc)).astype(BF16)
        qin_scr[rows, :] = q_in
        kdec_scr[rows, :] = (k * jnp.exp(bl - bc)).astype(BF16)
        dec_scr[rows, :] = jnp.exp(bl)
        att = jnp.where(tril, _nt_dot(q_in, k_in), 0.0).astype(BF16)
        acc_scr[rows, :] = jnp.dot(att, v_ref[rows, :], preferred_element_type=F32)
        return carry

    lax.fori_loop(0, s_len // GB, intra, 0)

    def inter(c, st):
        rows = pl.ds(pl.multiple_of(c * CHUNK, CHUNK), CHUNK)
        acc_scr[rows, :] += _nt_dot(qin_scr[rows, :], st.astype(BF16))
        ut = _tn_dot(v_ref[rows, :], kdec_scr[rows, :])
        dec = dec_scr[pl.ds(pl.multiple_of(c * CHUNK, CHUNK), 1), :]
        return dec * st + ut

    lax.fori_loop(0, s_len // CHUNK, inter, jnp.zeros((GLA_V_DIM, GLA_K_DIM), F32))

    def finish(g, carry):
        rows = pl.ds(pl.multiple_of(g * GB, GB), GB)
        o = acc_scr[rows, :]
        ms = jnp.mean(o * o, axis=-1, keepdims=True)
        y = o * lax.rsqrt(ms + RMS_EPS) * og_ref[...]
        o_ref[rows, :] = (y * _silu(r_ref[rows, :].astype(F32))).astype(BF16)
        return carry

    lax.fori_loop(0, s_len // GB, finish, 0)


def _gla(proj, ga, w2p, b2, og, batch, seq):
    hq = 3 * DIFF_HEADS
    kblk = lambda off: pl.BlockSpec((seq, GLA_K_DIM), lambda b, hh, off=off: (b, off + hh))
    vblk = lambda off: pl.BlockSpec((seq, GLA_V_DIM), lambda b, hh, off=off: (b, off + hh))
    return pl.pallas_call(
        _gla_kernel,
        grid=(batch, GLA_HEADS),
        in_specs=[kblk(hq), kblk(hq + GLA_HEADS), vblk(16), vblk(16 + GLA_HEADS),
                  pl.BlockSpec((seq, LANES), lambda b, hh: (b, 0)),
                  pl.BlockSpec((LANES, GLA_K_DIM), lambda b, hh: (0, hh)),
                  pl.BlockSpec((1, GLA_K_DIM), lambda b, hh: (0, hh)),
                  pl.BlockSpec((1, GLA_V_DIM), lambda b, hh: (0, 0))],
        out_specs=pl.BlockSpec((seq, GLA_V_DIM), lambda b, hh: (b, hh)),
        out_shape=jax.ShapeDtypeStruct((batch * seq, GLA_HEADS * GLA_V_DIM), BF16),
        scratch_shapes=[pltpu.VMEM((seq, GLA_K_DIM), BF16),
                        pltpu.VMEM((seq, GLA_K_DIM), BF16),
                        pltpu.VMEM((seq, GLA_K_DIM), F32),
                        pltpu.VMEM((seq, GLA_V_DIM), F32)],
        compiler_params=_params(2),
        name="gla",
    )(proj, proj, proj, proj, ga, w2p, b2, og)


def _out_proj_kernel(a_ref, g_ref, x_ref, wo_ref, g2_ref, wr_ref, br_ref,
                     x1_ref, h2_ref, lg_ref):
    half = a_ref.shape[1]

    def body(c, carry):
        r0 = pl.multiple_of(c * RB_OUT, RB_OUT)
        rows = pl.ds(r0, RB_OUT)
        mixed = (jnp.dot(a_ref[rows, :], wo_ref[0:half, :], preferred_element_type=F32)
                 + jnp.dot(g_ref[rows, :], wo_ref[half:, :], preferred_element_type=F32))
        x1 = x_ref[rows, :] + mixed
        x1_ref[rows, :] = x1
        ms = jnp.mean(x1 * x1, axis=-1, keepdims=True)
        h2 = x1 * lax.rsqrt(ms + RMS_EPS) * g2_ref[...]
        _store_token_tiles(h2_ref, h2, r0)
        lg_ref[rows, :] = jnp.dot(h2.astype(BF16), wr_ref[...],
                                  preferred_element_type=F32) + br_ref[...]
        return carry

    lax.fori_loop(0, TM_OUT // RB_OUT, body, 0)


def _out_proj(a_out, g_out, x2, wo, g2, wr, br):
    t = x2.shape[0]
    half = a_out.shape[1]
    row = lambda n: pl.BlockSpec((TM_OUT, n), lambda i: (i, 0))
    full = lambda r, n: pl.BlockSpec((r, n), lambda i: (0, 0))
    return pl.pallas_call(
        _out_proj_kernel,
        grid=(t // TM_OUT,),
        in_specs=[row(half), row(half), row(D_MODEL), full(D_MODEL, D_MODEL),
                  full(1, D_MODEL), full(D_MODEL, LANES), full(1, LANES)],
        out_specs=[row(D_MODEL), pl.BlockSpec((TM_OUT * SUB, LANES), lambda i: (i, 0)),
                   row(LANES)],
        out_shape=[jax.ShapeDtypeStruct((t, D_MODEL), F32),
                   jax.ShapeDtypeStruct((t * SUB, LANES), F32),
                   jax.ShapeDtypeStruct((t, LANES), F32)],
        compiler_params=_params(1),
        name="out_proj_router",
    )(a_out, g_out, x2, wo, g2, wr, br)


def _first_argmax(vals, lane_f, valid):
    v = jnp.where(valid, vals, -jnp.inf)
    m = jnp.max(v, axis=-1, keepdims=True)
    idx = jnp.min(jnp.where(valid & (v == m), lane_f, float(LANES)), axis=-1, keepdims=True)
    return m, idx


def _routing_kernel(lg_ref, slots_ref, cw_ref, cnt_ref, e1_scr, e2_scr, r1_scr, r2_scr):
    t = lg_ref.shape[0]
    lane = lax.broadcasted_iota(jnp.int32, (TB, LANES), 1)
    lane_f = lane.astype(F32)
    ri = lax.broadcasted_iota(jnp.int32, (TB, TB), 0)
    ci = lax.broadcasted_iota(jnp.int32, (TB, TB), 1)
    strict_lower = jnp.where(ci < ri, 1.0, 0.0).astype(BF16)

    def softmax_in(lg, valid):
        m = jnp.max(jnp.where(valid, lg, -jnp.inf), axis=-1, keepdims=True)
        ex = jnp.where(valid, jnp.exp(lg - m), 0.0)
        return ex / jnp.sum(ex, axis=-1, keepdims=True)

    def phase1(b, carry):
        rows = pl.ds(pl.multiple_of(b * TB, TB), TB)
        lg = lg_ref[rows, :]
        is_g = lane < N_GROUPS
        pg_sel, g_sel = _first_argmax(softmax_in(lg, is_g), lane_f, is_g)
        e_lo = N_GROUPS + g_sel * EXPERTS_PER_GROUP
        in_grp = (lane_f >= e_lo) & (lane_f < e_lo + EXPERTS_PER_GROUP)
        pe = softmax_in(lg, in_grp)
        v1, l1 = _first_argmax(pe, lane_f, in_grp)
        v2, l2 = _first_argmax(pe, lane_f, in_grp & (lane_f != l1))
        tot = v1 + v2
        c1 = (v1 / tot) * pg_sel
        c2 = (v2 / tot) * pg_sel
        e1 = (l1 - N_GROUPS).astype(jnp.int32)
        e2 = (l2 - N_GROUPS).astype(jnp.int32)
        oh1 = lane == e1
        oh2 = lane == e2
        a = jnp.where(oh1 | oh2, 1.0, 0.0)
        rank = jnp.dot(strict_lower, a.astype(BF16), preferred_element_type=F32) + carry
        e1_scr[rows, :] = jnp.broadcast_to(e1, (TB, LANES))
        e2_scr[rows, :] = jnp.broadcast_to(e2, (TB, LANES))
        r1_scr[rows, :] = jnp.broadcast_to(
            jnp.sum(jnp.where(oh1, rank, 0.0), axis=-1, keepdims=True), (TB, LANES))
        r2_scr[rows, :] = jnp.broadcast_to(
            jnp.sum(jnp.where(oh2, rank, 0.0), axis=-1, keepdims=True), (TB, LANES))
        cw_ref[rows, :] = jnp.where(lane == 0, c1, jnp.where(lane == 1, c2, 0.0))
        return carry + jnp.sum(a, axis=0, keepdims=True)

    counts = lax.fori_loop(0, t // TB, phase1, jnp.zeros((1, LANES), F32))
    cnt_ref[...] = jnp.broadcast_to(counts, cnt_ref.shape)
    n_tiles = jnp.floor((counts + (TM_E - 1)) * (1.0 / TM_E))
    ui = lax.broadcasted_iota(jnp.int32, (LANES, LANES), 0)
    uj = lax.broadcasted_iota(jnp.int32, (LANES, LANES), 1)
    strict_upper = jnp.where(ui < uj, 1.0, 0.0).astype(BF16)
    tile_off = jnp.dot(jnp.broadcast_to(n_tiles, (8, LANES)).astype(BF16), strict_upper,
                       preferred_element_type=F32)[0:1, :]
    row_off = tile_off * TM_E

    def phase2(b, carry):
        rows = pl.ds(pl.multiple_of(b * TB, TB), TB)
        off1 = jnp.sum(jnp.where(lane == e1_scr[rows, :], row_off, 0.0), axis=-1, keepdims=True)
        off2 = jnp.sum(jnp.where(lane == e2_scr[rows, :], row_off, 0.0), axis=-1, keepdims=True)
        sl1 = (off1 + r1_scr[rows, 0:1]).astype(jnp.int32)
        sl2 = (off2 + r2_scr[rows, 0:1]).astype(jnp.int32)
        slots_ref[rows, :] = jnp.where(lane == 0, sl1, jnp.where(lane == 1, sl2, 0))
        return carry

    lax.fori_loop(0, t // TB, phase2, 0)


def _routing(logits):
    t = logits.shape[0]
    return pl.pallas_call(
        _routing_kernel,
        out_shape=[jax.ShapeDtypeStruct((t, LANES), jnp.int32),
                   jax.ShapeDtypeStruct((t, LANES), F32),
                   jax.ShapeDtypeStruct((8, LANES), F32)],
        scratch_shapes=[pltpu.VMEM((t, LANES), jnp.int32), pltpu.VMEM((t, LANES), jnp.int32),
                        pltpu.VMEM((t, LANES), F32), pltpu.VMEM((t, LANES), F32)],
        compiler_params=pltpu.CompilerParams(vmem_limit_bytes=VMEM_LIMIT),
        name="routing",
    )(logits)


def _rows_copy(src, dst, sem, src_tok, dst_tok, n):
    first = lambda tok: tok * SUB if isinstance(tok, int) else pl.multiple_of(tok * SUB, SUB)
    s0 = first(src_tok)
    d0 = first(dst_tok)
    return pltpu.make_async_copy(src.at[pl.ds(s0, n * SUB), :], dst.at[pl.ds(d0, n * SUB), :], sem)


def _dispatch_kernel(sl1_ref, sl2_ref, ends_ref, h2_ref, xs_ref, zero_scr, sem):
    step = pl.program_id(0)

    @pl.when(step == 0)
    def _():
        zero_scr[...] = jnp.zeros_like(zero_scr)
        for e in range(N_EXPERTS):
            start = jnp.maximum(ends_ref[e] - TM_E, 0)
            _rows_copy(zero_scr, xs_ref, sem, 0, start, TM_E).start()
        for e in range(N_EXPERTS):
            _rows_copy(zero_scr, xs_ref, sem, 0, 0, TM_E).wait()

    base = step * TB

    def issue(r, carry):
        _rows_copy(h2_ref, xs_ref, sem, r, sl1_ref[base + r], 1).start(priority=0)
        _rows_copy(h2_ref, xs_ref, sem, r, sl2_ref[base + r], 1).start(priority=1)
        return carry

    lax.fori_loop(0, TB, issue, 0, unroll=8)
    _rows_copy(h2_ref, xs_ref, sem, 0, 0, TB).wait()
    _rows_copy(h2_ref, xs_ref, sem, 0, 0, TB).wait()


def _dispatch(h2, sl1, sl2, ends, n_rows):
    t = h2.shape[0] // SUB
    return pl.pallas_call(
        _dispatch_kernel,
        grid_spec=pltpu.PrefetchScalarGridSpec(
            num_scalar_prefetch=3,
            grid=(t // TB,),
            in_specs=[pl.BlockSpec((TB * SUB, LANES), lambda i, *_: (i, 0))],
            out_specs=pl.BlockSpec(memory_space=pl.ANY),
            scratch_shapes=[pltpu.VMEM((TM_E * SUB, LANES), F32), pltpu.SemaphoreType.DMA(())],
        ),
        out_shape=jax.ShapeDtypeStruct((n_rows * SUB, LANES), F32),
        compiler_params=_params(1),
        name="dispatch",
    )(sl1, sl2, ends, h2)


def _expert_kernel(tidx_ref, texp_ref, tfirst_ref, tslot_ref, tnext_ref, nt_ref,
                   xs_ref, wg_hbm, wu_hbm, wd_hbm, ys_ref,
                   wg_buf, wu_buf, wd_buf, wg_bf, wu_bf, wd_bf, sems):
    j = pl.program_id(0)

    def weight_copies(e, slot):
        return [pltpu.make_async_copy(hbm.at[e], buf.at[slot], sems.at[slot, k])
                for k, (hbm, buf) in enumerate(((wg_hbm, wg_buf), (wu_hbm, wu_buf),
                                                (wd_hbm, wd_buf)))]

    @pl.when(j < nt_ref[0])
    def _():
        slot = tslot_ref[j]

        @pl.when(tfirst_ref[j] == 1)
        def _():
            @pl.when(j == 0)
            def _():
                for c in weight_copies(texp_ref[j], slot):
                    c.start()
            for c in weight_copies(texp_ref[j], slot):
                c.wait()

            @pl.when(tnext_ref[j] >= 0)
            def _():
                for c in weight_copies(tnext_ref[j], 1 - slot):
                    c.start()
            wg_bf[...] = wg_buf[slot].astype(BF16)
            wu_bf[...] = wu_buf[slot].astype(BF16)
            wd_bf[...] = wd_buf[slot].astype(BF16)

        xb = _load_token_tiles(xs_ref, TM_E).astype(BF16)
        g = jnp.dot(xb, wg_bf[...], preferred_element_type=F32)
        u = jnp.dot(xb, wu_bf[...], preferred_element_type=F32)
        h = (_silu(g) * u).astype(BF16)
        _store_token_tiles(ys_ref, jnp.dot(h, wd_bf[...], preferred_element_type=F32))


def _expert_mlp(xs, tile_meta, wg, wu, wd):
    n_rows = xs.shape[0] // SUB
    tile = pl.BlockSpec((TM_E * SUB, LANES), lambda j, ti, *_: (ti[j], 0))
    hbm = pl.BlockSpec(memory_space=pl.ANY)
    return pl.pallas_call(
        _expert_kernel,
        grid_spec=pltpu.PrefetchScalarGridSpec(
            num_scalar_prefetch=len(tile_meta),
            grid=(n_rows // TM_E,),
            in_specs=[tile, hbm, hbm, hbm],
            out_specs=tile,
            scratch_shapes=[pltpu.VMEM((2, D_MODEL, D_FF), F32),
                            pltpu.VMEM((2, D_MODEL, D_FF), F32),
                            pltpu.VMEM((2, D_FF, D_MODEL), F32),
                            pltpu.VMEM((D_MODEL, D_FF), BF16),
                            pltpu.VMEM((D_MODEL, D_FF), BF16),
                            pltpu.VMEM((D_FF, D_MODEL), BF16),
                            pltpu.SemaphoreType.DMA((2, 3))],
        ),
        out_shape=jax.ShapeDtypeStruct((n_rows * SUB, LANES), F32),
        compiler_params=_params(1),
        name="expert_mlp",
    )(*tile_meta, xs, wg, wu, wd)


def _combine_kernel(sl1_ref, sl2_ref, x1_ref, cw_ref, ys_ref, o_ref, y1_scr, y2_scr, sem):
    base = pl.program_id(0) * TB

    def issue(r, carry):
        _rows_copy(ys_ref, y1_scr, sem, sl1_ref[base + r], r, 1).start(priority=0)
        _rows_copy(ys_ref, y2_scr, sem, sl2_ref[base + r], r, 1).start(priority=1)
        return carry

    lax.fori_loop(0, TB, issue, 0, unroll=8)
    _rows_copy(ys_ref, y1_scr, sem, 0, 0, TB).wait()
    _rows_copy(ys_ref, y2_scr, sem, 0, 0, TB).wait()
    cw = cw_ref[...]
    o_ref[...] = (x1_ref[...] + cw[:, 0:1] * _load_token_tiles(y1_scr, TB)
                  + cw[:, 1:2] * _load_token_tiles(y2_scr, TB))


def _combine(x1, cw, ys, sl1, sl2):
    t = x1.shape[0]
    return pl.pallas_call(
        _combine_kernel,
        grid_spec=pltpu.PrefetchScalarGridSpec(
            num_scalar_prefetch=2,
            grid=(t // TB,),
            in_specs=[pl.BlockSpec((TB, D_MODEL), lambda i, *_: (i, 0)),
                      pl.BlockSpec((TB, LANES), lambda i, *_: (i, 0)),
                      pl.BlockSpec(memory_space=pl.ANY)],
            out_specs=pl.BlockSpec((TB, D_MODEL), lambda i, *_: (i, 0)),
            scratch_shapes=[pltpu.VMEM((TB * SUB, LANES), F32),
                            pltpu.VMEM((TB * SUB, LANES), F32),
                            pltpu.SemaphoreType.DMA(())],
        ),
        out_shape=jax.ShapeDtypeStruct((t, D_MODEL), F32),
        compiler_params=_params(1),
        name="combine",
    )(sl1, sl2, x1, cw, ys)


def _lane_tile(v, reps):
    return jnp.tile(v.reshape(1, -1), (1, reps))


def kernel(x, positions, norm1_g, w_in, q_norm_g, k_norm_g, lambda_q1, lambda_k1, lambda_q2,
           lambda_k2, diff_out_norm_g, gla_w_gate2, gla_b_gate, gla_out_norm_g, w_out, norm2_g,
           w_router_group, b_router_group, w_router_expert, b_router_expert, w_gate_expert,
           w_up_expert, w_down_expert):
    batch, seq, d = x.shape
    t = batch * seq
    x2 = x.reshape(t, d)

    inv = ROPE_THETA ** (-jnp.arange(0, ROT_DIM, 2, dtype=F32) / ROT_DIM)
    lane_d = jnp.arange(LANES) % DIFF_QK_DIM
    invf = jnp.where(lane_d < ROT_DIM, inv[lane_d % (ROT_DIM // 2)], 0.0).reshape(1, LANES)
    pos_b = jnp.broadcast_to(positions.astype(F32).reshape(t, 1), (t, LANES))
    tabs = _rope_tables(pos_b, invf)

    w_in2 = w_in[0]
    wa = jnp.pad(w_in2[:, D_MAIN:], ((0, 0), (0, LANES - GLA_GATE_RANK)))
    proj, ga = _in_proj(x2, norm1_g, w_in2, wa)

    a_out = _diff_attention(
        proj, tabs, _lane_tile(q_norm_g[0], 2), _lane_tile(k_norm_g[0], 2),
        lambda_q1, lambda_k1, lambda_q2, lambda_k2, diff_out_norm_g, batch, seq)
    w2p = jnp.pad(gla_w_gate2[0], ((0, LANES - GLA_GATE_RANK), (0, 0)))
    g_out = _gla(proj, ga, w2p, gla_b_gate, gla_out_norm_g, batch, seq)

    n_r = N_GROUPS + N_EXPERTS
    wr = jnp.concatenate(
        [w_router_group[0], w_router_expert[0].transpose(1, 0, 2).reshape(d, N_EXPERTS)], axis=1)
    wr = jnp.pad(wr, ((0, 0), (0, LANES - n_r)))
    br = jnp.pad(jnp.concatenate([b_router_group[0], b_router_expert[0].reshape(-1)]),
                 (0, LANES - n_r)).reshape(1, LANES)
    x1, h2, logits = _out_proj(a_out, g_out, x2, w_out[0].astype(BF16), norm2_g,
                               wr.astype(BF16), br)

    slots, cw, counts = _routing(logits)
    sl1 = slots[:, 0]
    sl2 = slots[:, 1]
    n_rows = 2 * t + N_EXPERTS * TM_E
    max_tiles = n_rows // TM_E
    cnt = counts[0, :N_EXPERTS].astype(jnp.int32)
    tile_end = jnp.cumsum((cnt + (TM_E - 1)) // TM_E)
    n_tiles = tile_end[-1:]
    tile_idx = jnp.minimum(jnp.arange(max_tiles, dtype=jnp.int32), n_tiles[0] - 1)
    tile_exp = jnp.sum(tile_idx[:, None] >= tile_end[None, :], axis=1).astype(jnp.int32)
    row_end = (tile_end * TM_E).astype(jnp.int32)
    tile_first = jnp.concatenate(
        [jnp.ones((1,), jnp.int32), (tile_exp[1:] != tile_exp[:-1]).astype(jnp.int32)])
    tile_slot = (jnp.cumsum(tile_first) - 1) % 2
    nxt = tile_end[tile_exp]
    tile_next = jnp.where(nxt < n_tiles[0], tile_exp[jnp.minimum(nxt, max_tiles - 1)], -1)
    tile_meta = [a.astype(jnp.int32)
                 for a in (tile_idx, tile_exp, tile_first, tile_slot, tile_next, n_tiles)]

    xs = _dispatch(h2, sl1, sl2, row_end, n_rows)
    ys = _expert_mlp(xs, tile_meta,
                     w_gate_expert[0].reshape(N_EXPERTS, d, D_FF),
                     w_up_expert[0].reshape(N_EXPERTS, d, D_FF),
                     w_down_expert[0].reshape(N_EXPERTS, D_FF, d))
    out = _combine(x1, cw, ys, sl1, sl2)
    return out.reshape(batch, seq, d)
```

```python
import functools
import math

import jax
import jax.numpy as jnp
from jax import lax
from jax.experimental import pallas as pl
from jax.experimental.pallas import tpu as pltpu

D_MODEL = 2048
CHUNK = 64
DIFF_QK_DIM = 64
DIFF_V_DIM = 128
DIFF_HEADS = 8
ROT_DIM = 16
ROPE_THETA = 500000.0
GLA_HEADS = 4
GLA_V_DIM = 256
GLA_K_DIM = 128
GLA_GATE_RANK = 16
GLA_TAU = 16.0
N_GROUPS = 4
EXPERTS_PER_GROUP = 8
N_EXPERTS = N_GROUPS * EXPERTS_PER_GROUP
D_FF = 512
RMS_EPS = 1e-6
LAMBDA_INIT = 0.8 - 0.6 * math.exp(-0.3 * 0)
D_MAIN = 6144

LANES = 128
SUB = D_MODEL // LANES
VMEM_LIMIT = 56 * 1024 * 1024

TM_IN = 1024
TN_IN = 512
TQ = 256
GB = 256
TM_OUT = 512
RB_OUT = 256
TB = 256
TM_E = 256

F32 = jnp.float32
BF16 = jnp.bfloat16
HI = lax.Precision.HIGHEST


def _params(n_axes):
    return pltpu.CompilerParams(dimension_semantics=("arbitrary",) * n_axes,
                                vmem_limit_bytes=VMEM_LIMIT)


def _nt_dot(a, b):
    return lax.dot_general(a, b, (((1,), (1,)), ((), ())), preferred_element_type=F32)


def _tn_dot(a, b):
    return lax.dot_general(a, b, (((0,), (0,)), ((), ())), preferred_element_type=F32)


def _silu(x):
    return x * (1.0 / (1.0 + jnp.exp(-x)))


def _store_token_tiles(ref, val, tok0=0):
    n = val.shape[0]
    for s in range(SUB):
        ref[pl.ds(tok0 * SUB + s, n, stride=SUB), :] = val[:, s * LANES:(s + 1) * LANES]


def _load_token_tiles(ref, n, tok0=0):
    return jnp.concatenate(
        [ref[pl.ds(tok0 * SUB + s, n, stride=SUB), :] for s in range(SUB)], axis=1)


def _rope_table_kernel(pos_ref, invf_ref, c_ref, s1_ref, s2_ref):
    ang = pos_ref[...] * invf_ref[...]
    d = lax.broadcasted_iota(jnp.int32, ang.shape, 1) % DIFF_QK_DIM
    cos = jnp.cos(ang)
    sin = jnp.sin(ang)
    half = ROT_DIM // 2
    c_ref[...] = jnp.where(d < ROT_DIM, cos, 1.0)
    s1_ref[...] = jnp.where(d < half, -sin, 0.0)
    s2_ref[...] = jnp.where((d >= half) & (d < ROT_DIM), sin, 0.0)


def _rope_tables(pos_b, invf):
    t = pos_b.shape[0]
    tb = 1024
    spec = pl.BlockSpec((tb, LANES), lambda i: (i, 0))
    return pl.pallas_call(
        _rope_table_kernel,
        grid=(t // tb,),
        in_specs=[spec, pl.BlockSpec((1, LANES), lambda i: (0, 0))],
        out_specs=[spec, spec, spec],
        out_shape=[jax.ShapeDtypeStruct((t, LANES), F32)] * 3,
        compiler_params=_params(1),
        name="rope_tables",
    )(pos_b, invf)


def _in_proj_kernel(x_ref, g_ref, w_ref, wa_ref, proj_ref, ga_ref, h_scr):
    @pl.when(pl.program_id(1) == 0)
    def _():
        def body(c, carry):
            rows = pl.ds(c * 256, 256)
            x = x_ref[rows, :]
            ms = jnp.mean(x * x, axis=-1, keepdims=True)
            h_scr[rows, :] = (x * lax.rsqrt(ms + RMS_EPS) * g_ref[...]).astype(BF16)
            return carry
        lax.fori_loop(0, TM_IN // 256, body, 0)
        ga_ref[...] = jnp.dot(h_scr[...], wa_ref[...].astype(BF16), preferred_element_type=F32)

    proj_ref[...] = jnp.dot(h_scr[...], w_ref[...].astype(BF16),
                            preferred_element_type=F32).astype(BF16)


def _in_proj(x2, g1, w_in, wa):
    t = x2.shape[0]
    return pl.pallas_call(
        _in_proj_kernel,
        grid=(t // TM_IN, D_MAIN // TN_IN),
        in_specs=[
            pl.BlockSpec((TM_IN, D_MODEL), lambda i, j: (i, 0)),
            pl.BlockSpec((1, D_MODEL), lambda i, j: (0, 0)),
            pl.BlockSpec((D_MODEL, TN_IN), lambda i, j: (0, j)),
            pl.BlockSpec((D_MODEL, LANES), lambda i, j: (0, 0)),
        ],
        out_specs=[
            pl.BlockSpec((TM_IN, TN_IN), lambda i, j: (i, j)),
            pl.BlockSpec((TM_IN, LANES), lambda i, j: (i, 0)),
        ],
        out_shape=[jax.ShapeDtypeStruct((t, D_MAIN), BF16),
                   jax.ShapeDtypeStruct((t, LANES), F32)],
        scratch_shapes=[pltpu.VMEM((TM_IN, D_MODEL), BF16)],
        compiler_params=_params(2),
        name="in_proj",
    )(x2, g1, w_in, wa)


def _norm_rope(x, g, c, s1, s2):
    lo = lax.broadcasted_iota(jnp.int32, x.shape, 1) < DIFF_QK_DIM
    x2 = x * x
    s_lo = jnp.sum(jnp.where(lo, x2, 0.0), axis=-1, keepdims=True)
    s_hi = jnp.sum(jnp.where(lo, 0.0, x2), axis=-1, keepdims=True)
    ms = jnp.where(lo, s_lo, s_hi) * (1.0 / DIFF_QK_DIM)
    y = x * lax.rsqrt(ms + RMS_EPS) * g
    half = ROT_DIM // 2
    return y * c + pltpu.roll(y, LANES - half, 1) * s1 + pltpu.roll(y, half, 1) * s2


def _attn_kernel(q_ref, k_ref, v_ref, c_ref, s1_ref, s2_ref, qg_ref, kg_ref,
                 lq1_ref, lk1_ref, lq2_ref, lk2_ref, og_ref, o_ref,
                 q1_scr, q2_scr, k_scr, v_scr, *bufs):
    s_len = q_ref.shape[0]
    s_bufs, e_bufs = bufs[0:4], bufs[4:8]
    lam = (jnp.exp(jnp.sum(lq1_ref[...] * lk1_ref[...], axis=-1, keepdims=True))
           - jnp.exp(jnp.sum(lq2_ref[...] * lk2_ref[...], axis=-1, keepdims=True))
           + LAMBDA_INIT)
    lo = lax.broadcasted_iota(jnp.int32, (TQ, LANES), 1) < DIFF_QK_DIM
    v_scr[:, 0:LANES] = v_ref[...]
    v_scr[:, LANES:] = jnp.ones((s_len, LANES), BF16)

    def prepare(i):
        rows = slice(i * TQ, (i + 1) * TQ)
        c, s1, s2 = c_ref[rows, :], s1_ref[rows, :], s2_ref[rows, :]
        qn = (_norm_rope(q_ref[rows, :].astype(F32), qg_ref[...], c, s1, s2)
              * (DIFF_QK_DIM ** -0.5 * math.log2(math.e)))
        q1_scr[rows, :] = jnp.where(lo, qn, 0.0).astype(BF16)
        q2_scr[rows, :] = jnp.where(lo, 0.0, qn).astype(BF16)
        k_scr[rows, :] = _norm_rope(k_ref[rows, :].astype(F32), kg_ref[...], c, s1,
                                    s2).astype(BF16)

    diag = (lax.broadcasted_iota(jnp.int32, (TQ, TQ), 1) // CHUNK
            <= lax.broadcasted_iota(jnp.int32, (TQ, TQ), 0) // CHUNK)

    n_tiles = s_len // TQ
    items = [(i, q_scr) for i in range(n_tiles) for q_scr in (q1_scr, q2_scr)]

    def scores(r):
        i, q_scr = items[r]
        s_scr = s_bufs[r % 4]
        q = q_scr[i * TQ:(i + 1) * TQ, :]
        n_off = i * TQ
        s_scr[:, n_off:n_off + TQ] = jnp.where(
            diag, _nt_dot(q, k_scr[n_off:n_off + TQ, :]), -jnp.inf)
        if n_off:
            s_scr[:, 0:n_off] = _nt_dot(q, k_scr[0:n_off, :])

    def softmax_pv(r):
        nk = (items[r][0] + 1) * TQ
        s_scr, e_scr = s_bufs[r % 4], e_bufs[r % 4]
        m = jnp.max(s_scr[:, 0:nk], axis=-1, keepdims=True)
        e_scr[:, 0:nk] = jnp.exp2(s_scr[:, 0:nk] - m).astype(BF16)
        acc = jnp.dot(e_scr[:, 0:nk], v_scr[0:nk, :], preferred_element_type=F32)
        return acc[:, 0:LANES] / acc[:, LANES:]

    prepare(0)
    scores(0)
    scores(1)
    for i in range(n_tiles):
        rows = slice(i * TQ, (i + 1) * TQ)
        if i + 1 < n_tiles:
            prepare(i + 1)
            scores(2 * i + 2)
        o1 = softmax_pv(2 * i)
        if i + 1 < n_tiles:
            scores(2 * i + 3)
        o = o1 - lam * softmax_pv(2 * i + 1)
        ms = jnp.mean(o * o, axis=-1, keepdims=True)
        y = o * lax.rsqrt(ms + RMS_EPS) * og_ref[...] * (1.0 - LAMBDA_INIT)
        o_ref[rows, :] = y.astype(BF16)


def _diff_attention(proj, tabs, qg, kg, lq1, lk1, lq2, lk2, og, batch, seq):
    c, s1, s2 = tabs
    h = DIFF_HEADS
    blk = lambda off: pl.BlockSpec((seq, LANES), lambda b, hh, off=off: (b, off + hh))
    tab = pl.BlockSpec((seq, LANES), lambda b, hh: (b, 0))
    vec = lambda n: pl.BlockSpec((1, n), lambda b, hh: (0, 0))
    return pl.pallas_call(
        _attn_kernel,
        grid=(batch, h),
        in_specs=[blk(0), blk(h), blk(2 * h), tab, tab, tab,
                  vec(LANES), vec(LANES), vec(DIFF_QK_DIM), vec(DIFF_QK_DIM),
                  vec(DIFF_QK_DIM), vec(DIFF_QK_DIM), vec(LANES)],
        out_specs=pl.BlockSpec((seq, LANES), lambda b, hh: (b, hh)),
        out_shape=jax.ShapeDtypeStruct((batch * seq, h * DIFF_V_DIM), BF16),
        scratch_shapes=[pltpu.VMEM((seq, LANES), BF16)] * 3
        + [pltpu.VMEM((seq, 2 * LANES), BF16)]
        + [pltpu.VMEM((TQ, seq), F32)] * 4 + [pltpu.VMEM((TQ, seq), BF16)] * 4,
        compiler_params=_params(2),
        name="diff_attention",
    )(proj, proj, proj, c, s1, s2, qg, kg, lq1, lk1, lq2, lk2, og)


def _split_dot(ones_bf, x):
    hi = x.astype(BF16)
    lo = (x - hi.astype(F32)).astype(BF16)
    return (jnp.dot(ones_bf, hi, preferred_element_type=F32)
            + jnp.dot(ones_bf, lo, preferred_element_type=F32))


def _gla_kernel(q_ref, k_ref, v_ref, r_ref, ga_ref, w2_ref, b2_ref, og_ref, o_ref,
                qin_scr, ut_scr, dec_scr, st_scr, acc_scr):
    s_len = q_ref.shape[0]
    cpg = GB // CHUNK
    ri = lax.broadcasted_iota(jnp.int32, (GB, GB), 0)
    ci = lax.broadcasted_iota(jnp.int32, (GB, GB), 1)
    same = (ri // CHUNK) == (ci // CHUNK)
    blk_ones = jnp.where(same, 1.0, 0.0).astype(BF16)
    tril = same & (ci <= ri)
    tri_ones = jnp.where(tril, 1.0, 0.0).astype(BF16)

    def phase_a(g, carry):
        rows = pl.ds(pl.multiple_of(g * GB, GB), GB)
        pre = jnp.dot(ga_ref[rows, :].astype(BF16), w2_ref[...],
                      preferred_element_type=F32) + b2_ref[...]
        la = -(jnp.maximum(-pre, 0.0) + jnp.log1p(jnp.exp(-jnp.abs(pre)))) * (1.0 / GLA_TAU)
        bc = _split_dot(tri_ones, la)
        bl = _split_dot(blk_ones, la)
        e_neg = jnp.exp(-bc)
        e_last = jnp.exp(bl)
        k = k_ref[rows, :].astype(F32)
        q_in = (q_ref[rows, :].astype(F32) * (GLA_K_DIM ** -0.5) * jnp.exp(bc)).astype(BF16)
        k_in = (k * e_neg).astype(BF16)
        k_dec = (k * (e_last * e_neg)).astype(BF16)
        qin_scr[rows, :] = q_in
        dec_scr[rows, :] = e_last
        v = v_ref[rows, :]
        att = jnp.where(tril, _nt_dot(q_in, k_in), 0.0).astype(BF16)
        acc_scr[rows, :] = jnp.dot(att, v, preferred_element_type=F32)
        for c in range(cpg):
            cr = slice(c * CHUNK, (c + 1) * CHUNK)
            ut_scr[g * cpg + c] = _tn_dot(v[cr, :], k_dec[cr, :])
        return carry

    lax.fori_loop(0, s_len // GB, phase_a, 0)

    def phase_b(c, st):
        st_scr[c] = st.astype(BF16)
        dec = dec_scr[pl.ds(pl.multiple_of(c * CHUNK, CHUNK), 1), :]
        return dec * st + ut_scr[c]

    lax.fori_loop(0, s_len // CHUNK, phase_b, jnp.zeros((GLA_V_DIM, GLA_K_DIM), F32))

    def phase_c(g, carry):
        rows = pl.ds(pl.multiple_of(g * GB, GB), GB)
        inter = [_nt_dot(qin_scr[pl.ds(pl.multiple_of(g * GB + c * CHUNK, CHUNK), CHUNK), :],
                         st_scr[g * cpg + c]) for c in range(cpg)]
        o = acc_scr[rows, :] + jnp.concatenate(inter, axis=0)
        ms = jnp.mean(o * o, axis=-1, keepdims=True)
        y = o * lax.rsqrt(ms + RMS_EPS) * og_ref[...]
        o_ref[rows, :] = (y * _silu(r_ref[rows, :].astype(F32))).astype(BF16)
        return carry

    lax.fori_loop(0, s_len // GB, phase_c, 0)


def _gla(proj, ga, w2p, b2, og, batch, seq):
    hq = 3 * DIFF_HEADS
    kblk = lambda off: pl.BlockSpec((seq, GLA_K_DIM), lambda b, hh, off=off: (b, off + hh))
    vblk = lambda off: pl.BlockSpec((seq, GLA_V_DIM), lambda b, hh, off=off: (b, off + hh))
    return pl.pallas_call(
        _gla_kernel,
        grid=(batch, GLA_HEADS),
        in_specs=[kblk(hq), kblk(hq + GLA_HEADS), vblk(16), vblk(16 + GLA_HEADS),
                  pl.BlockSpec((seq, LANES), lambda b, hh: (b, 0)),
                  pl.BlockSpec((LANES, GLA_K_DIM), lambda b, hh: (0, hh)),
                  pl.BlockSpec((1, GLA_K_DIM), lambda b, hh: (0, hh)),
                  pl.BlockSpec((1, GLA_V_DIM), lambda b, hh: (0, 0))],
        out_specs=pl.BlockSpec((seq, GLA_V_DIM), lambda b, hh: (b, hh)),
        out_shape=jax.ShapeDtypeStruct((batch * seq, GLA_HEADS * GLA_V_DIM), BF16),
        scratch_shapes=[pltpu.VMEM((seq, GLA_K_DIM), BF16),
                        pltpu.VMEM((seq // CHUNK, GLA_V_DIM, GLA_K_DIM), F32),
                        pltpu.VMEM((seq, GLA_K_DIM), F32),
                        pltpu.VMEM((seq // CHUNK, GLA_V_DIM, GLA_K_DIM), BF16),
                        pltpu.VMEM((seq, GLA_V_DIM), F32)],
        compiler_params=_params(2),
        name="gla",
    )(proj, proj, proj, proj, ga, w2p, b2, og)


def _out_proj_kernel(a_ref, g_ref, x_ref, wo_ref, g2_ref, wr_ref, br_ref,
                     x1_ref, h2_ref, lg_ref):
    half = a_ref.shape[1]

    def body(c, carry):
        r0 = pl.multiple_of(c * RB_OUT, RB_OUT)
        rows = pl.ds(r0, RB_OUT)
        mixed = (jnp.dot(a_ref[rows, :], wo_ref[0:half, :], preferred_element_type=F32)
                 + jnp.dot(g_ref[rows, :], wo_ref[half:, :], preferred_element_type=F32))
        x1 = x_ref[rows, :] + mixed
        x1_ref[rows, :] = x1
        ms = jnp.mean(x1 * x1, axis=-1, keepdims=True)
        h2 = x1 * lax.rsqrt(ms + RMS_EPS) * g2_ref[...]
        _store_token_tiles(h2_ref, h2, r0)
        lg_ref[rows, :] = jnp.dot(h2.astype(BF16), wr_ref[...],
                                  preferred_element_type=F32) + br_ref[...]
        return carry

    lax.fori_loop(0, TM_OUT // RB_OUT, body, 0)


def _out_proj(a_out, g_out, x2, wo, g2, wr, br):
    t = x2.shape[0]
    half = a_out.shape[1]
    row = lambda n: pl.BlockSpec((TM_OUT, n), lambda i: (i, 0))
    full = lambda r, n: pl.BlockSpec((r, n), lambda i: (0, 0))
    return pl.pallas_call(
        _out_proj_kernel,
        grid=(t // TM_OUT,),
        in_specs=[row(half), row(half), row(D_MODEL), full(D_MODEL, D_MODEL),
                  full(1, D_MODEL), full(D_MODEL, LANES), full(1, LANES)],
        out_specs=[row(D_MODEL), pl.BlockSpec((TM_OUT * SUB, LANES), lambda i: (i, 0)),
                   row(LANES)],
        out_shape=[jax.ShapeDtypeStruct((t, D_MODEL), F32),
                   jax.ShapeDtypeStruct((t * SUB, LANES), F32),
                   jax.ShapeDtypeStruct((t, LANES), F32)],
        compiler_params=_params(1),
        name="out_proj_router",
    )(a_out, g_out, x2, wo, g2, wr, br)


def _first_argmax(vals, lane_f, valid):
    v = jnp.where(valid, vals, -jnp.inf)
    m = jnp.max(v, axis=-1, keepdims=True)
    idx = jnp.min(jnp.where(valid & (v == m), lane_f, float(LANES)), axis=-1, keepdims=True)
    return m, idx


def _routing_kernel(lg_ref, slots_ref, cw_ref, cnt_ref, e1_scr, e2_scr, r1_scr, r2_scr):
    t = lg_ref.shape[0]
    lane = lax.broadcasted_iota(jnp.int32, (TB, LANES), 1)
    lane_f = lane.astype(F32)
    ri = lax.broadcasted_iota(jnp.int32, (TB, TB), 0)
    ci = lax.broadcasted_iota(jnp.int32, (TB, TB), 1)
    strict_lower = jnp.where(ci < ri, 1.0, 0.0).astype(BF16)

    def softmax_in(lg, valid):
        m = jnp.max(jnp.where(valid, lg, -jnp.inf), axis=-1, keepdims=True)
        ex = jnp.where(valid, jnp.exp(lg - m), 0.0)
        return ex / jnp.sum(ex, axis=-1, keepdims=True)

    def phase1(b, carry):
        rows = pl.ds(pl.multiple_of(b * TB, TB), TB)
        lg = lg_ref[rows, :]
        is_g = lane < N_GROUPS
        pg_sel, g_sel = _first_argmax(softmax_in(lg, is_g), lane_f, is_g)
        e_lo = N_GROUPS + g_sel * EXPERTS_PER_GROUP
        in_grp = (lane_f >= e_lo) & (lane_f < e_lo + EXPERTS_PER_GROUP)
        pe = softmax_in(lg, in_grp)
        v1, l1 = _first_argmax(pe, lane_f, in_grp)
        v2, l2 = _first_argmax(pe, lane_f, in_grp & (lane_f != l1))
        tot = v1 + v2
        c1 = (v1 / tot) * pg_sel
        c2 = (v2 / tot) * pg_sel
        e1 = (l1 - N_GROUPS).astype(jnp.int32)
        e2 = (l2 - N_GROUPS).astype(jnp.int32)
        oh1 = lane == e1
        oh2 = lane == e2
        a = jnp.where(oh1 | oh2, 1.0, 0.0)
        rank = jnp.dot(strict_lower, a.astype(BF16), preferred_element_type=F32) + carry
        e1_scr[rows, :] = jnp.broadcast_to(e1, (TB, LANES))
        e2_scr[rows, :] = jnp.broadcast_to(e2, (TB, LANES))
        r1_scr[rows, :] = jnp.broadcast_to(
            jnp.sum(jnp.where(oh1, rank, 0.0), axis=-1, keepdims=True), (TB, LANES))
        r2_scr[rows, :] = jnp.broadcast_to(
            jnp.sum(jnp.where(oh2, rank, 0.0), axis=-1, keepdims=True), (TB, LANES))
        cw_ref[rows, :] = jnp.where(lane == 0, c1, jnp.where(lane == 1, c2, 0.0))
        return carry + jnp.sum(a, axis=0, keepdims=True)

    counts = lax.fori_loop(0, t // TB, phase1, jnp.zeros((1, LANES), F32))
    cnt_ref[...] = jnp.broadcast_to(counts, cnt_ref.shape)
    n_tiles = jnp.floor((counts + (TM_E - 1)) * (1.0 / TM_E))
    ui = lax.broadcasted_iota(jnp.int32, (LANES, LANES), 0)
    uj = lax.broadcasted_iota(jnp.int32, (LANES, LANES), 1)
    strict_upper = jnp.where(ui < uj, 1.0, 0.0).astype(BF16)
    tile_off = jnp.dot(jnp.broadcast_to(n_tiles, (8, LANES)).astype(BF16), strict_upper,
                       preferred_element_type=F32)[0:1, :]
    row_off = tile_off * TM_E

    def phase2(b, carry):
        rows = pl.ds(pl.multiple_of(b * TB, TB), TB)
        off1 = jnp.sum(jnp.where(lane == e1_scr[rows, :], row_off, 0.0), axis=-1, keepdims=True)
        off2 = jnp.sum(jnp.where(lane == e2_scr[rows, :], row_off, 0.0), axis=-1, keepdims=True)
        sl1 = (off1 + r1_scr[rows, 0:1]).astype(jnp.int32)
        sl2 = (off2 + r2_scr[rows, 0:1]).astype(jnp.int32)
        slots_ref[rows, :] = jnp.where(lane == 0, sl1, jnp.where(lane == 1, sl2, 0))
        return carry

    lax.fori_loop(0, t // TB, phase2, 0)


def _routing(logits):
    t = logits.shape[0]
    return pl.pallas_call(
        _routing_kernel,
        out_shape=[jax.ShapeDtypeStruct((t, LANES), jnp.int32),
                   jax.ShapeDtypeStruct((t, LANES), F32),
                   jax.ShapeDtypeStruct((8, LANES), F32)],
        scratch_shapes=[pltpu.VMEM((t, LANES), jnp.int32), pltpu.VMEM((t, LANES), jnp.int32),
                        pltpu.VMEM((t, LANES), F32), pltpu.VMEM((t, LANES), F32)],
        compiler_params=pltpu.CompilerParams(vmem_limit_bytes=VMEM_LIMIT),
        name="routing",
    )(logits)


def _rows_copy(src, dst, sem, src_tok, dst_tok, n):
    first = lambda tok: tok * SUB if isinstance(tok, int) else pl.multiple_of(tok * SUB, SUB)
    s0 = first(src_tok)
    d0 = first(dst_tok)
    return pltpu.make_async_copy(src.at[pl.ds(s0, n * SUB), :], dst.at[pl.ds(d0, n * SUB), :], sem)


def _dispatch_kernel(sl1_ref, sl2_ref, ends_ref, h2_ref, xs_ref, zero_scr, sem):
    step = pl.program_id(0)

    @pl.when(step == 0)
    def _():
        zero_scr[...] = jnp.zeros_like(zero_scr)
        for e in range(N_EXPERTS):
            start = jnp.maximum(ends_ref[e] - TM_E, 0)
            _rows_copy(zero_scr, xs_ref, sem, 0, start, TM_E).start()
        for e in range(N_EXPERTS):
            _rows_copy(zero_scr, xs_ref, sem, 0, 0, TM_E).wait()

    base = step * TB

    def issue(r, carry):
        _rows_copy(h2_ref, xs_ref, sem, r, sl1_ref[base + r], 1).start(priority=0)
        _rows_copy(h2_ref, xs_ref, sem, r, sl2_ref[base + r], 1).start(priority=1)
        return carry

    lax.fori_loop(0, TB, issue, 0, unroll=8)
    _rows_copy(h2_ref, xs_ref, sem, 0, 0, TB).wait()
    _rows_copy(h2_ref, xs_ref, sem, 0, 0, TB).wait()


def _dispatch(h2, sl1, sl2, ends, n_rows):
    t = h2.shape[0] // SUB
    return pl.pallas_call(
        _dispatch_kernel,
        grid_spec=pltpu.PrefetchScalarGridSpec(
            num_scalar_prefetch=3,
            grid=(t // TB,),
            in_specs=[pl.BlockSpec((TB * SUB, LANES), lambda i, *_: (i, 0))],
            out_specs=pl.BlockSpec(memory_space=pl.ANY),
            scratch_shapes=[pltpu.VMEM((TM_E * SUB, LANES), F32), pltpu.SemaphoreType.DMA(())],
        ),
        out_shape=jax.ShapeDtypeStruct((n_rows * SUB, LANES), F32),
        compiler_params=_params(1),
        name="dispatch",
    )(sl1, sl2, ends, h2)


def _expert_kernel(tidx_ref, texp_ref, tfirst_ref, tslot_ref, tnext_ref, nt_ref,
                   xs_ref, wg_hbm, wu_hbm, wd_hbm, ys_ref,
                   wg_buf, wu_buf, wd_buf, wg_bf, wu_bf, wd_bf, sems):
    j = pl.program_id(0)

    def weight_copies(e, slot):
        return [pltpu.make_async_copy(hbm.at[e], buf.at[slot], sems.at[slot, k])
                for k, (hbm, buf) in enumerate(((wg_hbm, wg_buf), (wu_hbm, wu_buf),
                                                (wd_hbm, wd_buf)))]

    @pl.when(j < nt_ref[0])
    def _():
        slot = tslot_ref[j]

        @pl.when(tfirst_ref[j] == 1)
        def _():
            @pl.when(j == 0)
            def _():
                for c in weight_copies(texp_ref[j], slot):
                    c.start()
            for c in weight_copies(texp_ref[j], slot):
                c.wait()

            @pl.when(tnext_ref[j] >= 0)
            def _():
                for c in weight_copies(tnext_ref[j], 1 - slot):
                    c.start()
            wg_bf[...] = wg_buf[slot].astype(BF16)
            wu_bf[...] = wu_buf[slot].astype(BF16)
            wd_bf[...] = wd_buf[slot].astype(BF16)

        xb = _load_token_tiles(xs_ref, TM_E).astype(BF16)
        g = jnp.dot(xb, wg_bf[...], preferred_element_type=F32)
        u = jnp.dot(xb, wu_bf[...], preferred_element_type=F32)
        h = (_silu(g) * u).astype(BF16)
        _store_token_tiles(ys_ref, jnp.dot(h, wd_bf[...], preferred_element_type=F32))


def _expert_mlp(xs, tile_meta, wg, wu, wd):
    n_rows = xs.shape[0] // SUB
    tile = pl.BlockSpec((TM_E * SUB, LANES), lambda j, ti, *_: (ti[j], 0))
    hbm = pl.BlockSpec(memory_space=pl.ANY)
    return pl.pallas_call(
        _expert_kernel,
        grid_spec=pltpu.PrefetchScalarGridSpec(
            num_scalar_prefetch=len(tile_meta),
            grid=(n_rows // TM_E,),
            in_specs=[tile, hbm, hbm, hbm],
            out_specs=tile,
            scratch_shapes=[pltpu.VMEM((2, D_MODEL, D_FF), F32),
                            pltpu.VMEM((2, D_MODEL, D_FF), F32),
                            pltpu.VMEM((2, D_FF, D_MODEL), F32),
                            pltpu.VMEM((D_MODEL, D_FF), BF16),
                            pltpu.VMEM((D_MODEL, D_FF), BF16),
                            pltpu.VMEM((D_FF, D_MODEL), BF16),
                            pltpu.SemaphoreType.DMA((2, 3))],
        ),
        out_shape=jax.ShapeDtypeStruct((n_rows * SUB, LANES), F32),
        compiler_params=_params(1),
        name="expert_mlp",
    )(*tile_meta, xs, wg, wu, wd)


def _combine_kernel(sl1_ref, sl2_ref, x1_ref, cw_ref, ys_ref, o_ref, y1_scr, y2_scr, sem):
    base = pl.program_id(0) * TB

    def issue(r, carry):
        _rows_copy(ys_ref, y1_scr, sem, sl1_ref[base + r], r, 1).start(priority=0)
        _rows_copy(ys_ref, y2_scr, sem, sl2_ref[base + r], r, 1).start(priority=1)
        return carry

    lax.fori_loop(0, TB, issue, 0, unroll=8)
    _rows_copy(ys_ref, y1_scr, sem, 0, 0, TB).wait()
    _rows_copy(ys_ref, y2_scr, sem, 0, 0, TB).wait()
    cw = cw_ref[...]
    o_ref[...] = (x1_ref[...] + cw[:, 0:1] * _load_token_tiles(y1_scr, TB)
                  + cw[:, 1:2] * _load_token_tiles(y2_scr, TB))


def _combine(x1, cw, ys, sl1, sl2):
    t = x1.shape[0]
    return pl.pallas_call(
        _combine_kernel,
        grid_spec=pltpu.PrefetchScalarGridSpec(
            num_scalar_prefetch=2,
            grid=(t // TB,),
            in_specs=[pl.BlockSpec((TB, D_MODEL), lambda i, *_: (i, 0)),
                      pl.BlockSpec((TB, LANES), lambda i, *_: (i, 0)),
                      pl.BlockSpec(memory_space=pl.ANY)],
            out_specs=pl.BlockSpec((TB, D_MODEL), lambda i, *_: (i, 0)),
            scratch_shapes=[pltpu.VMEM((TB * SUB, LANES), F32),
                            pltpu.VMEM((TB * SUB, LANES), F32),
                            pltpu.SemaphoreType.DMA(())],
        ),
        out_shape=jax.ShapeDtypeStruct((t, D_MODEL), F32),
        compiler_params=_params(1),
        name="combine",
    )(sl1, sl2, x1, cw, ys)


def _lane_tile(v, reps):
    return jnp.tile(v.reshape(1, -1), (1, reps))


def kernel(x, positions, norm1_g, w_in, q_norm_g, k_norm_g, lambda_q1, lambda_k1, lambda_q2,
           lambda_k2, diff_out_norm_g, gla_w_gate2, gla_b_gate, gla_out_norm_g, w_out, norm2_g,
           w_router_group, b_router_group, w_router_expert, b_router_expert, w_gate_expert,
           w_up_expert, w_down_expert):
    batch, seq, d = x.shape
    t = batch * seq
    x2 = x.reshape(t, d)

    inv = ROPE_THETA ** (-jnp.arange(0, ROT_DIM, 2, dtype=F32) / ROT_DIM)
    lane_d = jnp.arange(LANES) % DIFF_QK_DIM
    invf = jnp.where(lane_d < ROT_DIM, inv[lane_d % (ROT_DIM // 2)], 0.0).reshape(1, LANES)
    pos_b = jnp.broadcast_to(positions.astype(F32).reshape(t, 1), (t, LANES))
    tabs = _rope_tables(pos_b, invf)

    w_in2 = w_in[0]
    wa = jnp.pad(w_in2[:, D_MAIN:], ((0, 0), (0, LANES - GLA_GATE_RANK)))
    proj, ga = _in_proj(x2, norm1_g, w_in2, wa)

    a_out = _diff_attention(
        proj, tabs, _lane_tile(q_norm_g[0], 2), _lane_tile(k_norm_g[0], 2),
        lambda_q1, lambda_k1, lambda_q2, lambda_k2, diff_out_norm_g, batch, seq)
    w2p = jnp.pad(gla_w_gate2[0], ((0, LANES - GLA_GATE_RANK), (0, 0))).astype(BF16)
    g_out = _gla(proj, ga, w2p, gla_b_gate, gla_out_norm_g, batch, seq)

    n_r = N_GROUPS + N_EXPERTS
    wr = jnp.concatenate(
        [w_router_group[0], w_router_expert[0].transpose(1, 0, 2).reshape(d, N_EXPERTS)], axis=1)
    wr = jnp.pad(wr, ((0, 0), (0, LANES - n_r)))
    br = jnp.pad(jnp.concatenate([b_router_group[0], b_router_expert[0].reshape(-1)]),
                 (0, LANES - n_r)).reshape(1, LANES)
    x1, h2, logits = _out_proj(a_out, g_out, x2, w_out[0].astype(BF16), norm2_g,
                               wr.astype(BF16), br)

    slots, cw, counts = _routing(logits)
    sl1 = slots[:, 0]
    sl2 = slots[:, 1]
    n_rows = 2 * t + N_EXPERTS * TM_E
    max_tiles = n_rows // TM_E
    cnt = counts[0, :N_EXPERTS].astype(jnp.int32)
    tile_end = jnp.cumsum((cnt + (TM_E - 1)) // TM_E)
    n_tiles = tile_end[-1:]
    tile_idx = jnp.minimum(jnp.arange(max_tiles, dtype=jnp.int32), n_tiles[0] - 1)
    tile_exp = jnp.sum(tile_idx[:, None] >= tile_end[None, :], axis=1).astype(jnp.int32)
    row_end = (tile_end * TM_E).astype(jnp.int32)
    tile_first = jnp.concatenate(
        [jnp.ones((1,), jnp.int32), (tile_exp[1:] != tile_exp[:-1]).astype(jnp.int32)])
    tile_slot = (jnp.cumsum(tile_first) - 1) % 2
    nxt = tile_end[tile_exp]
    tile_next = jnp.where(nxt < n_tiles[0], tile_exp[jnp.minimum(nxt, max_tiles - 1)], -1)
    tile_meta = [a.astype(jnp.int32)
                 for a in (tile_idx, tile_exp, tile_first, tile_slot, tile_next, n_tiles)]

    xs = _dispatch(h2, sl1, sl2, row_end, n_rows)
    ys = _expert_mlp(xs, tile_meta,
                     w_gate_expert[0].reshape(N_EXPERTS, d, D_FF),
                     w_up_expert[0].reshape(N_EXPERTS, d, D_FF),
                     w_down_expert[0].reshape(N_EXPERTS, D_FF, d))
    out = _combine(x1, cw, ys, sl1, sl2)
    return out.reshape(batch, seq, d)
```

```python
import functools
import math

import jax
import jax.numpy as jnp
from jax import lax
from jax.experimental import pallas as pl
from jax.experimental.pallas import tpu as pltpu

D_MODEL = 2048
CHUNK = 64
DIFF_QK_DIM = 64
DIFF_V_DIM = 128
DIFF_HEADS = 8
ROT_DIM = 16
ROPE_THETA = 500000.0
GLA_HEADS = 4
GLA_V_DIM = 256
GLA_K_DIM = 128
GLA_GATE_RANK = 16
GLA_TAU = 16.0
N_GROUPS = 4
EXPERTS_PER_GROUP = 8
N_EXPERTS = N_GROUPS * EXPERTS_PER_GROUP
D_FF = 512
RMS_EPS = 1e-6
LAMBDA_INIT = 0.8 - 0.6 * math.exp(-0.3 * 0)
D_MAIN = 6144

LANES = 128
SUB = D_MODEL // 2 // LANES
VMEM_LIMIT = 56 * 1024 * 1024

TM_IN = 1024
TN_IN = 512
TQ = 256
GB = 256
TM_OUT = 512
RB_OUT = 256
TB = 256
TM_E = 256

F32 = jnp.float32
BF16 = jnp.bfloat16
HI = lax.Precision.HIGHEST


def _params(n_axes):
    return pltpu.CompilerParams(dimension_semantics=("arbitrary",) * n_axes,
                                vmem_limit_bytes=VMEM_LIMIT)


def _nt_dot(a, b):
    return lax.dot_general(a, b, (((1,), (1,)), ((), ())), preferred_element_type=F32)


def _tn_dot(a, b):
    return lax.dot_general(a, b, (((0,), (0,)), ((), ())), preferred_element_type=F32)


def _silu(x):
    return x * (1.0 / (1.0 + jnp.exp(-x)))


HI_MASK = 0xFFFF0000


def _bits(x):
    return lax.bitcast_convert_type(x, jnp.uint32)


def _store_token_tiles(ref, val, tok0=0):
    n = val.shape[0]
    half = D_MODEL // 2
    lo = _bits(val[:, 0:half].astype(BF16).astype(F32)) >> 16
    hi = _bits(val[:, half:].astype(BF16).astype(F32)) & jnp.uint32(HI_MASK)
    w = hi | lo
    for s in range(SUB):
        ref[pl.ds(tok0 * SUB + s, n, stride=SUB), :] = w[:, s * LANES:(s + 1) * LANES]


def _load_token_tiles(ref, n, tok0=0):
    w = jnp.concatenate(
        [ref[pl.ds(tok0 * SUB + s, n, stride=SUB), :] for s in range(SUB)], axis=1)
    return (lax.bitcast_convert_type(w << 16, F32),
            lax.bitcast_convert_type(w & jnp.uint32(HI_MASK), F32))


def _rope_table_kernel(pos_ref, invf_ref, c_ref, s1_ref, s2_ref):
    ang = pos_ref[...] * invf_ref[...]
    d = lax.broadcasted_iota(jnp.int32, ang.shape, 1) % DIFF_QK_DIM
    cos = jnp.cos(ang)
    sin = jnp.sin(ang)
    half = ROT_DIM // 2
    c_ref[...] = jnp.where(d < ROT_DIM, cos, 1.0)
    s1_ref[...] = jnp.where(d < half, -sin, 0.0)
    s2_ref[...] = jnp.where((d >= half) & (d < ROT_DIM), sin, 0.0)


def _rope_tables(pos_b, invf):
    t = pos_b.shape[0]
    tb = 1024
    spec = pl.BlockSpec((tb, LANES), lambda i: (i, 0))
    return pl.pallas_call(
        _rope_table_kernel,
        grid=(t // tb,),
        in_specs=[spec, pl.BlockSpec((1, LANES), lambda i: (0, 0))],
        out_specs=[spec, spec, spec],
        out_shape=[jax.ShapeDtypeStruct((t, LANES), F32)] * 3,
        compiler_params=_params(1),
        name="rope_tables",
    )(pos_b, invf)


def _in_proj_kernel(x_ref, g_ref, w_ref, wa_ref, proj_ref, ga_ref, h_scr):
    @pl.when(pl.program_id(1) == 0)
    def _():
        def body(c, carry):
            rows = pl.ds(c * 256, 256)
            x = x_ref[rows, :]
            ms = jnp.mean(x * x, axis=-1, keepdims=True)
            h_scr[rows, :] = (x * lax.rsqrt(ms + RMS_EPS) * g_ref[...]).astype(BF16)
            return carry
        lax.fori_loop(0, TM_IN // 256, body, 0)
        ga_ref[...] = jnp.dot(h_scr[...], wa_ref[...].astype(BF16), preferred_element_type=F32)

    proj_ref[...] = jnp.dot(h_scr[...], w_ref[...].astype(BF16),
                            preferred_element_type=F32).astype(BF16)


def _in_proj(x2, g1, w_in, wa):
    t = x2.shape[0]
    return pl.pallas_call(
        _in_proj_kernel,
        grid=(t // TM_IN, D_MAIN // TN_IN),
        in_specs=[
            pl.BlockSpec((TM_IN, D_MODEL), lambda i, j: (i, 0)),
            pl.BlockSpec((1, D_MODEL), lambda i, j: (0, 0)),
            pl.BlockSpec((D_MODEL, TN_IN), lambda i, j: (0, j)),
            pl.BlockSpec((D_MODEL, LANES), lambda i, j: (0, 0)),
        ],
        out_specs=[
            pl.BlockSpec((TM_IN, TN_IN), lambda i, j: (i, j)),
            pl.BlockSpec((TM_IN, LANES), lambda i, j: (i, 0)),
        ],
        out_shape=[jax.ShapeDtypeStruct((t, D_MAIN), BF16),
                   jax.ShapeDtypeStruct((t, LANES), F32)],
        scratch_shapes=[pltpu.VMEM((TM_IN, D_MODEL), BF16)],
        compiler_params=_params(2),
        name="in_proj",
    )(x2, g1, w_in, wa)


def _norm_rope(x, g, c, s1, s2):
    lo = lax.broadcasted_iota(jnp.int32, x.shape, 1) < DIFF_QK_DIM
    x2 = x * x
    s_lo = jnp.sum(jnp.where(lo, x2, 0.0), axis=-1, keepdims=True)
    s_hi = jnp.sum(jnp.where(lo, 0.0, x2), axis=-1, keepdims=True)
    ms = jnp.where(lo, s_lo, s_hi) * (1.0 / DIFF_QK_DIM)
    y = x * lax.rsqrt(ms + RMS_EPS) * g
    half = ROT_DIM // 2
    return y * c + pltpu.roll(y, LANES - half, 1) * s1 + pltpu.roll(y, half, 1) * s2


def _attn_kernel(q_ref, k_ref, v_ref, c_ref, s1_ref, s2_ref, qg_ref, kg_ref,
                 lq1_ref, lk1_ref, lq2_ref, lk2_ref, og_ref, o_ref,
                 q1_scr, q2_scr, k_scr, v_scr, *bufs):
    s_len = q_ref.shape[0]
    s_bufs, e_bufs = bufs[0:4], bufs[4:8]
    lam = (jnp.exp(jnp.sum(lq1_ref[...] * lk1_ref[...], axis=-1, keepdims=True))
           - jnp.exp(jnp.sum(lq2_ref[...] * lk2_ref[...], axis=-1, keepdims=True))
           + LAMBDA_INIT)
    lo = lax.broadcasted_iota(jnp.int32, (TQ, LANES), 1) < DIFF_QK_DIM
    v_scr[:, 0:LANES] = v_ref[...]
    v_scr[:, LANES:] = jnp.ones((s_len, LANES), BF16)

    def prepare(i):
        rows = slice(i * TQ, (i + 1) * TQ)
        c, s1, s2 = c_ref[rows, :], s1_ref[rows, :], s2_ref[rows, :]
        qn = (_norm_rope(q_ref[rows, :].astype(F32), qg_ref[...], c, s1, s2)
              * (DIFF_QK_DIM ** -0.5 * math.log2(math.e)))
        q1_scr[rows, :] = jnp.where(lo, qn, 0.0).astype(BF16)
        q2_scr[rows, :] = jnp.where(lo, 0.0, qn).astype(BF16)
        k_scr[rows, :] = _norm_rope(k_ref[rows, :].astype(F32), kg_ref[...], c, s1,
                                    s2).astype(BF16)

    diag = (lax.broadcasted_iota(jnp.int32, (TQ, TQ), 1) // CHUNK
            <= lax.broadcasted_iota(jnp.int32, (TQ, TQ), 0) // CHUNK)

    n_tiles = s_len // TQ
    items = [(i, q_scr) for i in range(n_tiles) for q_scr in (q1_scr, q2_scr)]

    def scores(r):
        i, q_scr = items[r]
        s_scr = s_bufs[r % 4]
        q = q_scr[i * TQ:(i + 1) * TQ, :]
        n_off = i * TQ
        s_scr[:, n_off:n_off + TQ] = jnp.where(
            diag, _nt_dot(q, k_scr[n_off:n_off + TQ, :]), -jnp.inf)
        if n_off:
            s_scr[:, 0:n_off] = _nt_dot(q, k_scr[0:n_off, :])

    def softmax_pv(r):
        nk = (items[r][0] + 1) * TQ
        s_scr, e_scr = s_bufs[r % 4], e_bufs[r % 4]
        m = jnp.max(s_scr[:, 0:nk], axis=-1, keepdims=True)
        e_scr[:, 0:nk] = jnp.exp2(s_scr[:, 0:nk] - m).astype(BF16)
        acc = jnp.dot(e_scr[:, 0:nk], v_scr[0:nk, :], preferred_element_type=F32)
        return acc[:, 0:LANES] / acc[:, LANES:]

    prepare(0)
    scores(0)
    scores(1)
    for i in range(n_tiles):
        rows = slice(i * TQ, (i + 1) * TQ)
        if i + 1 < n_tiles:
            prepare(i + 1)
            scores(2 * i + 2)
        o1 = softmax_pv(2 * i)
        if i + 1 < n_tiles:
            scores(2 * i + 3)
        o = o1 - lam * softmax_pv(2 * i + 1)
        ms = jnp.mean(o * o, axis=-1, keepdims=True)
        y = o * lax.rsqrt(ms + RMS_EPS) * og_ref[...] * (1.0 - LAMBDA_INIT)
        o_ref[rows, :] = y.astype(BF16)


def _diff_attention(proj, tabs, qg, kg, lq1, lk1, lq2, lk2, og, batch, seq):
    c, s1, s2 = tabs
    h = DIFF_HEADS
    blk = lambda off: pl.BlockSpec((seq, LANES), lambda b, hh, off=off: (b, off + hh))
    tab = pl.BlockSpec((seq, LANES), lambda b, hh: (b, 0))
    vec = lambda n: pl.BlockSpec((1, n), lambda b, hh: (0, 0))
    return pl.pallas_call(
        _attn_kernel,
        grid=(batch, h),
        in_specs=[blk(0), blk(h), blk(2 * h), tab, tab, tab,
                  vec(LANES), vec(LANES), vec(DIFF_QK_DIM), vec(DIFF_QK_DIM),
                  vec(DIFF_QK_DIM), vec(DIFF_QK_DIM), vec(LANES)],
        out_specs=pl.BlockSpec((seq, LANES), lambda b, hh: (b, hh)),
        out_shape=jax.ShapeDtypeStruct((batch * seq, h * DIFF_V_DIM), BF16),
        scratch_shapes=[pltpu.VMEM((seq, LANES), BF16)] * 3
        + [pltpu.VMEM((seq, 2 * LANES), BF16)]
        + [pltpu.VMEM((TQ, seq), F32)] * 4 + [pltpu.VMEM((TQ, seq), BF16)] * 4,
        compiler_params=_params(2),
        name="diff_attention",
    )(proj, proj, proj, c, s1, s2, qg, kg, lq1, lk1, lq2, lk2, og)


def _split_dot(ones_bf, x):
    hi = x.astype(BF16)
    lo = (x - hi.astype(F32)).astype(BF16)
    return (jnp.dot(ones_bf, hi, preferred_element_type=F32)
            + jnp.dot(ones_bf, lo, preferred_element_type=F32))


def _gla_kernel(q_ref, k_ref, v_ref, r_ref, ga_ref, w2_ref, b2_ref, og_ref, o_ref,
                qin_scr, ut_scr, dec_scr, st_scr, acc_scr):
    s_len = q_ref.shape[0]
    cpg = GB // CHUNK
    ri = lax.broadcasted_iota(jnp.int32, (GB, GB), 0)
    ci = lax.broadcasted_iota(jnp.int32, (GB, GB), 1)
    same = (ri // CHUNK) == (ci // CHUNK)
    blk_ones = jnp.where(same, 1.0, 0.0).astype(BF16)
    tril = same & (ci <= ri)
    tri_ones = jnp.where(tril, 1.0, 0.0).astype(BF16)

    def phase_a(g, carry):
        rows = pl.ds(pl.multiple_of(g * GB, GB), GB)
        pre = jnp.dot(ga_ref[rows, :].astype(BF16), w2_ref[...],
                      preferred_element_type=F32) + b2_ref[...]
        la = -(jnp.maximum(-pre, 0.0) + jnp.log1p(jnp.exp(-jnp.abs(pre)))) * (1.0 / GLA_TAU)
        bc = _split_dot(tri_ones, la)
        bl = _split_dot(blk_ones, la)
        e_neg = jnp.exp(-bc)
        e_last = jnp.exp(bl)
        k = k_ref[rows, :].astype(F32)
        q_in = (q_ref[rows, :].astype(F32) * (GLA_K_DIM ** -0.5) * jnp.exp(bc)).astype(BF16)
        k_in = (k * e_neg).astype(BF16)
        k_dec = (k * (e_last * e_neg)).astype(BF16)
        qin_scr[rows, :] = q_in
        dec_scr[rows, :] = e_last
        v = v_ref[rows, :]
        att = jnp.where(tril, _nt_dot(q_in, k_in), 0.0).astype(BF16)
        acc_scr[rows, :] = jnp.dot(att, v, preferred_element_type=F32)
        for c in range(cpg):
            cr = slice(c * CHUNK, (c + 1) * CHUNK)
            ut_scr[g * cpg + c] = _tn_dot(v[cr, :], k_dec[cr, :])
        return carry

    lax.fori_loop(0, s_len // GB, phase_a, 0)

    def phase_b(c, st):
        st_scr[c] = st.astype(BF16)
        dec = dec_scr[pl.ds(pl.multiple_of(c * CHUNK, CHUNK), 1), :]
        return dec * st + ut_scr[c]

    lax.fori_loop(0, s_len // CHUNK, phase_b, jnp.zeros((GLA_V_DIM, GLA_K_DIM), F32))

    def phase_c(g, carry):
        rows = pl.ds(pl.multiple_of(g * GB, GB), GB)
        inter = [_nt_dot(qin_scr[pl.ds(pl.multiple_of(g * GB + c * CHUNK, CHUNK), CHUNK), :],
                         st_scr[g * cpg + c]) for c in range(cpg)]
        o = acc_scr[rows, :] + jnp.concatenate(inter, axis=0)
        ms = jnp.mean(o * o, axis=-1, keepdims=True)
        y = o * lax.rsqrt(ms + RMS_EPS) * og_ref[...]
        o_ref[rows, :] = (y * _silu(r_ref[rows, :].astype(F32))).astype(BF16)
        return carry

    lax.fori_loop(0, s_len // GB, phase_c, 0)


def _gla(proj, ga, w2p, b2, og, batch, seq):
    hq = 3 * DIFF_HEADS
    kblk = lambda off: pl.BlockSpec((seq, GLA_K_DIM), lambda b, hh, off=off: (b, off + hh))
    vblk = lambda off: pl.BlockSpec((seq, GLA_V_DIM), lambda b, hh, off=off: (b, off + hh))
    return pl.pallas_call(
        _gla_kernel,
        grid=(batch, GLA_HEADS),
        in_specs=[kblk(hq), kblk(hq + GLA_HEADS), vblk(16), vblk(16 + GLA_HEADS),
                  pl.BlockSpec((seq, LANES), lambda b, hh: (b, 0)),
                  pl.BlockSpec((LANES, GLA_K_DIM), lambda b, hh: (0, hh)),
                  pl.BlockSpec((1, GLA_K_DIM), lambda b, hh: (0, hh)),
                  pl.BlockSpec((1, GLA_V_DIM), lambda b, hh: (0, 0))],
        out_specs=pl.BlockSpec((seq, GLA_V_DIM), lambda b, hh: (b, hh)),
        out_shape=jax.ShapeDtypeStruct((batch * seq, GLA_HEADS * GLA_V_DIM), BF16),
        scratch_shapes=[pltpu.VMEM((seq, GLA_K_DIM), BF16),
                        pltpu.VMEM((seq // CHUNK, GLA_V_DIM, GLA_K_DIM), F32),
                        pltpu.VMEM((seq, GLA_K_DIM), F32),
                        pltpu.VMEM((seq // CHUNK, GLA_V_DIM, GLA_K_DIM), BF16),
                        pltpu.VMEM((seq, GLA_V_DIM), F32)],
        compiler_params=_params(2),
        name="gla",
    )(proj, proj, proj, proj, ga, w2p, b2, og)


def _out_proj_kernel(a_ref, g_ref, x_ref, wo_ref, g2_ref, wr_ref, br_ref,
                     x1_ref, h2_ref, lg_ref):
    half = a_ref.shape[1]

    def body(c, carry):
        r0 = pl.multiple_of(c * RB_OUT, RB_OUT)
        rows = pl.ds(r0, RB_OUT)
        mixed = (jnp.dot(a_ref[rows, :], wo_ref[0:half, :], preferred_element_type=F32)
                 + jnp.dot(g_ref[rows, :], wo_ref[half:, :], preferred_element_type=F32))
        x1 = x_ref[rows, :] + mixed
        x1_ref[rows, :] = x1
        ms = jnp.mean(x1 * x1, axis=-1, keepdims=True)
        h2 = x1 * lax.rsqrt(ms + RMS_EPS) * g2_ref[...]
        _store_token_tiles(h2_ref, h2, r0)
        lg_ref[rows, :] = jnp.dot(h2.astype(BF16), wr_ref[...],
                                  preferred_element_type=F32) + br_ref[...]
        return carry

    lax.fori_loop(0, TM_OUT // RB_OUT, body, 0)


def _out_proj(a_out, g_out, x2, wo, g2, wr, br):
    t = x2.shape[0]
    half = a_out.shape[1]
    row = lambda n: pl.BlockSpec((TM_OUT, n), lambda i: (i, 0))
    full = lambda r, n: pl.BlockSpec((r, n), lambda i: (0, 0))
    return pl.pallas_call(
        _out_proj_kernel,
        grid=(t // TM_OUT,),
        in_specs=[row(half), row(half), row(D_MODEL), full(D_MODEL, D_MODEL),
                  full(1, D_MODEL), full(D_MODEL, LANES), full(1, LANES)],
        out_specs=[row(D_MODEL), pl.BlockSpec((TM_OUT * SUB, LANES), lambda i: (i, 0)),
                   row(LANES)],
        out_shape=[jax.ShapeDtypeStruct((t, D_MODEL), F32),
                   jax.ShapeDtypeStruct((t * SUB, LANES), jnp.uint32),
                   jax.ShapeDtypeStruct((t, LANES), F32)],
        compiler_params=_params(1),
        name="out_proj_router",
    )(a_out, g_out, x2, wo, g2, wr, br)


def _first_argmax(vals, lane_f, valid):
    v = jnp.where(valid, vals, -jnp.inf)
    m = jnp.max(v, axis=-1, keepdims=True)
    idx = jnp.min(jnp.where(valid & (v == m), lane_f, float(LANES)), axis=-1, keepdims=True)
    return m, idx


def _routing_kernel(lg_ref, slots_ref, cw_ref, cnt_ref, e1_scr, e2_scr, r1_scr, r2_scr):
    t = lg_ref.shape[0]
    lane = lax.broadcasted_iota(jnp.int32, (TB, LANES), 1)
    lane_f = lane.astype(F32)
    ri = lax.broadcasted_iota(jnp.int32, (TB, TB), 0)
    ci = lax.broadcasted_iota(jnp.int32, (TB, TB), 1)
    strict_lower = jnp.where(ci < ri, 1.0, 0.0).astype(BF16)

    def softmax_in(lg, valid):
        m = jnp.max(jnp.where(valid, lg, -jnp.inf), axis=-1, keepdims=True)
        ex = jnp.where(valid, jnp.exp(lg - m), 0.0)
        return ex / jnp.sum(ex, axis=-1, keepdims=True)

    def phase1(b, carry):
        rows = pl.ds(pl.multiple_of(b * TB, TB), TB)
        lg = lg_ref[rows, :]
        is_g = lane < N_GROUPS
        pg_sel, g_sel = _first_argmax(softmax_in(lg, is_g), lane_f, is_g)
        e_lo = N_GROUPS + g_sel * EXPERTS_PER_GROUP
        in_grp = (lane_f >= e_lo) & (lane_f < e_lo + EXPERTS_PER_GROUP)
        pe = softmax_in(lg, in_grp)
        v1, l1 = _first_argmax(pe, lane_f, in_grp)
        v2, l2 = _first_argmax(pe, lane_f, in_grp & (lane_f != l1))
        tot = v1 + v2
        c1 = (v1 / tot) * pg_sel
        c2 = (v2 / tot) * pg_sel
        e1 = (l1 - N_GROUPS).astype(jnp.int32)
        e2 = (l2 - N_GROUPS).astype(jnp.int32)
        oh1 = lane == e1
        oh2 = lane == e2
        a = jnp.where(oh1 | oh2, 1.0, 0.0)
        rank = jnp.dot(strict_lower, a.astype(BF16), preferred_element_type=F32) + carry
        e1_scr[rows, :] = jnp.broadcast_to(e1, (TB, LANES))
        e2_scr[rows, :] = jnp.broadcast_to(e2, (TB, LANES))
        r1_scr[rows, :] = jnp.broadcast_to(
            jnp.sum(jnp.where(oh1, rank, 0.0), axis=-1, keepdims=True), (TB, LANES))
        r2_scr[rows, :] = jnp.broadcast_to(
            jnp.sum(jnp.where(oh2, rank, 0.0), axis=-1, keepdims=True), (TB, LANES))
        cw_ref[rows, :] = jnp.where(lane == 0, c1, jnp.where(lane == 1, c2, 0.0))
        return carry + jnp.sum(a, axis=0, keepdims=True)

    counts = lax.fori_loop(0, t // TB, phase1, jnp.zeros((1, LANES), F32))
    cnt_ref[...] = jnp.broadcast_to(counts, cnt_ref.shape)
    n_tiles = jnp.floor((counts + (TM_E - 1)) * (1.0 / TM_E))
    ui = lax.broadcasted_iota(jnp.int32, (LANES, LANES), 0)
    uj = lax.broadcasted_iota(jnp.int32, (LANES, LANES), 1)
    strict_upper = jnp.where(ui < uj, 1.0, 0.0).astype(BF16)
    tile_off = jnp.dot(jnp.broadcast_to(n_tiles, (8, LANES)).astype(BF16), strict_upper,
                       preferred_element_type=F32)[0:1, :]
    row_off = tile_off * TM_E

    def phase2(b, carry):
        rows = pl.ds(pl.multiple_of(b * TB, TB), TB)
        off1 = jnp.sum(jnp.where(lane == e1_scr[rows, :], row_off, 0.0), axis=-1, keepdims=True)
        off2 = jnp.sum(jnp.where(lane == e2_scr[rows, :], row_off, 0.0), axis=-1, keepdims=True)
        sl1 = (off1 + r1_scr[rows, 0:1]).astype(jnp.int32)
        sl2 = (off2 + r2_scr[rows, 0:1]).astype(jnp.int32)
        slots_ref[rows, :] = jnp.where(lane == 0, sl1, jnp.where(lane == 1, sl2, 0))
        return carry

    lax.fori_loop(0, t // TB, phase2, 0)


def _routing(logits):
    t = logits.shape[0]
    return pl.pallas_call(
        _routing_kernel,
        out_shape=[jax.ShapeDtypeStruct((t, LANES), jnp.int32),
                   jax.ShapeDtypeStruct((t, LANES), F32),
                   jax.ShapeDtypeStruct((8, LANES), F32)],
        scratch_shapes=[pltpu.VMEM((t, LANES), jnp.int32), pltpu.VMEM((t, LANES), jnp.int32),
                        pltpu.VMEM((t, LANES), F32), pltpu.VMEM((t, LANES), F32)],
        compiler_params=pltpu.CompilerParams(vmem_limit_bytes=VMEM_LIMIT),
        name="routing",
    )(logits)


def _rows_copy(src, dst, sem, src_tok, dst_tok, n):
    first = lambda tok: tok * SUB if isinstance(tok, int) else pl.multiple_of(tok * SUB, SUB)
    s0 = first(src_tok)
    d0 = first(dst_tok)
    return pltpu.make_async_copy(src.at[pl.ds(s0, n * SUB), :], dst.at[pl.ds(d0, n * SUB), :], sem)


def _dispatch_kernel(sl1_ref, sl2_ref, ends_ref, h2_hbm, zero_hbm, xs_hbm, sem):
    t = h2_hbm.shape[0] // SUB

    for e in range(N_EXPERTS):
        start = jnp.maximum(ends_ref[e] - TM_E, 0)
        _rows_copy(zero_hbm, xs_hbm, sem, 0, start, TM_E).start()
    for e in range(N_EXPERTS):
        _rows_copy(zero_hbm, xs_hbm, sem, 0, 0, TM_E).wait()

    def wait_group():
        _rows_copy(h2_hbm, xs_hbm, sem, 0, 0, TB).wait()
        _rows_copy(h2_hbm, xs_hbm, sem, 0, 0, TB).wait()

    def group(g, carry):
        base = g * TB

        def issue(r, c):
            _rows_copy(h2_hbm, xs_hbm, sem, base + r, sl1_ref[base + r], 1).start()
            _rows_copy(h2_hbm, xs_hbm, sem, base + r, sl2_ref[base + r], 1).start()
            return c

        lax.fori_loop(0, TB, issue, 0, unroll=8)

        @pl.when(g > 0)
        def _():
            wait_group()
        return carry

    lax.fori_loop(0, t // TB, group, 0)
    wait_group()


def _dispatch(h2, sl1, sl2, ends, n_rows):
    hbm = pl.BlockSpec(memory_space=pl.ANY)
    return pl.pallas_call(
        _dispatch_kernel,
        grid_spec=pltpu.PrefetchScalarGridSpec(
            num_scalar_prefetch=3,
            grid=(1,),
            in_specs=[hbm, hbm],
            out_specs=hbm,
            scratch_shapes=[pltpu.SemaphoreType.DMA(())],
        ),
        out_shape=jax.ShapeDtypeStruct((n_rows * SUB, LANES), jnp.uint32),
        compiler_params=_params(1),
        name="dispatch",
    )(sl1, sl2, ends, h2, jnp.zeros((TM_E * SUB, LANES), jnp.uint32))


def _expert_kernel(tidx_ref, texp_ref, tfirst_ref, tslot_ref, tnext_ref, nt_ref,
                   xs_ref, wg_hbm, wu_hbm, wd_hbm, ys_ref,
                   wg_buf, wu_buf, wd_buf, wg_bf, wu_bf, wd_bf, sems):
    j = pl.program_id(0)

    def weight_copies(e, slot):
        return [pltpu.make_async_copy(hbm.at[e], buf.at[slot], sems.at[slot, k])
                for k, (hbm, buf) in enumerate(((wg_hbm, wg_buf), (wu_hbm, wu_buf),
                                                (wd_hbm, wd_buf)))]

    @pl.when(j < nt_ref[0])
    def _():
        slot = tslot_ref[j]

        @pl.when(tfirst_ref[j] == 1)
        def _():
            @pl.when(j == 0)
            def _():
                for c in weight_copies(texp_ref[j], slot):
                    c.start()
            for c in weight_copies(texp_ref[j], slot):
                c.wait()

            @pl.when(tnext_ref[j] >= 0)
            def _():
                for c in weight_copies(tnext_ref[j], 1 - slot):
                    c.start()
            wg_bf[...] = wg_buf[slot].astype(BF16)
            wu_bf[...] = wu_buf[slot].astype(BF16)
            wd_bf[...] = wd_buf[slot].astype(BF16)

        half = D_MODEL // 2
        x_lo, x_hi = (v.astype(BF16) for v in _load_token_tiles(xs_ref, TM_E))
        g = (jnp.dot(x_lo, wg_bf[0:half, :], preferred_element_type=F32)
             + jnp.dot(x_hi, wg_bf[half:, :], preferred_element_type=F32))
        u = (jnp.dot(x_lo, wu_bf[0:half, :], preferred_element_type=F32)
             + jnp.dot(x_hi, wu_bf[half:, :], preferred_element_type=F32))
        h = (_silu(g) * u).astype(BF16)
        _store_token_tiles(ys_ref, jnp.dot(h, wd_bf[...], preferred_element_type=F32))


def _expert_mlp(xs, tile_meta, wg, wu, wd):
    n_rows = xs.shape[0] // SUB
    tile = pl.BlockSpec((TM_E * SUB, LANES), lambda j, ti, *_: (ti[j], 0))
    hbm = pl.BlockSpec(memory_space=pl.ANY)
    return pl.pallas_call(
        _expert_kernel,
        grid_spec=pltpu.PrefetchScalarGridSpec(
            num_scalar_prefetch=len(tile_meta),
            grid=(n_rows // TM_E,),
            in_specs=[tile, hbm, hbm, hbm],
            out_specs=tile,
            scratch_shapes=[pltpu.VMEM((2, D_MODEL, D_FF), F32),
                            pltpu.VMEM((2, D_MODEL, D_FF), F32),
                            pltpu.VMEM((2, D_FF, D_MODEL), F32),
                            pltpu.VMEM((D_MODEL, D_FF), BF16),
                            pltpu.VMEM((D_MODEL, D_FF), BF16),
                            pltpu.VMEM((D_FF, D_MODEL), BF16),
                            pltpu.SemaphoreType.DMA((2, 3))],
        ),
        out_shape=jax.ShapeDtypeStruct((n_rows * SUB, LANES), jnp.uint32),
        compiler_params=_params(1),
        name="expert_mlp",
    )(*tile_meta, xs, wg, wu, wd)


def _combine_kernel(sl1_ref, sl2_ref, x1_ref, cw_ref, ys_hbm, o_ref, y1_scr, y2_scr, sems):
    i = pl.program_id(0)

    def issue(step, slot):
        base = step * TB

        def body(r, carry):
            _rows_copy(ys_hbm, y1_scr.at[slot], sems.at[slot], sl1_ref[base + r], r,
                       1).start(priority=0)
            _rows_copy(ys_hbm, y2_scr.at[slot], sems.at[slot], sl2_ref[base + r], r,
                       1).start(priority=1)
            return carry

        lax.fori_loop(0, TB, body, 0, unroll=8)

    @pl.when(i == 0)
    def _():
        issue(0, 0)

    @pl.when(i + 1 < pl.num_programs(0))
    def _():
        issue(i + 1, (i + 1) % 2)

    slot = i % 2
    _rows_copy(ys_hbm, y1_scr.at[slot], sems.at[slot], 0, 0, TB).wait()
    _rows_copy(ys_hbm, y2_scr.at[slot], sems.at[slot], 0, 0, TB).wait()
    cw = cw_ref[...]
    c1, c2 = cw[:, 0:1], cw[:, 1:2]
    half = D_MODEL // 2
    lo1, hi1 = _load_token_tiles(y1_scr.at[slot], TB)
    lo2, hi2 = _load_token_tiles(y2_scr.at[slot], TB)
    o_ref[:, 0:half] = x1_ref[:, 0:half] + c1 * lo1 + c2 * lo2
    o_ref[:, half:] = x1_ref[:, half:] + c1 * hi1 + c2 * hi2


def _combine(x1, cw, ys, sl1, sl2):
    t = x1.shape[0]
    return pl.pallas_call(
        _combine_kernel,
        grid_spec=pltpu.PrefetchScalarGridSpec(
            num_scalar_prefetch=2,
            grid=(t // TB,),
            in_specs=[pl.BlockSpec((TB, D_MODEL), lambda i, *_: (i, 0)),
                      pl.BlockSpec((TB, LANES), lambda i, *_: (i, 0)),
                      pl.BlockSpec(memory_space=pl.ANY)],
            out_specs=pl.BlockSpec((TB, D_MODEL), lambda i, *_: (i, 0)),
            scratch_shapes=[pltpu.VMEM((2, TB * SUB, LANES), jnp.uint32),
                            pltpu.VMEM((2, TB * SUB, LANES), jnp.uint32),
                            pltpu.SemaphoreType.DMA((2,))],
        ),
        out_shape=jax.ShapeDtypeStruct((t, D_MODEL), F32),
        compiler_params=_params(1),
        name="combine",
    )(sl1, sl2, x1, cw, ys)


def _lane_tile(v, reps):
    return jnp.tile(v.reshape(1, -1), (1, reps))


def kernel(x, positions, norm1_g, w_in, q_norm_g, k_norm_g, lambda_q1, lambda_k1, lambda_q2,
           lambda_k2, diff_out_norm_g, gla_w_gate2, gla_b_gate, gla_out_norm_g, w_out, norm2_g,
           w_router_group, b_router_group, w_router_expert, b_router_expert, w_gate_expert,
           w_up_expert, w_down_expert):
    batch, seq, d = x.shape
    t = batch * seq
    x2 = x.reshape(t, d)

    inv = ROPE_THETA ** (-jnp.arange(0, ROT_DIM, 2, dtype=F32) / ROT_DIM)
    lane_d = jnp.arange(LANES) % DIFF_QK_DIM
    invf = jnp.where(lane_d < ROT_DIM, inv[lane_d % (ROT_DIM // 2)], 0.0).reshape(1, LANES)
    pos_b = jnp.broadcast_to(positions.astype(F32).reshape(t, 1), (t, LANES))
    tabs = _rope_tables(pos_b, invf)

    w_in2 = w_in[0]
    wa = jnp.pad(w_in2[:, D_MAIN:], ((0, 0), (0, LANES - GLA_GATE_RANK)))
    proj, ga = _in_proj(x2, norm1_g, w_in2, wa)

    a_out = _diff_attention(
        proj, tabs, _lane_tile(q_norm_g[0], 2), _lane_tile(k_norm_g[0], 2),
        lambda_q1, lambda_k1, lambda_q2, lambda_k2, diff_out_norm_g, batch, seq)
    w2p = jnp.pad(gla_w_gate2[0], ((0, LANES - GLA_GATE_RANK), (0, 0))).astype(BF16)
    g_out = _gla(proj, ga, w2p, gla_b_gate, gla_out_norm_g, batch, seq)

    n_r = N_GROUPS + N_EXPERTS
    wr = jnp.concatenate(
        [w_router_group[0], w_router_expert[0].transpose(1, 0, 2).reshape(d, N_EXPERTS)], axis=1)
    wr = jnp.pad(wr, ((0, 0), (0, LANES - n_r)))
    br = jnp.pad(jnp.concatenate([b_router_group[0], b_router_expert[0].reshape(-1)]),
                 (0, LANES - n_r)).reshape(1, LANES)
    x1, h2, logits = _out_proj(a_out, g_out, x2, w_out[0].astype(BF16), norm2_g,
                               wr.astype(BF16), br)

    slots, cw, counts = _routing(logits)
    sl1 = slots[:, 0]
    sl2 = slots[:, 1]
    n_rows = 2 * t + N_EXPERTS * TM_E
    max_tiles = n_rows // TM_E
    cnt = counts[0, :N_EXPERTS].astype(jnp.int32)
    tile_end = jnp.cumsum((cnt + (TM_E - 1)) // TM_E)
    n_tiles = tile_end[-1:]
    tile_idx = jnp.minimum(jnp.arange(max_tiles, dtype=jnp.int32), n_tiles[0] - 1)
    tile_exp = jnp.sum(tile_idx[:, None] >= tile_end[None, :], axis=1).astype(jnp.int32)
    row_end = (tile_end * TM_E).astype(jnp.int32)
    tile_first = jnp.concatenate(
        [jnp.ones((1,), jnp.int32), (tile_exp[1:] != tile_exp[:-1]).astype(jnp.int32)])
    tile_slot = (jnp.cumsum(tile_first) - 1) % 2
    nxt = tile_end[tile_exp]
    tile_next = jnp.where(nxt < n_tiles[0], tile_exp[jnp.minimum(nxt, max_tiles - 1)], -1)
    tile_meta = [a.astype(jnp.int32)
                 for a in (tile_idx, tile_exp, tile_first, tile_slot, tile_next, n_tiles)]

    xs = _dispatch(h2, sl1, sl2, row_end, n_rows)
    ys = _expert_mlp(xs, tile_meta,
                     w_gate_expert[0].reshape(N_EXPERTS, d, D_FF),
                     w_up_expert[0].reshape(N_EXPERTS, d, D_FF),
                     w_down_expert[0].reshape(N_EXPERTS, D_FF, d))
    out = _combine(x1, cw, ys, sl1, sl2)
    return out.reshape(batch, seq, d)
```

```python
import functools
import math

import jax
import jax.numpy as jnp
from jax import lax
from jax.experimental import pallas as pl
from jax.experimental.pallas import tpu as pltpu

D_MODEL = 2048
CHUNK = 64
DIFF_QK_DIM = 64
DIFF_V_DIM = 128
DIFF_HEADS = 8
ROT_DIM = 16
ROPE_THETA = 500000.0
GLA_HEADS = 4
GLA_V_DIM = 256
GLA_K_DIM = 128
GLA_GATE_RANK = 16
GLA_TAU = 16.0
N_GROUPS = 4
EXPERTS_PER_GROUP = 8
N_EXPERTS = N_GROUPS * EXPERTS_PER_GROUP
D_FF = 512
RMS_EPS = 1e-6
LAMBDA_INIT = 0.8 - 0.6 * math.exp(-0.3 * 0)
D_MAIN = 6144

LANES = 128
SUB = D_MODEL // LANES
VMEM_LIMIT = 56 * 1024 * 1024

TM_IN = 1024
TN_IN = 512
TQ = 256
GB = 256
TM_OUT = 512
RB_OUT = 256
TB = 256
TB_D = 1024
TM_E = 256

F32 = jnp.float32
BF16 = jnp.bfloat16
HI = lax.Precision.HIGHEST


def _params(n_axes):
    return pltpu.CompilerParams(dimension_semantics=("arbitrary",) * n_axes,
                                vmem_limit_bytes=VMEM_LIMIT)


def _nt_dot(a, b):
    return lax.dot_general(a, b, (((1,), (1,)), ((), ())), preferred_element_type=F32)


def _tn_dot(a, b):
    return lax.dot_general(a, b, (((0,), (0,)), ((), ())), preferred_element_type=F32)


def _silu(x):
    return x * (1.0 / (1.0 + jnp.exp(-x)))


def _store_token_tiles(ref, val, scr, tok0=0):
    n = val.shape[0]
    for s in range(SUB):
        scr[pl.ds(s, n, stride=SUB), :] = val[:, s * LANES:(s + 1) * LANES]
    ref[pl.ds(tok0 * SUB, n * SUB), :] = scr[...].astype(BF16)


def _load_token_tiles(ref, scr, n, tok0=0):
    scr[...] = ref[pl.ds(tok0 * SUB, n * SUB), :].astype(F32)
    return jnp.concatenate([scr[pl.ds(s, n, stride=SUB), :] for s in range(SUB)], axis=1)


def _rope_table_kernel(pos_ref, invf_ref, c_ref, s1_ref, s2_ref):
    ang = pos_ref[...] * invf_ref[...]
    d = lax.broadcasted_iota(jnp.int32, ang.shape, 1) % DIFF_QK_DIM
    cos = jnp.cos(ang)
    sin = jnp.sin(ang)
    half = ROT_DIM // 2
    c_ref[...] = jnp.where(d < ROT_DIM, cos, 1.0)
    s1_ref[...] = jnp.where(d < half, -sin, 0.0)
    s2_ref[...] = jnp.where((d >= half) & (d < ROT_DIM), sin, 0.0)


def _rope_tables(pos_b, invf):
    t = pos_b.shape[0]
    tb = 1024
    spec = pl.BlockSpec((tb, LANES), lambda i: (i, 0))
    return pl.pallas_call(
        _rope_table_kernel,
        grid=(t // tb,),
        in_specs=[spec, pl.BlockSpec((1, LANES), lambda i: (0, 0))],
        out_specs=[spec, spec, spec],
        out_shape=[jax.ShapeDtypeStruct((t, LANES), F32)] * 3,
        compiler_params=_params(1),
        name="rope_tables",
    )(pos_b, invf)


def _in_proj_kernel(x_ref, g_ref, w_ref, wa_ref, proj_ref, ga_ref, h_scr):
    @pl.when(pl.program_id(1) == 0)
    def _():
        def body(c, carry):
            rows = pl.ds(c * 256, 256)
            x = x_ref[rows, :]
            ms = jnp.mean(x * x, axis=-1, keepdims=True)
            h_scr[rows, :] = (x * lax.rsqrt(ms + RMS_EPS) * g_ref[...]).astype(BF16)
            return carry
        lax.fori_loop(0, TM_IN // 256, body, 0)
        ga_ref[...] = jnp.dot(h_scr[...], wa_ref[...].astype(BF16), preferred_element_type=F32)

    proj_ref[...] = jnp.dot(h_scr[...], w_ref[...].astype(BF16),
                            preferred_element_type=F32).astype(BF16)


def _in_proj(x2, g1, w_in, wa):
    t = x2.shape[0]
    return pl.pallas_call(
        _in_proj_kernel,
        grid=(t // TM_IN, D_MAIN // TN_IN),
        in_specs=[
            pl.BlockSpec((TM_IN, D_MODEL), lambda i, j: (i, 0)),
            pl.BlockSpec((1, D_MODEL), lambda i, j: (0, 0)),
            pl.BlockSpec((D_MODEL, TN_IN), lambda i, j: (0, j)),
            pl.BlockSpec((D_MODEL, LANES), lambda i, j: (0, 0)),
        ],
        out_specs=[
            pl.BlockSpec((TM_IN, TN_IN), lambda i, j: (i, j)),
            pl.BlockSpec((TM_IN, LANES), lambda i, j: (i, 0)),
        ],
        out_shape=[jax.ShapeDtypeStruct((t, D_MAIN), BF16),
                   jax.ShapeDtypeStruct((t, LANES), F32)],
        scratch_shapes=[pltpu.VMEM((TM_IN, D_MODEL), BF16)],
        compiler_params=_params(2),
        name="in_proj",
    )(x2, g1, w_in, wa)


def _norm_rope(x, g, c, s1, s2):
    lo = lax.broadcasted_iota(jnp.int32, x.shape, 1) < DIFF_QK_DIM
    x2 = x * x
    s_lo = jnp.sum(jnp.where(lo, x2, 0.0), axis=-1, keepdims=True)
    s_hi = jnp.sum(jnp.where(lo, 0.0, x2), axis=-1, keepdims=True)
    ms = jnp.where(lo, s_lo, s_hi) * (1.0 / DIFF_QK_DIM)
    y = x * lax.rsqrt(ms + RMS_EPS) * g
    half = ROT_DIM // 2
    return y * c + pltpu.roll(y, LANES - half, 1) * s1 + pltpu.roll(y, half, 1) * s2


def _attn_kernel(q_ref, k_ref, v_ref, c_ref, s1_ref, s2_ref, qg_ref, kg_ref,
                 lq1_ref, lk1_ref, lq2_ref, lk2_ref, og_ref, o_ref,
                 q1_scr, q2_scr, k_scr, v_scr, *bufs):
    s_len = q_ref.shape[0]
    s_bufs, e_bufs = bufs[0:4], bufs[4:8]
    lam = (jnp.exp(jnp.sum(lq1_ref[...] * lk1_ref[...], axis=-1, keepdims=True))
           - jnp.exp(jnp.sum(lq2_ref[...] * lk2_ref[...], axis=-1, keepdims=True))
           + LAMBDA_INIT)
    lo = lax.broadcasted_iota(jnp.int32, (TQ, LANES), 1) < DIFF_QK_DIM
    v_scr[:, 0:LANES] = v_ref[...]
    v_scr[:, LANES:] = jnp.ones((s_len, LANES), BF16)

    def prepare(i):
        rows = slice(i * TQ, (i + 1) * TQ)
        c, s1, s2 = c_ref[rows, :], s1_ref[rows, :], s2_ref[rows, :]
        qn = (_norm_rope(q_ref[rows, :].astype(F32), qg_ref[...], c, s1, s2)
              * (DIFF_QK_DIM ** -0.5 * math.log2(math.e)))
        q1_scr[rows, :] = jnp.where(lo, qn, 0.0).astype(BF16)
        q2_scr[rows, :] = jnp.where(lo, 0.0, qn).astype(BF16)
        k_scr[rows, :] = _norm_rope(k_ref[rows, :].astype(F32), kg_ref[...], c, s1,
                                    s2).astype(BF16)

    diag = (lax.broadcasted_iota(jnp.int32, (TQ, TQ), 1) // CHUNK
            <= lax.broadcasted_iota(jnp.int32, (TQ, TQ), 0) // CHUNK)

    n_tiles = s_len // TQ
    items = [(i, q_scr) for i in range(n_tiles) for q_scr in (q1_scr, q2_scr)]

    def scores(r):
        i, q_scr = items[r]
        s_scr = s_bufs[r % 4]
        q = q_scr[i * TQ:(i + 1) * TQ, :]
        n_off = i * TQ
        s_scr[:, n_off:n_off + TQ] = jnp.where(
            diag, _nt_dot(q, k_scr[n_off:n_off + TQ, :]), -jnp.inf)
        if n_off:
            s_scr[:, 0:n_off] = _nt_dot(q, k_scr[0:n_off, :])

    def softmax_pv(r):
        nk = (items[r][0] + 1) * TQ
        s_scr, e_scr = s_bufs[r % 4], e_bufs[r % 4]
        m = jnp.max(s_scr[:, 0:nk], axis=-1, keepdims=True)
        e_scr[:, 0:nk] = jnp.exp2(s_scr[:, 0:nk] - m).astype(BF16)
        acc = jnp.dot(e_scr[:, 0:nk], v_scr[0:nk, :], preferred_element_type=F32)
        return acc[:, 0:LANES] / acc[:, LANES:]

    prepare(0)
    scores(0)
    scores(1)
    for i in range(n_tiles):
        rows = slice(i * TQ, (i + 1) * TQ)
        if i + 1 < n_tiles:
            prepare(i + 1)
            scores(2 * i + 2)
        o1 = softmax_pv(2 * i)
        if i + 1 < n_tiles:
            scores(2 * i + 3)
        o = o1 - lam * softmax_pv(2 * i + 1)
        ms = jnp.mean(o * o, axis=-1, keepdims=True)
        y = o * lax.rsqrt(ms + RMS_EPS) * og_ref[...] * (1.0 - LAMBDA_INIT)
        o_ref[rows, :] = y.astype(BF16)


def _diff_attention(proj, tabs, qg, kg, lq1, lk1, lq2, lk2, og, batch, seq):
    c, s1, s2 = tabs
    h = DIFF_HEADS
    blk = lambda off: pl.BlockSpec((seq, LANES), lambda b, hh, off=off: (b, off + hh))
    tab = pl.BlockSpec((seq, LANES), lambda b, hh: (b, 0))
    vec = lambda n: pl.BlockSpec((1, n), lambda b, hh: (0, 0))
    return pl.pallas_call(
        _attn_kernel,
        grid=(batch, h),
        in_specs=[blk(0), blk(h), blk(2 * h), tab, tab, tab,
                  vec(LANES), vec(LANES), vec(DIFF_QK_DIM), vec(DIFF_QK_DIM),
                  vec(DIFF_QK_DIM), vec(DIFF_QK_DIM), vec(LANES)],
        out_specs=pl.BlockSpec((seq, LANES), lambda b, hh: (b, hh)),
        out_shape=jax.ShapeDtypeStruct((batch * seq, h * DIFF_V_DIM), BF16),
        scratch_shapes=[pltpu.VMEM((seq, LANES), BF16)] * 3
        + [pltpu.VMEM((seq, 2 * LANES), BF16)]
        + [pltpu.VMEM((TQ, seq), F32)] * 4 + [pltpu.VMEM((TQ, seq), BF16)] * 4,
        compiler_params=_params(2),
        name="diff_attention",
    )(proj, proj, proj, c, s1, s2, qg, kg, lq1, lk1, lq2, lk2, og)


def _split_dot(ones_bf, x):
    hi = x.astype(BF16)
    lo = (x - hi.astype(F32)).astype(BF16)
    return (jnp.dot(ones_bf, hi, preferred_element_type=F32)
            + jnp.dot(ones_bf, lo, preferred_element_type=F32))


def _gla_kernel(q_ref, k_ref, v_ref, r_ref, ga_ref, w2_ref, b2_ref, og_ref, o_ref,
                qin_scr, ut_scr, dec_scr, st_scr, acc_scr):
    s_len = q_ref.shape[0]
    cpg = GB // CHUNK
    ri = lax.broadcasted_iota(jnp.int32, (GB, GB), 0)
    ci = lax.broadcasted_iota(jnp.int32, (GB, GB), 1)
    same = (ri // CHUNK) == (ci // CHUNK)
    blk_ones = jnp.where(same, 1.0, 0.0).astype(BF16)
    tril = same & (ci <= ri)
    tri_ones = jnp.where(tril, 1.0, 0.0).astype(BF16)

    def phase_a(g, carry):
        rows = pl.ds(pl.multiple_of(g * GB, GB), GB)
        pre = jnp.dot(ga_ref[rows, :].astype(BF16), w2_ref[...],
                      preferred_element_type=F32) + b2_ref[...]
        la = -(jnp.maximum(-pre, 0.0) + jnp.log1p(jnp.exp(-jnp.abs(pre)))) * (1.0 / GLA_TAU)
        bc = _split_dot(tri_ones, la)
        bl = _split_dot(blk_ones, la)
        e_neg = jnp.exp(-bc)
        e_last = jnp.exp(bl)
        k = k_ref[rows, :].astype(F32)
        q_in = (q_ref[rows, :].astype(F32) * (GLA_K_DIM ** -0.5) * jnp.exp(bc)).astype(BF16)
        k_in = (k * e_neg).astype(BF16)
        k_dec = (k * (e_last * e_neg)).astype(BF16)
        qin_scr[rows, :] = q_in
        dec_scr[rows, :] = e_last
        v = v_ref[rows, :]
        att = jnp.where(tril, _nt_dot(q_in, k_in), 0.0).astype(BF16)
        acc_scr[rows, :] = jnp.dot(att, v, preferred_element_type=F32)
        for c in range(cpg):
            cr = slice(c * CHUNK, (c + 1) * CHUNK)
            ut_scr[g * cpg + c] = _tn_dot(v[cr, :], k_dec[cr, :])
        return carry

    lax.fori_loop(0, s_len // GB, phase_a, 0)

    def phase_b(c, st):
        st_scr[c] = st.astype(BF16)
        dec = dec_scr[pl.ds(pl.multiple_of(c * CHUNK, CHUNK), 1), :]
        return dec * st + ut_scr[c]

    lax.fori_loop(0, s_len // CHUNK, phase_b, jnp.zeros((GLA_V_DIM, GLA_K_DIM), F32))

    def phase_c(g, carry):
        rows = pl.ds(pl.multiple_of(g * GB, GB), GB)
        inter = [_nt_dot(qin_scr[pl.ds(pl.multiple_of(g * GB + c * CHUNK, CHUNK), CHUNK), :],
                         st_scr[g * cpg + c]) for c in range(cpg)]
        o = acc_scr[rows, :] + jnp.concatenate(inter, axis=0)
        ms = jnp.mean(o * o, axis=-1, keepdims=True)
        y = o * lax.rsqrt(ms + RMS_EPS) * og_ref[...]
        o_ref[rows, :] = (y * _silu(r_ref[rows, :].astype(F32))).astype(BF16)
        return carry

    lax.fori_loop(0, s_len // GB, phase_c, 0)


def _gla(proj, ga, w2p, b2, og, batch, seq):
    hq = 3 * DIFF_HEADS
    kblk = lambda off: pl.BlockSpec((seq, GLA_K_DIM), lambda b, hh, off=off: (b, off + hh))
    vblk = lambda off: pl.BlockSpec((seq, GLA_V_DIM), lambda b, hh, off=off: (b, off + hh))
    return pl.pallas_call(
        _gla_kernel,
        grid=(batch, GLA_HEADS),
        in_specs=[kblk(hq), kblk(hq + GLA_HEADS), vblk(16), vblk(16 + GLA_HEADS),
                  pl.BlockSpec((seq, LANES), lambda b, hh: (b, 0)),
                  pl.BlockSpec((LANES, GLA_K_DIM), lambda b, hh: (0, hh)),
                  pl.BlockSpec((1, GLA_K_DIM), lambda b, hh: (0, hh)),
                  pl.BlockSpec((1, GLA_V_DIM), lambda b, hh: (0, 0))],
        out_specs=pl.BlockSpec((seq, GLA_V_DIM), lambda b, hh: (b, hh)),
        out_shape=jax.ShapeDtypeStruct((batch * seq, GLA_HEADS * GLA_V_DIM), BF16),
        scratch_shapes=[pltpu.VMEM((seq, GLA_K_DIM), BF16),
                        pltpu.VMEM((seq // CHUNK, GLA_V_DIM, GLA_K_DIM), F32),
                        pltpu.VMEM((seq, GLA_K_DIM), F32),
                        pltpu.VMEM((seq // CHUNK, GLA_V_DIM, GLA_K_DIM), BF16),
                        pltpu.VMEM((seq, GLA_V_DIM), F32)],
        compiler_params=_params(2),
        name="gla",
    )(proj, proj, proj, proj, ga, w2p, b2, og)


def _out_proj_kernel(a_ref, g_ref, x_ref, wo_ref, g2_ref, wr_ref, br_ref,
                     x1_ref, h2_ref, lg_ref, tt_scr):
    half = a_ref.shape[1]

    def body(c, carry):
        r0 = pl.multiple_of(c * RB_OUT, RB_OUT)
        rows = pl.ds(r0, RB_OUT)
        mixed = (jnp.dot(a_ref[rows, :], wo_ref[0:half, :], preferred_element_type=F32)
                 + jnp.dot(g_ref[rows, :], wo_ref[half:, :], preferred_element_type=F32))
        x1 = x_ref[rows, :] + mixed
        x1_ref[rows, :] = x1
        ms = jnp.mean(x1 * x1, axis=-1, keepdims=True)
        h2 = x1 * lax.rsqrt(ms + RMS_EPS) * g2_ref[...]
        _store_token_tiles(h2_ref, h2, tt_scr, r0)
        lg_ref[rows, :] = jnp.dot(h2.astype(BF16), wr_ref[...],
                                  preferred_element_type=F32) + br_ref[...]
        return carry

    lax.fori_loop(0, TM_OUT // RB_OUT, body, 0)


def _out_proj(a_out, g_out, x2, wo, g2, wr, br):
    t = x2.shape[0]
    half = a_out.shape[1]
    row = lambda n: pl.BlockSpec((TM_OUT, n), lambda i: (i, 0))
    full = lambda r, n: pl.BlockSpec((r, n), lambda i: (0, 0))
    return pl.pallas_call(
        _out_proj_kernel,
        grid=(t // TM_OUT,),
        in_specs=[row(half), row(half), row(D_MODEL), full(D_MODEL, D_MODEL),
                  full(1, D_MODEL), full(D_MODEL, LANES), full(1, LANES)],
        out_specs=[row(D_MODEL), pl.BlockSpec((TM_OUT * SUB, LANES), lambda i: (i, 0)),
                   row(LANES)],
        out_shape=[jax.ShapeDtypeStruct((t, D_MODEL), F32),
                   jax.ShapeDtypeStruct((t * SUB, LANES), BF16),
                   jax.ShapeDtypeStruct((t, LANES), F32)],
        scratch_shapes=[pltpu.VMEM((RB_OUT * SUB, LANES), F32)],
        compiler_params=_params(1),
        name="out_proj_router",
    )(a_out, g_out, x2, wo, g2, wr, br)


def _first_argmax(vals, lane_f, valid):
    v = jnp.where(valid, vals, -jnp.inf)
    m = jnp.max(v, axis=-1, keepdims=True)
    idx = jnp.min(jnp.where(valid & (v == m), lane_f, float(LANES)), axis=-1, keepdims=True)
    return m, idx


def _routing_kernel(lg_ref, slots_ref, cw_ref, cnt_ref, e1_scr, e2_scr, r1_scr, r2_scr):
    t = lg_ref.shape[0]
    lane = lax.broadcasted_iota(jnp.int32, (TB, LANES), 1)
    lane_f = lane.astype(F32)
    ri = lax.broadcasted_iota(jnp.int32, (TB, TB), 0)
    ci = lax.broadcasted_iota(jnp.int32, (TB, TB), 1)
    strict_lower = jnp.where(ci < ri, 1.0, 0.0).astype(BF16)

    def softmax_in(lg, valid):
        m = jnp.max(jnp.where(valid, lg, -jnp.inf), axis=-1, keepdims=True)
        ex = jnp.where(valid, jnp.exp(lg - m), 0.0)
        return ex / jnp.sum(ex, axis=-1, keepdims=True)

    def phase1(b, carry):
        rows = pl.ds(pl.multiple_of(b * TB, TB), TB)
        lg = lg_ref[rows, :]
        is_g = lane < N_GROUPS
        pg_sel, g_sel = _first_argmax(softmax_in(lg, is_g), lane_f, is_g)
        e_lo = N_GROUPS + g_sel * EXPERTS_PER_GROUP
        in_grp = (lane_f >= e_lo) & (lane_f < e_lo + EXPERTS_PER_GROUP)
        pe = softmax_in(lg, in_grp)
        v1, l1 = _first_argmax(pe, lane_f, in_grp)
        v2, l2 = _first_argmax(pe, lane_f, in_grp & (lane_f != l1))
        tot = v1 + v2
        c1 = (v1 / tot) * pg_sel
        c2 = (v2 / tot) * pg_sel
        e1 = (l1 - N_GROUPS).astype(jnp.int32)
        e2 = (l2 - N_GROUPS).astype(jnp.int32)
        oh1 = lane == e1
        oh2 = lane == e2
        a = jnp.where(oh1 | oh2, 1.0, 0.0)
        rank = jnp.dot(strict_lower, a.astype(BF16), preferred_element_type=F32) + carry
        e1_scr[rows, :] = jnp.broadcast_to(e1, (TB, LANES))
        e2_scr[rows, :] = jnp.broadcast_to(e2, (TB, LANES))
        r1_scr[rows, :] = jnp.broadcast_to(
            jnp.sum(jnp.where(oh1, rank, 0.0), axis=-1, keepdims=True), (TB, LANES))
        r2_scr[rows, :] = jnp.broadcast_to(
            jnp.sum(jnp.where(oh2, rank, 0.0), axis=-1, keepdims=True), (TB, LANES))
        cw_ref[rows, :] = jnp.where(lane == 0, c1, jnp.where(lane == 1, c2, 0.0))
        return carry + jnp.sum(a, axis=0, keepdims=True)

    counts = lax.fori_loop(0, t // TB, phase1, jnp.zeros((1, LANES), F32))
    cnt_ref[...] = jnp.broadcast_to(counts, cnt_ref.shape)
    n_tiles = jnp.floor((counts + (TM_E - 1)) * (1.0 / TM_E))
    ui = lax.broadcasted_iota(jnp.int32, (LANES, LANES), 0)
    uj = lax.broadcasted_iota(jnp.int32, (LANES, LANES), 1)
    strict_upper = jnp.where(ui < uj, 1.0, 0.0).astype(BF16)
    tile_off = jnp.dot(jnp.broadcast_to(n_tiles, (8, LANES)).astype(BF16), strict_upper,
                       preferred_element_type=F32)[0:1, :]
    row_off = tile_off * TM_E

    def phase2(b, carry):
        rows = pl.ds(pl.multiple_of(b * TB, TB), TB)
        off1 = jnp.sum(jnp.where(lane == e1_scr[rows, :], row_off, 0.0), axis=-1, keepdims=True)
        off2 = jnp.sum(jnp.where(lane == e2_scr[rows, :], row_off, 0.0), axis=-1, keepdims=True)
        sl1 = (off1 + r1_scr[rows, 0:1]).astype(jnp.int32)
        sl2 = (off2 + r2_scr[rows, 0:1]).astype(jnp.int32)
        slots_ref[rows, :] = jnp.where(lane == 0, sl1, jnp.where(lane == 1, sl2, 0))
        return carry

    lax.fori_loop(0, t // TB, phase2, 0)


def _routing(logits):
    t = logits.shape[0]
    return pl.pallas_call(
        _routing_kernel,
        out_shape=[jax.ShapeDtypeStruct((t, LANES), jnp.int32),
                   jax.ShapeDtypeStruct((t, LANES), F32),
                   jax.ShapeDtypeStruct((8, LANES), F32)],
        scratch_shapes=[pltpu.VMEM((t, LANES), jnp.int32), pltpu.VMEM((t, LANES), jnp.int32),
                        pltpu.VMEM((t, LANES), F32), pltpu.VMEM((t, LANES), F32)],
        compiler_params=pltpu.CompilerParams(vmem_limit_bytes=VMEM_LIMIT),
        name="routing",
    )(logits)


def _rows_copy(src, dst, sem, src_tok, dst_tok, n):
    first = lambda tok: tok * SUB if isinstance(tok, int) else pl.multiple_of(tok * SUB, SUB)
    s0 = first(src_tok)
    d0 = first(dst_tok)
    return pltpu.make_async_copy(src.at[pl.ds(s0, n * SUB), :], dst.at[pl.ds(d0, n * SUB), :], sem)


def _dispatch_kernel(sl1_ref, sl2_ref, ends_ref, h2_ref, xs_hbm, zero_scr, sem):
    step = pl.program_id(0)

    @pl.when(step == 0)
    def _():
        zero_scr[...] = jnp.zeros_like(zero_scr)
        for e in range(N_EXPERTS):
            start = jnp.maximum(ends_ref[e] - TM_E, 0)
            _rows_copy(zero_scr, xs_hbm, sem, 0, start, TM_E).start()
        for e in range(N_EXPERTS):
            _rows_copy(zero_scr, xs_hbm, sem, 0, 0, TM_E).wait()

    base = step * TB_D

    def issue(r, carry):
        _rows_copy(h2_ref, xs_hbm, sem, r, sl1_ref[base + r], 1).start(priority=0)
        _rows_copy(h2_ref, xs_hbm, sem, r, sl2_ref[base + r], 1).start(priority=1)
        return carry

    lax.fori_loop(0, TB_D, issue, 0, unroll=8)
    _rows_copy(h2_ref, xs_hbm, sem, 0, 0, TB_D).wait()
    _rows_copy(h2_ref, xs_hbm, sem, 0, 0, TB_D).wait()


def _dispatch(h2, sl1, sl2, ends, n_rows):
    t = h2.shape[0] // SUB
    return pl.pallas_call(
        _dispatch_kernel,
        grid_spec=pltpu.PrefetchScalarGridSpec(
            num_scalar_prefetch=3,
            grid=(t // TB_D,),
            in_specs=[pl.BlockSpec((TB_D * SUB, LANES), lambda i, *_: (i, 0))],
            out_specs=pl.BlockSpec(memory_space=pl.ANY),
            scratch_shapes=[pltpu.VMEM((TM_E * SUB, LANES), BF16), pltpu.SemaphoreType.DMA(())],
        ),
        out_shape=jax.ShapeDtypeStruct((n_rows * SUB, LANES), BF16),
        compiler_params=_params(1),
        name="dispatch",
    )(sl1, sl2, ends, h2)


def _expert_kernel(tidx_ref, texp_ref, tfirst_ref, tslot_ref, tnext_ref, nt_ref,
                   xs_ref, wg_hbm, wu_hbm, wd_hbm, ys_ref,
                   wg_buf, wu_buf, wd_buf, wg_bf, wu_bf, wd_bf, tt_scr, sems):
    j = pl.program_id(0)

    def weight_copies(e, slot):
        return [pltpu.make_async_copy(hbm.at[e], buf.at[slot], sems.at[slot, k])
                for k, (hbm, buf) in enumerate(((wg_hbm, wg_buf), (wu_hbm, wu_buf),
                                                (wd_hbm, wd_buf)))]

    @pl.when(j < nt_ref[0])
    def _():
        slot = tslot_ref[j]

        @pl.when(tfirst_ref[j] == 1)
        def _():
            @pl.when(j == 0)
            def _():
                for c in weight_copies(texp_ref[j], slot):
                    c.start()
            for c in weight_copies(texp_ref[j], slot):
                c.wait()

            @pl.when(tnext_ref[j] >= 0)
            def _():
                for c in weight_copies(tnext_ref[j], 1 - slot):
                    c.start()
            wg_bf[...] = wg_buf[slot].astype(BF16)
            wu_bf[...] = wu_buf[slot].astype(BF16)
            wd_bf[...] = wd_buf[slot].astype(BF16)

        xb = _load_token_tiles(xs_ref, tt_scr, TM_E).astype(BF16)
        g = jnp.dot(xb, wg_bf[...], preferred_element_type=F32)
        u = jnp.dot(xb, wu_bf[...], preferred_element_type=F32)
        h = (_silu(g) * u).astype(BF16)
        _store_token_tiles(ys_ref, jnp.dot(h, wd_bf[...], preferred_element_type=F32), tt_scr)


def _expert_mlp(xs, tile_meta, wg, wu, wd):
    n_rows = xs.shape[0] // SUB
    tile = pl.BlockSpec((TM_E * SUB, LANES), lambda j, ti, *_: (ti[j], 0))
    hbm = pl.BlockSpec(memory_space=pl.ANY)
    return pl.pallas_call(
        _expert_kernel,
        grid_spec=pltpu.PrefetchScalarGridSpec(
            num_scalar_prefetch=len(tile_meta),
            grid=(n_rows // TM_E,),
            in_specs=[tile, hbm, hbm, hbm],
            out_specs=tile,
            scratch_shapes=[pltpu.VMEM((2, D_MODEL, D_FF), F32),
                            pltpu.VMEM((2, D_MODEL, D_FF), F32),
                            pltpu.VMEM((2, D_FF, D_MODEL), F32),
                            pltpu.VMEM((D_MODEL, D_FF), BF16),
                            pltpu.VMEM((D_MODEL, D_FF), BF16),
                            pltpu.VMEM((D_FF, D_MODEL), BF16),
                            pltpu.VMEM((TM_E * SUB, LANES), F32),
                            pltpu.SemaphoreType.DMA((2, 3))],
        ),
        out_shape=jax.ShapeDtypeStruct((n_rows * SUB, LANES), BF16),
        compiler_params=_params(1),
        name="expert_mlp",
    )(*tile_meta, xs, wg, wu, wd)


def _combine_kernel(sl1_ref, sl2_ref, x1_ref, cw_ref, ys_hbm, o_ref, y1_scr, y2_scr, tt_scr,
                    sems):
    i = pl.program_id(0)

    def issue(step, slot):
        base = step * TB

        def body(r, carry):
            _rows_copy(ys_hbm, y1_scr.at[slot], sems.at[slot], sl1_ref[base + r], r,
                       1).start(priority=0)
            _rows_copy(ys_hbm, y2_scr.at[slot], sems.at[slot], sl2_ref[base + r], r,
                       1).start(priority=1)
            return carry

        lax.fori_loop(0, TB, body, 0, unroll=8)

    @pl.when(i == 0)
    def _():
        issue(0, 0)

    @pl.when(i + 1 < pl.num_programs(0))
    def _():
        issue(i + 1, (i + 1) % 2)

    slot = i % 2
    _rows_copy(ys_hbm, y1_scr.at[slot], sems.at[slot], 0, 0, TB).wait()
    _rows_copy(ys_hbm, y2_scr.at[slot], sems.at[slot], 0, 0, TB).wait()
    cw = cw_ref[...]
    o_ref[...] = x1_ref[...] + cw[:, 0:1] * _load_token_tiles(y1_scr.at[slot], tt_scr, TB)
    o_ref[...] += cw[:, 1:2] * _load_token_tiles(y2_scr.at[slot], tt_scr, TB)


def _combine(x1, cw, ys, sl1, sl2):
    t = x1.shape[0]
    return pl.pallas_call(
        _combine_kernel,
        grid_spec=pltpu.PrefetchScalarGridSpec(
            num_scalar_prefetch=2,
            grid=(t // TB,),
            in_specs=[pl.BlockSpec((TB, D_MODEL), lambda i, *_: (i, 0)),
                      pl.BlockSpec((TB, LANES), lambda i, *_: (i, 0)),
                      pl.BlockSpec(memory_space=pl.ANY)],
            out_specs=pl.BlockSpec((TB, D_MODEL), lambda i, *_: (i, 0)),
            scratch_shapes=[pltpu.VMEM((2, TB * SUB, LANES), BF16),
                            pltpu.VMEM((2, TB * SUB, LANES), BF16),
                            pltpu.VMEM((TB * SUB, LANES), F32),
                            pltpu.SemaphoreType.DMA((2,))],
        ),
        out_shape=jax.ShapeDtypeStruct((t, D_MODEL), F32),
        compiler_params=_params(1),
        name="combine",
    )(sl1, sl2, x1, cw, ys)


def _lane_tile(v, reps):
    return jnp.tile(v.reshape(1, -1), (1, reps))


def kernel(x, positions, norm1_g, w_in, q_norm_g, k_norm_g, lambda_q1, lambda_k1, lambda_q2,
           lambda_k2, diff_out_norm_g, gla_w_gate2, gla_b_gate, gla_out_norm_g, w_out, norm2_g,
           w_router_group, b_router_group, w_router_expert, b_router_expert, w_gate_expert,
           w_up_expert, w_down_expert):
    batch, seq, d = x.shape
    t = batch * seq
    x2 = x.reshape(t, d)

    inv = ROPE_THETA ** (-jnp.arange(0, ROT_DIM, 2, dtype=F32) / ROT_DIM)
    lane_d = jnp.arange(LANES) % DIFF_QK_DIM
    invf = jnp.where(lane_d < ROT_DIM, inv[lane_d % (ROT_DIM // 2)], 0.0).reshape(1, LANES)
    pos_b = jnp.broadcast_to(positions.astype(F32).reshape(t, 1), (t, LANES))
    tabs = _rope_tables(pos_b, invf)

    w_in2 = w_in[0]
    wa = jnp.pad(w_in2[:, D_MAIN:], ((0, 0), (0, LANES - GLA_GATE_RANK)))
    proj, ga = _in_proj(x2, norm1_g, w_in2, wa)

    a_out = _diff_attention(
        proj, tabs, _lane_tile(q_norm_g[0], 2), _lane_tile(k_norm_g[0], 2),
        lambda_q1, lambda_k1, lambda_q2, lambda_k2, diff_out_norm_g, batch, seq)
    w2p = jnp.pad(gla_w_gate2[0], ((0, LANES - GLA_GATE_RANK), (0, 0))).astype(BF16)
    g_out = _gla(proj, ga, w2p, gla_b_gate, gla_out_norm_g, batch, seq)

    n_r = N_GROUPS + N_EXPERTS
    wr = jnp.concatenate(
        [w_router_group[0], w_router_expert[0].transpose(1, 0, 2).reshape(d, N_EXPERTS)], axis=1)
    wr = jnp.pad(wr, ((0, 0), (0, LANES - n_r)))
    br = jnp.pad(jnp.concatenate([b_router_group[0], b_router_expert[0].reshape(-1)]),
                 (0, LANES - n_r)).reshape(1, LANES)
    x1, h2, logits = _out_proj(a_out, g_out, x2, w_out[0].astype(BF16), norm2_g,
                               wr.astype(BF16), br)

    slots, cw, counts = _routing(logits)
    sl1 = slots[:, 0]
    sl2 = slots[:, 1]
    n_rows = 2 * t + N_EXPERTS * TM_E
    max_tiles = n_rows // TM_E
    cnt = counts[0, :N_EXPERTS].astype(jnp.int32)
    tile_end = jnp.cumsum((cnt + (TM_E - 1)) // TM_E)
    n_tiles = tile_end[-1:]
    tile_idx = jnp.minimum(jnp.arange(max_tiles, dtype=jnp.int32), n_tiles[0] - 1)
    tile_exp = jnp.sum(tile_idx[:, None] >= tile_end[None, :], axis=1).astype(jnp.int32)
    row_end = (tile_end * TM_E).astype(jnp.int32)
    tile_first = jnp.concatenate(
        [jnp.ones((1,), jnp.int32), (tile_exp[1:] != tile_exp[:-1]).astype(jnp.int32)])
    tile_slot = (jnp.cumsum(tile_first) - 1) % 2
    nxt = tile_end[tile_exp]
    tile_next = jnp.where(nxt < n_tiles[0], tile_exp[jnp.minimum(nxt, max_tiles - 1)], -1)
    tile_meta = [a.astype(jnp.int32)
                 for a in (tile_idx, tile_exp, tile_first, tile_slot, tile_next, n_tiles)]

    xs = _dispatch(h2, sl1, sl2, row_end, n_rows)
    ys = _expert_mlp(xs, tile_meta,
                     w_gate_expert[0].reshape(N_EXPERTS, d, D_FF),
                     w_up_expert[0].reshape(N_EXPERTS, d, D_FF),
                     w_down_expert[0].reshape(N_EXPERTS, D_FF, d))
    out = _combine(x1, cw, ys, sl1, sl2)
    return out.reshape(batch, seq, d)
```

```python
import functools
import math

import jax
import jax.numpy as jnp
from jax import lax
from jax.experimental import pallas as pl
from jax.experimental.pallas import tpu as pltpu

D_MODEL = 2048
CHUNK = 64
DIFF_QK_DIM = 64
DIFF_V_DIM = 128
DIFF_HEADS = 8
ROT_DIM = 16
ROPE_THETA = 500000.0
GLA_HEADS = 4
GLA_V_DIM = 256
GLA_K_DIM = 128
GLA_GATE_RANK = 16
GLA_TAU = 16.0
N_GROUPS = 4
EXPERTS_PER_GROUP = 8
N_EXPERTS = N_GROUPS * EXPERTS_PER_GROUP
D_FF = 512
RMS_EPS = 1e-6
LAMBDA_INIT = 0.8 - 0.6 * math.exp(-0.3 * 0)
D_MAIN = 6144

LANES = 128
SUB = D_MODEL // LANES
VMEM_LIMIT = 56 * 1024 * 1024

TM_IN = 1024
TN_IN = 512
TQ = 256
GB = 256
TM_OUT = 512
RB_OUT = 256
TB = 256
TB_D = 1024
TM_E = 256

F32 = jnp.float32
BF16 = jnp.bfloat16
HI = lax.Precision.HIGHEST


def _params(n_axes):
    return pltpu.CompilerParams(dimension_semantics=("arbitrary",) * n_axes,
                                vmem_limit_bytes=VMEM_LIMIT)


def _nt_dot(a, b):
    return lax.dot_general(a, b, (((1,), (1,)), ((), ())), preferred_element_type=F32)


def _tn_dot(a, b):
    return lax.dot_general(a, b, (((0,), (0,)), ((), ())), preferred_element_type=F32)


def _silu(x):
    return x * (1.0 / (1.0 + jnp.exp(-x)))


def _store_token_tiles(ref, val, scr, tok0=0):
    n = val.shape[0]
    for s in range(SUB):
        scr[pl.ds(s, n, stride=SUB), :] = val[:, s * LANES:(s + 1) * LANES]
    ref[pl.ds(tok0 * SUB, n * SUB), :] = scr[...].astype(BF16)


def _load_token_tiles(ref, scr, n, tok0=0):
    scr[...] = ref[pl.ds(tok0 * SUB, n * SUB), :].astype(F32)
    return jnp.concatenate([scr[pl.ds(s, n, stride=SUB), :] for s in range(SUB)], axis=1)


def _rope_table_kernel(pos_ref, invf_ref, c_ref, s1_ref, s2_ref):
    ang = pos_ref[...] * invf_ref[...]
    d = lax.broadcasted_iota(jnp.int32, ang.shape, 1) % DIFF_QK_DIM
    cos = jnp.cos(ang)
    sin = jnp.sin(ang)
    half = ROT_DIM // 2
    c_ref[...] = jnp.where(d < ROT_DIM, cos, 1.0)
    s1_ref[...] = jnp.where(d < half, -sin, 0.0)
    s2_ref[...] = jnp.where((d >= half) & (d < ROT_DIM), sin, 0.0)


def _rope_tables(pos_b, invf):
    t = pos_b.shape[0]
    tb = 1024
    spec = pl.BlockSpec((tb, LANES), lambda i: (i, 0))
    return pl.pallas_call(
        _rope_table_kernel,
        grid=(t // tb,),
        in_specs=[spec, pl.BlockSpec((1, LANES), lambda i: (0, 0))],
        out_specs=[spec, spec, spec],
        out_shape=[jax.ShapeDtypeStruct((t, LANES), F32)] * 3,
        compiler_params=_params(1),
        name="rope_tables",
    )(pos_b, invf)


def _in_proj_kernel(x_ref, g_ref, wt_ref, wat_ref, proj_ref, ga_ref, h_scr):
    @pl.when(pl.program_id(1) == 0)
    def _():
        def body(c, carry):
            rows = pl.ds(c * 256, 256)
            x = x_ref[rows, :]
            ms = jnp.mean(x * x, axis=-1, keepdims=True)
            h_scr[rows, :] = (x * lax.rsqrt(ms + RMS_EPS) * g_ref[...]).astype(BF16)
            return carry
        lax.fori_loop(0, TM_IN // 256, body, 0)
        ga_ref[...] = _nt_dot(h_scr[...], wat_ref[...].astype(BF16))

    proj_ref[...] = _nt_dot(h_scr[...], wt_ref[...].astype(BF16)).astype(BF16)


def _in_proj(x2, g1, w_in, wa):
    t = x2.shape[0]
    return pl.pallas_call(
        _in_proj_kernel,
        grid=(t // TM_IN, D_MAIN // TN_IN),
        in_specs=[
            pl.BlockSpec((TM_IN, D_MODEL), lambda i, j: (i, 0)),
            pl.BlockSpec((1, D_MODEL), lambda i, j: (0, 0)),
            pl.BlockSpec((TN_IN, D_MODEL), lambda i, j: (j, 0)),
            pl.BlockSpec((LANES, D_MODEL), lambda i, j: (0, 0)),
        ],
        out_specs=[
            pl.BlockSpec((TM_IN, TN_IN), lambda i, j: (i, j)),
            pl.BlockSpec((TM_IN, LANES), lambda i, j: (i, 0)),
        ],
        out_shape=[jax.ShapeDtypeStruct((t, D_MAIN), BF16),
                   jax.ShapeDtypeStruct((t, LANES), F32)],
        scratch_shapes=[pltpu.VMEM((TM_IN, D_MODEL), BF16)],
        compiler_params=_params(2),
        name="in_proj",
    )(x2, g1, w_in, wa)


def _norm_rope(x, g, c, s1, s2):
    lo = lax.broadcasted_iota(jnp.int32, x.shape, 1) < DIFF_QK_DIM
    x2 = x * x
    s_lo = jnp.sum(jnp.where(lo, x2, 0.0), axis=-1, keepdims=True)
    s_hi = jnp.sum(jnp.where(lo, 0.0, x2), axis=-1, keepdims=True)
    ms = jnp.where(lo, s_lo, s_hi) * (1.0 / DIFF_QK_DIM)
    y = x * lax.rsqrt(ms + RMS_EPS) * g
    half = ROT_DIM // 2
    return y * c + pltpu.roll(y, LANES - half, 1) * s1 + pltpu.roll(y, half, 1) * s2


def _attn_kernel(q_ref, k_ref, v_ref, c_ref, s1_ref, s2_ref, qg_ref, kg_ref,
                 lq1_ref, lk1_ref, lq2_ref, lk2_ref, og_ref, o_ref,
                 q1_scr, q2_scr, k_scr, v_scr, *bufs):
    s_len = q_ref.shape[0]
    s_bufs, e_bufs = bufs[0:4], bufs[4:8]
    lam = (jnp.exp(jnp.sum(lq1_ref[...] * lk1_ref[...], axis=-1, keepdims=True))
           - jnp.exp(jnp.sum(lq2_ref[...] * lk2_ref[...], axis=-1, keepdims=True))
           + LAMBDA_INIT)
    lo = lax.broadcasted_iota(jnp.int32, (TQ, LANES), 1) < DIFF_QK_DIM
    v_scr[:, 0:LANES] = v_ref[...]
    v_scr[:, LANES:] = jnp.ones((s_len, LANES), BF16)

    def prepare(i):
        rows = slice(i * TQ, (i + 1) * TQ)
        c, s1, s2 = c_ref[rows, :], s1_ref[rows, :], s2_ref[rows, :]
        qn = (_norm_rope(q_ref[rows, :].astype(F32), qg_ref[...], c, s1, s2)
              * (DIFF_QK_DIM ** -0.5 * math.log2(math.e)))
        q1_scr[rows, :] = jnp.where(lo, qn, 0.0).astype(BF16)
        q2_scr[rows, :] = jnp.where(lo, 0.0, qn).astype(BF16)
        k_scr[rows, :] = _norm_rope(k_ref[rows, :].astype(F32), kg_ref[...], c, s1,
                                    s2).astype(BF16)

    diag = (lax.broadcasted_iota(jnp.int32, (TQ, TQ), 1) // CHUNK
            <= lax.broadcasted_iota(jnp.int32, (TQ, TQ), 0) // CHUNK)

    n_tiles = s_len // TQ
    items = [(i, q_scr) for i in range(n_tiles) for q_scr in (q1_scr, q2_scr)]

    def scores(r):
        i, q_scr = items[r]
        s_scr = s_bufs[r % 4]
        q = q_scr[i * TQ:(i + 1) * TQ, :]
        n_off = i * TQ
        s_scr[:, n_off:n_off + TQ] = jnp.where(
            diag, _nt_dot(q, k_scr[n_off:n_off + TQ, :]), -jnp.inf)
        if n_off:
            s_scr[:, 0:n_off] = _nt_dot(q, k_scr[0:n_off, :])

    def softmax_pv(r):
        nk = (items[r][0] + 1) * TQ
        s_scr, e_scr = s_bufs[r % 4], e_bufs[r % 4]
        m = jnp.max(s_scr[:, 0:nk], axis=-1, keepdims=True)
        e_scr[:, 0:nk] = jnp.exp2(s_scr[:, 0:nk] - m).astype(BF16)
        acc = jnp.dot(e_scr[:, 0:nk], v_scr[0:nk, :], preferred_element_type=F32)
        return acc[:, 0:LANES] / acc[:, LANES:]

    prepare(0)
    scores(0)
    scores(1)
    for i in range(n_tiles):
        rows = slice(i * TQ, (i + 1) * TQ)
        if i + 1 < n_tiles:
            prepare(i + 1)
            scores(2 * i + 2)
        o1 = softmax_pv(2 * i)
        if i + 1 < n_tiles:
            scores(2 * i + 3)
        o = o1 - lam * softmax_pv(2 * i + 1)
        ms = jnp.mean(o * o, axis=-1, keepdims=True)
        y = o * lax.rsqrt(ms + RMS_EPS) * og_ref[...] * (1.0 - LAMBDA_INIT)
        o_ref[rows, :] = y.astype(BF16)


def _diff_attention(proj, tabs, qg, kg, lq1, lk1, lq2, lk2, og, batch, seq):
    c, s1, s2 = tabs
    h = DIFF_HEADS
    blk = lambda off: pl.BlockSpec((seq, LANES), lambda b, hh, off=off: (b, off + hh))
    tab = pl.BlockSpec((seq, LANES), lambda b, hh: (b, 0))
    vec = lambda n: pl.BlockSpec((1, n), lambda b, hh: (0, 0))
    return pl.pallas_call(
        _attn_kernel,
        grid=(batch, h),
        in_specs=[blk(0), blk(h), blk(2 * h), tab, tab, tab,
                  vec(LANES), vec(LANES), vec(DIFF_QK_DIM), vec(DIFF_QK_DIM),
                  vec(DIFF_QK_DIM), vec(DIFF_QK_DIM), vec(LANES)],
        out_specs=pl.BlockSpec((seq, LANES), lambda b, hh: (b, hh)),
        out_shape=jax.ShapeDtypeStruct((batch * seq, h * DIFF_V_DIM), BF16),
        scratch_shapes=[pltpu.VMEM((seq, LANES), BF16)] * 3
        + [pltpu.VMEM((seq, 2 * LANES), BF16)]
        + [pltpu.VMEM((TQ, seq), F32)] * 4 + [pltpu.VMEM((TQ, seq), BF16)] * 4,
        compiler_params=_params(2),
        name="diff_attention",
    )(proj, proj, proj, c, s1, s2, qg, kg, lq1, lk1, lq2, lk2, og)


def _split_dot(ones_bf, x):
    hi = x.astype(BF16)
    lo = (x - hi.astype(F32)).astype(BF16)
    return (jnp.dot(ones_bf, hi, preferred_element_type=F32)
            + jnp.dot(ones_bf, lo, preferred_element_type=F32))


def _gla_kernel(q_ref, k_ref, v_ref, r_ref, ga_ref, w2_ref, b2_ref, og_ref, o_ref,
                qin_scr, ut_scr, dec_scr, st_scr, acc_scr):
    s_len = q_ref.shape[0]
    cpg = GB // CHUNK
    ri = lax.broadcasted_iota(jnp.int32, (GB, GB), 0)
    ci = lax.broadcasted_iota(jnp.int32, (GB, GB), 1)
    same = (ri // CHUNK) == (ci // CHUNK)
    blk_ones = jnp.where(same, 1.0, 0.0).astype(BF16)
    tril = same & (ci <= ri)
    tri_ones = jnp.where(tril, 1.0, 0.0).astype(BF16)

    def phase_a(g, carry):
        rows = pl.ds(pl.multiple_of(g * GB, GB), GB)
        pre = jnp.dot(ga_ref[rows, :].astype(BF16), w2_ref[...],
                      preferred_element_type=F32) + b2_ref[...]
        la = -(jnp.maximum(-pre, 0.0) + jnp.log1p(jnp.exp(-jnp.abs(pre)))) * (1.0 / GLA_TAU)
        bc = _split_dot(tri_ones, la)
        bl = _split_dot(blk_ones, la)
        e_neg = jnp.exp(-bc)
        e_last = jnp.exp(bl)
        k = k_ref[rows, :].astype(F32)
        q_in = (q_ref[rows, :].astype(F32) * (GLA_K_DIM ** -0.5) * jnp.exp(bc)).astype(BF16)
        k_in = (k * e_neg).astype(BF16)
        k_dec = (k * (e_last * e_neg)).astype(BF16)
        qin_scr[rows, :] = q_in
        dec_scr[rows, :] = e_last
        v = v_ref[rows, :]
        att = jnp.where(tril, _nt_dot(q_in, k_in), 0.0).astype(BF16)
        acc_scr[rows, :] = jnp.dot(att, v, preferred_element_type=F32)
        for c in range(cpg):
            cr = slice(c * CHUNK, (c + 1) * CHUNK)
            ut_scr[g * cpg + c] = _tn_dot(v[cr, :], k_dec[cr, :])
        return carry

    lax.fori_loop(0, s_len // GB, phase_a, 0)

    def phase_b(c, st):
        st_scr[c] = st.astype(BF16)
        dec = dec_scr[pl.ds(pl.multiple_of(c * CHUNK, CHUNK), 1), :]
        return dec * st + ut_scr[c]

    lax.fori_loop(0, s_len // CHUNK, phase_b, jnp.zeros((GLA_V_DIM, GLA_K_DIM), F32))

    def phase_c(g, carry):
        rows = pl.ds(pl.multiple_of(g * GB, GB), GB)
        inter = [_nt_dot(qin_scr[pl.ds(pl.multiple_of(g * GB + c * CHUNK, CHUNK), CHUNK), :],
                         st_scr[g * cpg + c]) for c in range(cpg)]
        o = acc_scr[rows, :] + jnp.concatenate(inter, axis=0)
        ms = jnp.mean(o * o, axis=-1, keepdims=True)
        y = o * lax.rsqrt(ms + RMS_EPS) * og_ref[...]
        o_ref[rows, :] = (y * _silu(r_ref[rows, :].astype(F32))).astype(BF16)
        return carry

    lax.fori_loop(0, s_len // GB, phase_c, 0)


def _gla(proj, ga, w2p, b2, og, batch, seq):
    hq = 3 * DIFF_HEADS
    kblk = lambda off: pl.BlockSpec((seq, GLA_K_DIM), lambda b, hh, off=off: (b, off + hh))
    vblk = lambda off: pl.BlockSpec((seq, GLA_V_DIM), lambda b, hh, off=off: (b, off + hh))
    return pl.pallas_call(
        _gla_kernel,
        grid=(batch, GLA_HEADS),
        in_specs=[kblk(hq), kblk(hq + GLA_HEADS), vblk(16), vblk(16 + GLA_HEADS),
                  pl.BlockSpec((seq, LANES), lambda b, hh: (b, 0)),
                  pl.BlockSpec((LANES, GLA_K_DIM), lambda b, hh: (0, hh)),
                  pl.BlockSpec((1, GLA_K_DIM), lambda b, hh: (0, hh)),
                  pl.BlockSpec((1, GLA_V_DIM), lambda b, hh: (0, 0))],
        out_specs=pl.BlockSpec((seq, GLA_V_DIM), lambda b, hh: (b, hh)),
        out_shape=jax.ShapeDtypeStruct((batch * seq, GLA_HEADS * GLA_V_DIM), BF16),
        scratch_shapes=[pltpu.VMEM((seq, GLA_K_DIM), BF16),
                        pltpu.VMEM((seq // CHUNK, GLA_V_DIM, GLA_K_DIM), F32),
                        pltpu.VMEM((seq, GLA_K_DIM), F32),
                        pltpu.VMEM((seq // CHUNK, GLA_V_DIM, GLA_K_DIM), BF16),
                        pltpu.VMEM((seq, GLA_V_DIM), F32)],
        compiler_params=_params(2),
        name="gla",
    )(proj, proj, proj, proj, ga, w2p, b2, og)


def _out_proj_kernel(a_ref, g_ref, x_ref, wo_ref, g2_ref, wr_ref, br_ref,
                     x1_ref, h2_ref, lg_ref, tt_scr):
    half = a_ref.shape[1]

    def body(c, carry):
        r0 = pl.multiple_of(c * RB_OUT, RB_OUT)
        rows = pl.ds(r0, RB_OUT)
        mixed = (jnp.dot(a_ref[rows, :], wo_ref[0:half, :], preferred_element_type=F32)
                 + jnp.dot(g_ref[rows, :], wo_ref[half:, :], preferred_element_type=F32))
        x1 = x_ref[rows, :] + mixed
        x1_ref[rows, :] = x1
        ms = jnp.mean(x1 * x1, axis=-1, keepdims=True)
        h2 = x1 * lax.rsqrt(ms + RMS_EPS) * g2_ref[...]
        _store_token_tiles(h2_ref, h2, tt_scr, r0)
        lg_ref[rows, :] = jnp.dot(h2.astype(BF16), wr_ref[...],
                                  preferred_element_type=F32) + br_ref[...]
        return carry

    lax.fori_loop(0, TM_OUT // RB_OUT, body, 0)


def _out_proj(a_out, g_out, x2, wo, g2, wr, br):
    t = x2.shape[0]
    half = a_out.shape[1]
    row = lambda n: pl.BlockSpec((TM_OUT, n), lambda i: (i, 0))
    full = lambda r, n: pl.BlockSpec((r, n), lambda i: (0, 0))
    return pl.pallas_call(
        _out_proj_kernel,
        grid=(t // TM_OUT,),
        in_specs=[row(half), row(half), row(D_MODEL), full(D_MODEL, D_MODEL),
                  full(1, D_MODEL), full(D_MODEL, LANES), full(1, LANES)],
        out_specs=[row(D_MODEL), pl.BlockSpec((TM_OUT * SUB, LANES), lambda i: (i, 0)),
                   row(LANES)],
        out_shape=[jax.ShapeDtypeStruct((t, D_MODEL), F32),
                   jax.ShapeDtypeStruct((t * SUB, LANES), BF16),
                   jax.ShapeDtypeStruct((t, LANES), F32)],
        scratch_shapes=[pltpu.VMEM((RB_OUT * SUB, LANES), F32)],
        compiler_params=_params(1),
        name="out_proj_router",
    )(a_out, g_out, x2, wo, g2, wr, br)


def _first_argmax(vals, lane_f, valid):
    v = jnp.where(valid, vals, -jnp.inf)
    m = jnp.max(v, axis=-1, keepdims=True)
    idx = jnp.min(jnp.where(valid & (v == m), lane_f, float(LANES)), axis=-1, keepdims=True)
    return m, idx


def _routing_kernel(lg_ref, slots_ref, cw_ref, cnt_ref, e1_scr, e2_scr, r1_scr, r2_scr):
    t = lg_ref.shape[0]
    lane = lax.broadcasted_iota(jnp.int32, (TB, LANES), 1)
    lane_f = lane.astype(F32)
    ri = lax.broadcasted_iota(jnp.int32, (TB, TB), 0)
    ci = lax.broadcasted_iota(jnp.int32, (TB, TB), 1)
    strict_lower = jnp.where(ci < ri, 1.0, 0.0).astype(BF16)

    def softmax_in(lg, valid):
        m = jnp.max(jnp.where(valid, lg, -jnp.inf), axis=-1, keepdims=True)
        ex = jnp.where(valid, jnp.exp(lg - m), 0.0)
        return ex / jnp.sum(ex, axis=-1, keepdims=True)

    def phase1(b, carry):
        rows = pl.ds(pl.multiple_of(b * TB, TB), TB)
        lg = lg_ref[rows, :]
        is_g = lane < N_GROUPS
        pg_sel, g_sel = _first_argmax(softmax_in(lg, is_g), lane_f, is_g)
        e_lo = N_GROUPS + g_sel * EXPERTS_PER_GROUP
        in_grp = (lane_f >= e_lo) & (lane_f < e_lo + EXPERTS_PER_GROUP)
        pe = softmax_in(lg, in_grp)
        v1, l1 = _first_argmax(pe, lane_f, in_grp)
        v2, l2 = _first_argmax(pe, lane_f, in_grp & (lane_f != l1))
        tot = v1 + v2
        c1 = (v1 / tot) * pg_sel
        c2 = (v2 / tot) * pg_sel
        e1 = (l1 - N_GROUPS).astype(jnp.int32)
        e2 = (l2 - N_GROUPS).astype(jnp.int32)
        oh1 = lane == e1
        oh2 = lane == e2
        a = jnp.where(oh1 | oh2, 1.0, 0.0)
        rank = jnp.dot(strict_lower, a.astype(BF16), preferred_element_type=F32) + carry
        e1_scr[rows, :] = jnp.broadcast_to(e1, (TB, LANES))
        e2_scr[rows, :] = jnp.broadcast_to(e2, (TB, LANES))
        r1_scr[rows, :] = jnp.broadcast_to(
            jnp.sum(jnp.where(oh1, rank, 0.0), axis=-1, keepdims=True), (TB, LANES))
        r2_scr[rows, :] = jnp.broadcast_to(
            jnp.sum(jnp.where(oh2, rank, 0.0), axis=-1, keepdims=True), (TB, LANES))
        cw_ref[rows, :] = jnp.where(lane == 0, c1, jnp.where(lane == 1, c2, 0.0))
        return carry + jnp.sum(a, axis=0, keepdims=True)

    counts = lax.fori_loop(0, t // TB, phase1, jnp.zeros((1, LANES), F32))
    cnt_ref[...] = jnp.broadcast_to(counts, cnt_ref.shape)
    n_tiles = jnp.floor((counts + (TM_E - 1)) * (1.0 / TM_E))
    ui = lax.broadcasted_iota(jnp.int32, (LANES, LANES), 0)
    uj = lax.broadcasted_iota(jnp.int32, (LANES, LANES), 1)
    strict_upper = jnp.where(ui < uj, 1.0, 0.0).astype(BF16)
    tile_off = jnp.dot(jnp.broadcast_to(n_tiles, (8, LANES)).astype(BF16), strict_upper,
                       preferred_element_type=F32)[0:1, :]
    row_off = tile_off * TM_E

    def phase2(b, carry):
        rows = pl.ds(pl.multiple_of(b * TB, TB), TB)
        off1 = jnp.sum(jnp.where(lane == e1_scr[rows, :], row_off, 0.0), axis=-1, keepdims=True)
        off2 = jnp.sum(jnp.where(lane == e2_scr[rows, :], row_off, 0.0), axis=-1, keepdims=True)
        sl1 = (off1 + r1_scr[rows, 0:1]).astype(jnp.int32)
        sl2 = (off2 + r2_scr[rows, 0:1]).astype(jnp.int32)
        slots_ref[rows, :] = jnp.where(lane == 0, sl1, jnp.where(lane == 1, sl2, 0))
        return carry

    lax.fori_loop(0, t // TB, phase2, 0)


def _routing(logits):
    t = logits.shape[0]
    return pl.pallas_call(
        _routing_kernel,
        out_shape=[jax.ShapeDtypeStruct((t, LANES), jnp.int32),
                   jax.ShapeDtypeStruct((t, LANES), F32),
                   jax.ShapeDtypeStruct((8, LANES), F32)],
        scratch_shapes=[pltpu.VMEM((t, LANES), jnp.int32), pltpu.VMEM((t, LANES), jnp.int32),
                        pltpu.VMEM((t, LANES), F32), pltpu.VMEM((t, LANES), F32)],
        compiler_params=pltpu.CompilerParams(vmem_limit_bytes=VMEM_LIMIT),
        name="routing",
    )(logits)


def _rows_copy(src, dst, sem, src_tok, dst_tok, n):
    first = lambda tok: tok * SUB if isinstance(tok, int) else pl.multiple_of(tok * SUB, SUB)
    s0 = first(src_tok)
    d0 = first(dst_tok)
    return pltpu.make_async_copy(src.at[pl.ds(s0, n * SUB), :], dst.at[pl.ds(d0, n * SUB), :], sem)


def _dispatch_kernel(sl1_ref, sl2_ref, ends_ref, h2_ref, xs_hbm, zero_scr, sem):
    step = pl.program_id(0)

    @pl.when(step == 0)
    def _():
        zero_scr[...] = jnp.zeros_like(zero_scr)
        for e in range(N_EXPERTS):
            start = jnp.maximum(ends_ref[e] - TM_E, 0)
            _rows_copy(zero_scr, xs_hbm, sem, 0, start, TM_E).start()
        for e in range(N_EXPERTS):
            _rows_copy(zero_scr, xs_hbm, sem, 0, 0, TM_E).wait()

    base = step * TB_D

    def issue(r, carry):
        _rows_copy(h2_ref, xs_hbm, sem, r, sl1_ref[base + r], 1).start(priority=0)
        _rows_copy(h2_ref, xs_hbm, sem, r, sl2_ref[base + r], 1).start(priority=1)
        return carry

    lax.fori_loop(0, TB_D, issue, 0, unroll=8)
    _rows_copy(h2_ref, xs_hbm, sem, 0, 0, TB_D).wait()
    _rows_copy(h2_ref, xs_hbm, sem, 0, 0, TB_D).wait()


def _dispatch(h2, sl1, sl2, ends, n_rows):
    t = h2.shape[0] // SUB
    return pl.pallas_call(
        _dispatch_kernel,
        grid_spec=pltpu.PrefetchScalarGridSpec(
            num_scalar_prefetch=3,
            grid=(t // TB_D,),
            in_specs=[pl.BlockSpec((TB_D * SUB, LANES), lambda i, *_: (i, 0))],
            out_specs=pl.BlockSpec(memory_space=pl.ANY),
            scratch_shapes=[pltpu.VMEM((TM_E * SUB, LANES), BF16), pltpu.SemaphoreType.DMA(())],
        ),
        out_shape=jax.ShapeDtypeStruct((n_rows * SUB, LANES), BF16),
        compiler_params=_params(1),
        name="dispatch",
    )(sl1, sl2, ends, h2)


def _expert_kernel(tidx_ref, texp_ref, tfirst_ref, tslot_ref, tnext_ref, nt_ref,
                   xs_ref, wg_hbm, wu_hbm, wd_hbm, ys_ref,
                   wg_buf, wu_buf, wd_buf, wg_bf, wu_bf, wd_bf, tt_scr, sems):
    j = pl.program_id(0)

    def weight_copies(e, slot):
        return [pltpu.make_async_copy(hbm.at[e], buf.at[slot], sems.at[slot, k])
                for k, (hbm, buf) in enumerate(((wg_hbm, wg_buf), (wu_hbm, wu_buf),
                                                (wd_hbm, wd_buf)))]

    @pl.when(j < nt_ref[0])
    def _():
        slot = tslot_ref[j]

        @pl.when(tfirst_ref[j] == 1)
        def _():
            @pl.when(j == 0)
            def _():
                for c in weight_copies(texp_ref[j], slot):
                    c.start()
            for c in weight_copies(texp_ref[j], slot):
                c.wait()

            @pl.when(tnext_ref[j] >= 0)
            def _():
                for c in weight_copies(tnext_ref[j], 1 - slot):
                    c.start()
            wg_bf[...] = wg_buf[slot].astype(BF16)
            wu_bf[...] = wu_buf[slot].astype(BF16)
            wd_bf[...] = wd_buf[slot].astype(BF16)

        xb = _load_token_tiles(xs_ref, tt_scr, TM_E).astype(BF16)
        g = jnp.dot(xb, wg_bf[...], preferred_element_type=F32)
        u = jnp.dot(xb, wu_bf[...], preferred_element_type=F32)
        h = (_silu(g) * u).astype(BF16)
        _store_token_tiles(ys_ref, jnp.dot(h, wd_bf[...], preferred_element_type=F32), tt_scr)


def _expert_mlp(xs, tile_meta, wg, wu, wd):
    n_rows = xs.shape[0] // SUB
    tile = pl.BlockSpec((TM_E * SUB, LANES), lambda j, ti, *_: (ti[j], 0))
    hbm = pl.BlockSpec(memory_space=pl.ANY)
    return pl.pallas_call(
        _expert_kernel,
        grid_spec=pltpu.PrefetchScalarGridSpec(
            num_scalar_prefetch=len(tile_meta),
            grid=(n_rows // TM_E,),
            in_specs=[tile, hbm, hbm, hbm],
            out_specs=tile,
            scratch_shapes=[pltpu.VMEM((2, D_MODEL, D_FF), F32),
                            pltpu.VMEM((2, D_MODEL, D_FF), F32),
                            pltpu.VMEM((2, D_FF, D_MODEL), F32),
                            pltpu.VMEM((D_MODEL, D_FF), BF16),
                            pltpu.VMEM((D_MODEL, D_FF), BF16),
                            pltpu.VMEM((D_FF, D_MODEL), BF16),
                            pltpu.VMEM((TM_E * SUB, LANES), F32),
                            pltpu.SemaphoreType.DMA((2, 3))],
        ),
        out_shape=jax.ShapeDtypeStruct((n_rows * SUB, LANES), BF16),
        compiler_params=_params(1),
        name="expert_mlp",
    )(*tile_meta, xs, wg, wu, wd)


def _combine_kernel(sl1_ref, sl2_ref, x1_ref, cw_ref, ys_hbm, o_ref, y1_scr, y2_scr, tt_scr,
                    sems):
    i = pl.program_id(0)

    def issue(step, slot):
        base = step * TB

        def body(r, carry):
            _rows_copy(ys_hbm, y1_scr.at[slot], sems.at[slot], sl1_ref[base + r], r,
                       1).start(priority=0)
            _rows_copy(ys_hbm, y2_scr.at[slot], sems.at[slot], sl2_ref[base + r], r,
                       1).start(priority=1)
            return carry

        lax.fori_loop(0, TB, body, 0, unroll=8)

    @pl.when(i == 0)
    def _():
        issue(0, 0)

    @pl.when(i + 1 < pl.num_programs(0))
    def _():
        issue(i + 1, (i + 1) % 2)

    slot = i % 2
    _rows_copy(ys_hbm, y1_scr.at[slot], sems.at[slot], 0, 0, TB).wait()
    _rows_copy(ys_hbm, y2_scr.at[slot], sems.at[slot], 0, 0, TB).wait()
    cw = cw_ref[...]
    o_ref[...] = x1_ref[...] + cw[:, 0:1] * _load_token_tiles(y1_scr.at[slot], tt_scr, TB)
    o_ref[...] += cw[:, 1:2] * _load_token_tiles(y2_scr.at[slot], tt_scr, TB)


def _combine(x1, cw, ys, sl1, sl2):
    t = x1.shape[0]
    return pl.pallas_call(
        _combine_kernel,
        grid_spec=pltpu.PrefetchScalarGridSpec(
            num_scalar_prefetch=2,
            grid=(t // TB,),
            in_specs=[pl.BlockSpec((TB, D_MODEL), lambda i, *_: (i, 0)),
                      pl.BlockSpec((TB, LANES), lambda i, *_: (i, 0)),
                      pl.BlockSpec(memory_space=pl.ANY)],
            out_specs=pl.BlockSpec((TB, D_MODEL), lambda i, *_: (i, 0)),
            scratch_shapes=[pltpu.VMEM((2, TB * SUB, LANES), BF16),
                            pltpu.VMEM((2, TB * SUB, LANES), BF16),
                            pltpu.VMEM((TB * SUB, LANES), F32),
                            pltpu.SemaphoreType.DMA((2,))],
        ),
        out_shape=jax.ShapeDtypeStruct((t, D_MODEL), F32),
        compiler_params=_params(1),
        name="combine",
    )(sl1, sl2, x1, cw, ys)


def _lane_tile(v, reps):
    return jnp.tile(v.reshape(1, -1), (1, reps))


def kernel(x, positions, norm1_g, w_in, q_norm_g, k_norm_g, lambda_q1, lambda_k1, lambda_q2,
           lambda_k2, diff_out_norm_g, gla_w_gate2, gla_b_gate, gla_out_norm_g, w_out, norm2_g,
           w_router_group, b_router_group, w_router_expert, b_router_expert, w_gate_expert,
           w_up_expert, w_down_expert):
    batch, seq, d = x.shape
    t = batch * seq
    x2 = x.reshape(t, d)

    inv = ROPE_THETA ** (-jnp.arange(0, ROT_DIM, 2, dtype=F32) / ROT_DIM)
    lane_d = jnp.arange(LANES) % DIFF_QK_DIM
    invf = jnp.where(lane_d < ROT_DIM, inv[lane_d % (ROT_DIM // 2)], 0.0).reshape(1, LANES)
    pos_b = jnp.broadcast_to(positions.astype(F32).reshape(t, 1), (t, LANES))
    tabs = _rope_tables(pos_b, invf)

    w_in_t = jnp.swapaxes(w_in, 1, 2)[0]
    wa_t = jnp.pad(w_in_t[D_MAIN:, :], ((0, LANES - GLA_GATE_RANK), (0, 0)))
    proj, ga = _in_proj(x2, norm1_g, w_in_t, wa_t)

    a_out = _diff_attention(
        proj, tabs, _lane_tile(q_norm_g[0], 2), _lane_tile(k_norm_g[0], 2),
        lambda_q1, lambda_k1, lambda_q2, lambda_k2, diff_out_norm_g, batch, seq)
    w2p = jnp.pad(gla_w_gate2[0], ((0, LANES - GLA_GATE_RANK), (0, 0))).astype(BF16)
    g_out = _gla(proj, ga, w2p, gla_b_gate, gla_out_norm_g, batch, seq)

    n_r = N_GROUPS + N_EXPERTS
    wr = jnp.concatenate(
        [w_router_group[0], w_router_expert[0].transpose(1, 0, 2).reshape(d, N_EXPERTS)], axis=1)
    wr = jnp.pad(wr, ((0, 0), (0, LANES - n_r)))
    br = jnp.pad(jnp.concatenate([b_router_group[0], b_router_expert[0].reshape(-1)]),
                 (0, LANES - n_r)).reshape(1, LANES)
    x1, h2, logits = _out_proj(a_out, g_out, x2, w_out[0].astype(BF16), norm2_g,
                               wr.astype(BF16), br)

    slots, cw, counts = _routing(logits)
    sl1 = slots[:, 0]
    sl2 = slots[:, 1]
    n_rows = 2 * t + N_EXPERTS * TM_E
    max_tiles = n_rows // TM_E
    cnt = counts[0, :N_EXPERTS].astype(jnp.int32)
    tile_end = jnp.cumsum((cnt + (TM_E - 1)) // TM_E)
    n_tiles = tile_end[-1:]
    tile_idx = jnp.minimum(jnp.arange(max_tiles, dtype=jnp.int32), n_tiles[0] - 1)
    tile_exp = jnp.sum(tile_idx[:, None] >= tile_end[None, :], axis=1).astype(jnp.int32)
    row_end = (tile_end * TM_E).astype(jnp.int32)
    tile_first = jnp.concatenate(
        [jnp.ones((1,), jnp.int32), (tile_exp[1:] != tile_exp[:-1]).astype(jnp.int32)])
    tile_slot = (jnp.cumsum(tile_first) - 1) % 2
    nxt = tile_end[tile_exp]
    tile_next = jnp.where(nxt < n_tiles[0], tile_exp[jnp.minimum(nxt, max_tiles - 1)], -1)
    tile_meta = [a.astype(jnp.int32)
                 for a in (tile_idx, tile_exp, tile_first, tile_slot, tile_next, n_tiles)]

    xs = _dispatch(h2, sl1, sl2, row_end, n_rows)
    ys = _expert_mlp(xs, tile_meta,
                     w_gate_expert[0].reshape(N_EXPERTS, d, D_FF),
                     w_up_expert[0].reshape(N_EXPERTS, d, D_FF),
                     w_down_expert[0].reshape(N_EXPERTS, D_FF, d))
    out = _combine(x1, cw, ys, sl1, sl2)
    return out.reshape(batch, seq, d)
```

```python
import functools
import math

import jax
import jax.numpy as jnp
from jax import lax
from jax.experimental import pallas as pl
from jax.experimental.pallas import tpu as pltpu

D_MODEL = 2048
CHUNK = 64
DIFF_QK_DIM = 64
DIFF_V_DIM = 128
DIFF_HEADS = 8
ROT_DIM = 16
ROPE_THETA = 500000.0
GLA_HEADS = 4
GLA_V_DIM = 256
GLA_K_DIM = 128
GLA_GATE_RANK = 16
GLA_TAU = 16.0
N_GROUPS = 4
EXPERTS_PER_GROUP = 8
N_EXPERTS = N_GROUPS * EXPERTS_PER_GROUP
D_FF = 512
RMS_EPS = 1e-6
LAMBDA_INIT = 0.8 - 0.6 * math.exp(-0.3 * 0)
D_MAIN = 6144

LANES = 128
SUB = D_MODEL // LANES
VMEM_LIMIT = 56 * 1024 * 1024

TM_IN = 1024
TN_IN = 1024
TQ = 256
GB = 256
TM_OUT = 512
RB_OUT = 256
TB = 256
TB_D = 1024
TM_E = 256

F32 = jnp.float32
BF16 = jnp.bfloat16
HI = lax.Precision.HIGHEST


def _params(n_axes):
    return pltpu.CompilerParams(dimension_semantics=("arbitrary",) * n_axes,
                                vmem_limit_bytes=VMEM_LIMIT)


def _nt_dot(a, b):
    return lax.dot_general(a, b, (((1,), (1,)), ((), ())), preferred_element_type=F32)


def _tn_dot(a, b):
    return lax.dot_general(a, b, (((0,), (0,)), ((), ())), preferred_element_type=F32)


def _silu(x):
    return x * (1.0 / (1.0 + jnp.exp(-x)))


def _store_token_tiles(ref, val, scr, tok0=0):
    n = val.shape[0]
    for s in range(SUB):
        scr[pl.ds(s, n, stride=SUB), :] = val[:, s * LANES:(s + 1) * LANES]
    ref[pl.ds(tok0 * SUB, n * SUB), :] = scr[...].astype(BF16)


def _load_token_tiles(ref, scr, n, tok0=0):
    scr[...] = ref[pl.ds(tok0 * SUB, n * SUB), :].astype(F32)
    return jnp.concatenate([scr[pl.ds(s, n, stride=SUB), :] for s in range(SUB)], axis=1)


def _rope_table_kernel(pos_ref, invf_ref, c_ref, s1_ref, s2_ref):
    ang = pos_ref[...] * invf_ref[...]
    d = lax.broadcasted_iota(jnp.int32, ang.shape, 1) % DIFF_QK_DIM
    cos = jnp.cos(ang)
    sin = jnp.sin(ang)
    half = ROT_DIM // 2
    c_ref[...] = jnp.where(d < ROT_DIM, cos, 1.0)
    s1_ref[...] = jnp.where(d < half, -sin, 0.0)
    s2_ref[...] = jnp.where((d >= half) & (d < ROT_DIM), sin, 0.0)


def _rope_tables(pos_b, invf):
    t = pos_b.shape[0]
    tb = 1024
    spec = pl.BlockSpec((tb, LANES), lambda i: (i, 0))
    return pl.pallas_call(
        _rope_table_kernel,
        grid=(t // tb,),
        in_specs=[spec, pl.BlockSpec((1, LANES), lambda i: (0, 0))],
        out_specs=[spec, spec, spec],
        out_shape=[jax.ShapeDtypeStruct((t, LANES), F32)] * 3,
        compiler_params=_params(1),
        name="rope_tables",
    )(pos_b, invf)


def _in_proj_kernel(x_ref, g_ref, wt_ref, wat_ref, proj_ref, ga_ref, h_scr):
    @pl.when(pl.program_id(1) == 0)
    def _():
        def body(c, carry):
            rows = pl.ds(c * 256, 256)
            x = x_ref[rows, :]
            ms = jnp.mean(x * x, axis=-1, keepdims=True)
            h_scr[rows, :] = (x * lax.rsqrt(ms + RMS_EPS) * g_ref[...]).astype(BF16)
            return carry
        lax.fori_loop(0, TM_IN // 256, body, 0)
        ga_ref[...] = _nt_dot(h_scr[...], wat_ref[...].astype(BF16))

    proj_ref[...] = _nt_dot(h_scr[...], wt_ref[...].astype(BF16)).astype(BF16)


def _in_proj(x2, g1, w_in, wa):
    t = x2.shape[0]
    return pl.pallas_call(
        _in_proj_kernel,
        grid=(t // TM_IN, D_MAIN // TN_IN),
        in_specs=[
            pl.BlockSpec((TM_IN, D_MODEL), lambda i, j: (i, 0)),
            pl.BlockSpec((1, D_MODEL), lambda i, j: (0, 0)),
            pl.BlockSpec((TN_IN, D_MODEL), lambda i, j: (j, 0)),
            pl.BlockSpec((LANES, D_MODEL), lambda i, j: (0, 0)),
        ],
        out_specs=[
            pl.BlockSpec((TM_IN, TN_IN), lambda i, j: (i, j)),
            pl.BlockSpec((TM_IN, LANES), lambda i, j: (i, 0)),
        ],
        out_shape=[jax.ShapeDtypeStruct((t, D_MAIN), BF16),
                   jax.ShapeDtypeStruct((t, LANES), F32)],
        scratch_shapes=[pltpu.VMEM((TM_IN, D_MODEL), BF16)],
        compiler_params=_params(2),
        name="in_proj",
    )(x2, g1, w_in, wa)


def _norm_rope(x, g, c, s1, s2):
    lo = lax.broadcasted_iota(jnp.int32, x.shape, 1) < DIFF_QK_DIM
    x2 = x * x
    s_lo = jnp.sum(jnp.where(lo, x2, 0.0), axis=-1, keepdims=True)
    s_hi = jnp.sum(jnp.where(lo, 0.0, x2), axis=-1, keepdims=True)
    ms = jnp.where(lo, s_lo, s_hi) * (1.0 / DIFF_QK_DIM)
    y = x * lax.rsqrt(ms + RMS_EPS) * g
    half = ROT_DIM // 2
    return y * c + pltpu.roll(y, LANES - half, 1) * s1 + pltpu.roll(y, half, 1) * s2


def _attn_kernel(q_ref, k_ref, v_ref, c_ref, s1_ref, s2_ref, qg_ref, kg_ref,
                 lq1_ref, lk1_ref, lq2_ref, lk2_ref, og_ref, o_ref,
                 q1_scr, q2_scr, k_scr, v_scr, *bufs):
    s_len = q_ref.shape[0]
    s_bufs, e_bufs = bufs[0:4], bufs[4:8]
    lam = (jnp.exp(jnp.sum(lq1_ref[...] * lk1_ref[...], axis=-1, keepdims=True))
           - jnp.exp(jnp.sum(lq2_ref[...] * lk2_ref[...], axis=-1, keepdims=True))
           + LAMBDA_INIT)
    lo = lax.broadcasted_iota(jnp.int32, (TQ, LANES), 1) < DIFF_QK_DIM
    v_scr[:, 0:LANES] = v_ref[...]
    v_scr[:, LANES:] = jnp.ones((s_len, LANES), BF16)

    def prepare(i):
        rows = slice(i * TQ, (i + 1) * TQ)
        c, s1, s2 = c_ref[rows, :], s1_ref[rows, :], s2_ref[rows, :]
        qn = (_norm_rope(q_ref[rows, :].astype(F32), qg_ref[...], c, s1, s2)
              * (DIFF_QK_DIM ** -0.5 * math.log2(math.e)))
        q1_scr[rows, :] = jnp.where(lo, qn, 0.0).astype(BF16)
        q2_scr[rows, :] = jnp.where(lo, 0.0, qn).astype(BF16)
        k_scr[rows, :] = _norm_rope(k_ref[rows, :].astype(F32), kg_ref[...], c, s1,
                                    s2).astype(BF16)

    diag = (lax.broadcasted_iota(jnp.int32, (TQ, TQ), 1) // CHUNK
            <= lax.broadcasted_iota(jnp.int32, (TQ, TQ), 0) // CHUNK)

    n_tiles = s_len // TQ
    items = [(i, q_scr) for i in range(n_tiles) for q_scr in (q1_scr, q2_scr)]

    def scores(r):
        i, q_scr = items[r]
        s_scr = s_bufs[r % 4]
        q = q_scr[i * TQ:(i + 1) * TQ, :]
        n_off = i * TQ
        s_scr[:, n_off:n_off + TQ] = jnp.where(
            diag, _nt_dot(q, k_scr[n_off:n_off + TQ, :]), -jnp.inf)
        if n_off:
            s_scr[:, 0:n_off] = _nt_dot(q, k_scr[0:n_off, :])

    def softmax_pv(r):
        nk = (items[r][0] + 1) * TQ
        s_scr, e_scr = s_bufs[r % 4], e_bufs[r % 4]
        m = jnp.max(s_scr[:, 0:nk], axis=-1, keepdims=True)
        e_scr[:, 0:nk] = jnp.exp2(s_scr[:, 0:nk] - m).astype(BF16)
        acc = jnp.dot(e_scr[:, 0:nk], v_scr[0:nk, :], preferred_element_type=F32)
        return acc[:, 0:LANES] / acc[:, LANES:]

    prepare(0)
    scores(0)
    scores(1)
    for i in range(n_tiles):
        rows = slice(i * TQ, (i + 1) * TQ)
        if i + 1 < n_tiles:
            prepare(i + 1)
            scores(2 * i + 2)
        o1 = softmax_pv(2 * i)
        if i + 1 < n_tiles:
            scores(2 * i + 3)
        o = o1 - lam * softmax_pv(2 * i + 1)
        ms = jnp.mean(o * o, axis=-1, keepdims=True)
        y = o * lax.rsqrt(ms + RMS_EPS) * og_ref[...] * (1.0 - LAMBDA_INIT)
        o_ref[rows, :] = y.astype(BF16)


def _diff_attention(proj, tabs, qg, kg, lq1, lk1, lq2, lk2, og, batch, seq):
    c, s1, s2 = tabs
    h = DIFF_HEADS
    blk = lambda off: pl.BlockSpec((seq, LANES), lambda b, hh, off=off: (b, off + hh))
    tab = pl.BlockSpec((seq, LANES), lambda b, hh: (b, 0))
    vec = lambda n: pl.BlockSpec((1, n), lambda b, hh: (0, 0))
    return pl.pallas_call(
        _attn_kernel,
        grid=(batch, h),
        in_specs=[blk(0), blk(h), blk(2 * h), tab, tab, tab,
                  vec(LANES), vec(LANES), vec(DIFF_QK_DIM), vec(DIFF_QK_DIM),
                  vec(DIFF_QK_DIM), vec(DIFF_QK_DIM), vec(LANES)],
        out_specs=pl.BlockSpec((seq, LANES), lambda b, hh: (b, hh)),
        out_shape=jax.ShapeDtypeStruct((batch * seq, h * DIFF_V_DIM), BF16),
        scratch_shapes=[pltpu.VMEM((seq, LANES), BF16)] * 3
        + [pltpu.VMEM((seq, 2 * LANES), BF16)]
        + [pltpu.VMEM((TQ, seq), F32)] * 4 + [pltpu.VMEM((TQ, seq), BF16)] * 4,
        compiler_params=_params(2),
        name="diff_attention",
    )(proj, proj, proj, c, s1, s2, qg, kg, lq1, lk1, lq2, lk2, og)


def _split_dot(ones_bf, x):
    hi = x.astype(BF16)
    lo = (x - hi.astype(F32)).astype(BF16)
    return (jnp.dot(ones_bf, hi, preferred_element_type=F32)
            + jnp.dot(ones_bf, lo, preferred_element_type=F32))


def _gla_kernel(q_ref, k_ref, v_ref, r_ref, ga_ref, w2_ref, b2_ref, og_ref, o_ref,
                qin_scr, ut_scr, dec_scr, st_scr, acc_scr):
    s_len = q_ref.shape[0]
    cpg = GB // CHUNK
    ri = lax.broadcasted_iota(jnp.int32, (GB, GB), 0)
    ci = lax.broadcasted_iota(jnp.int32, (GB, GB), 1)
    same = (ri // CHUNK) == (ci // CHUNK)
    blk_ones = jnp.where(same, 1.0, 0.0).astype(BF16)
    tril = same & (ci <= ri)
    tri_ones = jnp.where(tril, 1.0, 0.0).astype(BF16)

    def phase_a(g, carry):
        rows = pl.ds(pl.multiple_of(g * GB, GB), GB)
        pre = jnp.dot(ga_ref[rows, :].astype(BF16), w2_ref[...],
                      preferred_element_type=F32) + b2_ref[...]
        la = -(jnp.maximum(-pre, 0.0) + jnp.log1p(jnp.exp(-jnp.abs(pre)))) * (1.0 / GLA_TAU)
        bc = _split_dot(tri_ones, la)
        bl = _split_dot(blk_ones, la)
        e_neg = jnp.exp(-bc)
        e_last = jnp.exp(bl)
        k = k_ref[rows, :].astype(F32)
        q_in = (q_ref[rows, :].astype(F32) * (GLA_K_DIM ** -0.5) * jnp.exp(bc)).astype(BF16)
        k_in = (k * e_neg).astype(BF16)
        k_dec = (k * (e_last * e_neg)).astype(BF16)
        qin_scr[rows, :] = q_in
        dec_scr[rows, :] = e_last
        v = v_ref[rows, :]
        att = jnp.where(tril, _nt_dot(q_in, k_in), 0.0).astype(BF16)
        acc_scr[rows, :] = jnp.dot(att, v, preferred_element_type=F32)
        for c in range(cpg):
            cr = slice(c * CHUNK, (c + 1) * CHUNK)
            ut_scr[g * cpg + c] = _tn_dot(v[cr, :], k_dec[cr, :])
        return carry

    lax.fori_loop(0, s_len // GB, phase_a, 0)

    def phase_b(c, st):
        st_scr[c] = st.astype(BF16)
        dec = dec_scr[pl.ds(pl.multiple_of(c * CHUNK, CHUNK), 1), :]
        return dec * st + ut_scr[c]

    lax.fori_loop(0, s_len // CHUNK, phase_b, jnp.zeros((GLA_V_DIM, GLA_K_DIM), F32))

    def phase_c(g, carry):
        rows = pl.ds(pl.multiple_of(g * GB, GB), GB)
        inter = [_nt_dot(qin_scr[pl.ds(pl.multiple_of(g * GB + c * CHUNK, CHUNK), CHUNK), :],
                         st_scr[g * cpg + c]) for c in range(cpg)]
        o = acc_scr[rows, :] + jnp.concatenate(inter, axis=0)
        ms = jnp.mean(o * o, axis=-1, keepdims=True)
        y = o * lax.rsqrt(ms + RMS_EPS) * og_ref[...]
        o_ref[rows, :] = (y * _silu(r_ref[rows, :].astype(F32))).astype(BF16)
        return carry

    lax.fori_loop(0, s_len // GB, phase_c, 0)


def _gla(proj, ga, w2p, b2, og, batch, seq):
    hq = 3 * DIFF_HEADS
    kblk = lambda off: pl.BlockSpec((seq, GLA_K_DIM), lambda b, hh, off=off: (b, off + hh))
    vblk = lambda off: pl.BlockSpec((seq, GLA_V_DIM), lambda b, hh, off=off: (b, off + hh))
    return pl.pallas_call(
        _gla_kernel,
        grid=(batch, GLA_HEADS),
        in_specs=[kblk(hq), kblk(hq + GLA_HEADS), vblk(16), vblk(16 + GLA_HEADS),
                  pl.BlockSpec((seq, LANES), lambda b, hh: (b, 0)),
                  pl.BlockSpec((LANES, GLA_K_DIM), lambda b, hh: (0, hh)),
                  pl.BlockSpec((1, GLA_K_DIM), lambda b, hh: (0, hh)),
                  pl.BlockSpec((1, GLA_V_DIM), lambda b, hh: (0, 0))],
        out_specs=pl.BlockSpec((seq, GLA_V_DIM), lambda b, hh: (b, hh)),
        out_shape=jax.ShapeDtypeStruct((batch * seq, GLA_HEADS * GLA_V_DIM), BF16),
        scratch_shapes=[pltpu.VMEM((seq, GLA_K_DIM), BF16),
                        pltpu.VMEM((seq // CHUNK, GLA_V_DIM, GLA_K_DIM), F32),
                        pltpu.VMEM((seq, GLA_K_DIM), F32),
                        pltpu.VMEM((seq // CHUNK, GLA_V_DIM, GLA_K_DIM), BF16),
                        pltpu.VMEM((seq, GLA_V_DIM), F32)],
        compiler_params=_params(2),
        name="gla",
    )(proj, proj, proj, proj, ga, w2p, b2, og)


def _out_proj_kernel(a_ref, g_ref, x_ref, wo_ref, g2_ref, wr_ref, br_ref,
                     x1_ref, h2_ref, lg_ref, tt_scr):
    half = a_ref.shape[1]

    def body(c, carry):
        r0 = pl.multiple_of(c * RB_OUT, RB_OUT)
        rows = pl.ds(r0, RB_OUT)
        mixed = (jnp.dot(a_ref[rows, :], wo_ref[0:half, :], preferred_element_type=F32)
                 + jnp.dot(g_ref[rows, :], wo_ref[half:, :], preferred_element_type=F32))
        x1 = x_ref[rows, :] + mixed
        x1_ref[rows, :] = x1
        ms = jnp.mean(x1 * x1, axis=-1, keepdims=True)
        h2 = x1 * lax.rsqrt(ms + RMS_EPS) * g2_ref[...]
        _store_token_tiles(h2_ref, h2, tt_scr, r0)
        lg_ref[rows, :] = jnp.dot(h2.astype(BF16), wr_ref[...],
                                  preferred_element_type=F32) + br_ref[...]
        return carry

    lax.fori_loop(0, TM_OUT // RB_OUT, body, 0)


def _out_proj(a_out, g_out, x2, wo, g2, wr, br):
    t = x2.shape[0]
    half = a_out.shape[1]
    row = lambda n: pl.BlockSpec((TM_OUT, n), lambda i: (i, 0))
    full = lambda r, n: pl.BlockSpec((r, n), lambda i: (0, 0))
    return pl.pallas_call(
        _out_proj_kernel,
        grid=(t // TM_OUT,),
        in_specs=[row(half), row(half), row(D_MODEL), full(D_MODEL, D_MODEL),
                  full(1, D_MODEL), full(D_MODEL, LANES), full(1, LANES)],
        out_specs=[row(D_MODEL), pl.BlockSpec((TM_OUT * SUB, LANES), lambda i: (i, 0)),
                   row(LANES)],
        out_shape=[jax.ShapeDtypeStruct((t, D_MODEL), F32),
                   jax.ShapeDtypeStruct((t * SUB, LANES), BF16),
                   jax.ShapeDtypeStruct((t, LANES), F32)],
        scratch_shapes=[pltpu.VMEM((RB_OUT * SUB, LANES), F32)],
        compiler_params=_params(1),
        name="out_proj_router",
    )(a_out, g_out, x2, wo, g2, wr, br)


def _first_argmax(vals, lane_f, valid):
    v = jnp.where(valid, vals, -jnp.inf)
    m = jnp.max(v, axis=-1, keepdims=True)
    idx = jnp.min(jnp.where(valid & (v == m), lane_f, float(LANES)), axis=-1, keepdims=True)
    return m, idx


def _routing_kernel(lg_ref, slots_ref, cw_ref, cnt_ref, e1_scr, e2_scr, r1_scr, r2_scr):
    t = lg_ref.shape[0]
    lane = lax.broadcasted_iota(jnp.int32, (TB, LANES), 1)
    lane_f = lane.astype(F32)
    ri = lax.broadcasted_iota(jnp.int32, (TB, TB), 0)
    ci = lax.broadcasted_iota(jnp.int32, (TB, TB), 1)
    strict_lower = jnp.where(ci < ri, 1.0, 0.0).astype(BF16)

    def softmax_in(lg, valid):
        m = jnp.max(jnp.where(valid, lg, -jnp.inf), axis=-1, keepdims=True)
        ex = jnp.where(valid, jnp.exp(lg - m), 0.0)
        return ex / jnp.sum(ex, axis=-1, keepdims=True)

    def phase1(b, carry):
        rows = pl.ds(pl.multiple_of(b * TB, TB), TB)
        lg = lg_ref[rows, :]
        is_g = lane < N_GROUPS
        pg_sel, g_sel = _first_argmax(softmax_in(lg, is_g), lane_f, is_g)
        e_lo = N_GROUPS + g_sel * EXPERTS_PER_GROUP
        in_grp = (lane_f >= e_lo) & (lane_f < e_lo + EXPERTS_PER_GROUP)
        pe = softmax_in(lg, in_grp)
        v1, l1 = _first_argmax(pe, lane_f, in_grp)
        v2, l2 = _first_argmax(pe, lane_f, in_grp & (lane_f != l1))
        tot = v1 + v2
        c1 = (v1 / tot) * pg_sel
        c2 = (v2 / tot) * pg_sel
        e1 = (l1 - N_GROUPS).astype(jnp.int32)
        e2 = (l2 - N_GROUPS).astype(jnp.int32)
        oh1 = lane == e1
        oh2 = lane == e2
        a = jnp.where(oh1 | oh2, 1.0, 0.0)
        rank = jnp.dot(strict_lower, a.astype(BF16), preferred_element_type=F32) + carry
        e1_scr[rows, :] = jnp.broadcast_to(e1, (TB, LANES))
        e2_scr[rows, :] = jnp.broadcast_to(e2, (TB, LANES))
        r1_scr[rows, :] = jnp.broadcast_to(
            jnp.sum(jnp.where(oh1, rank, 0.0), axis=-1, keepdims=True), (TB, LANES))
        r2_scr[rows, :] = jnp.broadcast_to(
            jnp.sum(jnp.where(oh2, rank, 0.0), axis=-1, keepdims=True), (TB, LANES))
        cw_ref[rows, :] = jnp.where(lane == 0, c1, jnp.where(lane == 1, c2, 0.0))
        return carry + jnp.sum(a, axis=0, keepdims=True)

    counts = lax.fori_loop(0, t // TB, phase1, jnp.zeros((1, LANES), F32))
    cnt_ref[...] = jnp.broadcast_to(counts, cnt_ref.shape)
    n_tiles = jnp.floor((counts + (TM_E - 1)) * (1.0 / TM_E))
    ui = lax.broadcasted_iota(jnp.int32, (LANES, LANES), 0)
    uj = lax.broadcasted_iota(jnp.int32, (LANES, LANES), 1)
    strict_upper = jnp.where(ui < uj, 1.0, 0.0).astype(BF16)
    tile_off = jnp.dot(jnp.broadcast_to(n_tiles, (8, LANES)).astype(BF16), strict_upper,
                       preferred_element_type=F32)[0:1, :]
    row_off = tile_off * TM_E

    def phase2(b, carry):
        rows = pl.ds(pl.multiple_of(b * TB, TB), TB)
        off1 = jnp.sum(jnp.where(lane == e1_scr[rows, :], row_off, 0.0), axis=-1, keepdims=True)
        off2 = jnp.sum(jnp.where(lane == e2_scr[rows, :], row_off, 0.0), axis=-1, keepdims=True)
        sl1 = (off1 + r1_scr[rows, 0:1]).astype(jnp.int32)
        sl2 = (off2 + r2_scr[rows, 0:1]).astype(jnp.int32)
        slots_ref[rows, :] = jnp.where(lane == 0, sl1, jnp.where(lane == 1, sl2, 0))
        return carry

    lax.fori_loop(0, t // TB, phase2, 0)


def _routing(logits):
    t = logits.shape[0]
    return pl.pallas_call(
        _routing_kernel,
        out_shape=[jax.ShapeDtypeStruct((t, LANES), jnp.int32),
                   jax.ShapeDtypeStruct((t, LANES), F32),
                   jax.ShapeDtypeStruct((8, LANES), F32)],
        scratch_shapes=[pltpu.VMEM((t, LANES), jnp.int32), pltpu.VMEM((t, LANES), jnp.int32),
                        pltpu.VMEM((t, LANES), F32), pltpu.VMEM((t, LANES), F32)],
        compiler_params=pltpu.CompilerParams(vmem_limit_bytes=VMEM_LIMIT),
        name="routing",
    )(logits)


def _rows_copy(src, dst, sem, src_tok, dst_tok, n):
    first = lambda tok: tok * SUB if isinstance(tok, int) else pl.multiple_of(tok * SUB, SUB)
    s0 = first(src_tok)
    d0 = first(dst_tok)
    return pltpu.make_async_copy(src.at[pl.ds(s0, n * SUB), :], dst.at[pl.ds(d0, n * SUB), :], sem)


def _dispatch_kernel(sl1_ref, sl2_ref, ends_ref, h2_ref, xs_hbm, zero_scr, sem):
    step = pl.program_id(0)

    @pl.when(step == 0)
    def _():
        zero_scr[...] = jnp.zeros_like(zero_scr)
        for e in range(N_EXPERTS):
            start = jnp.maximum(ends_ref[e] - TM_E, 0)
            _rows_copy(zero_scr, xs_hbm, sem, 0, start, TM_E).start()
        for e in range(N_EXPERTS):
            _rows_copy(zero_scr, xs_hbm, sem, 0, 0, TM_E).wait()

    base = step * TB_D

    def issue(r, carry):
        _rows_copy(h2_ref, xs_hbm, sem, r, sl1_ref[base + r], 1).start(priority=0)
        _rows_copy(h2_ref, xs_hbm, sem, r, sl2_ref[base + r], 1).start(priority=1)
        return carry

    lax.fori_loop(0, TB_D, issue, 0, unroll=8)
    _rows_copy(h2_ref, xs_hbm, sem, 0, 0, TB_D).wait()
    _rows_copy(h2_ref, xs_hbm, sem, 0, 0, TB_D).wait()


def _dispatch(h2, sl1, sl2, ends, n_rows):
    t = h2.shape[0] // SUB
    return pl.pallas_call(
        _dispatch_kernel,
        grid_spec=pltpu.PrefetchScalarGridSpec(
            num_scalar_prefetch=3,
            grid=(t // TB_D,),
            in_specs=[pl.BlockSpec((TB_D * SUB, LANES), lambda i, *_: (i, 0))],
            out_specs=pl.BlockSpec(memory_space=pl.ANY),
            scratch_shapes=[pltpu.VMEM((TM_E * SUB, LANES), BF16), pltpu.SemaphoreType.DMA(())],
        ),
        out_shape=jax.ShapeDtypeStruct((n_rows * SUB, LANES), BF16),
        compiler_params=_params(1),
        name="dispatch",
    )(sl1, sl2, ends, h2)


def _expert_kernel(tidx_ref, texp_ref, tfirst_ref, tslot_ref, tnext_ref, nt_ref,
                   xs_ref, wg_hbm, wu_hbm, wd_hbm, ys_ref,
                   wg_buf, wu_buf, wd_buf, wg_bf, wu_bf, wd_bf, tt_scr, sems):
    j = pl.program_id(0)

    def weight_copies(e, slot):
        return [pltpu.make_async_copy(hbm.at[e], buf.at[slot], sems.at[slot, k])
                for k, (hbm, buf) in enumerate(((wg_hbm, wg_buf), (wu_hbm, wu_buf),
                                                (wd_hbm, wd_buf)))]

    @pl.when(j < nt_ref[0])
    def _():
        slot = tslot_ref[j]

        @pl.when(tfirst_ref[j] == 1)
        def _():
            @pl.when(j == 0)
            def _():
                for c in weight_copies(texp_ref[j], slot):
                    c.start()
            for c in weight_copies(texp_ref[j], slot):
                c.wait()

            @pl.when(tnext_ref[j] >= 0)
            def _():
                for c in weight_copies(tnext_ref[j], 1 - slot):
                    c.start()
            wg_bf[...] = wg_buf[slot].astype(BF16)
            wu_bf[...] = wu_buf[slot].astype(BF16)
            wd_bf[...] = wd_buf[slot].astype(BF16)

        xb = _load_token_tiles(xs_ref, tt_scr, TM_E).astype(BF16)
        g = jnp.dot(xb, wg_bf[...], preferred_element_type=F32)
        u = jnp.dot(xb, wu_bf[...], preferred_element_type=F32)
        h = (_silu(g) * u).astype(BF16)
        _store_token_tiles(ys_ref, jnp.dot(h, wd_bf[...], preferred_element_type=F32), tt_scr)


def _expert_mlp(xs, tile_meta, wg, wu, wd):
    n_rows = xs.shape[0] // SUB
    tile = pl.BlockSpec((TM_E * SUB, LANES), lambda j, ti, *_: (ti[j], 0))
    hbm = pl.BlockSpec(memory_space=pl.ANY)
    return pl.pallas_call(
        _expert_kernel,
        grid_spec=pltpu.PrefetchScalarGridSpec(
            num_scalar_prefetch=len(tile_meta),
            grid=(n_rows // TM_E,),
            in_specs=[tile, hbm, hbm, hbm],
            out_specs=tile,
            scratch_shapes=[pltpu.VMEM((2, D_MODEL, D_FF), F32),
                            pltpu.VMEM((2, D_MODEL, D_FF), F32),
                            pltpu.VMEM((2, D_FF, D_MODEL), F32),
                            pltpu.VMEM((D_MODEL, D_FF), BF16),
                            pltpu.VMEM((D_MODEL, D_FF), BF16),
                            pltpu.VMEM((D_FF, D_MODEL), BF16),
                            pltpu.VMEM((TM_E * SUB, LANES), F32),
                            pltpu.SemaphoreType.DMA((2, 3))],
        ),
        out_shape=jax.ShapeDtypeStruct((n_rows * SUB, LANES), BF16),
        compiler_params=_params(1),
        name="expert_mlp",
    )(*tile_meta, xs, wg, wu, wd)


def _combine_kernel(sl1_ref, sl2_ref, x1_ref, cw_ref, ys_hbm, o_ref, y1_scr, y2_scr, tt_scr,
                    sems):
    i = pl.program_id(0)

    def issue(step, slot):
        base = step * TB

        def body(r, carry):
            _rows_copy(ys_hbm, y1_scr.at[slot], sems.at[slot], sl1_ref[base + r], r,
                       1).start(priority=0)
            _rows_copy(ys_hbm, y2_scr.at[slot], sems.at[slot], sl2_ref[base + r], r,
                       1).start(priority=1)
            return carry

        lax.fori_loop(0, TB, body, 0, unroll=8)

    @pl.when(i == 0)
    def _():
        issue(0, 0)

    @pl.when(i + 1 < pl.num_programs(0))
    def _():
        issue(i + 1, (i + 1) % 2)

    slot = i % 2
    _rows_copy(ys_hbm, y1_scr.at[slot], sems.at[slot], 0, 0, TB).wait()
    _rows_copy(ys_hbm, y2_scr.at[slot], sems.at[slot], 0, 0, TB).wait()
    cw = cw_ref[...]
    o_ref[...] = x1_ref[...] + cw[:, 0:1] * _load_token_tiles(y1_scr.at[slot], tt_scr, TB)
    o_ref[...] += cw[:, 1:2] * _load_token_tiles(y2_scr.at[slot], tt_scr, TB)


def _combine(x1, cw, ys, sl1, sl2):
    t = x1.shape[0]
    return pl.pallas_call(
        _combine_kernel,
        grid_spec=pltpu.PrefetchScalarGridSpec(
            num_scalar_prefetch=2,
            grid=(t // TB,),
            in_specs=[pl.BlockSpec((TB, D_MODEL), lambda i, *_: (i, 0)),
                      pl.BlockSpec((TB, LANES), lambda i, *_: (i, 0)),
                      pl.BlockSpec(memory_space=pl.ANY)],
            out_specs=pl.BlockSpec((TB, D_MODEL), lambda i, *_: (i, 0)),
            scratch_shapes=[pltpu.VMEM((2, TB * SUB, LANES), BF16),
                            pltpu.VMEM((2, TB * SUB, LANES), BF16),
                            pltpu.VMEM((TB * SUB, LANES), F32),
                            pltpu.SemaphoreType.DMA((2,))],
        ),
        out_shape=jax.ShapeDtypeStruct((t, D_MODEL), F32),
        compiler_params=_params(1),
        name="combine",
    )(sl1, sl2, x1, cw, ys)


def _lane_tile(v, reps):
    return jnp.tile(v.reshape(1, -1), (1, reps))


def kernel(x, positions, norm1_g, w_in, q_norm_g, k_norm_g, lambda_q1, lambda_k1, lambda_q2,
           lambda_k2, diff_out_norm_g, gla_w_gate2, gla_b_gate, gla_out_norm_g, w_out, norm2_g,
           w_router_group, b_router_group, w_router_expert, b_router_expert, w_gate_expert,
           w_up_expert, w_down_expert):
    batch, seq, d = x.shape
    t = batch * seq
    x2 = x.reshape(t, d)

    inv = ROPE_THETA ** (-jnp.arange(0, ROT_DIM, 2, dtype=F32) / ROT_DIM)
    lane_d = jnp.arange(LANES) % DIFF_QK_DIM
    invf = jnp.where(lane_d < ROT_DIM, inv[lane_d % (ROT_DIM // 2)], 0.0).reshape(1, LANES)
    pos_b = jnp.broadcast_to(positions.astype(F32).reshape(t, 1), (t, LANES))
    tabs = _rope_tables(pos_b, invf)

    w_in_t = jnp.swapaxes(w_in, 1, 2)[0]
    wa_t = jnp.pad(w_in_t[D_MAIN:, :], ((0, LANES - GLA_GATE_RANK), (0, 0)))
    proj, ga = _in_proj(x2, norm1_g, w_in_t, wa_t)

    a_out = _diff_attention(
        proj, tabs, _lane_tile(q_norm_g[0], 2), _lane_tile(k_norm_g[0], 2),
        lambda_q1, lambda_k1, lambda_q2, lambda_k2, diff_out_norm_g, batch, seq)
    w2p = jnp.pad(gla_w_gate2[0], ((0, LANES - GLA_GATE_RANK), (0, 0))).astype(BF16)
    g_out = _gla(proj, ga, w2p, gla_b_gate, gla_out_norm_g, batch, seq)

    n_r = N_GROUPS + N_EXPERTS
    wr = jnp.concatenate(
        [w_router_group[0], w_router_expert[0].transpose(1, 0, 2).reshape(d, N_EXPERTS)], axis=1)
    wr = jnp.pad(wr, ((0, 0), (0, LANES - n_r)))
    br = jnp.pad(jnp.concatenate([b_router_group[0], b_router_expert[0].reshape(-1)]),
                 (0, LANES - n_r)).reshape(1, LANES)
    x1, h2, logits = _out_proj(a_out, g_out, x2, w_out[0].astype(BF16), norm2_g,
                               wr.astype(BF16), br)

    slots, cw, counts = _routing(logits)
    sl1 = slots[:, 0]
    sl2 = slots[:, 1]
    n_rows = 2 * t + N_EXPERTS * TM_E
    max_tiles = n_rows // TM_E
    cnt = counts[0, :N_EXPERTS].astype(jnp.int32)
    tile_end = jnp.cumsum((cnt + (TM_E - 1)) // TM_E)
    n_tiles = tile_end[-1:]
    tile_idx = jnp.minimum(jnp.arange(max_tiles, dtype=jnp.int32), n_tiles[0] - 1)
    tile_exp = jnp.sum(tile_idx[:, None] >= tile_end[None, :], axis=1).astype(jnp.int32)
    row_end = (tile_end * TM_E).astype(jnp.int32)
    tile_first = jnp.concatenate(
        [jnp.ones((1,), jnp.int32), (tile_exp[1:] != tile_exp[:-1]).astype(jnp.int32)])
    tile_slot = (jnp.cumsum(tile_first) - 1) % 2
    nxt = tile_end[tile_exp]
    tile_next = jnp.where(nxt < n_tiles[0], tile_exp[jnp.minimum(nxt, max_tiles - 1)], -1)
    tile_meta = [a.astype(jnp.int32)
                 for a in (tile_idx, tile_exp, tile_first, tile_slot, tile_next, n_tiles)]

    xs = _dispatch(h2, sl1, sl2, row_end, n_rows)
    ys = _expert_mlp(xs, tile_meta,
                     w_gate_expert[0].reshape(N_EXPERTS, d, D_FF),
                     w_up_expert[0].reshape(N_EXPERTS, d, D_FF),
                     w_down_expert[0].reshape(N_EXPERTS, D_FF, d))
    out = _combine(x1, cw, ys, sl1, sl2)
    return out.reshape(batch, seq, d)
```

```python
import functools
import math

import jax
import jax.numpy as jnp
from jax import lax
from jax.experimental import pallas as pl
from jax.experimental.pallas import tpu as pltpu

D_MODEL = 2048
CHUNK = 64
DIFF_QK_DIM = 64
DIFF_V_DIM = 128
DIFF_HEADS = 8
ROT_DIM = 16
ROPE_THETA = 500000.0
GLA_HEADS = 4
GLA_V_DIM = 256
GLA_K_DIM = 128
GLA_GATE_RANK = 16
GLA_TAU = 16.0
N_GROUPS = 4
EXPERTS_PER_GROUP = 8
N_EXPERTS = N_GROUPS * EXPERTS_PER_GROUP
D_FF = 512
RMS_EPS = 1e-6
LAMBDA_INIT = 0.8 - 0.6 * math.exp(-0.3 * 0)
D_MAIN = 6144
GROUP_ROWS = 8
N_LOGIT_ROWS = GROUP_ROWS + N_EXPERTS

LANES = 128
SUB = D_MODEL // LANES
VMEM_LIMIT = 56 * 1024 * 1024

TM_IN = 1024
TN_IN = 1024
TQ = 256
GB = 256
TM_OUT = 512
RB_OUT = 256
TB = 256
TB_D = 1024
TM_E = 256

F32 = jnp.float32
BF16 = jnp.bfloat16
HI = lax.Precision.HIGHEST


def _params(n_axes):
    return pltpu.CompilerParams(dimension_semantics=("arbitrary",) * n_axes,
                                vmem_limit_bytes=VMEM_LIMIT)


def _nt_dot(a, b):
    return lax.dot_general(a, b, (((1,), (1,)), ((), ())), preferred_element_type=F32)


def _tn_dot(a, b):
    return lax.dot_general(a, b, (((0,), (0,)), ((), ())), preferred_element_type=F32)


def _silu(x):
    return x * (1.0 / (1.0 + jnp.exp(-x)))


def _store_token_tiles(ref, val, scr, tok0=0):
    n = val.shape[0]
    for s in range(SUB):
        scr[pl.ds(s, n, stride=SUB), :] = val[:, s * LANES:(s + 1) * LANES]
    ref[pl.ds(tok0 * SUB, n * SUB), :] = scr[...].astype(BF16)


def _load_token_tiles(ref, scr, n, tok0=0):
    scr[...] = ref[pl.ds(tok0 * SUB, n * SUB), :].astype(F32)
    return jnp.concatenate([scr[pl.ds(s, n, stride=SUB), :] for s in range(SUB)], axis=1)


def _rope_table_kernel(pos_ref, invf_ref, c_ref, s1_ref, s2_ref):
    ang = pos_ref[...] * invf_ref[...]
    d = lax.broadcasted_iota(jnp.int32, ang.shape, 1) % DIFF_QK_DIM
    cos = jnp.cos(ang)
    sin = jnp.sin(ang)
    half = ROT_DIM // 2
    c_ref[...] = jnp.where(d < ROT_DIM, cos, 1.0)
    s1_ref[...] = jnp.where(d < half, -sin, 0.0)
    s2_ref[...] = jnp.where((d >= half) & (d < ROT_DIM), sin, 0.0)


def _rope_tables(pos_b, invf):
    t = pos_b.shape[0]
    tb = 1024
    spec = pl.BlockSpec((tb, LANES), lambda i: (i, 0))
    return pl.pallas_call(
        _rope_table_kernel,
        grid=(t // tb,),
        in_specs=[spec, pl.BlockSpec((1, LANES), lambda i: (0, 0))],
        out_specs=[spec, spec, spec],
        out_shape=[jax.ShapeDtypeStruct((t, LANES), F32)] * 3,
        compiler_params=_params(1),
        name="rope_tables",
    )(pos_b, invf)


def _in_proj_kernel(x_ref, g_ref, wt_ref, wat_ref, proj_ref, ga_ref, h_scr):
    @pl.when(pl.program_id(1) == 0)
    def _():
        def body(c, carry):
            rows = pl.ds(c * 256, 256)
            x = x_ref[rows, :]
            ms = jnp.mean(x * x, axis=-1, keepdims=True)
            h_scr[rows, :] = (x * lax.rsqrt(ms + RMS_EPS) * g_ref[...]).astype(BF16)
            return carry
        lax.fori_loop(0, TM_IN // 256, body, 0)
        ga_ref[...] = _nt_dot(h_scr[...], wat_ref[...].astype(BF16))

    proj_ref[...] = _nt_dot(h_scr[...], wt_ref[...].astype(BF16)).astype(BF16)


def _in_proj(x2, g1, w_in, wa):
    t = x2.shape[0]
    return pl.pallas_call(
        _in_proj_kernel,
        grid=(t // TM_IN, D_MAIN // TN_IN),
        in_specs=[
            pl.BlockSpec((TM_IN, D_MODEL), lambda i, j: (i, 0)),
            pl.BlockSpec((1, D_MODEL), lambda i, j: (0, 0)),
            pl.BlockSpec((TN_IN, D_MODEL), lambda i, j: (j, 0)),
            pl.BlockSpec((LANES, D_MODEL), lambda i, j: (0, 0)),
        ],
        out_specs=[
            pl.BlockSpec((TM_IN, TN_IN), lambda i, j: (i, j)),
            pl.BlockSpec((TM_IN, LANES), lambda i, j: (i, 0)),
        ],
        out_shape=[jax.ShapeDtypeStruct((t, D_MAIN), BF16),
                   jax.ShapeDtypeStruct((t, LANES), F32)],
        scratch_shapes=[pltpu.VMEM((TM_IN, D_MODEL), BF16)],
        compiler_params=_params(2),
        name="in_proj",
    )(x2, g1, w_in, wa)


def _norm_rope(x, g, c, s1, s2):
    lo = lax.broadcasted_iota(jnp.int32, x.shape, 1) < DIFF_QK_DIM
    x2 = x * x
    s_lo = jnp.sum(jnp.where(lo, x2, 0.0), axis=-1, keepdims=True)
    s_hi = jnp.sum(jnp.where(lo, 0.0, x2), axis=-1, keepdims=True)
    ms = jnp.where(lo, s_lo, s_hi) * (1.0 / DIFF_QK_DIM)
    y = x * lax.rsqrt(ms + RMS_EPS) * g
    half = ROT_DIM // 2
    return y * c + pltpu.roll(y, LANES - half, 1) * s1 + pltpu.roll(y, half, 1) * s2


def _attn_kernel(q_ref, k_ref, v_ref, c_ref, s1_ref, s2_ref, qg_ref, kg_ref,
                 lq1_ref, lk1_ref, lq2_ref, lk2_ref, og_ref, o_ref,
                 q1_scr, q2_scr, k_scr, v_scr, *bufs):
    s_len = q_ref.shape[0]
    s_bufs, e_bufs = bufs[0:4], bufs[4:8]
    lam = (jnp.exp(jnp.sum(lq1_ref[...] * lk1_ref[...], axis=-1, keepdims=True))
           - jnp.exp(jnp.sum(lq2_ref[...] * lk2_ref[...], axis=-1, keepdims=True))
           + LAMBDA_INIT)
    lo = lax.broadcasted_iota(jnp.int32, (TQ, LANES), 1) < DIFF_QK_DIM
    v_scr[:, 0:LANES] = v_ref[...]
    v_scr[:, LANES:] = jnp.ones((s_len, LANES), BF16)

    def prepare(i):
        rows = slice(i * TQ, (i + 1) * TQ)
        c, s1, s2 = c_ref[rows, :], s1_ref[rows, :], s2_ref[rows, :]
        qn = (_norm_rope(q_ref[rows, :].astype(F32), qg_ref[...], c, s1, s2)
              * (DIFF_QK_DIM ** -0.5 * math.log2(math.e)))
        q1_scr[rows, :] = jnp.where(lo, qn, 0.0).astype(BF16)
        q2_scr[rows, :] = jnp.where(lo, 0.0, qn).astype(BF16)
        k_scr[rows, :] = _norm_rope(k_ref[rows, :].astype(F32), kg_ref[...], c, s1,
                                    s2).astype(BF16)

    diag = (lax.broadcasted_iota(jnp.int32, (TQ, TQ), 1) // CHUNK
            <= lax.broadcasted_iota(jnp.int32, (TQ, TQ), 0) // CHUNK)

    n_tiles = s_len // TQ
    items = [(i, q_scr) for i in range(n_tiles) for q_scr in (q1_scr, q2_scr)]

    def scores(r):
        i, q_scr = items[r]
        s_scr = s_bufs[r % 4]
        q = q_scr[i * TQ:(i + 1) * TQ, :]
        n_off = i * TQ
        s_scr[:, n_off:n_off + TQ] = jnp.where(
            diag, _nt_dot(q, k_scr[n_off:n_off + TQ, :]), -jnp.inf)
        if n_off:
            s_scr[:, 0:n_off] = _nt_dot(q, k_scr[0:n_off, :])

    def softmax_pv(r):
        nk = (items[r][0] + 1) * TQ
        s_scr, e_scr = s_bufs[r % 4], e_bufs[r % 4]
        m = jnp.max(s_scr[:, 0:nk], axis=-1, keepdims=True)
        e_scr[:, 0:nk] = jnp.exp2(s_scr[:, 0:nk] - m).astype(BF16)
        acc = jnp.dot(e_scr[:, 0:nk], v_scr[0:nk, :], preferred_element_type=F32)
        return acc[:, 0:LANES] / acc[:, LANES:]

    prepare(0)
    scores(0)
    scores(1)
    for i in range(n_tiles):
        rows = slice(i * TQ, (i + 1) * TQ)
        if i + 1 < n_tiles:
            prepare(i + 1)
            scores(2 * i + 2)
        o1 = softmax_pv(2 * i)
        if i + 1 < n_tiles:
            scores(2 * i + 3)
        o = o1 - lam * softmax_pv(2 * i + 1)
        ms = jnp.mean(o * o, axis=-1, keepdims=True)
        y = o * lax.rsqrt(ms + RMS_EPS) * og_ref[...] * (1.0 - LAMBDA_INIT)
        o_ref[rows, :] = y.astype(BF16)


def _diff_attention(proj, tabs, qg, kg, lq1, lk1, lq2, lk2, og, batch, seq):
    c, s1, s2 = tabs
    h = DIFF_HEADS
    blk = lambda off: pl.BlockSpec((seq, LANES), lambda b, hh, off=off: (b, off + hh))
    tab = pl.BlockSpec((seq, LANES), lambda b, hh: (b, 0))
    vec = lambda n: pl.BlockSpec((1, n), lambda b, hh: (0, 0))
    return pl.pallas_call(
        _attn_kernel,
        grid=(batch, h),
        in_specs=[blk(0), blk(h), blk(2 * h), tab, tab, tab,
                  vec(LANES), vec(LANES), vec(DIFF_QK_DIM), vec(DIFF_QK_DIM),
                  vec(DIFF_QK_DIM), vec(DIFF_QK_DIM), vec(LANES)],
        out_specs=pl.BlockSpec((seq, LANES), lambda b, hh: (b, hh)),
        out_shape=jax.ShapeDtypeStruct((batch * seq, h * DIFF_V_DIM), BF16),
        scratch_shapes=[pltpu.VMEM((seq, LANES), BF16)] * 3
        + [pltpu.VMEM((seq, 2 * LANES), BF16)]
        + [pltpu.VMEM((TQ, seq), F32)] * 4 + [pltpu.VMEM((TQ, seq), BF16)] * 4,
        compiler_params=_params(2),
        name="diff_attention",
    )(proj, proj, proj, c, s1, s2, qg, kg, lq1, lk1, lq2, lk2, og)


def _split_dot(ones_bf, x):
    hi = x.astype(BF16)
    lo = (x - hi.astype(F32)).astype(BF16)
    return (jnp.dot(ones_bf, hi, preferred_element_type=F32)
            + jnp.dot(ones_bf, lo, preferred_element_type=F32))


def _gla_kernel(q_ref, k_ref, v_ref, r_ref, ga_ref, w2_ref, b2_ref, og_ref, o_ref,
                qin_scr, ut_scr, dec_scr, st_scr, acc_scr):
    s_len = q_ref.shape[0]
    cpg = GB // CHUNK
    ri = lax.broadcasted_iota(jnp.int32, (GB, GB), 0)
    ci = lax.broadcasted_iota(jnp.int32, (GB, GB), 1)
    same = (ri // CHUNK) == (ci // CHUNK)
    blk_ones = jnp.where(same, 1.0, 0.0).astype(BF16)
    tril = same & (ci <= ri)
    tri_ones = jnp.where(tril, 1.0, 0.0).astype(BF16)

    def phase_a(g, carry):
        rows = pl.ds(pl.multiple_of(g * GB, GB), GB)
        pre = jnp.dot(ga_ref[rows, :].astype(BF16), w2_ref[...],
                      preferred_element_type=F32) + b2_ref[...]
        la = -(jnp.maximum(-pre, 0.0) + jnp.log1p(jnp.exp(-jnp.abs(pre)))) * (1.0 / GLA_TAU)
        bc = _split_dot(tri_ones, la)
        bl = _split_dot(blk_ones, la)
        e_neg = jnp.exp(-bc)
        e_last = jnp.exp(bl)
        k = k_ref[rows, :].astype(F32)
        q_in = (q_ref[rows, :].astype(F32) * (GLA_K_DIM ** -0.5) * jnp.exp(bc)).astype(BF16)
        k_in = (k * e_neg).astype(BF16)
        k_dec = (k * (e_last * e_neg)).astype(BF16)
        qin_scr[rows, :] = q_in
        dec_scr[rows, :] = e_last
        v = v_ref[rows, :]
        att = jnp.where(tril, _nt_dot(q_in, k_in), 0.0).astype(BF16)
        acc_scr[rows, :] = jnp.dot(att, v, preferred_element_type=F32)
        for c in range(cpg):
            cr = slice(c * CHUNK, (c + 1) * CHUNK)
            ut_scr[g * cpg + c] = _tn_dot(v[cr, :], k_dec[cr, :])
        return carry

    lax.fori_loop(0, s_len // GB, phase_a, 0)

    def phase_b(c, st):
        st_scr[c] = st.astype(BF16)
        dec = dec_scr[pl.ds(pl.multiple_of(c * CHUNK, CHUNK), 1), :]
        return dec * st + ut_scr[c]

    lax.fori_loop(0, s_len // CHUNK, phase_b, jnp.zeros((GLA_V_DIM, GLA_K_DIM), F32))

    def phase_c(g, carry):
        rows = pl.ds(pl.multiple_of(g * GB, GB), GB)
        inter = [_nt_dot(qin_scr[pl.ds(pl.multiple_of(g * GB + c * CHUNK, CHUNK), CHUNK), :],
                         st_scr[g * cpg + c]) for c in range(cpg)]
        o = acc_scr[rows, :] + jnp.concatenate(inter, axis=0)
        ms = jnp.mean(o * o, axis=-1, keepdims=True)
        y = o * lax.rsqrt(ms + RMS_EPS) * og_ref[...]
        o_ref[rows, :] = (y * _silu(r_ref[rows, :].astype(F32))).astype(BF16)
        return carry

    lax.fori_loop(0, s_len // GB, phase_c, 0)


def _gla(proj, ga, w2p, b2, og, batch, seq):
    hq = 3 * DIFF_HEADS
    kblk = lambda off: pl.BlockSpec((seq, GLA_K_DIM), lambda b, hh, off=off: (b, off + hh))
    vblk = lambda off: pl.BlockSpec((seq, GLA_V_DIM), lambda b, hh, off=off: (b, off + hh))
    return pl.pallas_call(
        _gla_kernel,
        grid=(batch, GLA_HEADS),
        in_specs=[kblk(hq), kblk(hq + GLA_HEADS), vblk(16), vblk(16 + GLA_HEADS),
                  pl.BlockSpec((seq, LANES), lambda b, hh: (b, 0)),
                  pl.BlockSpec((LANES, GLA_K_DIM), lambda b, hh: (0, hh)),
                  pl.BlockSpec((1, GLA_K_DIM), lambda b, hh: (0, hh)),
                  pl.BlockSpec((1, GLA_V_DIM), lambda b, hh: (0, 0))],
        out_specs=pl.BlockSpec((seq, GLA_V_DIM), lambda b, hh: (b, hh)),
        out_shape=jax.ShapeDtypeStruct((batch * seq, GLA_HEADS * GLA_V_DIM), BF16),
        scratch_shapes=[pltpu.VMEM((seq, GLA_K_DIM), BF16),
                        pltpu.VMEM((seq // CHUNK, GLA_V_DIM, GLA_K_DIM), F32),
                        pltpu.VMEM((seq, GLA_K_DIM), F32),
                        pltpu.VMEM((seq // CHUNK, GLA_V_DIM, GLA_K_DIM), BF16),
                        pltpu.VMEM((seq, GLA_V_DIM), F32)],
        compiler_params=_params(2),
        name="gla",
    )(proj, proj, proj, proj, ga, w2p, b2, og)


def _out_proj_kernel(a_ref, g_ref, x_ref, wo_ref, g2_ref, wr_ref, br_ref,
                     x1_ref, h2_ref, lg_ref, tt_scr):
    half = a_ref.shape[1]

    def body(c, carry):
        r0 = pl.multiple_of(c * RB_OUT, RB_OUT)
        rows = pl.ds(r0, RB_OUT)
        mixed = (jnp.dot(a_ref[rows, :], wo_ref[0:half, :], preferred_element_type=F32)
                 + jnp.dot(g_ref[rows, :], wo_ref[half:, :], preferred_element_type=F32))
        x1 = x_ref[rows, :] + mixed
        x1_ref[rows, :] = x1
        ms = jnp.mean(x1 * x1, axis=-1, keepdims=True)
        h2 = x1 * lax.rsqrt(ms + RMS_EPS) * g2_ref[...]
        _store_token_tiles(h2_ref, h2, tt_scr, r0)
        lg = jnp.dot(h2.astype(BF16), wr_ref[...], preferred_element_type=F32) + br_ref[...]
        lg_ref[:, rows] = lg.T[0:N_LOGIT_ROWS, :]
        return carry

    lax.fori_loop(0, TM_OUT // RB_OUT, body, 0)


def _out_proj(a_out, g_out, x2, wo, g2, wr, br):
    t = x2.shape[0]
    half = a_out.shape[1]
    row = lambda n: pl.BlockSpec((TM_OUT, n), lambda i: (i, 0))
    full = lambda r, n: pl.BlockSpec((r, n), lambda i: (0, 0))
    return pl.pallas_call(
        _out_proj_kernel,
        grid=(t // TM_OUT,),
        in_specs=[row(half), row(half), row(D_MODEL), full(D_MODEL, D_MODEL),
                  full(1, D_MODEL), full(D_MODEL, LANES), full(1, LANES)],
        out_specs=[row(D_MODEL), pl.BlockSpec((TM_OUT * SUB, LANES), lambda i: (i, 0)),
                   pl.BlockSpec((N_LOGIT_ROWS, TM_OUT), lambda i: (0, i))],
        out_shape=[jax.ShapeDtypeStruct((t, D_MODEL), F32),
                   jax.ShapeDtypeStruct((t * SUB, LANES), BF16),
                   jax.ShapeDtypeStruct((N_LOGIT_ROWS, t), F32)],
        scratch_shapes=[pltpu.VMEM((RB_OUT * SUB, LANES), F32)],
        compiler_params=_params(1),
        name="out_proj_router",
    )(a_out, g_out, x2, wo, g2, wr, br)


def _first_argmax(vals, row_f):
    m = jnp.max(vals, axis=0, keepdims=True)
    idx = jnp.min(jnp.where(vals == m, row_f, float(vals.shape[0])), axis=0, keepdims=True)
    return m, idx


def _softmax_rows(v):
    ex = jnp.exp(v - jnp.max(v, axis=0, keepdims=True))
    return ex / jnp.sum(ex, axis=0, keepdims=True)


def _routing_kernel(lg_ref, slots_ref, cw_ref, cnt_ref, tok_scr):
    t = lg_ref.shape[1]
    epg = EXPERTS_PER_GROUP
    row8 = lax.broadcasted_iota(jnp.int32, (8, TB), 0).astype(F32)
    row_e = lax.broadcasted_iota(jnp.int32, (N_EXPERTS, TB), 0).astype(F32)
    ui = lax.broadcasted_iota(jnp.int32, (TB, TB), 0)
    uj = lax.broadcasted_iota(jnp.int32, (TB, TB), 1)
    earlier = jnp.where(ui < uj, 1.0, 0.0).astype(BF16)
    all_ones = jnp.ones((TB, TB), BF16)

    def rows01(r0, r1):
        return jnp.where(row8 == 0.0, r0, jnp.where(row8 == 1.0, r1, jnp.zeros_like(r0)))

    def pass1(b, counts):
        cols = pl.ds(pl.multiple_of(b * TB, TB), TB)
        lg = lg_ref[:, cols]
        lg_groups = jnp.where(row8 < float(N_GROUPS), lg[0:GROUP_ROWS, :], -jnp.inf)
        pg_sel, g_sel = _first_argmax(_softmax_rows(lg_groups), row8)
        le = lg[GROUP_ROWS:GROUP_ROWS + epg, :]
        for g in range(1, N_GROUPS):
            le = jnp.where(g_sel == float(g),
                           lg[GROUP_ROWS + g * epg:GROUP_ROWS + (g + 1) * epg, :], le)
        pe = _softmax_rows(le)
        v1, i1 = _first_argmax(pe, row8)
        v2, i2 = _first_argmax(jnp.where(row8 == i1, -jnp.inf, pe), row8)
        tot = v1 + v2
        e1 = g_sel * epg + i1
        e2 = g_sel * epg + i2
        oh1 = row_e == e1
        oh2 = row_e == e2
        a = jnp.where(oh1 | oh2, 1.0, 0.0).astype(BF16)
        rank = jnp.dot(a, earlier, preferred_element_type=F32) + counts
        r1 = jnp.sum(jnp.where(oh1, rank, 0.0), axis=0, keepdims=True)
        r2 = jnp.sum(jnp.where(oh2, rank, 0.0), axis=0, keepdims=True)
        tok_scr[:, cols] = jnp.where(row8 == 0.0, e1, jnp.where(
            row8 == 1.0, e2, jnp.where(row8 == 2.0, r1, jnp.where(row8 == 3.0, r2, 0.0))))
        cw_ref[:, cols] = rows01((v1 / tot) * pg_sel, (v2 / tot) * pg_sel)
        return counts + jnp.dot(a, all_ones, preferred_element_type=F32)

    counts = lax.fori_loop(0, t // TB, pass1, jnp.zeros((N_EXPERTS, TB), F32))
    cnt_ref[...] = counts[:, 0:LANES]
    n_tiles = jnp.floor((counts + (TM_E - 1)) * (1.0 / TM_E))
    li = lax.broadcasted_iota(jnp.int32, (N_EXPERTS, N_EXPERTS), 0)
    lj = lax.broadcasted_iota(jnp.int32, (N_EXPERTS, N_EXPERTS), 1)
    lower = jnp.where(lj < li, 1.0, 0.0).astype(BF16)
    row_off = jnp.dot(lower, n_tiles.astype(BF16), preferred_element_type=F32) * TM_E

    def pass2(b, carry):
        cols = pl.ds(pl.multiple_of(b * TB, TB), TB)
        tok = tok_scr[:, cols]
        off1 = jnp.sum(jnp.where(row_e == tok[0:1, :], row_off, 0.0), axis=0, keepdims=True)
        off2 = jnp.sum(jnp.where(row_e == tok[1:2, :], row_off, 0.0), axis=0, keepdims=True)
        slots_ref[:, cols] = rows01(off1 + tok[2:3, :], off2 + tok[3:4, :]).astype(jnp.int32)
        return carry

    lax.fori_loop(0, t // TB, pass2, 0)


def _routing(logits_t):
    t = logits_t.shape[1]
    return pl.pallas_call(
        _routing_kernel,
        out_shape=[jax.ShapeDtypeStruct((8, t), jnp.int32),
                   jax.ShapeDtypeStruct((8, t), F32),
                   jax.ShapeDtypeStruct((N_EXPERTS, LANES), F32)],
        scratch_shapes=[pltpu.VMEM((8, t), F32)],
        compiler_params=pltpu.CompilerParams(vmem_limit_bytes=VMEM_LIMIT),
        name="routing",
    )(logits_t)


def _rows_copy(src, dst, sem, src_tok, dst_tok, n):
    first = lambda tok: tok * SUB if isinstance(tok, int) else pl.multiple_of(tok * SUB, SUB)
    s0 = first(src_tok)
    d0 = first(dst_tok)
    return pltpu.make_async_copy(src.at[pl.ds(s0, n * SUB), :], dst.at[pl.ds(d0, n * SUB), :], sem)


def _dispatch_kernel(sl1_ref, sl2_ref, ends_ref, h2_ref, xs_hbm, zero_scr, sem):
    step = pl.program_id(0)

    @pl.when(step == 0)
    def _():
        zero_scr[...] = jnp.zeros_like(zero_scr)
        for e in range(N_EXPERTS):
            start = jnp.maximum(ends_ref[e] - TM_E, 0)
            _rows_copy(zero_scr, xs_hbm, sem, 0, start, TM_E).start()
        for e in range(N_EXPERTS):
            _rows_copy(zero_scr, xs_hbm, sem, 0, 0, TM_E).wait()

    base = step * TB_D

    def issue(r, carry):
        _rows_copy(h2_ref, xs_hbm, sem, r, sl1_ref[base + r], 1).start(priority=0)
        _rows_copy(h2_ref, xs_hbm, sem, r, sl2_ref[base + r], 1).start(priority=1)
        return carry

    lax.fori_loop(0, TB_D, issue, 0, unroll=8)
    _rows_copy(h2_ref, xs_hbm, sem, 0, 0, TB_D).wait()
    _rows_copy(h2_ref, xs_hbm, sem, 0, 0, TB_D).wait()


def _dispatch(h2, sl1, sl2, ends, n_rows):
    t = h2.shape[0] // SUB
    return pl.pallas_call(
        _dispatch_kernel,
        grid_spec=pltpu.PrefetchScalarGridSpec(
            num_scalar_prefetch=3,
            grid=(t // TB_D,),
            in_specs=[pl.BlockSpec((TB_D * SUB, LANES), lambda i, *_: (i, 0))],
            out_specs=pl.BlockSpec(memory_space=pl.ANY),
            scratch_shapes=[pltpu.VMEM((TM_E * SUB, LANES), BF16), pltpu.SemaphoreType.DMA(())],
        ),
        out_shape=jax.ShapeDtypeStruct((n_rows * SUB, LANES), BF16),
        compiler_params=_params(1),
        name="dispatch",
    )(sl1, sl2, ends, h2)


def _expert_kernel(tidx_ref, texp_ref, tfirst_ref, tslot_ref, tnext_ref, nt_ref,
                   xs_ref, wg_hbm, wu_hbm, wd_hbm, ys_ref,
                   wg_buf, wu_buf, wd_buf, wg_bf, wu_bf, wd_bf, tt_scr, sems):
    j = pl.program_id(0)

    def weight_copies(e, slot):
        return [pltpu.make_async_copy(hbm.at[e], buf.at[slot], sems.at[slot, k])
                for k, (hbm, buf) in enumerate(((wg_hbm, wg_buf), (wu_hbm, wu_buf),
                                                (wd_hbm, wd_buf)))]

    @pl.when(j < nt_ref[0])
    def _():
        slot = tslot_ref[j]

        @pl.when(tfirst_ref[j] == 1)
        def _():
            @pl.when(j == 0)
            def _():
                for c in weight_copies(texp_ref[j], slot):
                    c.start()
            for c in weight_copies(texp_ref[j], slot):
                c.wait()

            @pl.when(tnext_ref[j] >= 0)
            def _():
                for c in weight_copies(tnext_ref[j], 1 - slot):
                    c.start()
            wg_bf[...] = wg_buf[slot].astype(BF16)
            wu_bf[...] = wu_buf[slot].astype(BF16)
            wd_bf[...] = wd_buf[slot].astype(BF16)

        xb = _load_token_tiles(xs_ref, tt_scr, TM_E).astype(BF16)
        g = jnp.dot(xb, wg_bf[...], preferred_element_type=F32)
        u = jnp.dot(xb, wu_bf[...], preferred_element_type=F32)
        h = (_silu(g) * u).astype(BF16)
        _store_token_tiles(ys_ref, jnp.dot(h, wd_bf[...], preferred_element_type=F32), tt_scr)


def _expert_mlp(xs, tile_meta, wg, wu, wd):
    n_rows = xs.shape[0] // SUB
    tile = pl.BlockSpec((TM_E * SUB, LANES), lambda j, ti, *_: (ti[j], 0))
    hbm = pl.BlockSpec(memory_space=pl.ANY)
    return pl.pallas_call(
        _expert_kernel,
        grid_spec=pltpu.PrefetchScalarGridSpec(
            num_scalar_prefetch=len(tile_meta),
            grid=(n_rows // TM_E,),
            in_specs=[tile, hbm, hbm, hbm],
            out_specs=tile,
            scratch_shapes=[pltpu.VMEM((2, D_MODEL, D_FF), F32),
                            pltpu.VMEM((2, D_MODEL, D_FF), F32),
                            pltpu.VMEM((2, D_FF, D_MODEL), F32),
                            pltpu.VMEM((D_MODEL, D_FF), BF16),
                            pltpu.VMEM((D_MODEL, D_FF), BF16),
                            pltpu.VMEM((D_FF, D_MODEL), BF16),
                            pltpu.VMEM((TM_E * SUB, LANES), F32),
                            pltpu.SemaphoreType.DMA((2, 3))],
        ),
        out_shape=jax.ShapeDtypeStruct((n_rows * SUB, LANES), BF16),
        compiler_params=_params(1),
        name="expert_mlp",
    )(*tile_meta, xs, wg, wu, wd)


def _combine_kernel(sl1_ref, sl2_ref, x1_ref, cw_ref, ys_hbm, o_ref, y1_scr, y2_scr, tt_scr,
                    sems):
    i = pl.program_id(0)

    def issue(step, slot):
        base = step * TB

        def body(r, carry):
            _rows_copy(ys_hbm, y1_scr.at[slot], sems.at[slot], sl1_ref[base + r], r,
                       1).start(priority=0)
            _rows_copy(ys_hbm, y2_scr.at[slot], sems.at[slot], sl2_ref[base + r], r,
                       1).start(priority=1)
            return carry

        lax.fori_loop(0, TB, body, 0, unroll=8)

    @pl.when(i == 0)
    def _():
        issue(0, 0)

    @pl.when(i + 1 < pl.num_programs(0))
    def _():
        issue(i + 1, (i + 1) % 2)

    slot = i % 2
    _rows_copy(ys_hbm, y1_scr.at[slot], sems.at[slot], 0, 0, TB).wait()
    _rows_copy(ys_hbm, y2_scr.at[slot], sems.at[slot], 0, 0, TB).wait()
    cw = jnp.concatenate([cw_ref[...], jnp.zeros((LANES - 8, TB), F32)], axis=0).T
    o_ref[...] = x1_ref[...] + cw[:, 0:1] * _load_token_tiles(y1_scr.at[slot], tt_scr, TB)
    o_ref[...] += cw[:, 1:2] * _load_token_tiles(y2_scr.at[slot], tt_scr, TB)


def _combine(x1, cw, ys, sl1, sl2):
    t = x1.shape[0]
    return pl.pallas_call(
        _combine_kernel,
        grid_spec=pltpu.PrefetchScalarGridSpec(
            num_scalar_prefetch=2,
            grid=(t // TB,),
            in_specs=[pl.BlockSpec((TB, D_MODEL), lambda i, *_: (i, 0)),
                      pl.BlockSpec((8, TB), lambda i, *_: (0, i)),
                      pl.BlockSpec(memory_space=pl.ANY)],
            out_specs=pl.BlockSpec((TB, D_MODEL), lambda i, *_: (i, 0)),
            scratch_shapes=[pltpu.VMEM((2, TB * SUB, LANES), BF16),
                            pltpu.VMEM((2, TB * SUB, LANES), BF16),
                            pltpu.VMEM((TB * SUB, LANES), F32),
                            pltpu.SemaphoreType.DMA((2,))],
        ),
        out_shape=jax.ShapeDtypeStruct((t, D_MODEL), F32),
        compiler_params=_params(1),
        name="combine",
    )(sl1, sl2, x1, cw, ys)


def _lane_tile(v, reps):
    return jnp.tile(v.reshape(1, -1), (1, reps))


def kernel(x, positions, norm1_g, w_in, q_norm_g, k_norm_g, lambda_q1, lambda_k1, lambda_q2,
           lambda_k2, diff_out_norm_g, gla_w_gate2, gla_b_gate, gla_out_norm_g, w_out, norm2_g,
           w_router_group, b_router_group, w_router_expert, b_router_expert, w_gate_expert,
           w_up_expert, w_down_expert):
    batch, seq, d = x.shape
    t = batch * seq
    x2 = x.reshape(t, d)

    inv = ROPE_THETA ** (-jnp.arange(0, ROT_DIM, 2, dtype=F32) / ROT_DIM)
    lane_d = jnp.arange(LANES) % DIFF_QK_DIM
    invf = jnp.where(lane_d < ROT_DIM, inv[lane_d % (ROT_DIM // 2)], 0.0).reshape(1, LANES)
    pos_b = jnp.broadcast_to(positions.astype(F32).reshape(t, 1), (t, LANES))
    tabs = _rope_tables(pos_b, invf)

    w_in_t = jnp.swapaxes(w_in, 1, 2)[0]
    wa_t = jnp.pad(w_in_t[D_MAIN:, :], ((0, LANES - GLA_GATE_RANK), (0, 0)))
    proj, ga = _in_proj(x2, norm1_g, w_in_t, wa_t)

    a_out = _diff_attention(
        proj, tabs, _lane_tile(q_norm_g[0], 2), _lane_tile(k_norm_g[0], 2),
        lambda_q1, lambda_k1, lambda_q2, lambda_k2, diff_out_norm_g, batch, seq)
    w2p = jnp.pad(gla_w_gate2[0], ((0, LANES - GLA_GATE_RANK), (0, 0))).astype(BF16)
    g_out = _gla(proj, ga, w2p, gla_b_gate, gla_out_norm_g, batch, seq)

    gpad = GROUP_ROWS - N_GROUPS
    lpad = LANES - N_LOGIT_ROWS
    wr = jnp.concatenate(
        [jnp.pad(w_router_group[0], ((0, 0), (0, gpad))),
         jnp.pad(w_router_expert[0].transpose(1, 0, 2).reshape(d, N_EXPERTS),
                 ((0, 0), (0, lpad)))], axis=1)
    br = jnp.concatenate([jnp.pad(b_router_group[0], (0, gpad)),
                          jnp.pad(b_router_expert[0].reshape(-1), (0, lpad))]).reshape(1, LANES)
    x1, h2, logits_t = _out_proj(a_out, g_out, x2, w_out[0].astype(BF16), norm2_g,
                                 wr.astype(BF16), br)

    slots, cw, counts = _routing(logits_t)
    sl1 = slots[0]
    sl2 = slots[1]
    n_rows = 2 * t + N_EXPERTS * TM_E
    max_tiles = n_rows // TM_E
    cnt = counts[:, 0].astype(jnp.int32)
    tile_end = jnp.cumsum((cnt + (TM_E - 1)) // TM_E)
    n_tiles = tile_end[-1:]
    tile_idx = jnp.minimum(jnp.arange(max_tiles, dtype=jnp.int32), n_tiles[0] - 1)
    tile_exp = jnp.sum(tile_idx[:, None] >= tile_end[None, :], axis=1).astype(jnp.int32)
    row_end = (tile_end * TM_E).astype(jnp.int32)
    tile_first = jnp.concatenate(
        [jnp.ones((1,), jnp.int32), (tile_exp[1:] != tile_exp[:-1]).astype(jnp.int32)])
    tile_slot = (jnp.cumsum(tile_first) - 1) % 2
    nxt = tile_end[tile_exp]
    tile_next = jnp.where(nxt < n_tiles[0], tile_exp[jnp.minimum(nxt, max_tiles - 1)], -1)
    tile_meta = [a.astype(jnp.int32)
                 for a in (tile_idx, tile_exp, tile_first, tile_slot, tile_next, n_tiles)]

    xs = _dispatch(h2, sl1, sl2, row_end, n_rows)
    ys = _expert_mlp(xs, tile_meta,
                     w_gate_expert[0].reshape(N_EXPERTS, d, D_FF),
                     w_up_expert[0].reshape(N_EXPERTS, d, D_FF),
                     w_down_expert[0].reshape(N_EXPERTS, D_FF, d))
    out = _combine(x1, cw, ys, sl1, sl2)
    return out.reshape(batch, seq, d)
```

```python
import functools
import math

import jax
import jax.numpy as jnp
from jax import lax
from jax.experimental import pallas as pl
from jax.experimental.pallas import tpu as pltpu

D_MODEL = 2048
CHUNK = 64
DIFF_QK_DIM = 64
DIFF_V_DIM = 128
DIFF_HEADS = 8
ROT_DIM = 16
ROPE_THETA = 500000.0
GLA_HEADS = 4
GLA_V_DIM = 256
GLA_K_DIM = 128
GLA_GATE_RANK = 16
GLA_TAU = 16.0
N_GROUPS = 4
EXPERTS_PER_GROUP = 8
N_EXPERTS = N_GROUPS * EXPERTS_PER_GROUP
D_FF = 512
RMS_EPS = 1e-6
LAMBDA_INIT = 0.8 - 0.6 * math.exp(-0.3 * 0)
D_MAIN = 6144
GROUP_ROWS = 8
N_LOGIT_ROWS = GROUP_ROWS + N_EXPERTS

LANES = 128
SUB = D_MODEL // LANES
VMEM_LIMIT = 56 * 1024 * 1024

TM_IN = 1024
TN_IN = 1024
TQ = 256
GB = 256
TM_OUT = 512
RB_OUT = 256
TB = 256
TB_D = 1024
TM_E = 256

F32 = jnp.float32
BF16 = jnp.bfloat16
HI = lax.Precision.HIGHEST


def _params(n_axes):
    return pltpu.CompilerParams(dimension_semantics=("arbitrary",) * n_axes,
                                vmem_limit_bytes=VMEM_LIMIT)


def _nt_dot(a, b):
    return lax.dot_general(a, b, (((1,), (1,)), ((), ())), preferred_element_type=F32)


def _tn_dot(a, b):
    return lax.dot_general(a, b, (((0,), (0,)), ((), ())), preferred_element_type=F32)


def _silu(x):
    return x * (1.0 / (1.0 + jnp.exp(-x)))


def _store_token_tiles(ref, val, scr, tok0=0):
    n = val.shape[0]
    for s in range(SUB):
        scr[pl.ds(s, n, stride=SUB), :] = val[:, s * LANES:(s + 1) * LANES]
    ref[pl.ds(tok0 * SUB, n * SUB), :] = scr[...].astype(BF16)


def _load_token_tiles(ref, scr, n, tok0=0):
    scr[...] = ref[pl.ds(tok0 * SUB, n * SUB), :].astype(F32)
    return jnp.concatenate([scr[pl.ds(s, n, stride=SUB), :] for s in range(SUB)], axis=1)


def _rope_table_kernel(pos_ref, invf_ref, c_ref, s1_ref, s2_ref):
    ang = pos_ref[...] * invf_ref[...]
    d = lax.broadcasted_iota(jnp.int32, ang.shape, 1) % DIFF_QK_DIM
    cos = jnp.cos(ang)
    sin = jnp.sin(ang)
    half = ROT_DIM // 2
    c_ref[...] = jnp.where(d < ROT_DIM, cos, 1.0)
    s1_ref[...] = jnp.where(d < half, -sin, 0.0)
    s2_ref[...] = jnp.where((d >= half) & (d < ROT_DIM), sin, 0.0)


def _rope_tables(pos_b, invf):
    t = pos_b.shape[0]
    tb = 1024
    spec = pl.BlockSpec((tb, LANES), lambda i: (i, 0))
    return pl.pallas_call(
        _rope_table_kernel,
        grid=(t // tb,),
        in_specs=[spec, pl.BlockSpec((1, LANES), lambda i: (0, 0))],
        out_specs=[spec, spec, spec],
        out_shape=[jax.ShapeDtypeStruct((t, LANES), F32)] * 3,
        compiler_params=_params(1),
        name="rope_tables",
    )(pos_b, invf)


def _in_proj_kernel(x_ref, g_ref, wt_ref, wat_ref, proj_ref, ga_ref, h_scr):
    @pl.when(pl.program_id(1) == 0)
    def _():
        def body(c, carry):
            rows = pl.ds(c * 256, 256)
            x = x_ref[rows, :]
            ms = jnp.mean(x * x, axis=-1, keepdims=True)
            h_scr[rows, :] = (x * lax.rsqrt(ms + RMS_EPS) * g_ref[...]).astype(BF16)
            return carry
        lax.fori_loop(0, TM_IN // 256, body, 0)
        ga_ref[...] = _nt_dot(h_scr[...], wat_ref[...].astype(BF16))

    proj_ref[...] = _nt_dot(h_scr[...], wt_ref[...].astype(BF16)).astype(BF16)


def _in_proj(x2, g1, w_in, wa):
    t = x2.shape[0]
    return pl.pallas_call(
        _in_proj_kernel,
        grid=(t // TM_IN, D_MAIN // TN_IN),
        in_specs=[
            pl.BlockSpec((TM_IN, D_MODEL), lambda i, j: (i, 0)),
            pl.BlockSpec((1, D_MODEL), lambda i, j: (0, 0)),
            pl.BlockSpec((TN_IN, D_MODEL), lambda i, j: (j, 0)),
            pl.BlockSpec((LANES, D_MODEL), lambda i, j: (0, 0)),
        ],
        out_specs=[
            pl.BlockSpec((TM_IN, TN_IN), lambda i, j: (i, j)),
            pl.BlockSpec((TM_IN, LANES), lambda i, j: (i, 0)),
        ],
        out_shape=[jax.ShapeDtypeStruct((t, D_MAIN), BF16),
                   jax.ShapeDtypeStruct((t, LANES), F32)],
        scratch_shapes=[pltpu.VMEM((TM_IN, D_MODEL), BF16)],
        compiler_params=_params(2),
        name="in_proj",
    )(x2, g1, w_in, wa)


def _norm_rope(x, g, c, s1, s2):
    lo = lax.broadcasted_iota(jnp.int32, x.shape, 1) < DIFF_QK_DIM
    x2 = x * x
    s_lo = jnp.sum(jnp.where(lo, x2, 0.0), axis=-1, keepdims=True)
    s_hi = jnp.sum(jnp.where(lo, 0.0, x2), axis=-1, keepdims=True)
    ms = jnp.where(lo, s_lo, s_hi) * (1.0 / DIFF_QK_DIM)
    y = x * lax.rsqrt(ms + RMS_EPS) * g
    half = ROT_DIM // 2
    return y * c + pltpu.roll(y, LANES - half, 1) * s1 + pltpu.roll(y, half, 1) * s2


def _attn_kernel(q_ref, k_ref, v_ref, c_ref, s1_ref, s2_ref, qg_ref, kg_ref,
                 lq1_ref, lk1_ref, lq2_ref, lk2_ref, og_ref, o_ref,
                 q1_scr, q2_scr, k_scr, v_scr, *bufs):
    s_len = q_ref.shape[0]
    s_bufs, e_bufs = bufs[0:4], bufs[4:8]
    lam = (jnp.exp(jnp.sum(lq1_ref[...] * lk1_ref[...], axis=-1, keepdims=True))
           - jnp.exp(jnp.sum(lq2_ref[...] * lk2_ref[...], axis=-1, keepdims=True))
           + LAMBDA_INIT)
    lo = lax.broadcasted_iota(jnp.int32, (TQ, LANES), 1) < DIFF_QK_DIM
    v_scr[:, 0:LANES] = v_ref[...]
    v_scr[:, LANES:] = jnp.ones((s_len, LANES), BF16)

    def prepare(i):
        rows = slice(i * TQ, (i + 1) * TQ)
        c, s1, s2 = c_ref[rows, :], s1_ref[rows, :], s2_ref[rows, :]
        qn = (_norm_rope(q_ref[rows, :].astype(F32), qg_ref[...], c, s1, s2)
              * (DIFF_QK_DIM ** -0.5 * math.log2(math.e)))
        q1_scr[rows, :] = jnp.where(lo, qn, 0.0).astype(BF16)
        q2_scr[rows, :] = jnp.where(lo, 0.0, qn).astype(BF16)
        k_scr[rows, :] = _norm_rope(k_ref[rows, :].astype(F32), kg_ref[...], c, s1,
                                    s2).astype(BF16)

    diag = (lax.broadcasted_iota(jnp.int32, (TQ, TQ), 1) // CHUNK
            <= lax.broadcasted_iota(jnp.int32, (TQ, TQ), 0) // CHUNK)

    n_tiles = s_len // TQ
    items = [(i, q_scr) for i in range(n_tiles) for q_scr in (q1_scr, q2_scr)]

    def scores(r):
        i, q_scr = items[r]
        s_scr = s_bufs[r % 4]
        q = q_scr[i * TQ:(i + 1) * TQ, :]
        n_off = i * TQ
        s_scr[:, n_off:n_off + TQ] = jnp.where(
            diag, _nt_dot(q, k_scr[n_off:n_off + TQ, :]), -jnp.inf)
        if n_off:
            s_scr[:, 0:n_off] = _nt_dot(q, k_scr[0:n_off, :])

    def softmax_pv(r):
        nk = (items[r][0] + 1) * TQ
        s_scr, e_scr = s_bufs[r % 4], e_bufs[r % 4]
        m = jnp.max(s_scr[:, 0:nk], axis=-1, keepdims=True)
        e_scr[:, 0:nk] = jnp.exp2(s_scr[:, 0:nk] - m).astype(BF16)
        acc = jnp.dot(e_scr[:, 0:nk], v_scr[0:nk, :], preferred_element_type=F32)
        return acc[:, 0:LANES] / acc[:, LANES:]

    prepare(0)
    scores(0)
    scores(1)
    for i in range(n_tiles):
        rows = slice(i * TQ, (i + 1) * TQ)
        if i + 1 < n_tiles:
            prepare(i + 1)
            scores(2 * i + 2)
        o1 = softmax_pv(2 * i)
        if i + 1 < n_tiles:
            scores(2 * i + 3)
        o = o1 - lam * softmax_pv(2 * i + 1)
        ms = jnp.mean(o * o, axis=-1, keepdims=True)
        y = o * lax.rsqrt(ms + RMS_EPS) * og_ref[...] * (1.0 - LAMBDA_INIT)
        o_ref[rows, :] = y.astype(BF16)


def _diff_attention(proj, tabs, qg, kg, lq1, lk1, lq2, lk2, og, batch, seq):
    c, s1, s2 = tabs
    h = DIFF_HEADS
    blk = lambda off: pl.BlockSpec((seq, LANES), lambda b, hh, off=off: (b, off + hh))
    tab = pl.BlockSpec((seq, LANES), lambda b, hh: (b, 0))
    vec = lambda n: pl.BlockSpec((1, n), lambda b, hh: (0, 0))
    return pl.pallas_call(
        _attn_kernel,
        grid=(batch, h),
        in_specs=[blk(0), blk(h), blk(2 * h), tab, tab, tab,
                  vec(LANES), vec(LANES), vec(DIFF_QK_DIM), vec(DIFF_QK_DIM),
                  vec(DIFF_QK_DIM), vec(DIFF_QK_DIM), vec(LANES)],
        out_specs=pl.BlockSpec((seq, LANES), lambda b, hh: (b, hh)),
        out_shape=jax.ShapeDtypeStruct((batch * seq, h * DIFF_V_DIM), BF16),
        scratch_shapes=[pltpu.VMEM((seq, LANES), BF16)] * 3
        + [pltpu.VMEM((seq, 2 * LANES), BF16)]
        + [pltpu.VMEM((TQ, seq), F32)] * 4 + [pltpu.VMEM((TQ, seq), BF16)] * 4,
        compiler_params=_params(2),
        name="diff_attention",
    )(proj, proj, proj, c, s1, s2, qg, kg, lq1, lk1, lq2, lk2, og)


def _split_dot(ones_bf, x):
    hi = x.astype(BF16)
    lo = (x - hi.astype(F32)).astype(BF16)
    return (jnp.dot(ones_bf, hi, preferred_element_type=F32)
            + jnp.dot(ones_bf, lo, preferred_element_type=F32))


def _gla_kernel(q_ref, k_ref, v_ref, r_ref, ga_ref, w2_ref, b2_ref, og_ref, o_ref,
                qin_scr, ut_scr, dec_scr, st_scr, acc_scr):
    s_len = q_ref.shape[0]
    cpg = GB // CHUNK
    ri = lax.broadcasted_iota(jnp.int32, (GB, GB), 0)
    ci = lax.broadcasted_iota(jnp.int32, (GB, GB), 1)
    same = (ri // CHUNK) == (ci // CHUNK)
    blk_ones = jnp.where(same, 1.0, 0.0).astype(BF16)
    tril = same & (ci <= ri)
    tri_ones = jnp.where(tril, 1.0, 0.0).astype(BF16)

    def phase_a(g, carry):
        rows = pl.ds(pl.multiple_of(g * GB, GB), GB)
        pre = jnp.dot(ga_ref[rows, :].astype(BF16), w2_ref[...],
                      preferred_element_type=F32) + b2_ref[...]
        la = -(jnp.maximum(-pre, 0.0) + jnp.log1p(jnp.exp(-jnp.abs(pre)))) * (1.0 / GLA_TAU)
        bc = _split_dot(tri_ones, la)
        bl = _split_dot(blk_ones, la)
        e_neg = jnp.exp(-bc)
        e_last = jnp.exp(bl)
        k = k_ref[rows, :].astype(F32)
        q_in = (q_ref[rows, :].astype(F32) * (GLA_K_DIM ** -0.5) * jnp.exp(bc)).astype(BF16)
        k_in = (k * e_neg).astype(BF16)
        k_dec = (k * (e_last * e_neg)).astype(BF16)
        qin_scr[rows, :] = q_in
        dec_scr[rows, :] = e_last
        v = v_ref[rows, :]
        att = jnp.where(tril, _nt_dot(q_in, k_in), 0.0).astype(BF16)
        acc_scr[rows, :] = jnp.dot(att, v, preferred_element_type=F32)
        for c in range(cpg):
            cr = slice(c * CHUNK, (c + 1) * CHUNK)
            ut_scr[g * cpg + c] = _tn_dot(v[cr, :], k_dec[cr, :])
        return carry

    lax.fori_loop(0, s_len // GB, phase_a, 0, unroll=8)

    def phase_b(c, st):
        st_scr[c] = st.astype(BF16)
        dec = dec_scr[pl.ds(pl.multiple_of(c * CHUNK, CHUNK), 1), :]
        return dec * st + ut_scr[c]

    lax.fori_loop(0, s_len // CHUNK, phase_b, jnp.zeros((GLA_V_DIM, GLA_K_DIM), F32))

    def phase_c(g, carry):
        rows = pl.ds(pl.multiple_of(g * GB, GB), GB)
        inter = [_nt_dot(qin_scr[pl.ds(pl.multiple_of(g * GB + c * CHUNK, CHUNK), CHUNK), :],
                         st_scr[g * cpg + c]) for c in range(cpg)]
        o = acc_scr[rows, :] + jnp.concatenate(inter, axis=0)
        ms = jnp.mean(o * o, axis=-1, keepdims=True)
        y = o * lax.rsqrt(ms + RMS_EPS) * og_ref[...]
        o_ref[rows, :] = (y * _silu(r_ref[rows, :].astype(F32))).astype(BF16)
        return carry

    lax.fori_loop(0, s_len // GB, phase_c, 0, unroll=8)


def _gla(proj, ga, w2p, b2, og, batch, seq):
    hq = 3 * DIFF_HEADS
    kblk = lambda off: pl.BlockSpec((seq, GLA_K_DIM), lambda b, hh, off=off: (b, off + hh))
    vblk = lambda off: pl.BlockSpec((seq, GLA_V_DIM), lambda b, hh, off=off: (b, off + hh))
    return pl.pallas_call(
        _gla_kernel,
        grid=(batch, GLA_HEADS),
        in_specs=[kblk(hq), kblk(hq + GLA_HEADS), vblk(16), vblk(16 + GLA_HEADS),
                  pl.BlockSpec((seq, LANES), lambda b, hh: (b, 0)),
                  pl.BlockSpec((LANES, GLA_K_DIM), lambda b, hh: (0, hh)),
                  pl.BlockSpec((1, GLA_K_DIM), lambda b, hh: (0, hh)),
                  pl.BlockSpec((1, GLA_V_DIM), lambda b, hh: (0, 0))],
        out_specs=pl.BlockSpec((seq, GLA_V_DIM), lambda b, hh: (b, hh)),
        out_shape=jax.ShapeDtypeStruct((batch * seq, GLA_HEADS * GLA_V_DIM), BF16),
        scratch_shapes=[pltpu.VMEM((seq, GLA_K_DIM), BF16),
                        pltpu.VMEM((seq // CHUNK, GLA_V_DIM, GLA_K_DIM), F32),
                        pltpu.VMEM((seq, GLA_K_DIM), F32),
                        pltpu.VMEM((seq // CHUNK, GLA_V_DIM, GLA_K_DIM), BF16),
                        pltpu.VMEM((seq, GLA_V_DIM), F32)],
        compiler_params=_params(2),
        name="gla",
    )(proj, proj, proj, proj, ga, w2p, b2, og)


def _out_proj_kernel(a_ref, g_ref, x_ref, wo_ref, g2_ref, wr_ref, br_ref,
                     x1_ref, h2_ref, lg_ref, tt_scr):
    half = a_ref.shape[1]

    def body(c, carry):
        r0 = pl.multiple_of(c * RB_OUT, RB_OUT)
        rows = pl.ds(r0, RB_OUT)
        mixed = (jnp.dot(a_ref[rows, :], wo_ref[0:half, :], preferred_element_type=F32)
                 + jnp.dot(g_ref[rows, :], wo_ref[half:, :], preferred_element_type=F32))
        x1 = x_ref[rows, :] + mixed
        x1_ref[rows, :] = x1
        ms = jnp.mean(x1 * x1, axis=-1, keepdims=True)
        h2 = x1 * lax.rsqrt(ms + RMS_EPS) * g2_ref[...]
        _store_token_tiles(h2_ref, h2, tt_scr, r0)
        lg = jnp.dot(h2.astype(BF16), wr_ref[...], preferred_element_type=F32) + br_ref[...]
        lg_ref[:, rows] = lg.T[0:N_LOGIT_ROWS, :]
        return carry

    lax.fori_loop(0, TM_OUT // RB_OUT, body, 0)


def _out_proj(a_out, g_out, x2, wo, g2, wr, br):
    t = x2.shape[0]
    half = a_out.shape[1]
    row = lambda n: pl.BlockSpec((TM_OUT, n), lambda i: (i, 0))
    full = lambda r, n: pl.BlockSpec((r, n), lambda i: (0, 0))
    return pl.pallas_call(
        _out_proj_kernel,
        grid=(t // TM_OUT,),
        in_specs=[row(half), row(half), row(D_MODEL), full(D_MODEL, D_MODEL),
                  full(1, D_MODEL), full(D_MODEL, LANES), full(1, LANES)],
        out_specs=[row(D_MODEL), pl.BlockSpec((TM_OUT * SUB, LANES), lambda i: (i, 0)),
                   pl.BlockSpec((N_LOGIT_ROWS, TM_OUT), lambda i: (0, i))],
        out_shape=[jax.ShapeDtypeStruct((t, D_MODEL), F32),
                   jax.ShapeDtypeStruct((t * SUB, LANES), BF16),
                   jax.ShapeDtypeStruct((N_LOGIT_ROWS, t), F32)],
        scratch_shapes=[pltpu.VMEM((RB_OUT * SUB, LANES), F32)],
        compiler_params=_params(1),
        name="out_proj_router",
    )(a_out, g_out, x2, wo, g2, wr, br)


def _first_argmax(vals, row_f):
    m = jnp.max(vals, axis=0, keepdims=True)
    idx = jnp.min(jnp.where(vals == m, row_f, float(vals.shape[0])), axis=0, keepdims=True)
    return m, idx


def _softmax_rows(v):
    ex = jnp.exp(v - jnp.max(v, axis=0, keepdims=True))
    return ex / jnp.sum(ex, axis=0, keepdims=True)


def _routing_kernel(lg_ref, slots_ref, cw_ref, cnt_ref, tok_scr):
    t = lg_ref.shape[1]
    epg = EXPERTS_PER_GROUP
    row8 = lax.broadcasted_iota(jnp.int32, (8, TB), 0).astype(F32)
    row_e = lax.broadcasted_iota(jnp.int32, (N_EXPERTS, TB), 0).astype(F32)
    ui = lax.broadcasted_iota(jnp.int32, (TB, TB), 0)
    uj = lax.broadcasted_iota(jnp.int32, (TB, TB), 1)
    earlier = jnp.where(ui < uj, 1.0, 0.0).astype(BF16)
    all_ones = jnp.ones((TB, TB), BF16)

    def rows01(r0, r1):
        return jnp.where(row8 == 0.0, r0, jnp.where(row8 == 1.0, r1, jnp.zeros_like(r0)))

    def pass1(b, counts):
        cols = pl.ds(pl.multiple_of(b * TB, TB), TB)
        lg = lg_ref[:, cols]
        lg_groups = jnp.where(row8 < float(N_GROUPS), lg[0:GROUP_ROWS, :], -jnp.inf)
        pg_sel, g_sel = _first_argmax(_softmax_rows(lg_groups), row8)
        le = lg[GROUP_ROWS:GROUP_ROWS + epg, :]
        for g in range(1, N_GROUPS):
            le = jnp.where(g_sel == float(g),
                           lg[GROUP_ROWS + g * epg:GROUP_ROWS + (g + 1) * epg, :], le)
        pe = _softmax_rows(le)
        v1, i1 = _first_argmax(pe, row8)
        v2, i2 = _first_argmax(jnp.where(row8 == i1, -jnp.inf, pe), row8)
        tot = v1 + v2
        e1 = g_sel * epg + i1
        e2 = g_sel * epg + i2
        oh1 = row_e == e1
        oh2 = row_e == e2
        a = jnp.where(oh1 | oh2, 1.0, 0.0).astype(BF16)
        rank = jnp.dot(a, earlier, preferred_element_type=F32) + counts
        r1 = jnp.sum(jnp.where(oh1, rank, 0.0), axis=0, keepdims=True)
        r2 = jnp.sum(jnp.where(oh2, rank, 0.0), axis=0, keepdims=True)
        tok_scr[:, cols] = jnp.where(row8 == 0.0, e1, jnp.where(
            row8 == 1.0, e2, jnp.where(row8 == 2.0, r1, jnp.where(row8 == 3.0, r2, 0.0))))
        cw_ref[:, cols] = rows01((v1 / tot) * pg_sel, (v2 / tot) * pg_sel)
        return counts + jnp.dot(a, all_ones, preferred_element_type=F32)

    counts = lax.fori_loop(0, t // TB, pass1, jnp.zeros((N_EXPERTS, TB), F32))
    cnt_ref[...] = counts[:, 0:LANES]
    n_tiles = jnp.floor((counts + (TM_E - 1)) * (1.0 / TM_E))
    li = lax.broadcasted_iota(jnp.int32, (N_EXPERTS, N_EXPERTS), 0)
    lj = lax.broadcasted_iota(jnp.int32, (N_EXPERTS, N_EXPERTS), 1)
    lower = jnp.where(lj < li, 1.0, 0.0).astype(BF16)
    row_off = jnp.dot(lower, n_tiles.astype(BF16), preferred_element_type=F32) * TM_E

    def pass2(b, carry):
        cols = pl.ds(pl.multiple_of(b * TB, TB), TB)
        tok = tok_scr[:, cols]
        off1 = jnp.sum(jnp.where(row_e == tok[0:1, :], row_off, 0.0), axis=0, keepdims=True)
        off2 = jnp.sum(jnp.where(row_e == tok[1:2, :], row_off, 0.0), axis=0, keepdims=True)
        slots_ref[:, cols] = rows01(off1 + tok[2:3, :], off2 + tok[3:4, :]).astype(jnp.int32)
        return carry

    lax.fori_loop(0, t // TB, pass2, 0)


def _routing(logits_t):
    t = logits_t.shape[1]
    return pl.pallas_call(
        _routing_kernel,
        out_shape=[jax.ShapeDtypeStruct((8, t), jnp.int32),
                   jax.ShapeDtypeStruct((8, t), F32),
                   jax.ShapeDtypeStruct((N_EXPERTS, LANES), F32)],
        scratch_shapes=[pltpu.VMEM((8, t), F32)],
        compiler_params=pltpu.CompilerParams(vmem_limit_bytes=VMEM_LIMIT),
        name="routing",
    )(logits_t)


def _rows_copy(src, dst, sem, src_tok, dst_tok, n):
    first = lambda tok: tok * SUB if isinstance(tok, int) else pl.multiple_of(tok * SUB, SUB)
    s0 = first(src_tok)
    d0 = first(dst_tok)
    return pltpu.make_async_copy(src.at[pl.ds(s0, n * SUB), :], dst.at[pl.ds(d0, n * SUB), :], sem)


def _dispatch_kernel(sl1_ref, sl2_ref, ends_ref, h2_ref, xs_hbm, zero_scr, sem):
    step = pl.program_id(0)

    @pl.when(step == 0)
    def _():
        zero_scr[...] = jnp.zeros_like(zero_scr)
        for e in range(N_EXPERTS):
            start = jnp.maximum(ends_ref[e] - TM_E, 0)
            _rows_copy(zero_scr, xs_hbm, sem, 0, start, TM_E).start()
        for e in range(N_EXPERTS):
            _rows_copy(zero_scr, xs_hbm, sem, 0, 0, TM_E).wait()

    base = step * TB_D

    def issue(r, carry):
        _rows_copy(h2_ref, xs_hbm, sem, r, sl1_ref[base + r], 1).start(priority=0)
        _rows_copy(h2_ref, xs_hbm, sem, r, sl2_ref[base + r], 1).start(priority=1)
        return carry

    lax.fori_loop(0, TB_D, issue, 0, unroll=8)
    _rows_copy(h2_ref, xs_hbm, sem, 0, 0, TB_D).wait()
    _rows_copy(h2_ref, xs_hbm, sem, 0, 0, TB_D).wait()


def _dispatch(h2, sl1, sl2, ends, n_rows):
    t = h2.shape[0] // SUB
    return pl.pallas_call(
        _dispatch_kernel,
        grid_spec=pltpu.PrefetchScalarGridSpec(
            num_scalar_prefetch=3,
            grid=(t // TB_D,),
            in_specs=[pl.BlockSpec((TB_D * SUB, LANES), lambda i, *_: (i, 0))],
            out_specs=pl.BlockSpec(memory_space=pl.ANY),
            scratch_shapes=[pltpu.VMEM((TM_E * SUB, LANES), BF16), pltpu.SemaphoreType.DMA(())],
        ),
        out_shape=jax.ShapeDtypeStruct((n_rows * SUB, LANES), BF16),
        compiler_params=_params(1),
        name="dispatch",
    )(sl1, sl2, ends, h2)


def _expert_kernel(tidx_ref, texp_ref, tfirst_ref, tslot_ref, tnext_ref, nt_ref,
                   xs_ref, wg_hbm, wu_hbm, wd_hbm, ys_ref,
                   wg_buf, wu_buf, wd_buf, wg_bf, wu_bf, wd_bf, tt_scr, sems):
    j = pl.program_id(0)

    def weight_copies(e, slot):
        return [pltpu.make_async_copy(hbm.at[e], buf.at[slot], sems.at[slot, k])
                for k, (hbm, buf) in enumerate(((wg_hbm, wg_buf), (wu_hbm, wu_buf),
                                                (wd_hbm, wd_buf)))]

    @pl.when(j < nt_ref[0])
    def _():
        slot = tslot_ref[j]

        @pl.when(tfirst_ref[j] == 1)
        def _():
            @pl.when(j == 0)
            def _():
                for c in weight_copies(texp_ref[j], slot):
                    c.start()
            for c in weight_copies(texp_ref[j], slot):
                c.wait()

            @pl.when(tnext_ref[j] >= 0)
            def _():
                for c in weight_copies(tnext_ref[j], 1 - slot):
                    c.start()
            wg_bf[...] = wg_buf[slot].astype(BF16)
            wu_bf[...] = wu_buf[slot].astype(BF16)
            wd_bf[...] = wd_buf[slot].astype(BF16)

        xb = _load_token_tiles(xs_ref, tt_scr, TM_E).astype(BF16)
        g = jnp.dot(xb, wg_bf[...], preferred_element_type=F32)
        u = jnp.dot(xb, wu_bf[...], preferred_element_type=F32)
        h = (_silu(g) * u).astype(BF16)
        _store_token_tiles(ys_ref, jnp.dot(h, wd_bf[...], preferred_element_type=F32), tt_scr)


def _expert_mlp(xs, tile_meta, wg, wu, wd):
    n_rows = xs.shape[0] // SUB
    tile = pl.BlockSpec((TM_E * SUB, LANES), lambda j, ti, *_: (ti[j], 0))
    hbm = pl.BlockSpec(memory_space=pl.ANY)
    return pl.pallas_call(
        _expert_kernel,
        grid_spec=pltpu.PrefetchScalarGridSpec(
            num_scalar_prefetch=len(tile_meta),
            grid=(n_rows // TM_E,),
            in_specs=[tile, hbm, hbm, hbm],
            out_specs=tile,
            scratch_shapes=[pltpu.VMEM((2, D_MODEL, D_FF), F32),
                            pltpu.VMEM((2, D_MODEL, D_FF), F32),
                            pltpu.VMEM((2, D_FF, D_MODEL), F32),
                            pltpu.VMEM((D_MODEL, D_FF), BF16),
                            pltpu.VMEM((D_MODEL, D_FF), BF16),
                            pltpu.VMEM((D_FF, D_MODEL), BF16),
                            pltpu.VMEM((TM_E * SUB, LANES), F32),
                            pltpu.SemaphoreType.DMA((2, 3))],
        ),
        out_shape=jax.ShapeDtypeStruct((n_rows * SUB, LANES), BF16),
        compiler_params=_params(1),
        name="expert_mlp",
    )(*tile_meta, xs, wg, wu, wd)


def _combine_kernel(sl1_ref, sl2_ref, x1_ref, cw_ref, ys_hbm, o_ref, y1_scr, y2_scr, tt_scr,
                    sems):
    i = pl.program_id(0)

    def issue(step, slot):
        base = step * TB

        def body(r, carry):
            _rows_copy(ys_hbm, y1_scr.at[slot], sems.at[slot], sl1_ref[base + r], r,
                       1).start(priority=0)
            _rows_copy(ys_hbm, y2_scr.at[slot], sems.at[slot], sl2_ref[base + r], r,
                       1).start(priority=1)
            return carry

        lax.fori_loop(0, TB, body, 0, unroll=8)

    @pl.when(i == 0)
    def _():
        issue(0, 0)

    @pl.when(i + 1 < pl.num_programs(0))
    def _():
        issue(i + 1, (i + 1) % 2)

    slot = i % 2
    _rows_copy(ys_hbm, y1_scr.at[slot], sems.at[slot], 0, 0, TB).wait()
    _rows_copy(ys_hbm, y2_scr.at[slot], sems.at[slot], 0, 0, TB).wait()
    cw = jnp.concatenate([cw_ref[...], jnp.zeros((LANES - 8, TB), F32)], axis=0).T
    o_ref[...] = x1_ref[...] + cw[:, 0:1] * _load_token_tiles(y1_scr.at[slot], tt_scr, TB)
    o_ref[...] += cw[:, 1:2] * _load_token_tiles(y2_scr.at[slot], tt_scr, TB)


def _combine(x1, cw, ys, sl1, sl2):
    t = x1.shape[0]
    return pl.pallas_call(
        _combine_kernel,
        grid_spec=pltpu.PrefetchScalarGridSpec(
            num_scalar_prefetch=2,
            grid=(t // TB,),
            in_specs=[pl.BlockSpec((TB, D_MODEL), lambda i, *_: (i, 0)),
                      pl.BlockSpec((8, TB), lambda i, *_: (0, i)),
                      pl.BlockSpec(memory_space=pl.ANY)],
            out_specs=pl.BlockSpec((TB, D_MODEL), lambda i, *_: (i, 0)),
            scratch_shapes=[pltpu.VMEM((2, TB * SUB, LANES), BF16),
                            pltpu.VMEM((2, TB * SUB, LANES), BF16),
                            pltpu.VMEM((TB * SUB, LANES), F32),
                            pltpu.SemaphoreType.DMA((2,))],
        ),
        out_shape=jax.ShapeDtypeStruct((t, D_MODEL), F32),
        compiler_params=_params(1),
        name="combine",
    )(sl1, sl2, x1, cw, ys)


def _lane_tile(v, reps):
    return jnp.tile(v.reshape(1, -1), (1, reps))


def kernel(x, positions, norm1_g, w_in, q_norm_g, k_norm_g, lambda_q1, lambda_k1, lambda_q2,
           lambda_k2, diff_out_norm_g, gla_w_gate2, gla_b_gate, gla_out_norm_g, w_out, norm2_g,
           w_router_group, b_router_group, w_router_expert, b_router_expert, w_gate_expert,
           w_up_expert, w_down_expert):
    batch, seq, d = x.shape
    t = batch * seq
    x2 = x.reshape(t, d)

    inv = ROPE_THETA ** (-jnp.arange(0, ROT_DIM, 2, dtype=F32) / ROT_DIM)
    lane_d = jnp.arange(LANES) % DIFF_QK_DIM
    invf = jnp.where(lane_d < ROT_DIM, inv[lane_d % (ROT_DIM // 2)], 0.0).reshape(1, LANES)
    pos_b = jnp.broadcast_to(positions.astype(F32).reshape(t, 1), (t, LANES))
    tabs = _rope_tables(pos_b, invf)

    w_in_t = jnp.swapaxes(w_in, 1, 2)[0]
    wa_t = jnp.pad(w_in_t[D_MAIN:, :], ((0, LANES - GLA_GATE_RANK), (0, 0)))
    proj, ga = _in_proj(x2, norm1_g, w_in_t, wa_t)

    a_out = _diff_attention(
        proj, tabs, _lane_tile(q_norm_g[0], 2), _lane_tile(k_norm_g[0], 2),
        lambda_q1, lambda_k1, lambda_q2, lambda_k2, diff_out_norm_g, batch, seq)
    w2p = jnp.pad(gla_w_gate2[0], ((0, LANES - GLA_GATE_RANK), (0, 0))).astype(BF16)
    g_out = _gla(proj, ga, w2p, gla_b_gate, gla_out_norm_g, batch, seq)

    gpad = GROUP_ROWS - N_GROUPS
    lpad = LANES - N_LOGIT_ROWS
    wr = jnp.concatenate(
        [jnp.pad(w_router_group[0], ((0, 0), (0, gpad))),
         jnp.pad(w_router_expert[0].transpose(1, 0, 2).reshape(d, N_EXPERTS),
                 ((0, 0), (0, lpad)))], axis=1)
    br = jnp.concatenate([jnp.pad(b_router_group[0], (0, gpad)),
                          jnp.pad(b_router_expert[0].reshape(-1), (0, lpad))]).reshape(1, LANES)
    x1, h2, logits_t = _out_proj(a_out, g_out, x2, w_out[0].astype(BF16), norm2_g,
                                 wr.astype(BF16), br)

    slots, cw, counts = _routing(logits_t)
    sl1 = slots[0]
    sl2 = slots[1]
    n_rows = 2 * t + N_EXPERTS * TM_E
    max_tiles = n_rows // TM_E
    cnt = counts[:, 0].astype(jnp.int32)
    tile_end = jnp.cumsum((cnt + (TM_E - 1)) // TM_E)
    n_tiles = tile_end[-1:]
    tile_idx = jnp.minimum(jnp.arange(max_tiles, dtype=jnp.int32), n_tiles[0] - 1)
    tile_exp = jnp.sum(tile_idx[:, None] >= tile_end[None, :], axis=1).astype(jnp.int32)
    row_end = (tile_end * TM_E).astype(jnp.int32)
    tile_first = jnp.concatenate(
        [jnp.ones((1,), jnp.int32), (tile_exp[1:] != tile_exp[:-1]).astype(jnp.int32)])
    tile_slot = (jnp.cumsum(tile_first) - 1) % 2
    nxt = tile_end[tile_exp]
    tile_next = jnp.where(nxt < n_tiles[0], tile_exp[jnp.minimum(nxt, max_tiles - 1)], -1)
    tile_meta = [a.astype(jnp.int32)
                 for a in (tile_idx, tile_exp, tile_first, tile_slot, tile_next, n_tiles)]

    xs = _dispatch(h2, sl1, sl2, row_end, n_rows)
    ys = _expert_mlp(xs, tile_meta,
                     w_gate_expert[0].reshape(N_EXPERTS, d, D_FF),
                     w_up_expert[0].reshape(N_EXPERTS, d, D_FF),
                     w_down_expert[0].reshape(N_EXPERTS, D_FF, d))
    out = _combine(x1, cw, ys, sl1, sl2)
    return out.reshape(batch, seq, d)
```

```python
import functools
import math

import jax
import jax.numpy as jnp
from jax import lax
from jax.experimental import pallas as pl
from jax.experimental.pallas import tpu as pltpu

D_MODEL = 2048
CHUNK = 64
DIFF_QK_DIM = 64
DIFF_V_DIM = 128
DIFF_HEADS = 8
ROT_DIM = 16
ROPE_THETA = 500000.0
GLA_HEADS = 4
GLA_V_DIM = 256
GLA_K_DIM = 128
GLA_GATE_RANK = 16
GLA_TAU = 16.0
N_GROUPS = 4
EXPERTS_PER_GROUP = 8
N_EXPERTS = N_GROUPS * EXPERTS_PER_GROUP
D_FF = 512
RMS_EPS = 1e-6
LAMBDA_INIT = 0.8 - 0.6 * math.exp(-0.3 * 0)
D_MAIN = 6144
GROUP_ROWS = 8
N_LOGIT_ROWS = GROUP_ROWS + N_EXPERTS

LANES = 128
SUB = D_MODEL // LANES
VMEM_LIMIT = 56 * 1024 * 1024

TM_IN = 1024
TN_IN = 1024
TQ = 256
GB = 256
TM_OUT = 512
RB_OUT = 256
TB = 256
TB_D = 1024
TM_E = 256

F32 = jnp.float32
BF16 = jnp.bfloat16
HI = lax.Precision.HIGHEST


def _params(n_axes):
    return pltpu.CompilerParams(dimension_semantics=("arbitrary",) * n_axes,
                                vmem_limit_bytes=VMEM_LIMIT)


def _nt_dot(a, b):
    return lax.dot_general(a, b, (((1,), (1,)), ((), ())), preferred_element_type=F32)


def _tn_dot(a, b):
    return lax.dot_general(a, b, (((0,), (0,)), ((), ())), preferred_element_type=F32)


def _silu(x):
    return x * (1.0 / (1.0 + jnp.exp(-x)))


def _store_token_tiles(ref, val, scr, tok0=0):
    n = val.shape[0]
    for s in range(SUB):
        scr[pl.ds(s, n, stride=SUB), :] = val[:, s * LANES:(s + 1) * LANES]
    ref[pl.ds(tok0 * SUB, n * SUB), :] = scr[...].astype(BF16)


def _load_token_tiles(ref, scr, n, tok0=0):
    scr[...] = ref[pl.ds(tok0 * SUB, n * SUB), :].astype(F32)
    return jnp.concatenate([scr[pl.ds(s, n, stride=SUB), :] for s in range(SUB)], axis=1)


def _rope_table_kernel(pos_ref, invf_ref, c_ref, s1_ref, s2_ref):
    ang = pos_ref[...] * invf_ref[...]
    d = lax.broadcasted_iota(jnp.int32, ang.shape, 1) % DIFF_QK_DIM
    cos = jnp.cos(ang)
    sin = jnp.sin(ang)
    half = ROT_DIM // 2
    c_ref[...] = jnp.where(d < ROT_DIM, cos, 1.0)
    s1_ref[...] = jnp.where(d < half, -sin, 0.0)
    s2_ref[...] = jnp.where((d >= half) & (d < ROT_DIM), sin, 0.0)


def _rope_tables(pos_b, invf):
    t = pos_b.shape[0]
    tb = 1024
    spec = pl.BlockSpec((tb, LANES), lambda i: (i, 0))
    return pl.pallas_call(
        _rope_table_kernel,
        grid=(t // tb,),
        in_specs=[spec, pl.BlockSpec((1, LANES), lambda i: (0, 0))],
        out_specs=[spec, spec, spec],
        out_shape=[jax.ShapeDtypeStruct((t, LANES), F32)] * 3,
        compiler_params=_params(1),
        name="rope_tables",
    )(pos_b, invf)


def _in_proj_kernel(x_ref, g_ref, wt_ref, wat_ref, proj_ref, ga_ref, h_scr):
    @pl.when(pl.program_id(1) == 0)
    def _():
        def body(c, carry):
            rows = pl.ds(c * 256, 256)
            x = x_ref[rows, :]
            ms = jnp.mean(x * x, axis=-1, keepdims=True)
            h_scr[rows, :] = (x * lax.rsqrt(ms + RMS_EPS) * g_ref[...]).astype(BF16)
            return carry
        lax.fori_loop(0, TM_IN // 256, body, 0)
        ga_ref[...] = _nt_dot(h_scr[...], wat_ref[...].astype(BF16))

    proj_ref[...] = _nt_dot(h_scr[...], wt_ref[...].astype(BF16)).astype(BF16)


def _in_proj(x2, g1, w_in, wa):
    t = x2.shape[0]
    return pl.pallas_call(
        _in_proj_kernel,
        grid=(t // TM_IN, D_MAIN // TN_IN),
        in_specs=[
            pl.BlockSpec((TM_IN, D_MODEL), lambda i, j: (i, 0)),
            pl.BlockSpec((1, D_MODEL), lambda i, j: (0, 0)),
            pl.BlockSpec((TN_IN, D_MODEL), lambda i, j: (j, 0)),
            pl.BlockSpec((LANES, D_MODEL), lambda i, j: (0, 0)),
        ],
        out_specs=[
            pl.BlockSpec((TM_IN, TN_IN), lambda i, j: (i, j)),
            pl.BlockSpec((TM_IN, LANES), lambda i, j: (i, 0)),
        ],
        out_shape=[jax.ShapeDtypeStruct((t, D_MAIN), BF16),
                   jax.ShapeDtypeStruct((t, LANES), F32)],
        scratch_shapes=[pltpu.VMEM((TM_IN, D_MODEL), BF16)],
        compiler_params=_params(2),
        name="in_proj",
    )(x2, g1, w_in, wa)


def _norm_rope(x, g, c, s1, s2):
    lo = lax.broadcasted_iota(jnp.int32, x.shape, 1) < DIFF_QK_DIM
    x2 = x * x
    s_lo = jnp.sum(jnp.where(lo, x2, 0.0), axis=-1, keepdims=True)
    s_hi = jnp.sum(jnp.where(lo, 0.0, x2), axis=-1, keepdims=True)
    ms = jnp.where(lo, s_lo, s_hi) * (1.0 / DIFF_QK_DIM)
    y = x * lax.rsqrt(ms + RMS_EPS) * g
    half = ROT_DIM // 2
    return y * c + pltpu.roll(y, LANES - half, 1) * s1 + pltpu.roll(y, half, 1) * s2


def _attn_kernel(q_ref, k_ref, v_ref, c_ref, s1_ref, s2_ref, qg_ref, kg_ref,
                 lq1_ref, lk1_ref, lq2_ref, lk2_ref, og_ref, o_ref,
                 q1_scr, q2_scr, k_scr, v_scr, *bufs):
    s_len = q_ref.shape[0]
    s_bufs, e_bufs = bufs[0:4], bufs[4:8]
    lam = (jnp.exp(jnp.sum(lq1_ref[...] * lk1_ref[...], axis=-1, keepdims=True))
           - jnp.exp(jnp.sum(lq2_ref[...] * lk2_ref[...], axis=-1, keepdims=True))
           + LAMBDA_INIT)
    lo = lax.broadcasted_iota(jnp.int32, (TQ, LANES), 1) < DIFF_QK_DIM
    v_scr[:, 0:LANES] = v_ref[...]
    v_scr[:, LANES:] = jnp.ones((s_len, LANES), BF16)

    def prepare(i):
        rows = slice(i * TQ, (i + 1) * TQ)
        c, s1, s2 = c_ref[rows, :], s1_ref[rows, :], s2_ref[rows, :]
        qn = (_norm_rope(q_ref[rows, :].astype(F32), qg_ref[...], c, s1, s2)
              * (DIFF_QK_DIM ** -0.5 * math.log2(math.e)))
        q1_scr[rows, :] = jnp.where(lo, qn, 0.0).astype(BF16)
        q2_scr[rows, :] = jnp.where(lo, 0.0, qn).astype(BF16)
        k_scr[rows, :] = _norm_rope(k_ref[rows, :].astype(F32), kg_ref[...], c, s1,
                                    s2).astype(BF16)

    diag = (lax.broadcasted_iota(jnp.int32, (TQ, TQ), 1) // CHUNK
            <= lax.broadcasted_iota(jnp.int32, (TQ, TQ), 0) // CHUNK)

    n_tiles = s_len // TQ
    items = [(i, q_scr) for i in range(n_tiles) for q_scr in (q1_scr, q2_scr)]

    def scores(r):
        i, q_scr = items[r]
        s_scr = s_bufs[r % 4]
        q = q_scr[i * TQ:(i + 1) * TQ, :]
        n_off = i * TQ
        s_scr[:, n_off:n_off + TQ] = jnp.where(
            diag, _nt_dot(q, k_scr[n_off:n_off + TQ, :]), -jnp.inf)
        if n_off:
            s_scr[:, 0:n_off] = _nt_dot(q, k_scr[0:n_off, :])

    def softmax_pv(r):
        nk = (items[r][0] + 1) * TQ
        s_scr, e_scr = s_bufs[r % 4], e_bufs[r % 4]
        m = jnp.max(s_scr[:, 0:nk], axis=-1, keepdims=True)
        e_scr[:, 0:nk] = jnp.exp2(s_scr[:, 0:nk] - m).astype(BF16)
        acc = jnp.dot(e_scr[:, 0:nk], v_scr[0:nk, :], preferred_element_type=F32)
        return acc[:, 0:LANES] / acc[:, LANES:]

    prepare(0)
    scores(0)
    scores(1)
    for i in range(n_tiles):
        rows = slice(i * TQ, (i + 1) * TQ)
        if i + 1 < n_tiles:
            prepare(i + 1)
            scores(2 * i + 2)
        o1 = softmax_pv(2 * i)
        if i + 1 < n_tiles:
            scores(2 * i + 3)
        o = o1 - lam * softmax_pv(2 * i + 1)
        ms = jnp.mean(o * o, axis=-1, keepdims=True)
        y = o * lax.rsqrt(ms + RMS_EPS) * og_ref[...] * (1.0 - LAMBDA_INIT)
        o_ref[rows, :] = y.astype(BF16)


def _diff_attention(proj, tabs, qg, kg, lq1, lk1, lq2, lk2, og, batch, seq):
    c, s1, s2 = tabs
    h = DIFF_HEADS
    blk = lambda off: pl.BlockSpec((seq, LANES), lambda b, hh, off=off: (b, off + hh))
    tab = pl.BlockSpec((seq, LANES), lambda b, hh: (b, 0))
    vec = lambda n: pl.BlockSpec((1, n), lambda b, hh: (0, 0))
    return pl.pallas_call(
        _attn_kernel,
        grid=(batch, h),
        in_specs=[blk(0), blk(h), blk(2 * h), tab, tab, tab,
                  vec(LANES), vec(LANES), vec(DIFF_QK_DIM), vec(DIFF_QK_DIM),
                  vec(DIFF_QK_DIM), vec(DIFF_QK_DIM), vec(LANES)],
        out_specs=pl.BlockSpec((seq, LANES), lambda b, hh: (b, hh)),
        out_shape=jax.ShapeDtypeStruct((batch * seq, h * DIFF_V_DIM), BF16),
        scratch_shapes=[pltpu.VMEM((seq, LANES), BF16)] * 3
        + [pltpu.VMEM((seq, 2 * LANES), BF16)]
        + [pltpu.VMEM((TQ, seq), F32)] * 4 + [pltpu.VMEM((TQ, seq), BF16)] * 4,
        compiler_params=_params(2),
        name="diff_attention",
    )(proj, proj, proj, c, s1, s2, qg, kg, lq1, lk1, lq2, lk2, og)


def _split_dot(ones_bf, x):
    hi = x.astype(BF16)
    lo = (x - hi.astype(F32)).astype(BF16)
    return (jnp.dot(ones_bf, hi, preferred_element_type=F32)
            + jnp.dot(ones_bf, lo, preferred_element_type=F32))


def _gla_kernel(q_ref, k_ref, v_ref, r_ref, ga_ref, w2_ref, b2_ref, og_ref, o_ref,
                qin_scr, ut_scr, dec_scr, st_scr, acc_scr):
    s_len = q_ref.shape[0]
    cpg = GB // CHUNK
    ri = lax.broadcasted_iota(jnp.int32, (GB, GB), 0)
    ci = lax.broadcasted_iota(jnp.int32, (GB, GB), 1)
    same = (ri // CHUNK) == (ci // CHUNK)
    blk_ones = jnp.where(same, 1.0, 0.0).astype(BF16)
    tril = same & (ci <= ri)
    tri_ones = jnp.where(tril, 1.0, 0.0).astype(BF16)

    def phase_a(g, carry):
        rows = pl.ds(pl.multiple_of(g * GB, GB), GB)
        pre = jnp.dot(ga_ref[rows, :].astype(BF16), w2_ref[...],
                      preferred_element_type=F32) + b2_ref[...]
        la = -(jnp.maximum(-pre, 0.0) + jnp.log1p(jnp.exp(-jnp.abs(pre)))) * (1.0 / GLA_TAU)
        bc = _split_dot(tri_ones, la)
        bl = _split_dot(blk_ones, la)
        e_neg = jnp.exp(-bc)
        e_last = jnp.exp(bl)
        k = k_ref[rows, :].astype(F32)
        q_in = (q_ref[rows, :].astype(F32) * (GLA_K_DIM ** -0.5) * jnp.exp(bc)).astype(BF16)
        k_in = (k * e_neg).astype(BF16)
        k_dec = (k * (e_last * e_neg)).astype(BF16)
        qin_scr[rows, :] = q_in
        dec_scr[rows, :] = e_last
        v = v_ref[rows, :]
        att = jnp.where(tril, _nt_dot(q_in, k_in), 0.0).astype(BF16)
        acc_scr[rows, :] = jnp.dot(att, v, preferred_element_type=F32)
        for c in range(cpg):
            cr = slice(c * CHUNK, (c + 1) * CHUNK)
            ut_scr[g * cpg + c] = _tn_dot(v[cr, :], k_dec[cr, :])
        return carry

    lax.fori_loop(0, s_len // GB, phase_a, 0, unroll=8)

    def phase_b(c, st):
        st_scr[c] = st.astype(BF16)
        dec = dec_scr[pl.ds(pl.multiple_of(c * CHUNK, CHUNK), 1), :]
        return dec * st + ut_scr[c]

    lax.fori_loop(0, s_len // CHUNK, phase_b, jnp.zeros((GLA_V_DIM, GLA_K_DIM), F32))

    def phase_c(g, carry):
        rows = pl.ds(pl.multiple_of(g * GB, GB), GB)
        inter = [_nt_dot(qin_scr[pl.ds(pl.multiple_of(g * GB + c * CHUNK, CHUNK), CHUNK), :],
                         st_scr[g * cpg + c]) for c in range(cpg)]
        o = acc_scr[rows, :] + jnp.concatenate(inter, axis=0)
        ms = jnp.mean(o * o, axis=-1, keepdims=True)
        y = o * lax.rsqrt(ms + RMS_EPS) * og_ref[...]
        o_ref[rows, :] = (y * _silu(r_ref[rows, :].astype(F32))).astype(BF16)
        return carry

    lax.fori_loop(0, s_len // GB, phase_c, 0, unroll=8)


def _gla(proj, ga, w2p, b2, og, batch, seq):
    hq = 3 * DIFF_HEADS
    kblk = lambda off: pl.BlockSpec((seq, GLA_K_DIM), lambda b, hh, off=off: (b, off + hh))
    vblk = lambda off: pl.BlockSpec((seq, GLA_V_DIM), lambda b, hh, off=off: (b, off + hh))
    return pl.pallas_call(
        _gla_kernel,
        grid=(batch, GLA_HEADS),
        in_specs=[kblk(hq), kblk(hq + GLA_HEADS), vblk(16), vblk(16 + GLA_HEADS),
                  pl.BlockSpec((seq, LANES), lambda b, hh: (b, 0)),
                  pl.BlockSpec((LANES, GLA_K_DIM), lambda b, hh: (0, hh)),
                  pl.BlockSpec((1, GLA_K_DIM), lambda b, hh: (0, hh)),
                  pl.BlockSpec((1, GLA_V_DIM), lambda b, hh: (0, 0))],
        out_specs=pl.BlockSpec((seq, GLA_V_DIM), lambda b, hh: (b, hh)),
        out_shape=jax.ShapeDtypeStruct((batch * seq, GLA_HEADS * GLA_V_DIM), BF16),
        scratch_shapes=[pltpu.VMEM((seq, GLA_K_DIM), BF16),
                        pltpu.VMEM((seq // CHUNK, GLA_V_DIM, GLA_K_DIM), F32),
                        pltpu.VMEM((seq, GLA_K_DIM), F32),
                        pltpu.VMEM((seq // CHUNK, GLA_V_DIM, GLA_K_DIM), BF16),
                        pltpu.VMEM((seq, GLA_V_DIM), F32)],
        compiler_params=_params(2),
        name="gla",
    )(proj, proj, proj, proj, ga, w2p, b2, og)


def _out_proj_kernel(a_ref, g_ref, x_ref, wo_ref, g2_ref, wr_ref, br_ref,
                     x1_ref, h2_ref, lg_ref, tt_scr):
    half = a_ref.shape[1]

    def body(c, carry):
        r0 = pl.multiple_of(c * RB_OUT, RB_OUT)
        rows = pl.ds(r0, RB_OUT)
        mixed = (jnp.dot(a_ref[rows, :], wo_ref[0:half, :], preferred_element_type=F32)
                 + jnp.dot(g_ref[rows, :], wo_ref[half:, :], preferred_element_type=F32))
        x1 = x_ref[rows, :] + mixed
        x1_ref[rows, :] = x1
        ms = jnp.mean(x1 * x1, axis=-1, keepdims=True)
        h2 = x1 * lax.rsqrt(ms + RMS_EPS) * g2_ref[...]
        _store_token_tiles(h2_ref, h2, tt_scr, r0)
        lg = jnp.dot(h2.astype(BF16), wr_ref[...], preferred_element_type=F32) + br_ref[...]
        lg_ref[:, rows] = lg.T[0:N_LOGIT_ROWS, :]
        return carry

    lax.fori_loop(0, TM_OUT // RB_OUT, body, 0)


def _out_proj(a_out, g_out, x2, wo, g2, wr, br):
    t = x2.shape[0]
    half = a_out.shape[1]
    row = lambda n: pl.BlockSpec((TM_OUT, n), lambda i: (i, 0))
    full = lambda r, n: pl.BlockSpec((r, n), lambda i: (0, 0))
    return pl.pallas_call(
        _out_proj_kernel,
        grid=(t // TM_OUT,),
        in_specs=[row(half), row(half), row(D_MODEL), full(D_MODEL, D_MODEL),
                  full(1, D_MODEL), full(D_MODEL, LANES), full(1, LANES)],
        out_specs=[row(D_MODEL), pl.BlockSpec((TM_OUT * SUB, LANES), lambda i: (i, 0)),
                   pl.BlockSpec((N_LOGIT_ROWS, TM_OUT), lambda i: (0, i))],
        out_shape=[jax.ShapeDtypeStruct((t, D_MODEL), F32),
                   jax.ShapeDtypeStruct((t * SUB, LANES), BF16),
                   jax.ShapeDtypeStruct((N_LOGIT_ROWS, t), F32)],
        scratch_shapes=[pltpu.VMEM((RB_OUT * SUB, LANES), F32)],
        compiler_params=_params(1),
        name="out_proj_router",
    )(a_out, g_out, x2, wo, g2, wr, br)


def _first_argmax(vals, row_f):
    m = jnp.max(vals, axis=0, keepdims=True)
    idx = jnp.min(jnp.where(vals == m, row_f, float(vals.shape[0])), axis=0, keepdims=True)
    return m, idx


def _softmax_rows(v):
    ex = jnp.exp(v - jnp.max(v, axis=0, keepdims=True))
    return ex / jnp.sum(ex, axis=0, keepdims=True)


def _routing_kernel(lg_ref, slots_ref, cw_ref, cnt_ref, tok_scr):
    t = lg_ref.shape[1]
    epg = EXPERTS_PER_GROUP
    row8 = lax.broadcasted_iota(jnp.int32, (8, TB), 0).astype(F32)
    row_e = lax.broadcasted_iota(jnp.int32, (N_EXPERTS, TB), 0).astype(F32)
    ui = lax.broadcasted_iota(jnp.int32, (TB, TB), 0)
    uj = lax.broadcasted_iota(jnp.int32, (TB, TB), 1)
    earlier = jnp.where(ui < uj, 1.0, 0.0).astype(BF16)
    all_ones = jnp.ones((TB, TB), BF16)

    def rows01(r0, r1):
        return jnp.where(row8 == 0.0, r0, jnp.where(row8 == 1.0, r1, jnp.zeros_like(r0)))

    def pass1(b, counts):
        cols = pl.ds(pl.multiple_of(b * TB, TB), TB)
        lg = lg_ref[:, cols]
        lg_groups = jnp.where(row8 < float(N_GROUPS), lg[0:GROUP_ROWS, :], -jnp.inf)
        pg_sel, g_sel = _first_argmax(_softmax_rows(lg_groups), row8)
        le = lg[GROUP_ROWS:GROUP_ROWS + epg, :]
        for g in range(1, N_GROUPS):
            le = jnp.where(g_sel == float(g),
                           lg[GROUP_ROWS + g * epg:GROUP_ROWS + (g + 1) * epg, :], le)
        pe = _softmax_rows(le)
        v1, i1 = _first_argmax(pe, row8)
        v2, i2 = _first_argmax(jnp.where(row8 == i1, -jnp.inf, pe), row8)
        tot = v1 + v2
        e1 = g_sel * epg + i1
        e2 = g_sel * epg + i2
        oh1 = row_e == e1
        oh2 = row_e == e2
        a = jnp.where(oh1 | oh2, 1.0, 0.0).astype(BF16)
        rank = jnp.dot(a, earlier, preferred_element_type=F32) + counts
        r1 = jnp.sum(jnp.where(oh1, rank, 0.0), axis=0, keepdims=True)
        r2 = jnp.sum(jnp.where(oh2, rank, 0.0), axis=0, keepdims=True)
        tok_scr[:, cols] = jnp.where(row8 == 0.0, e1, jnp.where(
            row8 == 1.0, e2, jnp.where(row8 == 2.0, r1, jnp.where(row8 == 3.0, r2, 0.0))))
        cw_ref[:, cols] = rows01((v1 / tot) * pg_sel, (v2 / tot) * pg_sel)
        return counts + jnp.dot(a, all_ones, preferred_element_type=F32)

    counts = lax.fori_loop(0, t // TB, pass1, jnp.zeros((N_EXPERTS, TB), F32))
    cnt_ref[...] = counts[:, 0:LANES]
    n_tiles = jnp.floor((counts + (TM_E - 1)) * (1.0 / TM_E))
    li = lax.broadcasted_iota(jnp.int32, (N_EXPERTS, N_EXPERTS), 0)
    lj = lax.broadcasted_iota(jnp.int32, (N_EXPERTS, N_EXPERTS), 1)
    lower = jnp.where(lj < li, 1.0, 0.0).astype(BF16)
    row_off = jnp.dot(lower, n_tiles.astype(BF16), preferred_element_type=F32) * TM_E

    def pass2(b, carry):
        cols = pl.ds(pl.multiple_of(b * TB, TB), TB)
        tok = tok_scr[:, cols]
        off1 = jnp.sum(jnp.where(row_e == tok[0:1, :], row_off, 0.0), axis=0, keepdims=True)
        off2 = jnp.sum(jnp.where(row_e == tok[1:2, :], row_off, 0.0), axis=0, keepdims=True)
        slots_ref[:, cols] = rows01(off1 + tok[2:3, :], off2 + tok[3:4, :]).astype(jnp.int32)
        return carry

    lax.fori_loop(0, t // TB, pass2, 0)


def _routing(logits_t):
    t = logits_t.shape[1]
    return pl.pallas_call(
        _routing_kernel,
        out_shape=[jax.ShapeDtypeStruct((8, t), jnp.int32),
                   jax.ShapeDtypeStruct((8, t), F32),
                   jax.ShapeDtypeStruct((N_EXPERTS, LANES), F32)],
        scratch_shapes=[pltpu.VMEM((8, t), F32)],
        compiler_params=pltpu.CompilerParams(vmem_limit_bytes=VMEM_LIMIT),
        name="routing",
    )(logits_t)


def _rows_copy(src, dst, sem, src_tok, dst_tok, n):
    first = lambda tok: tok * SUB if isinstance(tok, int) else pl.multiple_of(tok * SUB, SUB)
    s0 = first(src_tok)
    d0 = first(dst_tok)
    return pltpu.make_async_copy(src.at[pl.ds(s0, n * SUB), :], dst.at[pl.ds(d0, n * SUB), :], sem)


def _dispatch_kernel(sl1_ref, sl2_ref, ends_ref, h2_ref, xs_hbm, zero_scr, sem):
    step = pl.program_id(0)

    @pl.when(step == 0)
    def _():
        zero_scr[...] = jnp.zeros_like(zero_scr)
        for e in range(N_EXPERTS):
            start = jnp.maximum(ends_ref[e] - TM_E, 0)
            _rows_copy(zero_scr, xs_hbm, sem, 0, start, TM_E).start()
        for e in range(N_EXPERTS):
            _rows_copy(zero_scr, xs_hbm, sem, 0, 0, TM_E).wait()

    base = step * TB_D

    def issue(r, carry):
        _rows_copy(h2_ref, xs_hbm, sem, r, sl1_ref[base + r], 1).start(priority=0)
        _rows_copy(h2_ref, xs_hbm, sem, r, sl2_ref[base + r], 1).start(priority=1)
        return carry

    lax.fori_loop(0, TB_D, issue, 0, unroll=8)
    _rows_copy(h2_ref, xs_hbm, sem, 0, 0, TB_D).wait()
    _rows_copy(h2_ref, xs_hbm, sem, 0, 0, TB_D).wait()


def _dispatch(h2, sl1, sl2, ends, n_rows):
    t = h2.shape[0] // SUB
    return pl.pallas_call(
        _dispatch_kernel,
        grid_spec=pltpu.PrefetchScalarGridSpec(
            num_scalar_prefetch=3,
            grid=(t // TB_D,),
            in_specs=[pl.BlockSpec((TB_D * SUB, LANES), lambda i, *_: (i, 0))],
            out_specs=pl.BlockSpec(memory_space=pl.ANY),
            scratch_shapes=[pltpu.VMEM((TM_E * SUB, LANES), BF16), pltpu.SemaphoreType.DMA(())],
        ),
        out_shape=jax.ShapeDtypeStruct((n_rows * SUB, LANES), BF16),
        compiler_params=_params(1),
        name="dispatch",
    )(sl1, sl2, ends, h2)


def _expert_kernel(tidx_ref, texp_ref, tfirst_ref, tslot_ref, tnext_ref, nt_ref,
                   xs_ref, wg_hbm, wu_hbm, wd_hbm, ys_ref,
                   wg_buf, wu_buf, wd_buf, wg_bf, wu_bf, wd_bf, tt_scr, sems):
    j = pl.program_id(0)

    def weight_copies(e, slot):
        return [pltpu.make_async_copy(hbm.at[e], buf.at[slot], sems.at[slot, k])
                for k, (hbm, buf) in enumerate(((wg_hbm, wg_buf), (wu_hbm, wu_buf),
                                                (wd_hbm, wd_buf)))]

    @pl.when(j < nt_ref[0])
    def _():
        slot = tslot_ref[j]

        @pl.when(tfirst_ref[j] == 1)
        def _():
            @pl.when(j == 0)
            def _():
                for c in weight_copies(texp_ref[j], slot):
                    c.start()
            for c in weight_copies(texp_ref[j], slot):
                c.wait()

            @pl.when(tnext_ref[j] >= 0)
            def _():
                for c in weight_copies(tnext_ref[j], 1 - slot):
                    c.start()
            wg_bf[...] = wg_buf[slot].astype(BF16)
            wu_bf[...] = wu_buf[slot].astype(BF16)
            wd_bf[...] = wd_buf[slot].astype(BF16)

        xb = _load_token_tiles(xs_ref, tt_scr, TM_E).astype(BF16)
        g = jnp.dot(xb, wg_bf[...], preferred_element_type=F32)
        u = jnp.dot(xb, wu_bf[...], preferred_element_type=F32)
        h = (_silu(g) * u).astype(BF16)
        _store_token_tiles(ys_ref, jnp.dot(h, wd_bf[...], preferred_element_type=F32), tt_scr)


def _expert_mlp(xs, tile_meta, wg, wu, wd):
    n_rows = xs.shape[0] // SUB
    tile = pl.BlockSpec((TM_E * SUB, LANES), lambda j, ti, *_: (ti[j], 0))
    hbm = pl.BlockSpec(memory_space=pl.ANY)
    return pl.pallas_call(
        _expert_kernel,
        grid_spec=pltpu.PrefetchScalarGridSpec(
            num_scalar_prefetch=len(tile_meta),
            grid=(n_rows // TM_E,),
            in_specs=[tile, hbm, hbm, hbm],
            out_specs=tile,
            scratch_shapes=[pltpu.VMEM((2, D_MODEL, D_FF), F32),
                            pltpu.VMEM((2, D_MODEL, D_FF), F32),
                            pltpu.VMEM((2, D_FF, D_MODEL), F32),
                            pltpu.VMEM((D_MODEL, D_FF), BF16),
                            pltpu.VMEM((D_MODEL, D_FF), BF16),
                            pltpu.VMEM((D_FF, D_MODEL), BF16),
                            pltpu.VMEM((TM_E * SUB, LANES), F32),
                            pltpu.SemaphoreType.DMA((2, 3))],
        ),
        out_shape=jax.ShapeDtypeStruct((n_rows * SUB, LANES), BF16),
        compiler_params=_params(1),
        name="expert_mlp",
    )(*tile_meta, xs, wg, wu, wd)


def _combine_kernel(sl1_ref, sl2_ref, x1_ref, cw_ref, ys_hbm, o_ref, y1_scr, y2_scr, tt1_scr,
                    tt2_scr, sems):
    i = pl.program_id(0)

    def issue(step, slot):
        base = step * TB

        def body(r, carry):
            _rows_copy(ys_hbm, y1_scr.at[slot], sems.at[slot], sl1_ref[base + r], r,
                       1).start(priority=0)
            _rows_copy(ys_hbm, y2_scr.at[slot], sems.at[slot], sl2_ref[base + r], r,
                       1).start(priority=1)
            return carry

        lax.fori_loop(0, TB, body, 0, unroll=8)

    @pl.when(i == 0)
    def _():
        issue(0, 0)

    for parity in (0, 1):
        @pl.when((i + 1 < pl.num_programs(0)) & (i % 2 == parity))
        def _():
            issue(i + 1, 1 - parity)

    slot = i % 2
    _rows_copy(ys_hbm, y1_scr.at[slot], sems.at[slot], 0, 0, TB).wait()
    _rows_copy(ys_hbm, y2_scr.at[slot], sems.at[slot], 0, 0, TB).wait()
    cw = jnp.concatenate([cw_ref[...], jnp.zeros((LANES - 8, TB), F32)], axis=0).T
    tt1_scr[...] = y1_scr[slot].astype(F32)
    tt2_scr[...] = y2_scr[slot].astype(F32)
    rb = 64
    for r0 in range(0, TB, rb):
        rows = slice(r0, r0 + rb)
        c1 = jnp.broadcast_to(cw[rows, 0:1], (rb, LANES))
        c2 = jnp.broadcast_to(cw[rows, 1:2], (rb, LANES))
        for s in range(SUB):
            cols = slice(s * LANES, (s + 1) * LANES)
            tiles = pl.ds(r0 * SUB + s, rb, stride=SUB)
            o_ref[rows, cols] = x1_ref[rows, cols] + c1 * tt1_scr[tiles, :] + c2 * tt2_scr[tiles, :]


def _combine(x1, cw, ys, sl1, sl2):
    t = x1.shape[0]
    return pl.pallas_call(
        _combine_kernel,
        grid_spec=pltpu.PrefetchScalarGridSpec(
            num_scalar_prefetch=2,
            grid=(t // TB,),
            in_specs=[pl.BlockSpec((TB, D_MODEL), lambda i, *_: (i, 0)),
                      pl.BlockSpec((8, TB), lambda i, *_: (0, i)),
                      pl.BlockSpec(memory_space=pl.ANY)],
            out_specs=pl.BlockSpec((TB, D_MODEL), lambda i, *_: (i, 0)),
            scratch_shapes=[pltpu.VMEM((2, TB * SUB, LANES), BF16),
                            pltpu.VMEM((2, TB * SUB, LANES), BF16),
                            pltpu.VMEM((TB * SUB, LANES), F32),
                            pltpu.VMEM((TB * SUB, LANES), F32),
                            pltpu.SemaphoreType.DMA((2,))],
        ),
        out_shape=jax.ShapeDtypeStruct((t, D_MODEL), F32),
        compiler_params=_params(1),
        name="combine",
    )(sl1, sl2, x1, cw, ys)


def _lane_tile(v, reps):
    return jnp.tile(v.reshape(1, -1), (1, reps))


def kernel(x, positions, norm1_g, w_in, q_norm_g, k_norm_g, lambda_q1, lambda_k1, lambda_q2,
           lambda_k2, diff_out_norm_g, gla_w_gate2, gla_b_gate, gla_out_norm_g, w_out, norm2_g,
           w_router_group, b_router_group, w_router_expert, b_router_expert, w_gate_expert,
           w_up_expert, w_down_expert):
    batch, seq, d = x.shape
    t = batch * seq
    x2 = x.reshape(t, d)

    inv = ROPE_THETA ** (-jnp.arange(0, ROT_DIM, 2, dtype=F32) / ROT_DIM)
    lane_d = jnp.arange(LANES) % DIFF_QK_DIM
    invf = jnp.where(lane_d < ROT_DIM, inv[lane_d % (ROT_DIM // 2)], 0.0).reshape(1, LANES)
    pos_b = jnp.broadcast_to(positions.astype(F32).reshape(t, 1), (t, LANES))
    tabs = _rope_tables(pos_b, invf)

    w_in_t = jnp.swapaxes(w_in, 1, 2)[0]
    wa_t = jnp.pad(w_in_t[D_MAIN:, :], ((0, LANES - GLA_GATE_RANK), (0, 0)))
    proj, ga = _in_proj(x2, norm1_g, w_in_t, wa_t)

    a_out = _diff_attention(
        proj, tabs, _lane_tile(q_norm_g[0], 2), _lane_tile(k_norm_g[0], 2),
        lambda_q1, lambda_k1, lambda_q2, lambda_k2, diff_out_norm_g, batch, seq)
    w2p = jnp.pad(gla_w_gate2[0], ((0, LANES - GLA_GATE_RANK), (0, 0))).astype(BF16)
    g_out = _gla(proj, ga, w2p, gla_b_gate, gla_out_norm_g, batch, seq)

    gpad = GROUP_ROWS - N_GROUPS
    lpad = LANES - N_LOGIT_ROWS
    wr = jnp.concatenate(
        [jnp.pad(w_router_group[0], ((0, 0), (0, gpad))),
         jnp.pad(w_router_expert[0].transpose(1, 0, 2).reshape(d, N_EXPERTS),
                 ((0, 0), (0, lpad)))], axis=1)
    br = jnp.concatenate([jnp.pad(b_router_group[0], (0, gpad)),
                          jnp.pad(b_router_expert[0].reshape(-1), (0, lpad))]).reshape(1, LANES)
    x1, h2, logits_t = _out_proj(a_out, g_out, x2, w_out[0].astype(BF16), norm2_g,
                                 wr.astype(BF16), br)

    slots, cw, counts = _routing(logits_t)
    sl1 = slots[0]
    sl2 = slots[1]
    n_rows = 2 * t + N_EXPERTS * TM_E
    max_tiles = n_rows // TM_E
    cnt = counts[:, 0].astype(jnp.int32)
    tile_end = jnp.cumsum((cnt + (TM_E - 1)) // TM_E)
    n_tiles = tile_end[-1:]
    tile_idx = jnp.minimum(jnp.arange(max_tiles, dtype=jnp.int32), n_tiles[0] - 1)
    tile_exp = jnp.sum(tile_idx[:, None] >= tile_end[None, :], axis=1).astype(jnp.int32)
    row_end = (tile_end * TM_E).astype(jnp.int32)
    tile_first = jnp.concatenate(
        [jnp.ones((1,), jnp.int32), (tile_exp[1:] != tile_exp[:-1]).astype(jnp.int32)])
    tile_slot = (jnp.cumsum(tile_first) - 1) % 2
    nxt = tile_end[tile_exp]
    tile_next = jnp.where(nxt < n_tiles[0], tile_exp[jnp.minimum(nxt, max_tiles - 1)], -1)
    tile_meta = [a.astype(jnp.int32)
                 for a in (tile_idx, tile_exp, tile_first, tile_slot, tile_next, n_tiles)]

    xs = _dispatch(h2, sl1, sl2, row_end, n_rows)
    ys = _expert_mlp(xs, tile_meta,
                     w_gate_expert[0].reshape(N_EXPERTS, d, D_FF),
                     w_up_expert[0].reshape(N_EXPERTS, d, D_FF),
                     w_down_expert[0].reshape(N_EXPERTS, D_FF, d))
    out = _combine(x1, cw, ys, sl1, sl2)
    return out.reshape(batch, seq, d)
```

```python
import functools
import math

import jax
import jax.numpy as jnp
from jax import lax
from jax.experimental import pallas as pl
from jax.experimental.pallas import tpu as pltpu

D_MODEL = 2048
CHUNK = 64
DIFF_QK_DIM = 64
DIFF_V_DIM = 128
DIFF_HEADS = 8
ROT_DIM = 16
ROPE_THETA = 500000.0
GLA_HEADS = 4
GLA_V_DIM = 256
GLA_K_DIM = 128
GLA_GATE_RANK = 16
GLA_TAU = 16.0
N_GROUPS = 4
EXPERTS_PER_GROUP = 8
N_EXPERTS = N_GROUPS * EXPERTS_PER_GROUP
D_FF = 512
RMS_EPS = 1e-6
LAMBDA_INIT = 0.8 - 0.6 * math.exp(-0.3 * 0)
D_MAIN = 6144
GROUP_ROWS = 8
N_LOGIT_ROWS = GROUP_ROWS + N_EXPERTS

LANES = 128
SUB = D_MODEL // LANES
VMEM_LIMIT = 56 * 1024 * 1024

TM_IN = 1024
TN_IN = 1024
TQ = 256
GB = 256
TM_OUT = 512
RB_OUT = 256
TB = 256
TB_D = 1024
TM_E = 256

F32 = jnp.float32
BF16 = jnp.bfloat16
HI = lax.Precision.HIGHEST


def _params(n_axes):
    return pltpu.CompilerParams(dimension_semantics=("arbitrary",) * n_axes,
                                vmem_limit_bytes=VMEM_LIMIT)


def _nt_dot(a, b):
    return lax.dot_general(a, b, (((1,), (1,)), ((), ())), preferred_element_type=F32)


def _tn_dot(a, b):
    return lax.dot_general(a, b, (((0,), (0,)), ((), ())), preferred_element_type=F32)


def _silu(x):
    return x * (1.0 / (1.0 + jnp.exp(-x)))


def _store_token_tiles(ref, val, scr, tok0=0):
    n = val.shape[0]
    for s in range(SUB):
        scr[pl.ds(s, n, stride=SUB), :] = val[:, s * LANES:(s + 1) * LANES]
    ref[pl.ds(tok0 * SUB, n * SUB), :] = scr[...].astype(BF16)


def _load_token_tiles(ref, scr, n, tok0=0):
    scr[...] = ref[pl.ds(tok0 * SUB, n * SUB), :].astype(F32)
    return jnp.concatenate([scr[pl.ds(s, n, stride=SUB), :] for s in range(SUB)], axis=1)


def _rope_table_kernel(pos_ref, invf_ref, c_ref, s1_ref, s2_ref):
    ang = pos_ref[...] * invf_ref[...]
    d = lax.broadcasted_iota(jnp.int32, ang.shape, 1) % DIFF_QK_DIM
    cos = jnp.cos(ang)
    sin = jnp.sin(ang)
    half = ROT_DIM // 2
    c_ref[...] = jnp.where(d < ROT_DIM, cos, 1.0)
    s1_ref[...] = jnp.where(d < half, -sin, 0.0)
    s2_ref[...] = jnp.where((d >= half) & (d < ROT_DIM), sin, 0.0)


def _rope_tables(pos_b, invf):
    t = pos_b.shape[0]
    tb = 1024
    spec = pl.BlockSpec((tb, LANES), lambda i: (i, 0))
    return pl.pallas_call(
        _rope_table_kernel,
        grid=(t // tb,),
        in_specs=[spec, pl.BlockSpec((1, LANES), lambda i: (0, 0))],
        out_specs=[spec, spec, spec],
        out_shape=[jax.ShapeDtypeStruct((t, LANES), F32)] * 3,
        compiler_params=_params(1),
        name="rope_tables",
    )(pos_b, invf)


def _in_proj_kernel(x_ref, g_ref, wt_ref, wat_ref, proj_ref, ga_ref, h_scr):
    @pl.when(pl.program_id(1) == 0)
    def _():
        def body(c, carry):
            rows = pl.ds(c * 256, 256)
            x = x_ref[rows, :]
            ms = jnp.mean(x * x, axis=-1, keepdims=True)
            h_scr[rows, :] = (x * lax.rsqrt(ms + RMS_EPS) * g_ref[...]).astype(BF16)
            return carry
        lax.fori_loop(0, TM_IN // 256, body, 0)
        ga_ref[...] = _nt_dot(h_scr[...], wat_ref[...].astype(BF16))

    proj_ref[...] = _nt_dot(h_scr[...], wt_ref[...].astype(BF16)).astype(BF16)


def _in_proj(x2, g1, w_in, wa):
    t = x2.shape[0]
    return pl.pallas_call(
        _in_proj_kernel,
        grid=(t // TM_IN, D_MAIN // TN_IN),
        in_specs=[
            pl.BlockSpec((TM_IN, D_MODEL), lambda i, j: (i, 0)),
            pl.BlockSpec((1, D_MODEL), lambda i, j: (0, 0)),
            pl.BlockSpec((TN_IN, D_MODEL), lambda i, j: (j, 0)),
            pl.BlockSpec((LANES, D_MODEL), lambda i, j: (0, 0)),
        ],
        out_specs=[
            pl.BlockSpec((TM_IN, TN_IN), lambda i, j: (i, j)),
            pl.BlockSpec((TM_IN, LANES), lambda i, j: (i, 0)),
        ],
        out_shape=[jax.ShapeDtypeStruct((t, D_MAIN), BF16),
                   jax.ShapeDtypeStruct((t, LANES), F32)],
        scratch_shapes=[pltpu.VMEM((TM_IN, D_MODEL), BF16)],
        compiler_params=_params(2),
        name="in_proj",
    )(x2, g1, w_in, wa)


def _norm_rope(x, g, c, s1, s2):
    lo = lax.broadcasted_iota(jnp.int32, x.shape, 1) < DIFF_QK_DIM
    x2 = x * x
    s_lo = jnp.sum(jnp.where(lo, x2, 0.0), axis=-1, keepdims=True)
    s_hi = jnp.sum(jnp.where(lo, 0.0, x2), axis=-1, keepdims=True)
    ms = jnp.where(lo, s_lo, s_hi) * (1.0 / DIFF_QK_DIM)
    y = x * lax.rsqrt(ms + RMS_EPS) * g
    half = ROT_DIM // 2
    return y * c + pltpu.roll(y, LANES - half, 1) * s1 + pltpu.roll(y, half, 1) * s2


def _attn_kernel(q_ref, k_ref, v_ref, c_ref, s1_ref, s2_ref, qg_ref, kg_ref,
                 lq1_ref, lk1_ref, lq2_ref, lk2_ref, og_ref, o_ref,
                 q1_scr, q2_scr, k_scr, v_scr, *bufs):
    s_len = q_ref.shape[0]
    s_bufs, e_bufs = bufs[0:4], bufs[4:8]
    lam = (jnp.exp(jnp.sum(lq1_ref[...] * lk1_ref[...], axis=-1, keepdims=True))
           - jnp.exp(jnp.sum(lq2_ref[...] * lk2_ref[...], axis=-1, keepdims=True))
           + LAMBDA_INIT)
    lo = lax.broadcasted_iota(jnp.int32, (TQ, LANES), 1) < DIFF_QK_DIM
    v_scr[:, 0:LANES] = v_ref[...]
    v_scr[:, LANES:] = jnp.ones((s_len, LANES), BF16)

    def prepare(i):
        rows = slice(i * TQ, (i + 1) * TQ)
        c, s1, s2 = c_ref[rows, :], s1_ref[rows, :], s2_ref[rows, :]
        qn = (_norm_rope(q_ref[rows, :].astype(F32), qg_ref[...], c, s1, s2)
              * (DIFF_QK_DIM ** -0.5 * math.log2(math.e)))
        q1_scr[rows, :] = jnp.where(lo, qn, 0.0).astype(BF16)
        q2_scr[rows, :] = jnp.where(lo, 0.0, qn).astype(BF16)
        k_scr[rows, :] = _norm_rope(k_ref[rows, :].astype(F32), kg_ref[...], c, s1,
                                    s2).astype(BF16)

    diag = (lax.broadcasted_iota(jnp.int32, (TQ, TQ), 1) // CHUNK
            <= lax.broadcasted_iota(jnp.int32, (TQ, TQ), 0) // CHUNK)

    n_tiles = s_len // TQ
    items = [(i, q_scr) for i in range(n_tiles) for q_scr in (q1_scr, q2_scr)]

    def scores(r):
        i, q_scr = items[r]
        s_scr = s_bufs[r % 4]
        q = q_scr[i * TQ:(i + 1) * TQ, :]
        n_off = i * TQ
        s_scr[:, n_off:n_off + TQ] = jnp.where(
            diag, _nt_dot(q, k_scr[n_off:n_off + TQ, :]), -jnp.inf)
        if n_off:
            s_scr[:, 0:n_off] = _nt_dot(q, k_scr[0:n_off, :])

    def softmax_pv(r):
        nk = (items[r][0] + 1) * TQ
        s_scr, e_scr = s_bufs[r % 4], e_bufs[r % 4]
        m = jnp.max(s_scr[:, 0:nk], axis=-1, keepdims=True)
        e_scr[:, 0:nk] = jnp.exp2(s_scr[:, 0:nk] - m).astype(BF16)
        acc = jnp.dot(e_scr[:, 0:nk], v_scr[0:nk, :], preferred_element_type=F32)
        return acc[:, 0:LANES] / acc[:, LANES:]

    prepare(0)
    scores(0)
    scores(1)
    for i in range(n_tiles):
        rows = slice(i * TQ, (i + 1) * TQ)
        if i + 1 < n_tiles:
            prepare(i + 1)
            scores(2 * i + 2)
        o1 = softmax_pv(2 * i)
        if i + 1 < n_tiles:
            scores(2 * i + 3)
        o = o1 - lam * softmax_pv(2 * i + 1)
        ms = jnp.mean(o * o, axis=-1, keepdims=True)
        y = o * lax.rsqrt(ms + RMS_EPS) * og_ref[...] * (1.0 - LAMBDA_INIT)
        o_ref[rows, :] = y.astype(BF16)


def _diff_attention(proj, tabs, qg, kg, lq1, lk1, lq2, lk2, og, batch, seq):
    c, s1, s2 = tabs
    h = DIFF_HEADS
    blk = lambda off: pl.BlockSpec((seq, LANES), lambda b, hh, off=off: (b, off + hh))
    tab = pl.BlockSpec((seq, LANES), lambda b, hh: (b, 0))
    vec = lambda n: pl.BlockSpec((1, n), lambda b, hh: (0, 0))
    return pl.pallas_call(
        _attn_kernel,
        grid=(batch, h),
        in_specs=[blk(0), blk(h), blk(2 * h), tab, tab, tab,
                  vec(LANES), vec(LANES), vec(DIFF_QK_DIM), vec(DIFF_QK_DIM),
                  vec(DIFF_QK_DIM), vec(DIFF_QK_DIM), vec(LANES)],
        out_specs=pl.BlockSpec((seq, LANES), lambda b, hh: (b, hh)),
        out_shape=jax.ShapeDtypeStruct((batch * seq, h * DIFF_V_DIM), BF16),
        scratch_shapes=[pltpu.VMEM((seq, LANES), BF16)] * 3
        + [pltpu.VMEM((seq, 2 * LANES), BF16)]
        + [pltpu.VMEM((TQ, seq), F32)] * 4 + [pltpu.VMEM((TQ, seq), BF16)] * 4,
        compiler_params=_params(2),
        name="diff_attention",
    )(proj, proj, proj, c, s1, s2, qg, kg, lq1, lk1, lq2, lk2, og)


def _split_dot(ones_bf, x):
    hi = x.astype(BF16)
    lo = (x - hi.astype(F32)).astype(BF16)
    return (jnp.dot(ones_bf, hi, preferred_element_type=F32)
            + jnp.dot(ones_bf, lo, preferred_element_type=F32))


def _gla_kernel(q_ref, k_ref, v_ref, r_ref, ga_ref, w2_ref, b2_ref, og_ref, o_ref,
                qin_scr, ut_scr, dec_scr, st_scr, acc_scr):
    s_len = q_ref.shape[0]
    cpg = GB // CHUNK
    ri = lax.broadcasted_iota(jnp.int32, (GB, GB), 0)
    ci = lax.broadcasted_iota(jnp.int32, (GB, GB), 1)
    same = (ri // CHUNK) == (ci // CHUNK)
    blk_ones = jnp.where(same, 1.0, 0.0).astype(BF16)
    tril = same & (ci <= ri)
    tri_ones = jnp.where(tril, 1.0, 0.0).astype(BF16)

    def phase_a(g, carry):
        rows = pl.ds(pl.multiple_of(g * GB, GB), GB)
        pre = jnp.dot(ga_ref[rows, :].astype(BF16), w2_ref[...],
                      preferred_element_type=F32) + b2_ref[...]
        la = -(jnp.maximum(-pre, 0.0) + jnp.log1p(jnp.exp(-jnp.abs(pre)))) * (1.0 / GLA_TAU)
        bc = _split_dot(tri_ones, la)
        bl = _split_dot(blk_ones, la)
        e_neg = jnp.exp(-bc)
        e_last = jnp.exp(bl)
        k = k_ref[rows, :].astype(F32)
        q_in = (q_ref[rows, :].astype(F32) * (GLA_K_DIM ** -0.5) * jnp.exp(bc)).astype(BF16)
        k_in = (k * e_neg).astype(BF16)
        k_dec = (k * (e_last * e_neg)).astype(BF16)
        qin_scr[rows, :] = q_in
        dec_scr[rows, :] = e_last
        v = v_ref[rows, :]
        att = jnp.where(tril, _nt_dot(q_in, k_in), 0.0).astype(BF16)
        acc_scr[rows, :] = jnp.dot(att, v, preferred_element_type=F32)
        for c in range(cpg):
            cr = slice(c * CHUNK, (c + 1) * CHUNK)
            ut_scr[g * cpg + c] = _tn_dot(v[cr, :], k_dec[cr, :])
        return carry

    lax.fori_loop(0, s_len // GB, phase_a, 0, unroll=8)

    def phase_b(c, st):
        st_scr[c] = st.astype(BF16)
        dec = dec_scr[pl.ds(pl.multiple_of(c * CHUNK, CHUNK), 1), :]
        return dec * st + ut_scr[c]

    lax.fori_loop(0, s_len // CHUNK, phase_b, jnp.zeros((GLA_V_DIM, GLA_K_DIM), F32))

    def phase_c(g, carry):
        rows = pl.ds(pl.multiple_of(g * GB, GB), GB)
        inter = [_nt_dot(qin_scr[pl.ds(pl.multiple_of(g * GB + c * CHUNK, CHUNK), CHUNK), :],
                         st_scr[g * cpg + c]) for c in range(cpg)]
        o = acc_scr[rows, :] + jnp.concatenate(inter, axis=0)
        ms = jnp.mean(o * o, axis=-1, keepdims=True)
        y = o * lax.rsqrt(ms + RMS_EPS) * og_ref[...]
        o_ref[rows, :] = (y * _silu(r_ref[rows, :].astype(F32))).astype(BF16)
        return carry

    lax.fori_loop(0, s_len // GB, phase_c, 0, unroll=8)


def _gla(proj, ga, w2p, b2, og, batch, seq):
    hq = 3 * DIFF_HEADS
    kblk = lambda off: pl.BlockSpec((seq, GLA_K_DIM), lambda b, hh, off=off: (b, off + hh))
    vblk = lambda off: pl.BlockSpec((seq, GLA_V_DIM), lambda b, hh, off=off: (b, off + hh))
    return pl.pallas_call(
        _gla_kernel,
        grid=(batch, GLA_HEADS),
        in_specs=[kblk(hq), kblk(hq + GLA_HEADS), vblk(16), vblk(16 + GLA_HEADS),
                  pl.BlockSpec((seq, LANES), lambda b, hh: (b, 0)),
                  pl.BlockSpec((LANES, GLA_K_DIM), lambda b, hh: (0, hh)),
                  pl.BlockSpec((1, GLA_K_DIM), lambda b, hh: (0, hh)),
                  pl.BlockSpec((1, GLA_V_DIM), lambda b, hh: (0, 0))],
        out_specs=pl.BlockSpec((seq, GLA_V_DIM), lambda b, hh: (b, hh)),
        out_shape=jax.ShapeDtypeStruct((batch * seq, GLA_HEADS * GLA_V_DIM), BF16),
        scratch_shapes=[pltpu.VMEM((seq, GLA_K_DIM), BF16),
                        pltpu.VMEM((seq // CHUNK, GLA_V_DIM, GLA_K_DIM), F32),
                        pltpu.VMEM((seq, GLA_K_DIM), F32),
                        pltpu.VMEM((seq // CHUNK, GLA_V_DIM, GLA_K_DIM), BF16),
                        pltpu.VMEM((seq, GLA_V_DIM), F32)],
        compiler_params=_params(2),
        name="gla",
    )(proj, proj, proj, proj, ga, w2p, b2, og)


def _out_proj_kernel(a_ref, g_ref, x_ref, wo_ref, g2_ref, wr_ref, br_ref,
                     x1_ref, h2_ref, lg_ref, tt_scr):
    half = a_ref.shape[1]

    def body(c, carry):
        r0 = pl.multiple_of(c * RB_OUT, RB_OUT)
        rows = pl.ds(r0, RB_OUT)
        mixed = (jnp.dot(a_ref[rows, :], wo_ref[0:half, :], preferred_element_type=F32)
                 + jnp.dot(g_ref[rows, :], wo_ref[half:, :], preferred_element_type=F32))
        x1 = x_ref[rows, :] + mixed
        x1_ref[rows, :] = x1
        ms = jnp.mean(x1 * x1, axis=-1, keepdims=True)
        h2 = x1 * lax.rsqrt(ms + RMS_EPS) * g2_ref[...]
        _store_token_tiles(h2_ref, h2, tt_scr, r0)
        lg = jnp.dot(h2.astype(BF16), wr_ref[...], preferred_element_type=F32) + br_ref[...]
        lg_ref[:, rows] = lg.T[0:N_LOGIT_ROWS, :]
        return carry

    lax.fori_loop(0, TM_OUT // RB_OUT, body, 0)


def _out_proj(a_out, g_out, x2, wo, g2, wr, br):
    t = x2.shape[0]
    half = a_out.shape[1]
    row = lambda n: pl.BlockSpec((TM_OUT, n), lambda i: (i, 0))
    full = lambda r, n: pl.BlockSpec((r, n), lambda i: (0, 0))
    return pl.pallas_call(
        _out_proj_kernel,
        grid=(t // TM_OUT,),
        in_specs=[row(half), row(half), row(D_MODEL), full(D_MODEL, D_MODEL),
                  full(1, D_MODEL), full(D_MODEL, LANES), full(1, LANES)],
        out_specs=[row(D_MODEL), pl.BlockSpec((TM_OUT * SUB, LANES), lambda i: (i, 0)),
                   pl.BlockSpec((N_LOGIT_ROWS, TM_OUT), lambda i: (0, i))],
        out_shape=[jax.ShapeDtypeStruct((t, D_MODEL), F32),
                   jax.ShapeDtypeStruct((t * SUB, LANES), BF16),
                   jax.ShapeDtypeStruct((N_LOGIT_ROWS, t), F32)],
        scratch_shapes=[pltpu.VMEM((RB_OUT * SUB, LANES), F32)],
        compiler_params=_params(1),
        name="out_proj_router",
    )(a_out, g_out, x2, wo, g2, wr, br)


def _first_argmax(vals, row_f):
    m = jnp.max(vals, axis=0, keepdims=True)
    idx = jnp.min(jnp.where(vals == m, row_f, float(vals.shape[0])), axis=0, keepdims=True)
    return m, idx


def _softmax_rows(v):
    ex = jnp.exp(v - jnp.max(v, axis=0, keepdims=True))
    return ex / jnp.sum(ex, axis=0, keepdims=True)


def _routing_kernel(lg_ref, slots_ref, cw_ref, cnt_ref, tok_scr):
    t = lg_ref.shape[1]
    epg = EXPERTS_PER_GROUP
    row8 = lax.broadcasted_iota(jnp.int32, (8, TB), 0).astype(F32)
    row_e = lax.broadcasted_iota(jnp.int32, (N_EXPERTS, TB), 0).astype(F32)
    ui = lax.broadcasted_iota(jnp.int32, (TB, TB), 0)
    uj = lax.broadcasted_iota(jnp.int32, (TB, TB), 1)
    earlier = jnp.where(ui < uj, 1.0, 0.0).astype(BF16)
    all_ones = jnp.ones((TB, TB), BF16)

    def rows01(r0, r1):
        return jnp.where(row8 == 0.0, r0, jnp.where(row8 == 1.0, r1, jnp.zeros_like(r0)))

    def pass1(b, counts):
        cols = pl.ds(pl.multiple_of(b * TB, TB), TB)
        lg = lg_ref[:, cols]
        lg_groups = jnp.where(row8 < float(N_GROUPS), lg[0:GROUP_ROWS, :], -jnp.inf)
        pg_sel, g_sel = _first_argmax(_softmax_rows(lg_groups), row8)
        le = lg[GROUP_ROWS:GROUP_ROWS + epg, :]
        for g in range(1, N_GROUPS):
            le = jnp.where(g_sel == float(g),
                           lg[GROUP_ROWS + g * epg:GROUP_ROWS + (g + 1) * epg, :], le)
        pe = _softmax_rows(le)
        v1, i1 = _first_argmax(pe, row8)
        v2, i2 = _first_argmax(jnp.where(row8 == i1, -jnp.inf, pe), row8)
        tot = v1 + v2
        e1 = g_sel * epg + i1
        e2 = g_sel * epg + i2
        oh1 = row_e == e1
        oh2 = row_e == e2
        a = jnp.where(oh1 | oh2, 1.0, 0.0).astype(BF16)
        rank = jnp.dot(a, earlier, preferred_element_type=F32) + counts
        r1 = jnp.sum(jnp.where(oh1, rank, 0.0), axis=0, keepdims=True)
        r2 = jnp.sum(jnp.where(oh2, rank, 0.0), axis=0, keepdims=True)
        tok_scr[:, cols] = jnp.where(row8 == 0.0, e1, jnp.where(
            row8 == 1.0, e2, jnp.where(row8 == 2.0, r1, jnp.where(row8 == 3.0, r2, 0.0))))
        cw_ref[:, cols] = rows01((v1 / tot) * pg_sel, (v2 / tot) * pg_sel)
        return counts + jnp.dot(a, all_ones, preferred_element_type=F32)

    counts = lax.fori_loop(0, t // TB, pass1, jnp.zeros((N_EXPERTS, TB), F32))
    cnt_ref[...] = counts[:, 0:LANES]
    n_tiles = jnp.floor((counts + (TM_E - 1)) * (1.0 / TM_E))
    li = lax.broadcasted_iota(jnp.int32, (N_EXPERTS, N_EXPERTS), 0)
    lj = lax.broadcasted_iota(jnp.int32, (N_EXPERTS, N_EXPERTS), 1)
    lower = jnp.where(lj < li, 1.0, 0.0).astype(BF16)
    row_off = jnp.dot(lower, n_tiles.astype(BF16), preferred_element_type=F32) * TM_E

    def pass2(b, carry):
        cols = pl.ds(pl.multiple_of(b * TB, TB), TB)
        tok = tok_scr[:, cols]
        off1 = jnp.sum(jnp.where(row_e == tok[0:1, :], row_off, 0.0), axis=0, keepdims=True)
        off2 = jnp.sum(jnp.where(row_e == tok[1:2, :], row_off, 0.0), axis=0, keepdims=True)
        slots_ref[:, cols] = rows01(off1 + tok[2:3, :], off2 + tok[3:4, :]).astype(jnp.int32)
        return carry

    lax.fori_loop(0, t // TB, pass2, 0)


def _routing(logits_t):
    t = logits_t.shape[1]
    return pl.pallas_call(
        _routing_kernel,
        out_shape=[jax.ShapeDtypeStruct((8, t), jnp.int32),
                   jax.ShapeDtypeStruct((8, t), F32),
                   jax.ShapeDtypeStruct((N_EXPERTS, LANES), F32)],
        scratch_shapes=[pltpu.VMEM((8, t), F32)],
        compiler_params=pltpu.CompilerParams(vmem_limit_bytes=VMEM_LIMIT),
        name="routing",
    )(logits_t)


def _rows_copy(src, dst, sem, src_tok, dst_tok, n):
    first = lambda tok: tok * SUB if isinstance(tok, int) else pl.multiple_of(tok * SUB, SUB)
    s0 = first(src_tok)
    d0 = first(dst_tok)
    return pltpu.make_async_copy(src.at[pl.ds(s0, n * SUB), :], dst.at[pl.ds(d0, n * SUB), :], sem)


def _dispatch_kernel(sl1_ref, sl2_ref, ends_ref, h2_ref, xs_hbm, zero_scr, sem):
    step = pl.program_id(0)

    @pl.when(step == 0)
    def _():
        zero_scr[...] = jnp.zeros_like(zero_scr)
        for e in range(N_EXPERTS):
            start = jnp.maximum(ends_ref[e] - TM_E, 0)
            _rows_copy(zero_scr, xs_hbm, sem, 0, start, TM_E).start()
        for e in range(N_EXPERTS):
            _rows_copy(zero_scr, xs_hbm, sem, 0, 0, TM_E).wait()

    base = step * TB_D

    def issue(r, carry):
        _rows_copy(h2_ref, xs_hbm, sem, r, sl1_ref[base + r], 1).start(priority=0)
        _rows_copy(h2_ref, xs_hbm, sem, r, sl2_ref[base + r], 1).start(priority=1)
        return carry

    lax.fori_loop(0, TB_D, issue, 0, unroll=8)
    _rows_copy(h2_ref, xs_hbm, sem, 0, 0, TB_D).wait()
    _rows_copy(h2_ref, xs_hbm, sem, 0, 0, TB_D).wait()


def _dispatch(h2, sl1, sl2, ends, n_rows):
    t = h2.shape[0] // SUB
    return pl.pallas_call(
        _dispatch_kernel,
        grid_spec=pltpu.PrefetchScalarGridSpec(
            num_scalar_prefetch=3,
            grid=(t // TB_D,),
            in_specs=[pl.BlockSpec((TB_D * SUB, LANES), lambda i, *_: (i, 0))],
            out_specs=pl.BlockSpec(memory_space=pl.ANY),
            scratch_shapes=[pltpu.VMEM((TM_E * SUB, LANES), BF16), pltpu.SemaphoreType.DMA(())],
        ),
        out_shape=jax.ShapeDtypeStruct((n_rows * SUB, LANES), BF16),
        compiler_params=_params(1),
        name="dispatch",
    )(sl1, sl2, ends, h2)


def _expert_kernel(tidx_ref, texp_ref, tfirst_ref, tslot_ref, tnext_ref, nt_ref,
                   xs_ref, wg_hbm, wu_hbm, wd_hbm, ys_ref,
                   wg_buf, wu_buf, wd_buf, wg_bf, wu_bf, wd_bf, tt_scr, sems):
    j = pl.program_id(0)

    def weight_copies(e, slot):
        return [pltpu.make_async_copy(hbm.at[e], buf.at[slot], sems.at[slot, k])
                for k, (hbm, buf) in enumerate(((wg_hbm, wg_buf), (wu_hbm, wu_buf),
                                                (wd_hbm, wd_buf)))]

    @pl.when(j < nt_ref[0])
    def _():
        slot = tslot_ref[j]

        @pl.when(tfirst_ref[j] == 1)
        def _():
            @pl.when(j == 0)
            def _():
                for c in weight_copies(texp_ref[j], slot):
                    c.start()
            for c in weight_copies(texp_ref[j], slot):
                c.wait()

            @pl.when(tnext_ref[j] >= 0)
            def _():
                for c in weight_copies(tnext_ref[j], 1 - slot):
                    c.start()
            wg_bf[...] = wg_buf[slot].astype(BF16)
            wu_bf[...] = wu_buf[slot].astype(BF16)
            wd_bf[...] = wd_buf[slot].astype(BF16)

        xb = _load_token_tiles(xs_ref, tt_scr, TM_E).astype(BF16)
        g = jnp.dot(xb, wg_bf[...], preferred_element_type=F32)
        u = jnp.dot(xb, wu_bf[...], preferred_element_type=F32)
        h = (_silu(g) * u).astype(BF16)
        _store_token_tiles(ys_ref, jnp.dot(h, wd_bf[...], preferred_element_type=F32), tt_scr)


def _expert_mlp(xs, tile_meta, wg, wu, wd):
    n_rows = xs.shape[0] // SUB
    tile = pl.BlockSpec((TM_E * SUB, LANES), lambda j, ti, *_: (ti[j], 0))
    hbm = pl.BlockSpec(memory_space=pl.ANY)
    return pl.pallas_call(
        _expert_kernel,
        grid_spec=pltpu.PrefetchScalarGridSpec(
            num_scalar_prefetch=len(tile_meta),
            grid=(n_rows // TM_E,),
            in_specs=[tile, hbm, hbm, hbm],
            out_specs=tile,
            scratch_shapes=[pltpu.VMEM((2, D_MODEL, D_FF), F32),
                            pltpu.VMEM((2, D_MODEL, D_FF), F32),
                            pltpu.VMEM((2, D_FF, D_MODEL), F32),
                            pltpu.VMEM((D_MODEL, D_FF), BF16),
                            pltpu.VMEM((D_MODEL, D_FF), BF16),
                            pltpu.VMEM((D_FF, D_MODEL), BF16),
                            pltpu.VMEM((TM_E * SUB, LANES), F32),
                            pltpu.SemaphoreType.DMA((2, 3))],
        ),
        out_shape=jax.ShapeDtypeStruct((n_rows * SUB, LANES), BF16),
        compiler_params=_params(1),
        name="expert_mlp",
    )(*tile_meta, xs, wg, wu, wd)


def _combine_kernel(sl1_ref, sl2_ref, x1_ref, cw_ref, ys_hbm, o_ref, y1_scr, y2_scr, tt1_scr,
                    tt2_scr, sems):
    i = pl.program_id(0)

    def issue(step, slot):
        base = step * TB

        def body(r, carry):
            _rows_copy(ys_hbm, y1_scr.at[slot], sems.at[slot], sl1_ref[base + r], r,
                       1).start(priority=0)
            _rows_copy(ys_hbm, y2_scr.at[slot], sems.at[slot], sl2_ref[base + r], r,
                       1).start(priority=1)
            return carry

        lax.fori_loop(0, TB, body, 0, unroll=8)

    @pl.when(i == 0)
    def _():
        issue(0, 0)

    last = pl.num_programs(0) - 1
    rb = 64
    n_pieces = (TB // rb) * SUB
    per_piece = TB // n_pieces

    def step(slot):
        _rows_copy(ys_hbm, y1_scr.at[slot], sems.at[slot], 0, 0, TB).wait()
        _rows_copy(ys_hbm, y2_scr.at[slot], sems.at[slot], 0, 0, TB).wait()
        nbase = jnp.minimum(i + 1, last) * TB
        cw = jnp.concatenate([cw_ref[...], jnp.zeros((LANES - 8, TB), F32)], axis=0).T
        tt1_scr[...] = y1_scr[slot].astype(F32)
        tt2_scr[...] = y2_scr[slot].astype(F32)
        piece = 0
        for r0 in range(0, TB, rb):
            rows = slice(r0, r0 + rb)
            c1 = jnp.broadcast_to(cw[rows, 0:1], (rb, LANES))
            c2 = jnp.broadcast_to(cw[rows, 1:2], (rb, LANES))
            for s in range(SUB):
                for r in range(piece * per_piece, (piece + 1) * per_piece):
                    _rows_copy(ys_hbm, y1_scr.at[1 - slot], sems.at[1 - slot],
                               sl1_ref[nbase + r], r, 1).start(priority=0)
                    _rows_copy(ys_hbm, y2_scr.at[1 - slot], sems.at[1 - slot],
                               sl2_ref[nbase + r], r, 1).start(priority=1)
                piece += 1
                cols = slice(s * LANES, (s + 1) * LANES)
                tiles = pl.ds(r0 * SUB + s, rb, stride=SUB)
                o_ref[rows, cols] = (x1_ref[rows, cols] + c1 * tt1_scr[tiles, :]
                                     + c2 * tt2_scr[tiles, :])

        @pl.when(i == last)
        def _():
            _rows_copy(ys_hbm, y1_scr.at[1 - slot], sems.at[1 - slot], 0, 0, TB).wait()
            _rows_copy(ys_hbm, y2_scr.at[1 - slot], sems.at[1 - slot], 0, 0, TB).wait()

    for parity in (0, 1):
        @pl.when(i % 2 == parity)
        def _():
            step(parity)


def _combine(x1, cw, ys, sl1, sl2):
    t = x1.shape[0]
    return pl.pallas_call(
        _combine_kernel,
        grid_spec=pltpu.PrefetchScalarGridSpec(
            num_scalar_prefetch=2,
            grid=(t // TB,),
            in_specs=[pl.BlockSpec((TB, D_MODEL), lambda i, *_: (i, 0)),
                      pl.BlockSpec((8, TB), lambda i, *_: (0, i)),
                      pl.BlockSpec(memory_space=pl.ANY)],
            out_specs=pl.BlockSpec((TB, D_MODEL), lambda i, *_: (i, 0)),
            scratch_shapes=[pltpu.VMEM((2, TB * SUB, LANES), BF16),
                            pltpu.VMEM((2, TB * SUB, LANES), BF16),
                            pltpu.VMEM((TB * SUB, LANES), F32),
                            pltpu.VMEM((TB * SUB, LANES), F32),
                            pltpu.SemaphoreType.DMA((2,))],
        ),
        out_shape=jax.ShapeDtypeStruct((t, D_MODEL), F32),
        compiler_params=_params(1),
        name="combine",
    )(sl1, sl2, x1, cw, ys)


def _lane_tile(v, reps):
    return jnp.tile(v.reshape(1, -1), (1, reps))


def kernel(x, positions, norm1_g, w_in, q_norm_g, k_norm_g, lambda_q1, lambda_k1, lambda_q2,
           lambda_k2, diff_out_norm_g, gla_w_gate2, gla_b_gate, gla_out_norm_g, w_out, norm2_g,
           w_router_group, b_router_group, w_router_expert, b_router_expert, w_gate_expert,
           w_up_expert, w_down_expert):
    batch, seq, d = x.shape
    t = batch * seq
    x2 = x.reshape(t, d)

    inv = ROPE_THETA ** (-jnp.arange(0, ROT_DIM, 2, dtype=F32) / ROT_DIM)
    lane_d = jnp.arange(LANES) % DIFF_QK_DIM
    invf = jnp.where(lane_d < ROT_DIM, inv[lane_d % (ROT_DIM // 2)], 0.0).reshape(1, LANES)
    pos_b = jnp.broadcast_to(positions.astype(F32).reshape(t, 1), (t, LANES))
    tabs = _rope_tables(pos_b, invf)

    w_in_t = jnp.swapaxes(w_in, 1, 2)[0]
    wa_t = jnp.pad(w_in_t[D_MAIN:, :], ((0, LANES - GLA_GATE_RANK), (0, 0)))
    proj, ga = _in_proj(x2, norm1_g, w_in_t, wa_t)

    a_out = _diff_attention(
        proj, tabs, _lane_tile(q_norm_g[0], 2), _lane_tile(k_norm_g[0], 2),
        lambda_q1, lambda_k1, lambda_q2, lambda_k2, diff_out_norm_g, batch, seq)
    w2p = jnp.pad(gla_w_gate2[0], ((0, LANES - GLA_GATE_RANK), (0, 0))).astype(BF16)
    g_out = _gla(proj, ga, w2p, gla_b_gate, gla_out_norm_g, batch, seq)

    gpad = GROUP_ROWS - N_GROUPS
    lpad = LANES - N_LOGIT_ROWS
    wr = jnp.concatenate(
        [jnp.pad(w_router_group[0], ((0, 0), (0, gpad))),
         jnp.pad(w_router_expert[0].transpose(1, 0, 2).reshape(d, N_EXPERTS),
                 ((0, 0), (0, lpad)))], axis=1)
    br = jnp.concatenate([jnp.pad(b_router_group[0], (0, gpad)),
                          jnp.pad(b_router_expert[0].reshape(-1), (0, lpad))]).reshape(1, LANES)
    x1, h2, logits_t = _out_proj(a_out, g_out, x2, w_out[0].astype(BF16), norm2_g,
                                 wr.astype(BF16), br)

    slots, cw, counts = _routing(logits_t)
    sl1 = slots[0]
    sl2 = slots[1]
    n_rows = 2 * t + N_EXPERTS * TM_E
    max_tiles = n_rows // TM_E
    cnt = counts[:, 0].astype(jnp.int32)
    tile_end = jnp.cumsum((cnt + (TM_E - 1)) // TM_E)
    n_tiles = tile_end[-1:]
    tile_idx = jnp.minimum(jnp.arange(max_tiles, dtype=jnp.int32), n_tiles[0] - 1)
    tile_exp = jnp.sum(tile_idx[:, None] >= tile_end[None, :], axis=1).astype(jnp.int32)
    row_end = (tile_end * TM_E).astype(jnp.int32)
    tile_first = jnp.concatenate(
        [jnp.ones((1,), jnp.int32), (tile_exp[1:] != tile_exp[:-1]).astype(jnp.int32)])
    tile_slot = (jnp.cumsum(tile_first) - 1) % 2
    nxt = tile_end[tile_exp]
    tile_next = jnp.where(nxt < n_tiles[0], tile_exp[jnp.minimum(nxt, max_tiles - 1)], -1)
    tile_meta = [a.astype(jnp.int32)
                 for a in (tile_idx, tile_exp, tile_first, tile_slot, tile_next, n_tiles)]

    xs = _dispatch(h2, sl1, sl2, row_end, n_rows)
    ys = _expert_mlp(xs, tile_meta,
                     w_gate_expert[0].reshape(N_EXPERTS, d, D_FF),
                     w_up_expert[0].reshape(N_EXPERTS, d, D_FF),
                     w_down_expert[0].reshape(N_EXPERTS, D_FF, d))
    out = _combine(x1, cw, ys, sl1, sl2)
    return out.reshape(batch, seq, d)
```

```python
import functools
import math

import jax
import jax.numpy as jnp
from jax import lax
from jax.experimental import pallas as pl
from jax.experimental.pallas import tpu as pltpu

D_MODEL = 2048
CHUNK = 64
DIFF_QK_DIM = 64
DIFF_V_DIM = 128
DIFF_HEADS = 8
ROT_DIM = 16
ROPE_THETA = 500000.0
GLA_HEADS = 4
GLA_V_DIM = 256
GLA_K_DIM = 128
GLA_GATE_RANK = 16
GLA_TAU = 16.0
N_GROUPS = 4
EXPERTS_PER_GROUP = 8
N_EXPERTS = N_GROUPS * EXPERTS_PER_GROUP
D_FF = 512
RMS_EPS = 1e-6
LAMBDA_INIT = 0.8 - 0.6 * math.exp(-0.3 * 0)
D_MAIN = 6144
GROUP_ROWS = 8
N_LOGIT_ROWS = GROUP_ROWS + N_EXPERTS

LANES = 128
SUB = D_MODEL // LANES
TT_PITCH = 24
VMEM_LIMIT = 56 * 1024 * 1024

TM_IN = 1024
TN_IN = 1024
TQ = 256
GB = 256
TM_OUT = 512
RB_OUT = 256
TB = 256
TB_D = 1024
TM_E = 256

F32 = jnp.float32
BF16 = jnp.bfloat16
HI = lax.Precision.HIGHEST


def _params(n_axes):
    return pltpu.CompilerParams(dimension_semantics=("arbitrary",) * n_axes,
                                vmem_limit_bytes=VMEM_LIMIT)


def _nt_dot(a, b):
    return lax.dot_general(a, b, (((1,), (1,)), ((), ())), preferred_element_type=F32)


def _tn_dot(a, b):
    return lax.dot_general(a, b, (((0,), (0,)), ((), ())), preferred_element_type=F32)


def _silu(x):
    return x * (1.0 / (1.0 + jnp.exp(-x)))


def _tt_fill(scr, ref, n, tok0=0):
    for t in range(n):
        scr[t * TT_PITCH:t * TT_PITCH + SUB, :] = ref[pl.ds((tok0 + t) * SUB, SUB), :].astype(F32)


def _tt_drain(ref, scr, n, tok0=0):
    for t in range(n):
        ref[pl.ds((tok0 + t) * SUB, SUB), :] = scr[t * TT_PITCH:t * TT_PITCH + SUB, :].astype(BF16)


def _tt_piece(s, n, tok0=0):
    return (pl.ds(tok0 * TT_PITCH + s, n, stride=TT_PITCH), slice(None))


def _store_token_tiles(ref, val, scr, tok0=0):
    n = val.shape[0]
    for s in range(SUB):
        scr[_tt_piece(s, n)] = val[:, s * LANES:(s + 1) * LANES]
    _tt_drain(ref, scr, n, tok0)


def _load_token_tiles(ref, scr, n, tok0=0):
    _tt_fill(scr, ref, n, tok0)
    return jnp.concatenate([scr[_tt_piece(s, n)] for s in range(SUB)], axis=1)


def _rope_table_kernel(pos_ref, invf_ref, c_ref, s1_ref, s2_ref):
    ang = pos_ref[...] * invf_ref[...]
    d = lax.broadcasted_iota(jnp.int32, ang.shape, 1) % DIFF_QK_DIM
    cos = jnp.cos(ang)
    sin = jnp.sin(ang)
    half = ROT_DIM // 2
    c_ref[...] = jnp.where(d < ROT_DIM, cos, 1.0)
    s1_ref[...] = jnp.where(d < half, -sin, 0.0)
    s2_ref[...] = jnp.where((d >= half) & (d < ROT_DIM), sin, 0.0)


def _rope_tables(pos_b, invf):
    t = pos_b.shape[0]
    tb = 1024
    spec = pl.BlockSpec((tb, LANES), lambda i: (i, 0))
    return pl.pallas_call(
        _rope_table_kernel,
        grid=(t // tb,),
        in_specs=[spec, pl.BlockSpec((1, LANES), lambda i: (0, 0))],
        out_specs=[spec, spec, spec],
        out_shape=[jax.ShapeDtypeStruct((t, LANES), F32)] * 3,
        compiler_params=_params(1),
        name="rope_tables",
    )(pos_b, invf)


def _in_proj_kernel(x_ref, g_ref, wt_ref, wat_ref, proj_ref, ga_ref, h_scr):
    @pl.when(pl.program_id(1) == 0)
    def _():
        def body(c, carry):
            rows = pl.ds(c * 256, 256)
            x = x_ref[rows, :]
            ms = jnp.mean(x * x, axis=-1, keepdims=True)
            h_scr[rows, :] = (x * lax.rsqrt(ms + RMS_EPS) * g_ref[...]).astype(BF16)
            return carry
        lax.fori_loop(0, TM_IN // 256, body, 0)
        ga_ref[...] = _nt_dot(h_scr[...], wat_ref[...].astype(BF16))

    proj_ref[...] = _nt_dot(h_scr[...], wt_ref[...].astype(BF16)).astype(BF16)


def _in_proj(x2, g1, w_in, wa):
    t = x2.shape[0]
    return pl.pallas_call(
        _in_proj_kernel,
        grid=(t // TM_IN, D_MAIN // TN_IN),
        in_specs=[
            pl.BlockSpec((TM_IN, D_MODEL), lambda i, j: (i, 0)),
            pl.BlockSpec((1, D_MODEL), lambda i, j: (0, 0)),
            pl.BlockSpec((TN_IN, D_MODEL), lambda i, j: (j, 0)),
            pl.BlockSpec((LANES, D_MODEL), lambda i, j: (0, 0)),
        ],
        out_specs=[
            pl.BlockSpec((TM_IN, TN_IN), lambda i, j: (i, j)),
            pl.BlockSpec((TM_IN, LANES), lambda i, j: (i, 0)),
        ],
        out_shape=[jax.ShapeDtypeStruct((t, D_MAIN), BF16),
                   jax.ShapeDtypeStruct((t, LANES), F32)],
        scratch_shapes=[pltpu.VMEM((TM_IN, D_MODEL), BF16)],
        compiler_params=_params(2),
        name="in_proj",
    )(x2, g1, w_in, wa)


def _norm_rope(x, g, c, s1, s2):
    lo = lax.broadcasted_iota(jnp.int32, x.shape, 1) < DIFF_QK_DIM
    x2 = x * x
    s_lo = jnp.sum(jnp.where(lo, x2, 0.0), axis=-1, keepdims=True)
    s_hi = jnp.sum(jnp.where(lo, 0.0, x2), axis=-1, keepdims=True)
    ms = jnp.where(lo, s_lo, s_hi) * (1.0 / DIFF_QK_DIM)
    y = x * lax.rsqrt(ms + RMS_EPS) * g
    half = ROT_DIM // 2
    return y * c + pltpu.roll(y, LANES - half, 1) * s1 + pltpu.roll(y, half, 1) * s2


def _attn_kernel(q_ref, k_ref, v_ref, c_ref, s1_ref, s2_ref, qg_ref, kg_ref,
                 lq1_ref, lk1_ref, lq2_ref, lk2_ref, og_ref, o_ref,
                 q1_scr, q2_scr, k_scr, v_scr, *bufs):
    s_len = q_ref.shape[0]
    s_bufs, e_bufs = bufs[0:4], bufs[4:8]
    lam = (jnp.exp(jnp.sum(lq1_ref[...] * lk1_ref[...], axis=-1, keepdims=True))
           - jnp.exp(jnp.sum(lq2_ref[...] * lk2_ref[...], axis=-1, keepdims=True))
           + LAMBDA_INIT)
    lo = lax.broadcasted_iota(jnp.int32, (TQ, LANES), 1) < DIFF_QK_DIM
    v_scr[:, 0:LANES] = v_ref[...]
    v_scr[:, LANES:] = jnp.ones((s_len, LANES), BF16)

    def prepare(i):
        rows = slice(i * TQ, (i + 1) * TQ)
        c, s1, s2 = c_ref[rows, :], s1_ref[rows, :], s2_ref[rows, :]
        qn = (_norm_rope(q_ref[rows, :].astype(F32), qg_ref[...], c, s1, s2)
              * (DIFF_QK_DIM ** -0.5 * math.log2(math.e)))
        q1_scr[rows, :] = jnp.where(lo, qn, 0.0).astype(BF16)
        q2_scr[rows, :] = jnp.where(lo, 0.0, qn).astype(BF16)
        k_scr[rows, :] = _norm_rope(k_ref[rows, :].astype(F32), kg_ref[...], c, s1,
                                    s2).astype(BF16)

    diag = (lax.broadcasted_iota(jnp.int32, (TQ, TQ), 1) // CHUNK
            <= lax.broadcasted_iota(jnp.int32, (TQ, TQ), 0) // CHUNK)

    n_tiles = s_len // TQ
    items = [(i, q_scr) for i in range(n_tiles) for q_scr in (q1_scr, q2_scr)]

    def scores(r):
        i, q_scr = items[r]
        s_scr = s_bufs[r % 4]
        q = q_scr[i * TQ:(i + 1) * TQ, :]
        n_off = i * TQ
        s_scr[:, n_off:n_off + TQ] = jnp.where(
            diag, _nt_dot(q, k_scr[n_off:n_off + TQ, :]), -jnp.inf)
        if n_off:
            s_scr[:, 0:n_off] = _nt_dot(q, k_scr[0:n_off, :])

    def softmax_pv(r):
        nk = (items[r][0] + 1) * TQ
        s_scr, e_scr = s_bufs[r % 4], e_bufs[r % 4]
        m = jnp.max(s_scr[:, 0:nk], axis=-1, keepdims=True)
        e_scr[:, 0:nk] = jnp.exp2(s_scr[:, 0:nk] - m).astype(BF16)
        acc = jnp.dot(e_scr[:, 0:nk], v_scr[0:nk, :], preferred_element_type=F32)
        return acc[:, 0:LANES] / acc[:, LANES:]

    prepare(0)
    scores(0)
    scores(1)
    for i in range(n_tiles):
        rows = slice(i * TQ, (i + 1) * TQ)
        if i + 1 < n_tiles:
            prepare(i + 1)
            scores(2 * i + 2)
        o1 = softmax_pv(2 * i)
        if i + 1 < n_tiles:
            scores(2 * i + 3)
        o = o1 - lam * softmax_pv(2 * i + 1)
        ms = jnp.mean(o * o, axis=-1, keepdims=True)
        y = o * lax.rsqrt(ms + RMS_EPS) * og_ref[...] * (1.0 - LAMBDA_INIT)
        o_ref[rows, :] = y.astype(BF16)


def _diff_attention(proj, tabs, qg, kg, lq1, lk1, lq2, lk2, og, batch, seq):
    c, s1, s2 = tabs
    h = DIFF_HEADS
    blk = lambda off: pl.BlockSpec((seq, LANES), lambda b, hh, off=off: (b, off + hh))
    tab = pl.BlockSpec((seq, LANES), lambda b, hh: (b, 0))
    vec = lambda n: pl.BlockSpec((1, n), lambda b, hh: (0, 0))
    return pl.pallas_call(
        _attn_kernel,
        grid=(batch, h),
        in_specs=[blk(0), blk(h), blk(2 * h), tab, tab, tab,
                  vec(LANES), vec(LANES), vec(DIFF_QK_DIM), vec(DIFF_QK_DIM),
                  vec(DIFF_QK_DIM), vec(DIFF_QK_DIM), vec(LANES)],
        out_specs=pl.BlockSpec((seq, LANES), lambda b, hh: (b, hh)),
        out_shape=jax.ShapeDtypeStruct((batch * seq, h * DIFF_V_DIM), BF16),
        scratch_shapes=[pltpu.VMEM((seq, LANES), BF16)] * 3
        + [pltpu.VMEM((seq, 2 * LANES), BF16)]
        + [pltpu.VMEM((TQ, seq), F32)] * 4 + [pltpu.VMEM((TQ, seq), BF16)] * 4,
        compiler_params=_params(2),
        name="diff_attention",
    )(proj, proj, proj, c, s1, s2, qg, kg, lq1, lk1, lq2, lk2, og)


def _split_dot(ones_bf, x):
    hi = x.astype(BF16)
    lo = (x - hi.astype(F32)).astype(BF16)
    return (jnp.dot(ones_bf, hi, preferred_element_type=F32)
            + jnp.dot(ones_bf, lo, preferred_element_type=F32))


def _gla_kernel(q_ref, k_ref, v_ref, r_ref, ga_ref, w2_ref, b2_ref, og_ref, o_ref,
                qin_scr, ut_scr, dec_scr, st_scr, acc_scr):
    s_len = q_ref.shape[0]
    cpg = GB // CHUNK
    ri = lax.broadcasted_iota(jnp.int32, (GB, GB), 0)
    ci = lax.broadcasted_iota(jnp.int32, (GB, GB), 1)
    same = (ri // CHUNK) == (ci // CHUNK)
    blk_ones = jnp.where(same, 1.0, 0.0).astype(BF16)
    tril = same & (ci <= ri)
    tri_ones = jnp.where(tril, 1.0, 0.0).astype(BF16)

    def phase_a(g, carry):
        rows = pl.ds(pl.multiple_of(g * GB, GB), GB)
        pre = jnp.dot(ga_ref[rows, :].astype(BF16), w2_ref[...],
                      preferred_element_type=F32) + b2_ref[...]
        la = -(jnp.maximum(-pre, 0.0) + jnp.log1p(jnp.exp(-jnp.abs(pre)))) * (1.0 / GLA_TAU)
        bc = _split_dot(tri_ones, la)
        bl = _split_dot(blk_ones, la)
        e_neg = jnp.exp(-bc)
        e_last = jnp.exp(bl)
        k = k_ref[rows, :].astype(F32)
        q_in = (q_ref[rows, :].astype(F32) * (GLA_K_DIM ** -0.5) * jnp.exp(bc)).astype(BF16)
        k_in = (k * e_neg).astype(BF16)
        k_dec = (k * (e_last * e_neg)).astype(BF16)
        qin_scr[rows, :] = q_in
        dec_scr[rows, :] = e_last
        v = v_ref[rows, :]
        att = jnp.where(tril, _nt_dot(q_in, k_in), 0.0).astype(BF16)
        acc_scr[rows, :] = jnp.dot(att, v, preferred_element_type=F32)
        for c in range(cpg):
            cr = slice(c * CHUNK, (c + 1) * CHUNK)
            ut_scr[g * cpg + c] = _tn_dot(v[cr, :], k_dec[cr, :])
        return carry

    lax.fori_loop(0, s_len // GB, phase_a, 0, unroll=8)

    def phase_b(c, st):
        st_scr[c] = st.astype(BF16)
        dec = dec_scr[pl.ds(pl.multiple_of(c * CHUNK, CHUNK), 1), :]
        return dec * st + ut_scr[c]

    lax.fori_loop(0, s_len // CHUNK, phase_b, jnp.zeros((GLA_V_DIM, GLA_K_DIM), F32))

    def phase_c(g, carry):
        rows = pl.ds(pl.multiple_of(g * GB, GB), GB)
        inter = [_nt_dot(qin_scr[pl.ds(pl.multiple_of(g * GB + c * CHUNK, CHUNK), CHUNK), :],
                         st_scr[g * cpg + c]) for c in range(cpg)]
        o = acc_scr[rows, :] + jnp.concatenate(inter, axis=0)
        ms = jnp.mean(o * o, axis=-1, keepdims=True)
        y = o * lax.rsqrt(ms + RMS_EPS) * og_ref[...]
        o_ref[rows, :] = (y * _silu(r_ref[rows, :].astype(F32))).astype(BF16)
        return carry

    lax.fori_loop(0, s_len // GB, phase_c, 0, unroll=8)


def _gla(proj, ga, w2p, b2, og, batch, seq):
    hq = 3 * DIFF_HEADS
    kblk = lambda off: pl.BlockSpec((seq, GLA_K_DIM), lambda b, hh, off=off: (b, off + hh))
    vblk = lambda off: pl.BlockSpec((seq, GLA_V_DIM), lambda b, hh, off=off: (b, off + hh))
    return pl.pallas_call(
        _gla_kernel,
        grid=(batch, GLA_HEADS),
        in_specs=[kblk(hq), kblk(hq + GLA_HEADS), vblk(16), vblk(16 + GLA_HEADS),
                  pl.BlockSpec((seq, LANES), lambda b, hh: (b, 0)),
                  pl.BlockSpec((LANES, GLA_K_DIM), lambda b, hh: (0, hh)),
                  pl.BlockSpec((1, GLA_K_DIM), lambda b, hh: (0, hh)),
                  pl.BlockSpec((1, GLA_V_DIM), lambda b, hh: (0, 0))],
        out_specs=pl.BlockSpec((seq, GLA_V_DIM), lambda b, hh: (b, hh)),
        out_shape=jax.ShapeDtypeStruct((batch * seq, GLA_HEADS * GLA_V_DIM), BF16),
        scratch_shapes=[pltpu.VMEM((seq, GLA_K_DIM), BF16),
                        pltpu.VMEM((seq // CHUNK, GLA_V_DIM, GLA_K_DIM), F32),
                        pltpu.VMEM((seq, GLA_K_DIM), F32),
                        pltpu.VMEM((seq // CHUNK, GLA_V_DIM, GLA_K_DIM), BF16),
                        pltpu.VMEM((seq, GLA_V_DIM), F32)],
        compiler_params=_params(2),
        name="gla",
    )(proj, proj, proj, proj, ga, w2p, b2, og)


def _out_proj_kernel(a_ref, g_ref, x_ref, wo_ref, g2_ref, wr_ref, br_ref,
                     x1_ref, h2_ref, lg_ref, tt_scr):
    half = a_ref.shape[1]

    def body(c, carry):
        r0 = pl.multiple_of(c * RB_OUT, RB_OUT)
        rows = pl.ds(r0, RB_OUT)
        mixed = (jnp.dot(a_ref[rows, :], wo_ref[0:half, :], preferred_element_type=F32)
                 + jnp.dot(g_ref[rows, :], wo_ref[half:, :], preferred_element_type=F32))
        x1 = x_ref[rows, :] + mixed
        x1_ref[rows, :] = x1
        ms = jnp.mean(x1 * x1, axis=-1, keepdims=True)
        h2 = x1 * lax.rsqrt(ms + RMS_EPS) * g2_ref[...]
        _store_token_tiles(h2_ref, h2, tt_scr, r0)
        lg = jnp.dot(h2.astype(BF16), wr_ref[...], preferred_element_type=F32) + br_ref[...]
        lg_ref[:, rows] = lg.T[0:N_LOGIT_ROWS, :]
        return carry

    lax.fori_loop(0, TM_OUT // RB_OUT, body, 0)


def _out_proj(a_out, g_out, x2, wo, g2, wr, br):
    t = x2.shape[0]
    half = a_out.shape[1]
    row = lambda n: pl.BlockSpec((TM_OUT, n), lambda i: (i, 0))
    full = lambda r, n: pl.BlockSpec((r, n), lambda i: (0, 0))
    return pl.pallas_call(
        _out_proj_kernel,
        grid=(t // TM_OUT,),
        in_specs=[row(half), row(half), row(D_MODEL), full(D_MODEL, D_MODEL),
                  full(1, D_MODEL), full(D_MODEL, LANES), full(1, LANES)],
        out_specs=[row(D_MODEL), pl.BlockSpec((TM_OUT * SUB, LANES), lambda i: (i, 0)),
                   pl.BlockSpec((N_LOGIT_ROWS, TM_OUT), lambda i: (0, i))],
        out_shape=[jax.ShapeDtypeStruct((t, D_MODEL), F32),
                   jax.ShapeDtypeStruct((t * SUB, LANES), BF16),
                   jax.ShapeDtypeStruct((N_LOGIT_ROWS, t), F32)],
        scratch_shapes=[pltpu.VMEM((RB_OUT * TT_PITCH, LANES), F32)],
        compiler_params=_params(1),
        name="out_proj_router",
    )(a_out, g_out, x2, wo, g2, wr, br)


def _first_argmax(vals, row_f):
    m = jnp.max(vals, axis=0, keepdims=True)
    idx = jnp.min(jnp.where(vals == m, row_f, float(vals.shape[0])), axis=0, keepdims=True)
    return m, idx


def _softmax_rows(v):
    ex = jnp.exp(v - jnp.max(v, axis=0, keepdims=True))
    return ex / jnp.sum(ex, axis=0, keepdims=True)


def _routing_kernel(lg_ref, slots_ref, cw_ref, cnt_ref, tok_scr):
    t = lg_ref.shape[1]
    epg = EXPERTS_PER_GROUP
    row8 = lax.broadcasted_iota(jnp.int32, (8, TB), 0).astype(F32)
    row_e = lax.broadcasted_iota(jnp.int32, (N_EXPERTS, TB), 0).astype(F32)
    ui = lax.broadcasted_iota(jnp.int32, (TB, TB), 0)
    uj = lax.broadcasted_iota(jnp.int32, (TB, TB), 1)
    earlier = jnp.where(ui < uj, 1.0, 0.0).astype(BF16)
    all_ones = jnp.ones((TB, TB), BF16)

    def rows01(r0, r1):
        return jnp.where(row8 == 0.0, r0, jnp.where(row8 == 1.0, r1, jnp.zeros_like(r0)))

    def pass1(b, counts):
        cols = pl.ds(pl.multiple_of(b * TB, TB), TB)
        lg = lg_ref[:, cols]
        lg_groups = jnp.where(row8 < float(N_GROUPS), lg[0:GROUP_ROWS, :], -jnp.inf)
        pg_sel, g_sel = _first_argmax(_softmax_rows(lg_groups), row8)
        le = lg[GROUP_ROWS:GROUP_ROWS + epg, :]
        for g in range(1, N_GROUPS):
            le = jnp.where(g_sel == float(g),
                           lg[GROUP_ROWS + g * epg:GROUP_ROWS + (g + 1) * epg, :], le)
        pe = _softmax_rows(le)
        v1, i1 = _first_argmax(pe, row8)
        v2, i2 = _first_argmax(jnp.where(row8 == i1, -jnp.inf, pe), row8)
        tot = v1 + v2
        e1 = g_sel * epg + i1
        e2 = g_sel * epg + i2
        oh1 = row_e == e1
        oh2 = row_e == e2
        a = jnp.where(oh1 | oh2, 1.0, 0.0).astype(BF16)
        rank = jnp.dot(a, earlier, preferred_element_type=F32) + counts
        r1 = jnp.sum(jnp.where(oh1, rank, 0.0), axis=0, keepdims=True)
        r2 = jnp.sum(jnp.where(oh2, rank, 0.0), axis=0, keepdims=True)
        tok_scr[:, cols] = jnp.where(row8 == 0.0, e1, jnp.where(
            row8 == 1.0, e2, jnp.where(row8 == 2.0, r1, jnp.where(row8 == 3.0, r2, 0.0))))
        cw_ref[:, cols] = rows01((v1 / tot) * pg_sel, (v2 / tot) * pg_sel)
        return counts + jnp.dot(a, all_ones, preferred_element_type=F32)

    counts = lax.fori_loop(0, t // TB, pass1, jnp.zeros((N_EXPERTS, TB), F32))
    cnt_ref[...] = counts[:, 0:LANES]
    n_tiles = jnp.floor((counts + (TM_E - 1)) * (1.0 / TM_E))
    li = lax.broadcasted_iota(jnp.int32, (N_EXPERTS, N_EXPERTS), 0)
    lj = lax.broadcasted_iota(jnp.int32, (N_EXPERTS, N_EXPERTS), 1)
    lower = jnp.where(lj < li, 1.0, 0.0).astype(BF16)
    row_off = jnp.dot(lower, n_tiles.astype(BF16), preferred_element_type=F32) * TM_E

    def pass2(b, carry):
        cols = pl.ds(pl.multiple_of(b * TB, TB), TB)
        tok = tok_scr[:, cols]
        off1 = jnp.sum(jnp.where(row_e == tok[0:1, :], row_off, 0.0), axis=0, keepdims=True)
        off2 = jnp.sum(jnp.where(row_e == tok[1:2, :], row_off, 0.0), axis=0, keepdims=True)
        slots_ref[:, cols] = rows01(off1 + tok[2:3, :], off2 + tok[3:4, :]).astype(jnp.int32)
        return carry

    lax.fori_loop(0, t // TB, pass2, 0)


def _routing(logits_t):
    t = logits_t.shape[1]
    return pl.pallas_call(
        _routing_kernel,
        out_shape=[jax.ShapeDtypeStruct((8, t), jnp.int32),
                   jax.ShapeDtypeStruct((8, t), F32),
                   jax.ShapeDtypeStruct((N_EXPERTS, LANES), F32)],
        scratch_shapes=[pltpu.VMEM((8, t), F32)],
        compiler_params=pltpu.CompilerParams(vmem_limit_bytes=VMEM_LIMIT),
        name="routing",
    )(logits_t)


def _rows_copy(src, dst, sem, src_tok, dst_tok, n):
    first = lambda tok: tok * SUB if isinstance(tok, int) else pl.multiple_of(tok * SUB, SUB)
    s0 = first(src_tok)
    d0 = first(dst_tok)
    return pltpu.make_async_copy(src.at[pl.ds(s0, n * SUB), :], dst.at[pl.ds(d0, n * SUB), :], sem)


def _dispatch_kernel(sl1_ref, sl2_ref, ends_ref, h2_ref, xs_hbm, zero_scr, sem):
    step = pl.program_id(0)

    @pl.when(step == 0)
    def _():
        zero_scr[...] = jnp.zeros_like(zero_scr)
        for e in range(N_EXPERTS):
            start = jnp.maximum(ends_ref[e] - TM_E, 0)
            _rows_copy(zero_scr, xs_hbm, sem, 0, start, TM_E).start()
        for e in range(N_EXPERTS):
            _rows_copy(zero_scr, xs_hbm, sem, 0, 0, TM_E).wait()

    base = step * TB_D

    def issue(r, carry):
        _rows_copy(h2_ref, xs_hbm, sem, r, sl1_ref[base + r], 1).start(priority=0)
        _rows_copy(h2_ref, xs_hbm, sem, r, sl2_ref[base + r], 1).start(priority=1)
        return carry

    lax.fori_loop(0, TB_D, issue, 0, unroll=8)
    _rows_copy(h2_ref, xs_hbm, sem, 0, 0, TB_D).wait()
    _rows_copy(h2_ref, xs_hbm, sem, 0, 0, TB_D).wait()


def _dispatch(h2, sl1, sl2, ends, n_rows):
    t = h2.shape[0] // SUB
    return pl.pallas_call(
        _dispatch_kernel,
        grid_spec=pltpu.PrefetchScalarGridSpec(
            num_scalar_prefetch=3,
            grid=(t // TB_D,),
            in_specs=[pl.BlockSpec((TB_D * SUB, LANES), lambda i, *_: (i, 0))],
            out_specs=pl.BlockSpec(memory_space=pl.ANY),
            scratch_shapes=[pltpu.VMEM((TM_E * SUB, LANES), BF16), pltpu.SemaphoreType.DMA(())],
        ),
        out_shape=jax.ShapeDtypeStruct((n_rows * SUB, LANES), BF16),
        compiler_params=_params(1),
        name="dispatch",
    )(sl1, sl2, ends, h2)


def _expert_kernel(tidx_ref, texp_ref, tfirst_ref, tslot_ref, tnext_ref, nt_ref,
                   xs_ref, wg_hbm, wu_hbm, wd_hbm, ys_ref,
                   wg_buf, wu_buf, wd_buf, wg_bf, wu_bf, wd_bf, tt_scr, sems):
    j = pl.program_id(0)

    def weight_copies(e, slot):
        return [pltpu.make_async_copy(hbm.at[e], buf.at[slot], sems.at[slot, k])
                for k, (hbm, buf) in enumerate(((wg_hbm, wg_buf), (wu_hbm, wu_buf),
                                                (wd_hbm, wd_buf)))]

    @pl.when(j < nt_ref[0])
    def _():
        slot = tslot_ref[j]

        @pl.when(tfirst_ref[j] == 1)
        def _():
            @pl.when(j == 0)
            def _():
                for c in weight_copies(texp_ref[j], slot):
                    c.start()
            for c in weight_copies(texp_ref[j], slot):
                c.wait()

            @pl.when(tnext_ref[j] >= 0)
            def _():
                for c in weight_copies(tnext_ref[j], 1 - slot):
                    c.start()
            wg_bf[...] = wg_buf[slot].astype(BF16)
            wu_bf[...] = wu_buf[slot].astype(BF16)
            wd_bf[...] = wd_buf[slot].astype(BF16)

        xb = _load_token_tiles(xs_ref, tt_scr, TM_E).astype(BF16)
        g = jnp.dot(xb, wg_bf[...], preferred_element_type=F32)
        u = jnp.dot(xb, wu_bf[...], preferred_element_type=F32)
        h = (_silu(g) * u).astype(BF16)
        _store_token_tiles(ys_ref, jnp.dot(h, wd_bf[...], preferred_element_type=F32), tt_scr)


def _expert_mlp(xs, tile_meta, wg, wu, wd):
    n_rows = xs.shape[0] // SUB
    tile = pl.BlockSpec((TM_E * SUB, LANES), lambda j, ti, *_: (ti[j], 0))
    hbm = pl.BlockSpec(memory_space=pl.ANY)
    return pl.pallas_call(
        _expert_kernel,
        grid_spec=pltpu.PrefetchScalarGridSpec(
            num_scalar_prefetch=len(tile_meta),
            grid=(n_rows // TM_E,),
            in_specs=[tile, hbm, hbm, hbm],
            out_specs=tile,
            scratch_shapes=[pltpu.VMEM((2, D_MODEL, D_FF), F32),
                            pltpu.VMEM((2, D_MODEL, D_FF), F32),
                            pltpu.VMEM((2, D_FF, D_MODEL), F32),
                            pltpu.VMEM((D_MODEL, D_FF), BF16),
                            pltpu.VMEM((D_MODEL, D_FF), BF16),
                            pltpu.VMEM((D_FF, D_MODEL), BF16),
                            pltpu.VMEM((TM_E * TT_PITCH, LANES), F32),
                            pltpu.SemaphoreType.DMA((2, 3))],
        ),
        out_shape=jax.ShapeDtypeStruct((n_rows * SUB, LANES), BF16),
        compiler_params=_params(1),
        name="expert_mlp",
    )(*tile_meta, xs, wg, wu, wd)


def _combine_kernel(sl1_ref, sl2_ref, x1_ref, cw_ref, ys_hbm, o_ref, y1_scr, y2_scr, tt1_scr,
                    tt2_scr, sems):
    i = pl.program_id(0)

    def issue(step, slot):
        base = step * TB

        def body(r, carry):
            _rows_copy(ys_hbm, y1_scr.at[slot], sems.at[slot], sl1_ref[base + r], r,
                       1).start(priority=0)
            _rows_copy(ys_hbm, y2_scr.at[slot], sems.at[slot], sl2_ref[base + r], r,
                       1).start(priority=1)
            return carry

        lax.fori_loop(0, TB, body, 0, unroll=8)

    @pl.when(i == 0)
    def _():
        issue(0, 0)

    last = pl.num_programs(0) - 1
    rb = 64
    n_pieces = (TB // rb) * SUB
    per_piece = TB // n_pieces

    def step(slot):
        _rows_copy(ys_hbm, y1_scr.at[slot], sems.at[slot], 0, 0, TB).wait()
        _rows_copy(ys_hbm, y2_scr.at[slot], sems.at[slot], 0, 0, TB).wait()
        nbase = jnp.minimum(i + 1, last) * TB
        cw = jnp.concatenate([cw_ref[...], jnp.zeros((LANES - 8, TB), F32)], axis=0).T
        _tt_fill(tt1_scr, y1_scr.at[slot], TB)
        _tt_fill(tt2_scr, y2_scr.at[slot], TB)
        piece = 0
        for r0 in range(0, TB, rb):
            rows = slice(r0, r0 + rb)
            c1 = jnp.broadcast_to(cw[rows, 0:1], (rb, LANES))
            c2 = jnp.broadcast_to(cw[rows, 1:2], (rb, LANES))
            for s in range(SUB):
                for r in range(piece * per_piece, (piece + 1) * per_piece):
                    _rows_copy(ys_hbm, y1_scr.at[1 - slot], sems.at[1 - slot],
                               sl1_ref[nbase + r], r, 1).start(priority=0)
                    _rows_copy(ys_hbm, y2_scr.at[1 - slot], sems.at[1 - slot],
                               sl2_ref[nbase + r], r, 1).start(priority=1)
                piece += 1
                cols = slice(s * LANES, (s + 1) * LANES)
                tiles = _tt_piece(s, rb, r0)
                o_ref[rows, cols] = (x1_ref[rows, cols] + c1 * tt1_scr[tiles]
                                     + c2 * tt2_scr[tiles])

        @pl.when(i == last)
        def _():
            _rows_copy(ys_hbm, y1_scr.at[1 - slot], sems.at[1 - slot], 0, 0, TB).wait()
            _rows_copy(ys_hbm, y2_scr.at[1 - slot], sems.at[1 - slot], 0, 0, TB).wait()

    for parity in (0, 1):
        @pl.when(i % 2 == parity)
        def _():
            step(parity)


def _combine(x1, cw, ys, sl1, sl2):
    t = x1.shape[0]
    return pl.pallas_call(
        _combine_kernel,
        grid_spec=pltpu.PrefetchScalarGridSpec(
            num_scalar_prefetch=2,
            grid=(t // TB,),
            in_specs=[pl.BlockSpec((TB, D_MODEL), lambda i, *_: (i, 0)),
                      pl.BlockSpec((8, TB), lambda i, *_: (0, i)),
                      pl.BlockSpec(memory_space=pl.ANY)],
            out_specs=pl.BlockSpec((TB, D_MODEL), lambda i, *_: (i, 0)),
            scratch_shapes=[pltpu.VMEM((2, TB * SUB, LANES), BF16),
                            pltpu.VMEM((2, TB * SUB, LANES), BF16),
                            pltpu.VMEM((TB * TT_PITCH, LANES), F32),
                            pltpu.VMEM((TB * TT_PITCH, LANES), F32),
                            pltpu.SemaphoreType.DMA((2,))],
        ),
        out_shape=jax.ShapeDtypeStruct((t, D_MODEL), F32),
        compiler_params=_params(1),
        name="combine",
    )(sl1, sl2, x1, cw, ys)


def _lane_tile(v, reps):
    return jnp.tile(v.reshape(1, -1), (1, reps))


def kernel(x, positions, norm1_g, w_in, q_norm_g, k_norm_g, lambda_q1, lambda_k1, lambda_q2,
           lambda_k2, diff_out_norm_g, gla_w_gate2, gla_b_gate, gla_out_norm_g, w_out, norm2_g,
           w_router_group, b_router_group, w_router_expert, b_router_expert, w_gate_expert,
           w_up_expert, w_down_expert):
    batch, seq, d = x.shape
    t = batch * seq
    x2 = x.reshape(t, d)

    inv = ROPE_THETA ** (-jnp.arange(0, ROT_DIM, 2, dtype=F32) / ROT_DIM)
    lane_d = jnp.arange(LANES) % DIFF_QK_DIM
    invf = jnp.where(lane_d < ROT_DIM, inv[lane_d % (ROT_DIM // 2)], 0.0).reshape(1, LANES)
    pos_b = jnp.broadcast_to(positions.astype(F32).reshape(t, 1), (t, LANES))
    tabs = _rope_tables(pos_b, invf)

    w_in_t = jnp.swapaxes(w_in, 1, 2)[0]
    wa_t = jnp.pad(w_in_t[D_MAIN:, :], ((0, LANES - GLA_GATE_RANK), (0, 0)))
    proj, ga = _in_proj(x2, norm1_g, w_in_t, wa_t)

    a_out = _diff_attention(
        proj, tabs, _lane_tile(q_norm_g[0], 2), _lane_tile(k_norm_g[0], 2),
        lambda_q1, lambda_k1, lambda_q2, lambda_k2, diff_out_norm_g, batch, seq)
    w2p = jnp.pad(gla_w_gate2[0], ((0, LANES - GLA_GATE_RANK), (0, 0))).astype(BF16)
    g_out = _gla(proj, ga, w2p, gla_b_gate, gla_out_norm_g, batch, seq)

    gpad = GROUP_ROWS - N_GROUPS
    lpad = LANES - N_LOGIT_ROWS
    wr = jnp.concatenate(
        [jnp.pad(w_router_group[0], ((0, 0), (0, gpad))),
         jnp.pad(w_router_expert[0].transpose(1, 0, 2).reshape(d, N_EXPERTS),
                 ((0, 0), (0, lpad)))], axis=1)
    br = jnp.concatenate([jnp.pad(b_router_group[0], (0, gpad)),
                          jnp.pad(b_router_expert[0].reshape(-1), (0, lpad))]).reshape(1, LANES)
    x1, h2, logits_t = _out_proj(a_out, g_out, x2, w_out[0].astype(BF16), norm2_g,
                                 wr.astype(BF16), br)

    slots, cw, counts = _routing(logits_t)
    sl1 = slots[0]
    sl2 = slots[1]
    n_rows = 2 * t + N_EXPERTS * TM_E
    max_tiles = n_rows // TM_E
    cnt = counts[:, 0].astype(jnp.int32)
    tile_end = jnp.cumsum((cnt + (TM_E - 1)) // TM_E)
    n_tiles = tile_end[-1:]
    tile_idx = jnp.minimum(jnp.arange(max_tiles, dtype=jnp.int32), n_tiles[0] - 1)
    tile_exp = jnp.sum(tile_idx[:, None] >= tile_end[None, :], axis=1).astype(jnp.int32)
    row_end = (tile_end * TM_E).astype(jnp.int32)
    tile_first = jnp.concatenate(
        [jnp.ones((1,), jnp.int32), (tile_exp[1:] != tile_exp[:-1]).astype(jnp.int32)])
    tile_slot = (jnp.cumsum(tile_first) - 1) % 2
    nxt = tile_end[tile_exp]
    tile_next = jnp.where(nxt < n_tiles[0], tile_exp[jnp.minimum(nxt, max_tiles - 1)], -1)
    tile_meta = [a.astype(jnp.int32)
                 for a in (tile_idx, tile_exp, tile_first, tile_slot, tile_next, n_tiles)]

    xs = _dispatch(h2, sl1, sl2, row_end, n_rows)
    ys = _expert_mlp(xs, tile_meta,
                     w_gate_expert[0].reshape(N_EXPERTS, d, D_FF),
                     w_up_expert[0].reshape(N_EXPERTS, d, D_FF),
                     w_down_expert[0].reshape(N_EXPERTS, D_FF, d))
    out = _combine(x1, cw, ys, sl1, sl2)
    return out.reshape(batch, seq, d)
```

```python
import functools
import math

import jax
import jax.numpy as jnp
from jax import lax
from jax.experimental import pallas as pl
from jax.experimental.pallas import tpu as pltpu

D_MODEL = 2048
CHUNK = 64
DIFF_QK_DIM = 64
DIFF_V_DIM = 128
DIFF_HEADS = 8
ROT_DIM = 16
ROPE_THETA = 500000.0
GLA_HEADS = 4
GLA_V_DIM = 256
GLA_K_DIM = 128
GLA_GATE_RANK = 16
GLA_TAU = 16.0
N_GROUPS = 4
EXPERTS_PER_GROUP = 8
N_EXPERTS = N_GROUPS * EXPERTS_PER_GROUP
D_FF = 512
RMS_EPS = 1e-6
LAMBDA_INIT = 0.8 - 0.6 * math.exp(-0.3 * 0)
D_MAIN = 6144
GROUP_ROWS = 8
N_LOGIT_ROWS = GROUP_ROWS + N_EXPERTS

LANES = 128
SUB = D_MODEL // LANES
TT_PITCH = 24
VMEM_LIMIT = 56 * 1024 * 1024

TM_IN = 1024
TN_IN = 1024
TQ = 256
GB = 256
TM_OUT = 512
RB_OUT = 256
TB = 256
TM_E = 256

F32 = jnp.float32
BF16 = jnp.bfloat16
HI = lax.Precision.HIGHEST


def _params(n_axes):
    return pltpu.CompilerParams(dimension_semantics=("arbitrary",) * n_axes,
                                vmem_limit_bytes=VMEM_LIMIT)


def _nt_dot(a, b):
    return lax.dot_general(a, b, (((1,), (1,)), ((), ())), preferred_element_type=F32)


def _tn_dot(a, b):
    return lax.dot_general(a, b, (((0,), (0,)), ((), ())), preferred_element_type=F32)


def _silu(x):
    return x * (1.0 / (1.0 + jnp.exp(-x)))


def _tt_fill(scr, ref, n, tok0=0):
    for t in range(n):
        scr[t * TT_PITCH:t * TT_PITCH + SUB, :] = ref[pl.ds((tok0 + t) * SUB, SUB), :].astype(F32)


def _tt_drain(ref, scr, n, tok0=0):
    for t in range(n):
        ref[pl.ds((tok0 + t) * SUB, SUB), :] = scr[t * TT_PITCH:t * TT_PITCH + SUB, :].astype(BF16)


def _tt_piece(s, n, tok0=0):
    return (pl.ds(tok0 * TT_PITCH + s, n, stride=TT_PITCH), slice(None))


def _store_token_tiles(ref, val, scr, tok0=0):
    n = val.shape[0]
    for s in range(SUB):
        scr[_tt_piece(s, n)] = val[:, s * LANES:(s + 1) * LANES]
    _tt_drain(ref, scr, n, tok0)


def _load_token_tiles(ref, scr, n, tok0=0):
    _tt_fill(scr, ref, n, tok0)
    return jnp.concatenate([scr[_tt_piece(s, n)] for s in range(SUB)], axis=1)


def _rope_table_kernel(pos_ref, invf_ref, c_ref, s1_ref, s2_ref):
    ang = pos_ref[...] * invf_ref[...]
    d = lax.broadcasted_iota(jnp.int32, ang.shape, 1) % DIFF_QK_DIM
    cos = jnp.cos(ang)
    sin = jnp.sin(ang)
    half = ROT_DIM // 2
    c_ref[...] = jnp.where(d < ROT_DIM, cos, 1.0)
    s1_ref[...] = jnp.where(d < half, -sin, 0.0)
    s2_ref[...] = jnp.where((d >= half) & (d < ROT_DIM), sin, 0.0)


def _rope_tables(pos_b, invf):
    t = pos_b.shape[0]
    tb = 1024
    spec = pl.BlockSpec((tb, LANES), lambda i: (i, 0))
    return pl.pallas_call(
        _rope_table_kernel,
        grid=(t // tb,),
        in_specs=[spec, pl.BlockSpec((1, LANES), lambda i: (0, 0))],
        out_specs=[spec, spec, spec],
        out_shape=[jax.ShapeDtypeStruct((t, LANES), F32)] * 3,
        compiler_params=_params(1),
        name="rope_tables",
    )(pos_b, invf)


def _in_proj_kernel(x_ref, g_ref, wt_ref, wat_ref, proj_ref, ga_ref, h_scr):
    @pl.when(pl.program_id(1) == 0)
    def _():
        def body(c, carry):
            rows = pl.ds(c * 256, 256)
            x = x_ref[rows, :]
            ms = jnp.mean(x * x, axis=-1, keepdims=True)
            h_scr[rows, :] = (x * lax.rsqrt(ms + RMS_EPS) * g_ref[...]).astype(BF16)
            return carry
        lax.fori_loop(0, TM_IN // 256, body, 0)
        ga_ref[...] = _nt_dot(h_scr[...], wat_ref[...].astype(BF16))

    proj_ref[...] = _nt_dot(h_scr[...], wt_ref[...].astype(BF16)).astype(BF16)


def _in_proj(x2, g1, w_in, wa):
    t = x2.shape[0]
    return pl.pallas_call(
        _in_proj_kernel,
        grid=(t // TM_IN, D_MAIN // TN_IN),
        in_specs=[
            pl.BlockSpec((TM_IN, D_MODEL), lambda i, j: (i, 0)),
            pl.BlockSpec((1, D_MODEL), lambda i, j: (0, 0)),
            pl.BlockSpec((TN_IN, D_MODEL), lambda i, j: (j, 0)),
            pl.BlockSpec((LANES, D_MODEL), lambda i, j: (0, 0)),
        ],
        out_specs=[
            pl.BlockSpec((TM_IN, TN_IN), lambda i, j: (i, j)),
            pl.BlockSpec((TM_IN, LANES), lambda i, j: (i, 0)),
        ],
        out_shape=[jax.ShapeDtypeStruct((t, D_MAIN), BF16),
                   jax.ShapeDtypeStruct((t, LANES), F32)],
        scratch_shapes=[pltpu.VMEM((TM_IN, D_MODEL), BF16)],
        compiler_params=_params(2),
        name="in_proj",
    )(x2, g1, w_in, wa)


def _norm_rope(x, g, c, s1, s2):
    lo = lax.broadcasted_iota(jnp.int32, x.shape, 1) < DIFF_QK_DIM
    x2 = x * x
    s_lo = jnp.sum(jnp.where(lo, x2, 0.0), axis=-1, keepdims=True)
    s_hi = jnp.sum(jnp.where(lo, 0.0, x2), axis=-1, keepdims=True)
    ms = jnp.where(lo, s_lo, s_hi) * (1.0 / DIFF_QK_DIM)
    y = x * lax.rsqrt(ms + RMS_EPS) * g
    half = ROT_DIM // 2
    return y * c + pltpu.roll(y, LANES - half, 1) * s1 + pltpu.roll(y, half, 1) * s2


def _attn_kernel(q_ref, k_ref, v_ref, c_ref, s1_ref, s2_ref, qg_ref, kg_ref,
                 lq1_ref, lk1_ref, lq2_ref, lk2_ref, og_ref, o_ref,
                 q1_scr, q2_scr, k_scr, v_scr, *bufs):
    s_len = q_ref.shape[0]
    s_bufs, e_bufs = bufs[0:4], bufs[4:8]
    lam = (jnp.exp(jnp.sum(lq1_ref[...] * lk1_ref[...], axis=-1, keepdims=True))
           - jnp.exp(jnp.sum(lq2_ref[...] * lk2_ref[...], axis=-1, keepdims=True))
           + LAMBDA_INIT)
    lo = lax.broadcasted_iota(jnp.int32, (TQ, LANES), 1) < DIFF_QK_DIM
    v_scr[:, 0:LANES] = v_ref[...]
    v_scr[:, LANES:] = jnp.ones((s_len, LANES), BF16)

    def prepare(i):
        rows = slice(i * TQ, (i + 1) * TQ)
        c, s1, s2 = c_ref[rows, :], s1_ref[rows, :], s2_ref[rows, :]
        qn = (_norm_rope(q_ref[rows, :].astype(F32), qg_ref[...], c, s1, s2)
              * (DIFF_QK_DIM ** -0.5 * math.log2(math.e)))
        q1_scr[rows, :] = jnp.where(lo, qn, 0.0).astype(BF16)
        q2_scr[rows, :] = jnp.where(lo, 0.0, qn).astype(BF16)
        k_scr[rows, :] = _norm_rope(k_ref[rows, :].astype(F32), kg_ref[...], c, s1,
                                    s2).astype(BF16)

    diag = (lax.broadcasted_iota(jnp.int32, (TQ, TQ), 1) // CHUNK
            <= lax.broadcasted_iota(jnp.int32, (TQ, TQ), 0) // CHUNK)

    n_tiles = s_len // TQ
    items = [(i, q_scr) for i in range(n_tiles) for q_scr in (q1_scr, q2_scr)]

    def scores(r):
        i, q_scr = items[r]
        s_scr = s_bufs[r % 4]
        q = q_scr[i * TQ:(i + 1) * TQ, :]
        n_off = i * TQ
        s_scr[:, n_off:n_off + TQ] = jnp.where(
            diag, _nt_dot(q, k_scr[n_off:n_off + TQ, :]), -jnp.inf)
        if n_off:
            s_scr[:, 0:n_off] = _nt_dot(q, k_scr[0:n_off, :])

    def softmax_pv(r):
        nk = (items[r][0] + 1) * TQ
        s_scr, e_scr = s_bufs[r % 4], e_bufs[r % 4]
        m = jnp.max(s_scr[:, 0:nk], axis=-1, keepdims=True)
        e_scr[:, 0:nk] = jnp.exp2(s_scr[:, 0:nk] - m).astype(BF16)
        acc = jnp.dot(e_scr[:, 0:nk], v_scr[0:nk, :], preferred_element_type=F32)
        return acc[:, 0:LANES] / acc[:, LANES:]

    prepare(0)
    scores(0)
    scores(1)
    for i in range(n_tiles):
        rows = slice(i * TQ, (i + 1) * TQ)
        if i + 1 < n_tiles:
            prepare(i + 1)
            scores(2 * i + 2)
        o1 = softmax_pv(2 * i)
        if i + 1 < n_tiles:
            scores(2 * i + 3)
        o = o1 - lam * softmax_pv(2 * i + 1)
        ms = jnp.mean(o * o, axis=-1, keepdims=True)
        y = o * lax.rsqrt(ms + RMS_EPS) * og_ref[...] * (1.0 - LAMBDA_INIT)
        o_ref[rows, :] = y.astype(BF16)


def _diff_attention(proj, tabs, qg, kg, lq1, lk1, lq2, lk2, og, batch, seq):
    c, s1, s2 = tabs
    h = DIFF_HEADS
    blk = lambda off: pl.BlockSpec((seq, LANES), lambda b, hh, off=off: (b, off + hh))
    tab = pl.BlockSpec((seq, LANES), lambda b, hh: (b, 0))
    vec = lambda n: pl.BlockSpec((1, n), lambda b, hh: (0, 0))
    return pl.pallas_call(
        _attn_kernel,
        grid=(batch, h),
        in_specs=[blk(0), blk(h), blk(2 * h), tab, tab, tab,
                  vec(LANES), vec(LANES), vec(DIFF_QK_DIM), vec(DIFF_QK_DIM),
                  vec(DIFF_QK_DIM), vec(DIFF_QK_DIM), vec(LANES)],
        out_specs=pl.BlockSpec((seq, LANES), lambda b, hh: (b, hh)),
        out_shape=jax.ShapeDtypeStruct((batch * seq, h * DIFF_V_DIM), BF16),
        scratch_shapes=[pltpu.VMEM((seq, LANES), BF16)] * 3
        + [pltpu.VMEM((seq, 2 * LANES), BF16)]
        + [pltpu.VMEM((TQ, seq), F32)] * 4 + [pltpu.VMEM((TQ, seq), BF16)] * 4,
        compiler_params=_params(2),
        name="diff_attention",
    )(proj, proj, proj, c, s1, s2, qg, kg, lq1, lk1, lq2, lk2, og)


def _split_dot(ones_bf, x):
    hi = x.astype(BF16)
    lo = (x - hi.astype(F32)).astype(BF16)
    return (jnp.dot(ones_bf, hi, preferred_element_type=F32)
            + jnp.dot(ones_bf, lo, preferred_element_type=F32))


def _gla_kernel(q_ref, k_ref, v_ref, r_ref, ga_ref, w2_ref, b2_ref, og_ref, o_ref,
                qin_scr, ut_scr, dec_scr, st_scr, acc_scr):
    s_len = q_ref.shape[0]
    cpg = GB // CHUNK
    ri = lax.broadcasted_iota(jnp.int32, (GB, GB), 0)
    ci = lax.broadcasted_iota(jnp.int32, (GB, GB), 1)
    same = (ri // CHUNK) == (ci // CHUNK)
    blk_ones = jnp.where(same, 1.0, 0.0).astype(BF16)
    tril = same & (ci <= ri)
    tri_ones = jnp.where(tril, 1.0, 0.0).astype(BF16)

    def phase_a(g, carry):
        rows = pl.ds(pl.multiple_of(g * GB, GB), GB)
        pre = jnp.dot(ga_ref[rows, :].astype(BF16), w2_ref[...],
                      preferred_element_type=F32) + b2_ref[...]
        la = -(jnp.maximum(-pre, 0.0) + jnp.log1p(jnp.exp(-jnp.abs(pre)))) * (1.0 / GLA_TAU)
        bc = _split_dot(tri_ones, la)
        bl = _split_dot(blk_ones, la)
        e_neg = jnp.exp(-bc)
        e_last = jnp.exp(bl)
        k = k_ref[rows, :].astype(F32)
        q_in = (q_ref[rows, :].astype(F32) * (GLA_K_DIM ** -0.5) * jnp.exp(bc)).astype(BF16)
        k_in = (k * e_neg).astype(BF16)
        k_dec = (k * (e_last * e_neg)).astype(BF16)
        qin_scr[rows, :] = q_in
        dec_scr[rows, :] = e_last
        v = v_ref[rows, :]
        att = jnp.where(tril, _nt_dot(q_in, k_in), 0.0).astype(BF16)
        acc_scr[rows, :] = jnp.dot(att, v, preferred_element_type=F32)
        for c in range(cpg):
            cr = slice(c * CHUNK, (c + 1) * CHUNK)
            ut_scr[g * cpg + c] = _tn_dot(v[cr, :], k_dec[cr, :])
        return carry

    lax.fori_loop(0, s_len // GB, phase_a, 0, unroll=8)

    def phase_b(c, st):
        st_scr[c] = st.astype(BF16)
        dec = dec_scr[pl.ds(pl.multiple_of(c * CHUNK, CHUNK), 1), :]
        return dec * st + ut_scr[c]

    lax.fori_loop(0, s_len // CHUNK, phase_b, jnp.zeros((GLA_V_DIM, GLA_K_DIM), F32))

    def phase_c(g, carry):
        rows = pl.ds(pl.multiple_of(g * GB, GB), GB)
        inter = [_nt_dot(qin_scr[pl.ds(pl.multiple_of(g * GB + c * CHUNK, CHUNK), CHUNK), :],
                         st_scr[g * cpg + c]) for c in range(cpg)]
        o = acc_scr[rows, :] + jnp.concatenate(inter, axis=0)
        ms = jnp.mean(o * o, axis=-1, keepdims=True)
        y = o * lax.rsqrt(ms + RMS_EPS) * og_ref[...]
        o_ref[rows, :] = (y * _silu(r_ref[rows, :].astype(F32))).astype(BF16)
        return carry

    lax.fori_loop(0, s_len // GB, phase_c, 0, unroll=8)


def _gla(proj, ga, w2p, b2, og, batch, seq):
    hq = 3 * DIFF_HEADS
    kblk = lambda off: pl.BlockSpec((seq, GLA_K_DIM), lambda b, hh, off=off: (b, off + hh))
    vblk = lambda off: pl.BlockSpec((seq, GLA_V_DIM), lambda b, hh, off=off: (b, off + hh))
    return pl.pallas_call(
        _gla_kernel,
        grid=(batch, GLA_HEADS),
        in_specs=[kblk(hq), kblk(hq + GLA_HEADS), vblk(16), vblk(16 + GLA_HEADS),
                  pl.BlockSpec((seq, LANES), lambda b, hh: (b, 0)),
                  pl.BlockSpec((LANES, GLA_K_DIM), lambda b, hh: (0, hh)),
                  pl.BlockSpec((1, GLA_K_DIM), lambda b, hh: (0, hh)),
                  pl.BlockSpec((1, GLA_V_DIM), lambda b, hh: (0, 0))],
        out_specs=pl.BlockSpec((seq, GLA_V_DIM), lambda b, hh: (b, hh)),
        out_shape=jax.ShapeDtypeStruct((batch * seq, GLA_HEADS * GLA_V_DIM), BF16),
        scratch_shapes=[pltpu.VMEM((seq, GLA_K_DIM), BF16),
                        pltpu.VMEM((seq // CHUNK, GLA_V_DIM, GLA_K_DIM), F32),
                        pltpu.VMEM((seq, GLA_K_DIM), F32),
                        pltpu.VMEM((seq // CHUNK, GLA_V_DIM, GLA_K_DIM), BF16),
                        pltpu.VMEM((seq, GLA_V_DIM), F32)],
        compiler_params=_params(2),
        name="gla",
    )(proj, proj, proj, proj, ga, w2p, b2, og)


def _out_proj_kernel(a_ref, g_ref, x_ref, wo_ref, g2_ref, wr_ref, br_ref,
                     x1_ref, h2_ref, lg_ref, tt_scr):
    half = a_ref.shape[1]

    def body(c, carry):
        r0 = pl.multiple_of(c * RB_OUT, RB_OUT)
        rows = pl.ds(r0, RB_OUT)
        mixed = (jnp.dot(a_ref[rows, :], wo_ref[0:half, :], preferred_element_type=F32)
                 + jnp.dot(g_ref[rows, :], wo_ref[half:, :], preferred_element_type=F32))
        x1 = x_ref[rows, :] + mixed
        x1_ref[rows, :] = x1
        ms = jnp.mean(x1 * x1, axis=-1, keepdims=True)
        h2 = x1 * lax.rsqrt(ms + RMS_EPS) * g2_ref[...]
        _store_token_tiles(h2_ref, h2, tt_scr, r0)
        lg = jnp.dot(h2.astype(BF16), wr_ref[...], preferred_element_type=F32) + br_ref[...]
        lg_ref[:, rows] = lg.T[0:N_LOGIT_ROWS, :]
        return carry

    lax.fori_loop(0, TM_OUT // RB_OUT, body, 0)


def _out_proj(a_out, g_out, x2, wo, g2, wr, br):
    t = x2.shape[0]
    half = a_out.shape[1]
    row = lambda n: pl.BlockSpec((TM_OUT, n), lambda i: (i, 0))
    full = lambda r, n: pl.BlockSpec((r, n), lambda i: (0, 0))
    return pl.pallas_call(
        _out_proj_kernel,
        grid=(t // TM_OUT,),
        in_specs=[row(half), row(half), row(D_MODEL), full(D_MODEL, D_MODEL),
                  full(1, D_MODEL), full(D_MODEL, LANES), full(1, LANES)],
        out_specs=[row(D_MODEL), pl.BlockSpec((TM_OUT * SUB, LANES), lambda i: (i, 0)),
                   pl.BlockSpec((N_LOGIT_ROWS, TM_OUT), lambda i: (0, i))],
        out_shape=[jax.ShapeDtypeStruct((t, D_MODEL), F32),
                   jax.ShapeDtypeStruct((t * SUB, LANES), BF16),
                   jax.ShapeDtypeStruct((N_LOGIT_ROWS, t), F32)],
        scratch_shapes=[pltpu.VMEM((RB_OUT * TT_PITCH, LANES), F32)],
        compiler_params=_params(1),
        name="out_proj_router",
    )(a_out, g_out, x2, wo, g2, wr, br)


def _first_argmax(vals, row_f):
    m = jnp.max(vals, axis=0, keepdims=True)
    idx = jnp.min(jnp.where(vals == m, row_f, float(vals.shape[0])), axis=0, keepdims=True)
    return m, idx


def _softmax_rows(v):
    ex = jnp.exp(v - jnp.max(v, axis=0, keepdims=True))
    return ex / jnp.sum(ex, axis=0, keepdims=True)


def _routing_kernel(lg_ref, slots_ref, cw_ref, cnt_ref, tok_scr):
    t = lg_ref.shape[1]
    epg = EXPERTS_PER_GROUP
    row8 = lax.broadcasted_iota(jnp.int32, (8, TB), 0).astype(F32)
    row_e = lax.broadcasted_iota(jnp.int32, (N_EXPERTS, TB), 0).astype(F32)
    ui = lax.broadcasted_iota(jnp.int32, (TB, TB), 0)
    uj = lax.broadcasted_iota(jnp.int32, (TB, TB), 1)
    earlier = jnp.where(ui < uj, 1.0, 0.0).astype(BF16)
    all_ones = jnp.ones((TB, TB), BF16)

    def rows01(r0, r1):
        return jnp.where(row8 == 0.0, r0, jnp.where(row8 == 1.0, r1, jnp.zeros_like(r0)))

    def pass1(b, counts):
        cols = pl.ds(pl.multiple_of(b * TB, TB), TB)
        lg = lg_ref[:, cols]
        lg_groups = jnp.where(row8 < float(N_GROUPS), lg[0:GROUP_ROWS, :], -jnp.inf)
        pg_sel, g_sel = _first_argmax(_softmax_rows(lg_groups), row8)
        le = lg[GROUP_ROWS:GROUP_ROWS + epg, :]
        for g in range(1, N_GROUPS):
            le = jnp.where(g_sel == float(g),
                           lg[GROUP_ROWS + g * epg:GROUP_ROWS + (g + 1) * epg, :], le)
        pe = _softmax_rows(le)
        v1, i1 = _first_argmax(pe, row8)
        v2, i2 = _first_argmax(jnp.where(row8 == i1, -jnp.inf, pe), row8)
        tot = v1 + v2
        e1 = g_sel * epg + i1
        e2 = g_sel * epg + i2
        oh1 = row_e == e1
        oh2 = row_e == e2
        a = jnp.where(oh1 | oh2, 1.0, 0.0).astype(BF16)
        rank = jnp.dot(a, earlier, preferred_element_type=F32) + counts
        r1 = jnp.sum(jnp.where(oh1, rank, 0.0), axis=0, keepdims=True)
        r2 = jnp.sum(jnp.where(oh2, rank, 0.0), axis=0, keepdims=True)
        tok_scr[:, cols] = jnp.where(row8 == 0.0, e1, jnp.where(
            row8 == 1.0, e2, jnp.where(row8 == 2.0, r1, jnp.where(row8 == 3.0, r2, 0.0))))
        cw_ref[:, cols] = rows01((v1 / tot) * pg_sel, (v2 / tot) * pg_sel)
        return counts + jnp.dot(a, all_ones, preferred_element_type=F32)

    counts = lax.fori_loop(0, t // TB, pass1, jnp.zeros((N_EXPERTS, TB), F32))
    cnt_ref[...] = counts[:, 0:LANES]
    n_tiles = jnp.floor((counts + (TM_E - 1)) * (1.0 / TM_E))
    li = lax.broadcasted_iota(jnp.int32, (N_EXPERTS, N_EXPERTS), 0)
    lj = lax.broadcasted_iota(jnp.int32, (N_EXPERTS, N_EXPERTS), 1)
    lower = jnp.where(lj < li, 1.0, 0.0).astype(BF16)
    row_off = jnp.dot(lower, n_tiles.astype(BF16), preferred_element_type=F32) * TM_E

    def pass2(b, carry):
        cols = pl.ds(pl.multiple_of(b * TB, TB), TB)
        tok = tok_scr[:, cols]
        off1 = jnp.sum(jnp.where(row_e == tok[0:1, :], row_off, 0.0), axis=0, keepdims=True)
        off2 = jnp.sum(jnp.where(row_e == tok[1:2, :], row_off, 0.0), axis=0, keepdims=True)
        slots_ref[:, cols] = rows01(off1 + tok[2:3, :], off2 + tok[3:4, :]).astype(jnp.int32)
        return carry

    lax.fori_loop(0, t // TB, pass2, 0)


def _routing(logits_t):
    t = logits_t.shape[1]
    return pl.pallas_call(
        _routing_kernel,
        out_shape=[jax.ShapeDtypeStruct((8, t), jnp.int32),
                   jax.ShapeDtypeStruct((8, t), F32),
                   jax.ShapeDtypeStruct((N_EXPERTS, LANES), F32)],
        scratch_shapes=[pltpu.VMEM((8, t), F32)],
        compiler_params=pltpu.CompilerParams(vmem_limit_bytes=VMEM_LIMIT),
        name="routing",
    )(logits_t)


def _rows_copy(src, dst, sem, src_tok, dst_tok, n):
    first = lambda tok: tok * SUB if isinstance(tok, int) else pl.multiple_of(tok * SUB, SUB)
    s0 = first(src_tok)
    d0 = first(dst_tok)
    return pltpu.make_async_copy(src.at[pl.ds(s0, n * SUB), :], dst.at[pl.ds(d0, n * SUB), :], sem)


def _row_token_kernel(sl1_ref, sl2_ref, o_ref):
    def clear(r, carry):
        o_ref[r] = 0
        return carry

    lax.fori_loop(0, o_ref.shape[0], clear, 0, unroll=8)

    def mark(t, carry):
        o_ref[sl1_ref[t]] = t
        o_ref[sl2_ref[t]] = t
        return carry

    lax.fori_loop(0, sl1_ref.shape[0], mark, 0, unroll=8)


def _row_tokens(sl1, sl2, n_rows):
    return pl.pallas_call(
        _row_token_kernel,
        grid_spec=pltpu.PrefetchScalarGridSpec(
            num_scalar_prefetch=2,
            grid=(1,),
            in_specs=[],
            out_specs=pl.BlockSpec(memory_space=pltpu.SMEM),
        ),
        out_shape=jax.ShapeDtypeStruct((n_rows,), jnp.int32),
        compiler_params=_params(1),
        name="row_tokens",
    )(sl1, sl2)


def _expert_kernel(tidx_ref, texp_ref, tfirst_ref, tslot_ref, tnext_ref, nt_ref, rtok_ref,
                   h2_hbm, wg_hbm, wu_hbm, wd_hbm, ys_ref,
                   wg_buf, wu_buf, wd_buf, wg_bf, wu_bf, wd_bf, tt_scr, xg_even, xg_odd,
                   sems, gsems):
    j = pl.program_id(0)
    xg = (xg_even, xg_odd)

    def weight_copies(e, slot):
        return [pltpu.make_async_copy(hbm.at[e], buf.at[slot], sems.at[slot, k])
                for k, (hbm, buf) in enumerate(((wg_hbm, wg_buf), (wu_hbm, wu_buf),
                                                (wd_hbm, wd_buf)))]

    def gather(tile, r, par):
        return _rows_copy(h2_hbm, xg[par], gsems.at[par], rtok_ref[tile * TM_E + r], r, 1)

    @pl.when(j == 0)
    def _():
        def first(r, carry):
            gather(0, r, 0).start()
            return carry
        lax.fori_loop(0, TM_E, first, 0, unroll=8)

    @pl.when(j < nt_ref[0])
    def _():
        slot = tslot_ref[j]

        @pl.when(tfirst_ref[j] == 1)
        def _():
            @pl.when(j == 0)
            def _():
                for c in weight_copies(texp_ref[j], slot):
                    c.start()
            for c in weight_copies(texp_ref[j], slot):
                c.wait()

            @pl.when(tnext_ref[j] >= 0)
            def _():
                for c in weight_copies(tnext_ref[j], 1 - slot):
                    c.start()
            wg_bf[...] = wg_buf[slot].astype(BF16)
            wu_bf[...] = wu_buf[slot].astype(BF16)
            wd_bf[...] = wd_buf[slot].astype(BF16)

        def tile_body(par):
            pltpu.make_async_copy(xg[par], xg[par], gsems.at[par]).wait()
            nxt = jnp.minimum(j + 1, nt_ref[0] - 1)
            for r in range(TM_E):
                gather(nxt, r, 1 - par).start()
            xb = _load_token_tiles(xg[par], tt_scr, TM_E).astype(BF16)
            g = jnp.dot(xb, wg_bf[...], preferred_element_type=F32)
            u = jnp.dot(xb, wu_bf[...], preferred_element_type=F32)
            h = (_silu(g) * u).astype(BF16)
            _store_token_tiles(ys_ref, jnp.dot(h, wd_bf[...], preferred_element_type=F32),
                               tt_scr)

            @pl.when(j == nt_ref[0] - 1)
            def _():
                pltpu.make_async_copy(xg[1 - par], xg[1 - par], gsems.at[1 - par]).wait()

        for par in (0, 1):
            @pl.when(j % 2 == par)
            def _():
                tile_body(par)


def _expert_mlp(h2, row_tok, tile_meta, wg, wu, wd):
    n_rows = row_tok.shape[0]
    tile = pl.BlockSpec((TM_E * SUB, LANES), lambda j, ti, *_: (ti[j], 0))
    hbm = pl.BlockSpec(memory_space=pl.ANY)
    return pl.pallas_call(
        _expert_kernel,
        grid_spec=pltpu.PrefetchScalarGridSpec(
            num_scalar_prefetch=len(tile_meta) + 1,
            grid=(n_rows // TM_E,),
            in_specs=[hbm, hbm, hbm, hbm],
            out_specs=tile,
            scratch_shapes=[pltpu.VMEM((2, D_MODEL, D_FF), F32),
                            pltpu.VMEM((2, D_MODEL, D_FF), F32),
                            pltpu.VMEM((2, D_FF, D_MODEL), F32),
                            pltpu.VMEM((D_MODEL, D_FF), BF16),
                            pltpu.VMEM((D_MODEL, D_FF), BF16),
                            pltpu.VMEM((D_FF, D_MODEL), BF16),
                            pltpu.VMEM((TM_E * TT_PITCH, LANES), F32),
                            pltpu.VMEM((TM_E * SUB, LANES), BF16),
                            pltpu.VMEM((TM_E * SUB, LANES), BF16),
                            pltpu.SemaphoreType.DMA((2, 3)),
                            pltpu.SemaphoreType.DMA((2,))],
        ),
        out_shape=jax.ShapeDtypeStruct((n_rows * SUB, LANES), BF16),
        compiler_params=_params(1),
        name="expert_mlp",
    )(*tile_meta, row_tok, h2, wg, wu, wd)


def _combine_kernel(sl1_ref, sl2_ref, x1_ref, cw_ref, ys_hbm, o_ref, y1_scr, y2_scr, tt1_scr,
                    tt2_scr, sems):
    i = pl.program_id(0)

    def issue(step, slot):
        base = step * TB

        def body(r, carry):
            _rows_copy(ys_hbm, y1_scr.at[slot], sems.at[slot], sl1_ref[base + r], r,
                       1).start(priority=0)
            _rows_copy(ys_hbm, y2_scr.at[slot], sems.at[slot], sl2_ref[base + r], r,
                       1).start(priority=1)
            return carry

        lax.fori_loop(0, TB, body, 0, unroll=8)

    @pl.when(i == 0)
    def _():
        issue(0, 0)

    last = pl.num_programs(0) - 1
    rb = 64
    n_pieces = (TB // rb) * SUB
    per_piece = TB // n_pieces

    def step(slot):
        _rows_copy(ys_hbm, y1_scr.at[slot], sems.at[slot], 0, 0, TB).wait()
        _rows_copy(ys_hbm, y2_scr.at[slot], sems.at[slot], 0, 0, TB).wait()
        nbase = jnp.minimum(i + 1, last) * TB
        cw = jnp.concatenate([cw_ref[...], jnp.zeros((LANES - 8, TB), F32)], axis=0).T
        _tt_fill(tt1_scr, y1_scr.at[slot], TB)
        _tt_fill(tt2_scr, y2_scr.at[slot], TB)
        piece = 0
        for r0 in range(0, TB, rb):
            rows = slice(r0, r0 + rb)
            c1 = jnp.broadcast_to(cw[rows, 0:1], (rb, LANES))
            c2 = jnp.broadcast_to(cw[rows, 1:2], (rb, LANES))
            for s in range(SUB):
                for r in range(piece * per_piece, (piece + 1) * per_piece):
                    _rows_copy(ys_hbm, y1_scr.at[1 - slot], sems.at[1 - slot],
                               sl1_ref[nbase + r], r, 1).start(priority=0)
                    _rows_copy(ys_hbm, y2_scr.at[1 - slot], sems.at[1 - slot],
                               sl2_ref[nbase + r], r, 1).start(priority=1)
                piece += 1
                cols = slice(s * LANES, (s + 1) * LANES)
                tiles = _tt_piece(s, rb, r0)
                o_ref[rows, cols] = (x1_ref[rows, cols] + c1 * tt1_scr[tiles]
                                     + c2 * tt2_scr[tiles])

        @pl.when(i == last)
        def _():
            _rows_copy(ys_hbm, y1_scr.at[1 - slot], sems.at[1 - slot], 0, 0, TB).wait()
            _rows_copy(ys_hbm, y2_scr.at[1 - slot], sems.at[1 - slot], 0, 0, TB).wait()

    for parity in (0, 1):
        @pl.when(i % 2 == parity)
        def _():
            step(parity)


def _combine(x1, cw, ys, sl1, sl2):
    t = x1.shape[0]
    return pl.pallas_call(
        _combine_kernel,
        grid_spec=pltpu.PrefetchScalarGridSpec(
            num_scalar_prefetch=2,
            grid=(t // TB,),
            in_specs=[pl.BlockSpec((TB, D_MODEL), lambda i, *_: (i, 0)),
                      pl.BlockSpec((8, TB), lambda i, *_: (0, i)),
                      pl.BlockSpec(memory_space=pl.ANY)],
            out_specs=pl.BlockSpec((TB, D_MODEL), lambda i, *_: (i, 0)),
            scratch_shapes=[pltpu.VMEM((2, TB * SUB, LANES), BF16),
                            pltpu.VMEM((2, TB * SUB, LANES), BF16),
                            pltpu.VMEM((TB * TT_PITCH, LANES), F32),
                            pltpu.VMEM((TB * TT_PITCH, LANES), F32),
                            pltpu.SemaphoreType.DMA((2,))],
        ),
        out_shape=jax.ShapeDtypeStruct((t, D_MODEL), F32),
        compiler_params=_params(1),
        name="combine",
    )(sl1, sl2, x1, cw, ys)


def _lane_tile(v, reps):
    return jnp.tile(v.reshape(1, -1), (1, reps))


def kernel(x, positions, norm1_g, w_in, q_norm_g, k_norm_g, lambda_q1, lambda_k1, lambda_q2,
           lambda_k2, diff_out_norm_g, gla_w_gate2, gla_b_gate, gla_out_norm_g, w_out, norm2_g,
           w_router_group, b_router_group, w_router_expert, b_router_expert, w_gate_expert,
           w_up_expert, w_down_expert):
    batch, seq, d = x.shape
    t = batch * seq
    x2 = x.reshape(t, d)

    inv = ROPE_THETA ** (-jnp.arange(0, ROT_DIM, 2, dtype=F32) / ROT_DIM)
    lane_d = jnp.arange(LANES) % DIFF_QK_DIM
    invf = jnp.where(lane_d < ROT_DIM, inv[lane_d % (ROT_DIM // 2)], 0.0).reshape(1, LANES)
    pos_b = jnp.broadcast_to(positions.astype(F32).reshape(t, 1), (t, LANES))
    tabs = _rope_tables(pos_b, invf)

    w_in_t = jnp.swapaxes(w_in, 1, 2)[0]
    wa_t = jnp.pad(w_in_t[D_MAIN:, :], ((0, LANES - GLA_GATE_RANK), (0, 0)))
    proj, ga = _in_proj(x2, norm1_g, w_in_t, wa_t)

    a_out = _diff_attention(
        proj, tabs, _lane_tile(q_norm_g[0], 2), _lane_tile(k_norm_g[0], 2),
        lambda_q1, lambda_k1, lambda_q2, lambda_k2, diff_out_norm_g, batch, seq)
    w2p = jnp.pad(gla_w_gate2[0], ((0, LANES - GLA_GATE_RANK), (0, 0))).astype(BF16)
    g_out = _gla(proj, ga, w2p, gla_b_gate, gla_out_norm_g, batch, seq)

    gpad = GROUP_ROWS - N_GROUPS
    lpad = LANES - N_LOGIT_ROWS
    wr = jnp.concatenate(
        [jnp.pad(w_router_group[0], ((0, 0), (0, gpad))),
         jnp.pad(w_router_expert[0].transpose(1, 0, 2).reshape(d, N_EXPERTS),
                 ((0, 0), (0, lpad)))], axis=1)
    br = jnp.concatenate([jnp.pad(b_router_group[0], (0, gpad)),
                          jnp.pad(b_router_expert[0].reshape(-1), (0, lpad))]).reshape(1, LANES)
    x1, h2, logits_t = _out_proj(a_out, g_out, x2, w_out[0].astype(BF16), norm2_g,
                                 wr.astype(BF16), br)

    slots, cw, counts = _routing(logits_t)
    sl1 = slots[0]
    sl2 = slots[1]
    n_rows = 2 * t + N_EXPERTS * TM_E
    max_tiles = n_rows // TM_E
    cnt = counts[:, 0].astype(jnp.int32)
    tile_end = jnp.cumsum((cnt + (TM_E - 1)) // TM_E)
    n_tiles = tile_end[-1:]
    tile_idx = jnp.minimum(jnp.arange(max_tiles, dtype=jnp.int32), n_tiles[0] - 1)
    tile_exp = jnp.sum(tile_idx[:, None] >= tile_end[None, :], axis=1).astype(jnp.int32)
    tile_first = jnp.concatenate(
        [jnp.ones((1,), jnp.int32), (tile_exp[1:] != tile_exp[:-1]).astype(jnp.int32)])
    tile_slot = (jnp.cumsum(tile_first) - 1) % 2
    nxt = tile_end[tile_exp]
    tile_next = jnp.where(nxt < n_tiles[0], tile_exp[jnp.minimum(nxt, max_tiles - 1)], -1)
    tile_meta = [a.astype(jnp.int32)
                 for a in (tile_idx, tile_exp, tile_first, tile_slot, tile_next, n_tiles)]

    row_tok = _row_tokens(sl1, sl2, n_rows)
    ys = _expert_mlp(h2, row_tok, tile_meta,
                     w_gate_expert[0].reshape(N_EXPERTS, d, D_FF),
                     w_up_expert[0].reshape(N_EXPERTS, d, D_FF),
                     w_down_expert[0].reshape(N_EXPERTS, D_FF, d))
    out = _combine(x1, cw, ys, sl1, sl2)
    return out.reshape(batch, seq, d)
```

```python
import functools
import math

import jax
import jax.numpy as jnp
from jax import lax
from jax.experimental import pallas as pl
from jax.experimental.pallas import tpu as pltpu

D_MODEL = 2048
CHUNK = 64
DIFF_QK_DIM = 64
DIFF_V_DIM = 128
DIFF_HEADS = 8
ROT_DIM = 16
ROPE_THETA = 500000.0
GLA_HEADS = 4
GLA_V_DIM = 256
GLA_K_DIM = 128
GLA_GATE_RANK = 16
GLA_TAU = 16.0
N_GROUPS = 4
EXPERTS_PER_GROUP = 8
N_EXPERTS = N_GROUPS * EXPERTS_PER_GROUP
D_FF = 512
RMS_EPS = 1e-6
LAMBDA_INIT = 0.8 - 0.6 * math.exp(-0.3 * 0)
D_MAIN = 6144
GROUP_ROWS = 8
N_LOGIT_ROWS = GROUP_ROWS + N_EXPERTS

LANES = 128
SUB = D_MODEL // LANES
TT_PITCH = 24
VMEM_LIMIT = 56 * 1024 * 1024

TM_IN = 1024
TN_IN = 1024
TQ = 256
GB = 256
TM_OUT = 512
RB_OUT = 256
TB = 256
TM_E = 256

F32 = jnp.float32
BF16 = jnp.bfloat16
HI = lax.Precision.HIGHEST


def _params(n_axes):
    return pltpu.CompilerParams(dimension_semantics=("arbitrary",) * n_axes,
                                vmem_limit_bytes=VMEM_LIMIT)


def _nt_dot(a, b):
    return lax.dot_general(a, b, (((1,), (1,)), ((), ())), preferred_element_type=F32)


def _tn_dot(a, b):
    return lax.dot_general(a, b, (((0,), (0,)), ((), ())), preferred_element_type=F32)


def _silu(x):
    return x * (1.0 / (1.0 + jnp.exp(-x)))


def _tt_fill(scr, ref, n, tok0=0):
    for t in range(n):
        scr[t * TT_PITCH:t * TT_PITCH + SUB, :] = ref[pl.ds((tok0 + t) * SUB, SUB), :].astype(F32)


def _tt_drain(ref, scr, n, tok0=0):
    for t in range(n):
        ref[pl.ds((tok0 + t) * SUB, SUB), :] = scr[t * TT_PITCH:t * TT_PITCH + SUB, :].astype(BF16)


def _tt_piece(s, n, tok0=0):
    return (pl.ds(tok0 * TT_PITCH + s, n, stride=TT_PITCH), slice(None))


def _store_token_tiles(ref, val, scr, tok0=0):
    n = val.shape[0]
    for s in range(SUB):
        scr[_tt_piece(s, n)] = val[:, s * LANES:(s + 1) * LANES]
    _tt_drain(ref, scr, n, tok0)


def _load_token_tiles(ref, scr, n, tok0=0):
    _tt_fill(scr, ref, n, tok0)
    return jnp.concatenate([scr[_tt_piece(s, n)] for s in range(SUB)], axis=1)


def _rope_table_kernel(pos_ref, invf_ref, c_ref, s1_ref, s2_ref):
    ang = pos_ref[...] * invf_ref[...]
    d = lax.broadcasted_iota(jnp.int32, ang.shape, 1) % DIFF_QK_DIM
    cos = jnp.cos(ang)
    sin = jnp.sin(ang)
    half = ROT_DIM // 2
    c_ref[...] = jnp.where(d < ROT_DIM, cos, 1.0)
    s1_ref[...] = jnp.where(d < half, -sin, 0.0)
    s2_ref[...] = jnp.where((d >= half) & (d < ROT_DIM), sin, 0.0)


def _rope_tables(pos_b, invf):
    t = pos_b.shape[0]
    tb = 1024
    spec = pl.BlockSpec((tb, LANES), lambda i: (i, 0))
    return pl.pallas_call(
        _rope_table_kernel,
        grid=(t // tb,),
        in_specs=[spec, pl.BlockSpec((1, LANES), lambda i: (0, 0))],
        out_specs=[spec, spec, spec],
        out_shape=[jax.ShapeDtypeStruct((t, LANES), F32)] * 3,
        compiler_params=_params(1),
        name="rope_tables",
    )(pos_b, invf)


def _in_proj_kernel(x_ref, g_ref, wt_ref, wat_ref, proj_ref, ga_ref, h_scr):
    @pl.when(pl.program_id(1) == 0)
    def _():
        def body(c, carry):
            rows = pl.ds(c * 256, 256)
            x = x_ref[rows, :]
            ms = jnp.mean(x * x, axis=-1, keepdims=True)
            h_scr[rows, :] = (x * lax.rsqrt(ms + RMS_EPS) * g_ref[...]).astype(BF16)
            return carry
        lax.fori_loop(0, TM_IN // 256, body, 0)
        ga_ref[...] = _nt_dot(h_scr[...], wat_ref[...].astype(BF16))

    proj_ref[...] = _nt_dot(h_scr[...], wt_ref[...].astype(BF16)).astype(BF16)


def _in_proj(x2, g1, w_in, wa):
    t = x2.shape[0]
    return pl.pallas_call(
        _in_proj_kernel,
        grid=(t // TM_IN, D_MAIN // TN_IN),
        in_specs=[
            pl.BlockSpec((TM_IN, D_MODEL), lambda i, j: (i, 0)),
            pl.BlockSpec((1, D_MODEL), lambda i, j: (0, 0)),
            pl.BlockSpec((TN_IN, D_MODEL), lambda i, j: (j, 0)),
            pl.BlockSpec((LANES, D_MODEL), lambda i, j: (0, 0)),
        ],
        out_specs=[
            pl.BlockSpec((TM_IN, TN_IN), lambda i, j: (i, j)),
            pl.BlockSpec((TM_IN, LANES), lambda i, j: (i, 0)),
        ],
        out_shape=[jax.ShapeDtypeStruct((t, D_MAIN), BF16),
                   jax.ShapeDtypeStruct((t, LANES), F32)],
        scratch_shapes=[pltpu.VMEM((TM_IN, D_MODEL), BF16)],
        compiler_params=_params(2),
        name="in_proj",
    )(x2, g1, w_in, wa)


def _norm_rope(x, g, c, s1, s2):
    lo = lax.broadcasted_iota(jnp.int32, x.shape, 1) < DIFF_QK_DIM
    x2 = x * x
    s_lo = jnp.sum(jnp.where(lo, x2, 0.0), axis=-1, keepdims=True)
    s_hi = jnp.sum(jnp.where(lo, 0.0, x2), axis=-1, keepdims=True)
    ms = jnp.where(lo, s_lo, s_hi) * (1.0 / DIFF_QK_DIM)
    y = x * lax.rsqrt(ms + RMS_EPS) * g
    half = ROT_DIM // 2
    return y * c + pltpu.roll(y, LANES - half, 1) * s1 + pltpu.roll(y, half, 1) * s2


def _attn_kernel(q_ref, k_ref, v_ref, c_ref, s1_ref, s2_ref, qg_ref, kg_ref,
                 lq1_ref, lk1_ref, lq2_ref, lk2_ref, og_ref, o_ref,
                 q1_scr, q2_scr, k_scr, v_scr, *bufs):
    s_len = q_ref.shape[0]
    s_bufs, e_bufs = bufs[0:4], bufs[4:8]
    lam = (jnp.exp(jnp.sum(lq1_ref[...] * lk1_ref[...], axis=-1, keepdims=True))
           - jnp.exp(jnp.sum(lq2_ref[...] * lk2_ref[...], axis=-1, keepdims=True))
           + LAMBDA_INIT)
    lo = lax.broadcasted_iota(jnp.int32, (TQ, LANES), 1) < DIFF_QK_DIM
    v_scr[:, 0:LANES] = v_ref[...]
    v_scr[:, LANES:] = jnp.ones((s_len, LANES), BF16)

    def prepare(i):
        rows = slice(i * TQ, (i + 1) * TQ)
        c, s1, s2 = c_ref[rows, :], s1_ref[rows, :], s2_ref[rows, :]
        qn = (_norm_rope(q_ref[rows, :].astype(F32), qg_ref[...], c, s1, s2)
              * (DIFF_QK_DIM ** -0.5 * math.log2(math.e)))
        q1_scr[rows, :] = jnp.where(lo, qn, 0.0).astype(BF16)
        q2_scr[rows, :] = jnp.where(lo, 0.0, qn).astype(BF16)
        k_scr[rows, :] = _norm_rope(k_ref[rows, :].astype(F32), kg_ref[...], c, s1,
                                    s2).astype(BF16)

    diag = (lax.broadcasted_iota(jnp.int32, (TQ, TQ), 1) // CHUNK
            <= lax.broadcasted_iota(jnp.int32, (TQ, TQ), 0) // CHUNK)

    n_tiles = s_len // TQ
    items = [(i, q_scr) for i in range(n_tiles) for q_scr in (q1_scr, q2_scr)]

    def scores(r):
        i, q_scr = items[r]
        s_scr = s_bufs[r % 4]
        q = q_scr[i * TQ:(i + 1) * TQ, :]
        n_off = i * TQ
        s_scr[:, n_off:n_off + TQ] = jnp.where(
            diag, _nt_dot(q, k_scr[n_off:n_off + TQ, :]), -jnp.inf)
        if n_off:
            s_scr[:, 0:n_off] = _nt_dot(q, k_scr[0:n_off, :])

    def softmax_pv(r):
        nk = (items[r][0] + 1) * TQ
        s_scr, e_scr = s_bufs[r % 4], e_bufs[r % 4]
        m = jnp.max(s_scr[:, 0:nk], axis=-1, keepdims=True)
        e_scr[:, 0:nk] = jnp.exp2(s_scr[:, 0:nk] - m).astype(BF16)
        acc = jnp.dot(e_scr[:, 0:nk], v_scr[0:nk, :], preferred_element_type=F32)
        return acc[:, 0:LANES] / acc[:, LANES:]

    prepare(0)
    scores(0)
    scores(1)
    for i in range(n_tiles):
        rows = slice(i * TQ, (i + 1) * TQ)
        if i + 1 < n_tiles:
            prepare(i + 1)
            scores(2 * i + 2)
        o1 = softmax_pv(2 * i)
        if i + 1 < n_tiles:
            scores(2 * i + 3)
        o = o1 - lam * softmax_pv(2 * i + 1)
        ms = jnp.mean(o * o, axis=-1, keepdims=True)
        y = o * lax.rsqrt(ms + RMS_EPS) * og_ref[...] * (1.0 - LAMBDA_INIT)
        o_ref[rows, :] = y.astype(BF16)


def _diff_attention(proj, tabs, qg, kg, lq1, lk1, lq2, lk2, og, batch, seq):
    c, s1, s2 = tabs
    h = DIFF_HEADS
    blk = lambda off: pl.BlockSpec((seq, LANES), lambda b, hh, off=off: (b, off + hh))
    tab = pl.BlockSpec((seq, LANES), lambda b, hh: (b, 0))
    vec = lambda n: pl.BlockSpec((1, n), lambda b, hh: (0, 0))
    return pl.pallas_call(
        _attn_kernel,
        grid=(batch, h),
        in_specs=[blk(0), blk(h), blk(2 * h), tab, tab, tab,
                  vec(LANES), vec(LANES), vec(DIFF_QK_DIM), vec(DIFF_QK_DIM),
                  vec(DIFF_QK_DIM), vec(DIFF_QK_DIM), vec(LANES)],
        out_specs=pl.BlockSpec((seq, LANES), lambda b, hh: (b, hh)),
        out_shape=jax.ShapeDtypeStruct((batch * seq, h * DIFF_V_DIM), BF16),
        scratch_shapes=[pltpu.VMEM((seq, LANES), BF16)] * 3
        + [pltpu.VMEM((seq, 2 * LANES), BF16)]
        + [pltpu.VMEM((TQ, seq), F32)] * 4 + [pltpu.VMEM((TQ, seq), BF16)] * 4,
        compiler_params=_params(2),
        name="diff_attention",
    )(proj, proj, proj, c, s1, s2, qg, kg, lq1, lk1, lq2, lk2, og)


def _split_dot(ones_bf, x):
    hi = x.astype(BF16)
    lo = (x - hi.astype(F32)).astype(BF16)
    return (jnp.dot(ones_bf, hi, preferred_element_type=F32)
            + jnp.dot(ones_bf, lo, preferred_element_type=F32))


def _gla_kernel(q_ref, k_ref, v_ref, r_ref, ga_ref, w2_ref, b2_ref, og_ref, o_ref,
                qin_scr, ut_scr, dec_scr, st_scr, acc_scr):
    s_len = q_ref.shape[0]
    cpg = GB // CHUNK
    ri = lax.broadcasted_iota(jnp.int32, (GB, GB), 0)
    ci = lax.broadcasted_iota(jnp.int32, (GB, GB), 1)
    same = (ri // CHUNK) == (ci // CHUNK)
    blk_ones = jnp.where(same, 1.0, 0.0).astype(BF16)
    tril = same & (ci <= ri)
    tri_ones = jnp.where(tril, 1.0, 0.0).astype(BF16)

    def phase_a(g, carry):
        rows = pl.ds(pl.multiple_of(g * GB, GB), GB)
        pre = jnp.dot(ga_ref[rows, :].astype(BF16), w2_ref[...],
                      preferred_element_type=F32) + b2_ref[...]
        la = -(jnp.maximum(-pre, 0.0) + jnp.log1p(jnp.exp(-jnp.abs(pre)))) * (1.0 / GLA_TAU)
        bc = _split_dot(tri_ones, la)
        bl = _split_dot(blk_ones, la)
        e_neg = jnp.exp(-bc)
        e_last = jnp.exp(bl)
        k = k_ref[rows, :].astype(F32)
        q_in = (q_ref[rows, :].astype(F32) * (GLA_K_DIM ** -0.5) * jnp.exp(bc)).astype(BF16)
        k_in = (k * e_neg).astype(BF16)
        k_dec = (k * (e_last * e_neg)).astype(BF16)
        qin_scr[rows, :] = q_in
        dec_scr[rows, :] = e_last
        v = v_ref[rows, :]
        att = jnp.where(tril, _nt_dot(q_in, k_in), 0.0).astype(BF16)
        acc_scr[rows, :] = jnp.dot(att, v, preferred_element_type=F32)
        for c in range(cpg):
            cr = slice(c * CHUNK, (c + 1) * CHUNK)
            ut_scr[g * cpg + c] = _tn_dot(v[cr, :], k_dec[cr, :])
        return carry

    lax.fori_loop(0, s_len // GB, phase_a, 0, unroll=8)

    def phase_b(c, st):
        st_scr[c] = st.astype(BF16)
        dec = dec_scr[pl.ds(pl.multiple_of(c * CHUNK, CHUNK), 1), :]
        return dec * st + ut_scr[c]

    lax.fori_loop(0, s_len // CHUNK, phase_b, jnp.zeros((GLA_V_DIM, GLA_K_DIM), F32))

    def phase_c(g, carry):
        rows = pl.ds(pl.multiple_of(g * GB, GB), GB)
        inter = [_nt_dot(qin_scr[pl.ds(pl.multiple_of(g * GB + c * CHUNK, CHUNK), CHUNK), :],
                         st_scr[g * cpg + c]) for c in range(cpg)]
        o = acc_scr[rows, :] + jnp.concatenate(inter, axis=0)
        ms = jnp.mean(o * o, axis=-1, keepdims=True)
        y = o * lax.rsqrt(ms + RMS_EPS) * og_ref[...]
        o_ref[rows, :] = (y * _silu(r_ref[rows, :].astype(F32))).astype(BF16)
        return carry

    lax.fori_loop(0, s_len // GB, phase_c, 0, unroll=8)


def _gla(proj, ga, w2p, b2, og, batch, seq):
    hq = 3 * DIFF_HEADS
    kblk = lambda off: pl.BlockSpec((seq, GLA_K_DIM), lambda b, hh, off=off: (b, off + hh))
    vblk = lambda off: pl.BlockSpec((seq, GLA_V_DIM), lambda b, hh, off=off: (b, off + hh))
    return pl.pallas_call(
        _gla_kernel,
        grid=(batch, GLA_HEADS),
        in_specs=[kblk(hq), kblk(hq + GLA_HEADS), vblk(16), vblk(16 + GLA_HEADS),
                  pl.BlockSpec((seq, LANES), lambda b, hh: (b, 0)),
                  pl.BlockSpec((LANES, GLA_K_DIM), lambda b, hh: (0, hh)),
                  pl.BlockSpec((1, GLA_K_DIM), lambda b, hh: (0, hh)),
                  pl.BlockSpec((1, GLA_V_DIM), lambda b, hh: (0, 0))],
        out_specs=pl.BlockSpec((seq, GLA_V_DIM), lambda b, hh: (b, hh)),
        out_shape=jax.ShapeDtypeStruct((batch * seq, GLA_HEADS * GLA_V_DIM), BF16),
        scratch_shapes=[pltpu.VMEM((seq, GLA_K_DIM), BF16),
                        pltpu.VMEM((seq // CHUNK, GLA_V_DIM, GLA_K_DIM), F32),
                        pltpu.VMEM((seq, GLA_K_DIM), F32),
                        pltpu.VMEM((seq // CHUNK, GLA_V_DIM, GLA_K_DIM), BF16),
                        pltpu.VMEM((seq, GLA_V_DIM), F32)],
        compiler_params=_params(2),
        name="gla",
    )(proj, proj, proj, proj, ga, w2p, b2, og)


def _out_proj_kernel(a_ref, g_ref, x_ref, wo_ref, g2_ref, wr_ref, br_ref,
                     x1_ref, h2_ref, lg_ref, tt_scr):
    half = a_ref.shape[1]

    def body(c, carry):
        r0 = pl.multiple_of(c * RB_OUT, RB_OUT)
        rows = pl.ds(r0, RB_OUT)
        mixed = (jnp.dot(a_ref[rows, :], wo_ref[0:half, :], preferred_element_type=F32)
                 + jnp.dot(g_ref[rows, :], wo_ref[half:, :], preferred_element_type=F32))
        x1 = x_ref[rows, :] + mixed
        x1_ref[rows, :] = x1
        ms = jnp.mean(x1 * x1, axis=-1, keepdims=True)
        h2 = x1 * lax.rsqrt(ms + RMS_EPS) * g2_ref[...]
        _store_token_tiles(h2_ref, h2, tt_scr, r0)
        lg = jnp.dot(h2.astype(BF16), wr_ref[...], preferred_element_type=F32) + br_ref[...]
        lg_ref[:, rows] = lg.T[0:N_LOGIT_ROWS, :]
        return carry

    lax.fori_loop(0, TM_OUT // RB_OUT, body, 0)


def _out_proj(a_out, g_out, x2, wo, g2, wr, br):
    t = x2.shape[0]
    half = a_out.shape[1]
    row = lambda n: pl.BlockSpec((TM_OUT, n), lambda i: (i, 0))
    full = lambda r, n: pl.BlockSpec((r, n), lambda i: (0, 0))
    return pl.pallas_call(
        _out_proj_kernel,
        grid=(t // TM_OUT,),
        in_specs=[row(half), row(half), row(D_MODEL), full(D_MODEL, D_MODEL),
                  full(1, D_MODEL), full(D_MODEL, LANES), full(1, LANES)],
        out_specs=[row(D_MODEL), pl.BlockSpec((TM_OUT * SUB, LANES), lambda i: (i, 0)),
                   pl.BlockSpec((N_LOGIT_ROWS, TM_OUT), lambda i: (0, i))],
        out_shape=[jax.ShapeDtypeStruct((t, D_MODEL), F32),
                   jax.ShapeDtypeStruct((t * SUB, LANES), BF16),
                   jax.ShapeDtypeStruct((N_LOGIT_ROWS, t), F32)],
        scratch_shapes=[pltpu.VMEM((RB_OUT * TT_PITCH, LANES), F32)],
        compiler_params=_params(1),
        name="out_proj_router",
    )(a_out, g_out, x2, wo, g2, wr, br)


def _first_argmax(vals, row_f):
    m = jnp.max(vals, axis=0, keepdims=True)
    idx = jnp.min(jnp.where(vals == m, row_f, float(vals.shape[0])), axis=0, keepdims=True)
    return m, idx


def _softmax_rows(v):
    ex = jnp.exp(v - jnp.max(v, axis=0, keepdims=True))
    return ex / jnp.sum(ex, axis=0, keepdims=True)


def _routing_kernel(lg_ref, slots_ref, cw_ref, cnt_ref, tok_scr):
    t = lg_ref.shape[1]
    epg = EXPERTS_PER_GROUP
    row8 = lax.broadcasted_iota(jnp.int32, (8, TB), 0).astype(F32)
    row_e = lax.broadcasted_iota(jnp.int32, (N_EXPERTS, TB), 0).astype(F32)
    ui = lax.broadcasted_iota(jnp.int32, (TB, TB), 0)
    uj = lax.broadcasted_iota(jnp.int32, (TB, TB), 1)
    earlier = jnp.where(ui < uj, 1.0, 0.0).astype(BF16)
    all_ones = jnp.ones((TB, TB), BF16)

    def rows01(r0, r1):
        return jnp.where(row8 == 0.0, r0, jnp.where(row8 == 1.0, r1, jnp.zeros_like(r0)))

    def pass1(b, counts):
        cols = pl.ds(pl.multiple_of(b * TB, TB), TB)
        lg = lg_ref[:, cols]
        lg_groups = jnp.where(row8 < float(N_GROUPS), lg[0:GROUP_ROWS, :], -jnp.inf)
        pg_sel, g_sel = _first_argmax(_softmax_rows(lg_groups), row8)
        le = lg[GROUP_ROWS:GROUP_ROWS + epg, :]
        for g in range(1, N_GROUPS):
            le = jnp.where(g_sel == float(g),
                           lg[GROUP_ROWS + g * epg:GROUP_ROWS + (g + 1) * epg, :], le)
        pe = _softmax_rows(le)
        v1, i1 = _first_argmax(pe, row8)
        v2, i2 = _first_argmax(jnp.where(row8 == i1, -jnp.inf, pe), row8)
        tot = v1 + v2
        e1 = g_sel * epg + i1
        e2 = g_sel * epg + i2
        oh1 = row_e == e1
        oh2 = row_e == e2
        a = jnp.where(oh1 | oh2, 1.0, 0.0).astype(BF16)
        rank = jnp.dot(a, earlier, preferred_element_type=F32) + counts
        r1 = jnp.sum(jnp.where(oh1, rank, 0.0), axis=0, keepdims=True)
        r2 = jnp.sum(jnp.where(oh2, rank, 0.0), axis=0, keepdims=True)
        tok_scr[:, cols] = jnp.where(row8 == 0.0, e1, jnp.where(
            row8 == 1.0, e2, jnp.where(row8 == 2.0, r1, jnp.where(row8 == 3.0, r2, 0.0))))
        cw_ref[:, cols] = rows01((v1 / tot) * pg_sel, (v2 / tot) * pg_sel)
        return counts + jnp.dot(a, all_ones, preferred_element_type=F32)

    counts = lax.fori_loop(0, t // TB, pass1, jnp.zeros((N_EXPERTS, TB), F32))
    cnt_ref[...] = counts[:, 0:LANES]
    n_tiles = jnp.floor((counts + (TM_E - 1)) * (1.0 / TM_E))
    li = lax.broadcasted_iota(jnp.int32, (N_EXPERTS, N_EXPERTS), 0)
    lj = lax.broadcasted_iota(jnp.int32, (N_EXPERTS, N_EXPERTS), 1)
    lower = jnp.where(lj < li, 1.0, 0.0).astype(BF16)
    row_off = jnp.dot(lower, n_tiles.astype(BF16), preferred_element_type=F32) * TM_E

    def pass2(b, carry):
        cols = pl.ds(pl.multiple_of(b * TB, TB), TB)
        tok = tok_scr[:, cols]
        off1 = jnp.sum(jnp.where(row_e == tok[0:1, :], row_off, 0.0), axis=0, keepdims=True)
        off2 = jnp.sum(jnp.where(row_e == tok[1:2, :], row_off, 0.0), axis=0, keepdims=True)
        slots_ref[:, cols] = rows01(off1 + tok[2:3, :], off2 + tok[3:4, :]).astype(jnp.int32)
        return carry

    lax.fori_loop(0, t // TB, pass2, 0)


def _routing(logits_t):
    t = logits_t.shape[1]
    return pl.pallas_call(
        _routing_kernel,
        out_shape=[jax.ShapeDtypeStruct((8, t), jnp.int32),
                   jax.ShapeDtypeStruct((8, t), F32),
                   jax.ShapeDtypeStruct((N_EXPERTS, LANES), F32)],
        scratch_shapes=[pltpu.VMEM((8, t), F32)],
        compiler_params=pltpu.CompilerParams(vmem_limit_bytes=VMEM_LIMIT),
        name="routing",
    )(logits_t)


def _rows_copy(src, dst, sem, src_tok, dst_tok, n):
    first = lambda tok: tok * SUB if isinstance(tok, int) else pl.multiple_of(tok * SUB, SUB)
    s0 = first(src_tok)
    d0 = first(dst_tok)
    return pltpu.make_async_copy(src.at[pl.ds(s0, n * SUB), :], dst.at[pl.ds(d0, n * SUB), :], sem)


def _row_token_kernel(sl1_ref, sl2_ref, o_ref):
    def clear(r, carry):
        o_ref[r] = 0
        return carry

    lax.fori_loop(0, o_ref.shape[0], clear, 0, unroll=8)

    def mark(t, carry):
        o_ref[sl1_ref[t]] = t
        o_ref[sl2_ref[t]] = t
        return carry

    lax.fori_loop(0, sl1_ref.shape[0], mark, 0, unroll=8)


def _row_tokens(sl1, sl2, n_rows):
    return pl.pallas_call(
        _row_token_kernel,
        grid_spec=pltpu.PrefetchScalarGridSpec(
            num_scalar_prefetch=2,
            grid=(1,),
            in_specs=[],
            out_specs=pl.BlockSpec(memory_space=pltpu.SMEM),
        ),
        out_shape=jax.ShapeDtypeStruct((n_rows,), jnp.int32),
        compiler_params=_params(1),
        name="row_tokens",
    )(sl1, sl2)


def _expert_kernel(tidx_ref, texp_ref, tfirst_ref, tslot_ref, tnext_ref, nt_ref, rtok_ref,
                   h2_hbm, wg_hbm, wu_hbm, wd_hbm, ys_ref,
                   wg_buf, wu_buf, wd_buf, wg_bf, wu_bf, wd_bf, tt_scr, xg_even, xg_odd,
                   sems, gsems):
    j = pl.program_id(0)
    xg = (xg_even, xg_odd)

    def weight_copies(e, slot):
        return [pltpu.make_async_copy(hbm.at[e], buf.at[slot], sems.at[slot, k])
                for k, (hbm, buf) in enumerate(((wg_hbm, wg_buf), (wu_hbm, wu_buf),
                                                (wd_hbm, wd_buf)))]

    def gather(tile, r, par):
        return _rows_copy(h2_hbm, xg[par], gsems.at[par], rtok_ref[tile * TM_E + r], r, 1)

    @pl.when(j == 0)
    def _():
        def first(r, carry):
            gather(0, r, 0).start()
            return carry
        lax.fori_loop(0, TM_E, first, 0, unroll=8)

    @pl.when(j < nt_ref[0])
    def _():
        slot = tslot_ref[j]

        @pl.when(tfirst_ref[j] == 1)
        def _():
            @pl.when(j == 0)
            def _():
                for c in weight_copies(texp_ref[j], slot):
                    c.start()
            for c in weight_copies(texp_ref[j], slot):
                c.wait()

            @pl.when(tnext_ref[j] >= 0)
            def _():
                for c in weight_copies(tnext_ref[j], 1 - slot):
                    c.start(priority=1)
            wg_bf[...] = wg_buf[slot].astype(BF16)
            wu_bf[...] = wu_buf[slot].astype(BF16)
            wd_bf[...] = wd_buf[slot].astype(BF16)

        def tile_body(par):
            pltpu.make_async_copy(xg[par], xg[par], gsems.at[par]).wait()
            nxt = jnp.minimum(j + 1, nt_ref[0] - 1)
            for r in range(TM_E):
                gather(nxt, r, 1 - par).start()
            xb = _load_token_tiles(xg[par], tt_scr, TM_E).astype(BF16)
            g = jnp.dot(xb, wg_bf[...], preferred_element_type=F32)
            u = jnp.dot(xb, wu_bf[...], preferred_element_type=F32)
            h = (_silu(g) * u).astype(BF16)
            _store_token_tiles(ys_ref, jnp.dot(h, wd_bf[...], preferred_element_type=F32),
                               tt_scr)

            @pl.when(j == nt_ref[0] - 1)
            def _():
                pltpu.make_async_copy(xg[1 - par], xg[1 - par], gsems.at[1 - par]).wait()

        for par in (0, 1):
            @pl.when(j % 2 == par)
            def _():
                tile_body(par)


def _expert_mlp(h2, row_tok, tile_meta, wg, wu, wd):
    n_rows = row_tok.shape[0]
    tile = pl.BlockSpec((TM_E * SUB, LANES), lambda j, ti, *_: (ti[j], 0))
    hbm = pl.BlockSpec(memory_space=pl.ANY)
    return pl.pallas_call(
        _expert_kernel,
        grid_spec=pltpu.PrefetchScalarGridSpec(
            num_scalar_prefetch=len(tile_meta) + 1,
            grid=(n_rows // TM_E,),
            in_specs=[hbm, hbm, hbm, hbm],
            out_specs=tile,
            scratch_shapes=[pltpu.VMEM((2, D_MODEL, D_FF), F32),
                            pltpu.VMEM((2, D_MODEL, D_FF), F32),
                            pltpu.VMEM((2, D_FF, D_MODEL), F32),
                            pltpu.VMEM((D_MODEL, D_FF), BF16),
                            pltpu.VMEM((D_MODEL, D_FF), BF16),
                            pltpu.VMEM((D_FF, D_MODEL), BF16),
                            pltpu.VMEM((TM_E * TT_PITCH, LANES), F32),
                            pltpu.VMEM((TM_E * SUB, LANES), BF16),
                            pltpu.VMEM((TM_E * SUB, LANES), BF16),
                            pltpu.SemaphoreType.DMA((2, 3)),
                            pltpu.SemaphoreType.DMA((2,))],
        ),
        out_shape=jax.ShapeDtypeStruct((n_rows * SUB, LANES), BF16),
        compiler_params=_params(1),
        name="expert_mlp",
    )(*tile_meta, row_tok, h2, wg, wu, wd)


def _combine_kernel(sl1_ref, sl2_ref, x1_ref, cw_ref, ys_hbm, o_ref, y1_scr, y2_scr, tt1_scr,
                    tt2_scr, sems):
    i = pl.program_id(0)

    def issue(step, slot):
        base = step * TB

        def body(r, carry):
            _rows_copy(ys_hbm, y1_scr.at[slot], sems.at[slot], sl1_ref[base + r], r,
                       1).start(priority=0)
            _rows_copy(ys_hbm, y2_scr.at[slot], sems.at[slot], sl2_ref[base + r], r,
                       1).start(priority=1)
            return carry

        lax.fori_loop(0, TB, body, 0, unroll=8)

    @pl.when(i == 0)
    def _():
        issue(0, 0)

    last = pl.num_programs(0) - 1
    rb = 64
    n_pieces = (TB // rb) * SUB
    per_piece = TB // n_pieces

    def step(slot):
        _rows_copy(ys_hbm, y1_scr.at[slot], sems.at[slot], 0, 0, TB).wait()
        _rows_copy(ys_hbm, y2_scr.at[slot], sems.at[slot], 0, 0, TB).wait()
        nbase = jnp.minimum(i + 1, last) * TB
        cw = jnp.concatenate([cw_ref[...], jnp.zeros((LANES - 8, TB), F32)], axis=0).T
        _tt_fill(tt1_scr, y1_scr.at[slot], TB)
        _tt_fill(tt2_scr, y2_scr.at[slot], TB)
        piece = 0
        for r0 in range(0, TB, rb):
            rows = slice(r0, r0 + rb)
            c1 = jnp.broadcast_to(cw[rows, 0:1], (rb, LANES))
            c2 = jnp.broadcast_to(cw[rows, 1:2], (rb, LANES))
            for s in range(SUB):
                for r in range(piece * per_piece, (piece + 1) * per_piece):
                    _rows_copy(ys_hbm, y1_scr.at[1 - slot], sems.at[1 - slot],
                               sl1_ref[nbase + r], r, 1).start(priority=0)
                    _rows_copy(ys_hbm, y2_scr.at[1 - slot], sems.at[1 - slot],
                               sl2_ref[nbase + r], r, 1).start(priority=1)
                piece += 1
                cols = slice(s * LANES, (s + 1) * LANES)
                tiles = _tt_piece(s, rb, r0)
                o_ref[rows, cols] = (x1_ref[rows, cols] + c1 * tt1_scr[tiles]
                                     + c2 * tt2_scr[tiles])

        @pl.when(i == last)
        def _():
            _rows_copy(ys_hbm, y1_scr.at[1 - slot], sems.at[1 - slot], 0, 0, TB).wait()
            _rows_copy(ys_hbm, y2_scr.at[1 - slot], sems.at[1 - slot], 0, 0, TB).wait()

    for parity in (0, 1):
        @pl.when(i % 2 == parity)
        def _():
            step(parity)


def _combine(x1, cw, ys, sl1, sl2):
    t = x1.shape[0]
    return pl.pallas_call(
        _combine_kernel,
        grid_spec=pltpu.PrefetchScalarGridSpec(
            num_scalar_prefetch=2,
            grid=(t // TB,),
            in_specs=[pl.BlockSpec((TB, D_MODEL), lambda i, *_: (i, 0)),
                      pl.BlockSpec((8, TB), lambda i, *_: (0, i)),
                      pl.BlockSpec(memory_space=pl.ANY)],
            out_specs=pl.BlockSpec((TB, D_MODEL), lambda i, *_: (i, 0)),
            scratch_shapes=[pltpu.VMEM((2, TB * SUB, LANES), BF16),
                            pltpu.VMEM((2, TB * SUB, LANES), BF16),
                            pltpu.VMEM((TB * TT_PITCH, LANES), F32),
                            pltpu.VMEM((TB * TT_PITCH, LANES), F32),
                            pltpu.SemaphoreType.DMA((2,))],
        ),
        out_shape=jax.ShapeDtypeStruct((t, D_MODEL), F32),
        compiler_params=_params(1),
        name="combine",
    )(sl1, sl2, x1, cw, ys)


def _lane_tile(v, reps):
    return jnp.tile(v.reshape(1, -1), (1, reps))


def kernel(x, positions, norm1_g, w_in, q_norm_g, k_norm_g, lambda_q1, lambda_k1, lambda_q2,
           lambda_k2, diff_out_norm_g, gla_w_gate2, gla_b_gate, gla_out_norm_g, w_out, norm2_g,
           w_router_group, b_router_group, w_router_expert, b_router_expert, w_gate_expert,
           w_up_expert, w_down_expert):
    batch, seq, d = x.shape
    t = batch * seq
    x2 = x.reshape(t, d)

    inv = ROPE_THETA ** (-jnp.arange(0, ROT_DIM, 2, dtype=F32) / ROT_DIM)
    lane_d = jnp.arange(LANES) % DIFF_QK_DIM
    invf = jnp.where(lane_d < ROT_DIM, inv[lane_d % (ROT_DIM // 2)], 0.0).reshape(1, LANES)
    pos_b = jnp.broadcast_to(positions.astype(F32).reshape(t, 1), (t, LANES))
    tabs = _rope_tables(pos_b, invf)

    w_in_t = jnp.swapaxes(w_in, 1, 2)[0]
    wa_t = jnp.pad(w_in_t[D_MAIN:, :], ((0, LANES - GLA_GATE_RANK), (0, 0)))
    proj, ga = _in_proj(x2, norm1_g, w_in_t, wa_t)

    a_out = _diff_attention(
        proj, tabs, _lane_tile(q_norm_g[0], 2), _lane_tile(k_norm_g[0], 2),
        lambda_q1, lambda_k1, lambda_q2, lambda_k2, diff_out_norm_g, batch, seq)
    w2p = jnp.pad(gla_w_gate2[0], ((0, LANES - GLA_GATE_RANK), (0, 0))).astype(BF16)
    g_out = _gla(proj, ga, w2p, gla_b_gate, gla_out_norm_g, batch, seq)

    gpad = GROUP_ROWS - N_GROUPS
    lpad = LANES - N_LOGIT_ROWS
    wr = jnp.concatenate(
        [jnp.pad(w_router_group[0], ((0, 0), (0, gpad))),
         jnp.pad(w_router_expert[0].transpose(1, 0, 2).reshape(d, N_EXPERTS),
                 ((0, 0), (0, lpad)))], axis=1)
    br = jnp.concatenate([jnp.pad(b_router_group[0], (0, gpad)),
                          jnp.pad(b_router_expert[0].reshape(-1), (0, lpad))]).reshape(1, LANES)
    x1, h2, logits_t = _out_proj(a_out, g_out, x2, w_out[0].astype(BF16), norm2_g,
                                 wr.astype(BF16), br)

    slots, cw, counts = _routing(logits_t)
    sl1 = slots[0]
    sl2 = slots[1]
    n_rows = 2 * t + N_EXPERTS * TM_E
    max_tiles = n_rows // TM_E
    cnt = counts[:, 0].astype(jnp.int32)
    tile_end = jnp.cumsum((cnt + (TM_E - 1)) // TM_E)
    n_tiles = tile_end[-1:]
    tile_idx = jnp.minimum(jnp.arange(max_tiles, dtype=jnp.int32), n_tiles[0] - 1)
    tile_exp = jnp.sum(tile_idx[:, None] >= tile_end[None, :], axis=1).astype(jnp.int32)
    tile_first = jnp.concatenate(
        [jnp.ones((1,), jnp.int32), (tile_exp[1:] != tile_exp[:-1]).astype(jnp.int32)])
    tile_slot = (jnp.cumsum(tile_first) - 1) % 2
    nxt = tile_end[tile_exp]
    tile_next = jnp.where(nxt < n_tiles[0], tile_exp[jnp.minimum(nxt, max_tiles - 1)], -1)
    tile_meta = [a.astype(jnp.int32)
                 for a in (tile_idx, tile_exp, tile_first, tile_slot, tile_next, n_tiles)]

    row_tok = _row_tokens(sl1, sl2, n_rows)
    ys = _expert_mlp(h2, row_tok, tile_meta,
                     w_gate_expert[0].reshape(N_EXPERTS, d, D_FF),
                     w_up_expert[0].reshape(N_EXPERTS, d, D_FF),
                     w_down_expert[0].reshape(N_EXPERTS, D_FF, d))
    out = _combine(x1, cw, ys, sl1, sl2)
    return out.reshape(batch, seq, d)
```

```python
import functools
import math

import jax
import jax.numpy as jnp
from jax import lax
from jax.experimental import pallas as pl
from jax.experimental.pallas import tpu as pltpu

D_MODEL = 2048
CHUNK = 64
DIFF_QK_DIM = 64
DIFF_V_DIM = 128
DIFF_HEADS = 8
ROT_DIM = 16
ROPE_THETA = 500000.0
GLA_HEADS = 4
GLA_V_DIM = 256
GLA_K_DIM = 128
GLA_GATE_RANK = 16
GLA_TAU = 16.0
N_GROUPS = 4
EXPERTS_PER_GROUP = 8
N_EXPERTS = N_GROUPS * EXPERTS_PER_GROUP
D_FF = 512
RMS_EPS = 1e-6
LAMBDA_INIT = 0.8 - 0.6 * math.exp(-0.3 * 0)
D_MAIN = 6144
GROUP_ROWS = 8
N_LOGIT_ROWS = GROUP_ROWS + N_EXPERTS

LANES = 128
SUB = D_MODEL // LANES
TT_PITCH = 24
VMEM_LIMIT = 56 * 1024 * 1024

TM_IN = 1024
TN_IN = 1024
TQ = 256
ATT_HPS = 2
GB = 256
TM_OUT = 512
RB_OUT = 256
TB = 256
TB_D = 1024
TM_E = 256

F32 = jnp.float32
BF16 = jnp.bfloat16
HI = lax.Precision.HIGHEST


def _params(n_axes):
    return pltpu.CompilerParams(dimension_semantics=("arbitrary",) * n_axes,
                                vmem_limit_bytes=VMEM_LIMIT)


def _nt_dot(a, b):
    return lax.dot_general(a, b, (((1,), (1,)), ((), ())), preferred_element_type=F32)


def _tn_dot(a, b):
    return lax.dot_general(a, b, (((0,), (0,)), ((), ())), preferred_element_type=F32)


def _silu(x):
    return x * (1.0 / (1.0 + jnp.exp(-x)))


def _tt_fill(scr, ref, n, tok0=0):
    for t in range(n):
        scr[t * TT_PITCH:t * TT_PITCH + SUB, :] = ref[pl.ds((tok0 + t) * SUB, SUB), :].astype(F32)


def _tt_drain(ref, scr, n, tok0=0):
    for t in range(n):
        ref[pl.ds((tok0 + t) * SUB, SUB), :] = scr[t * TT_PITCH:t * TT_PITCH + SUB, :].astype(BF16)


def _tt_piece(s, n, tok0=0):
    return (pl.ds(tok0 * TT_PITCH + s, n, stride=TT_PITCH), slice(None))


def _store_token_tiles(ref, val, scr, tok0=0):
    n = val.shape[0]
    for s in range(SUB):
        scr[_tt_piece(s, n)] = val[:, s * LANES:(s + 1) * LANES]
    _tt_drain(ref, scr, n, tok0)


def _load_token_tiles(ref, scr, n, tok0=0):
    _tt_fill(scr, ref, n, tok0)
    return jnp.concatenate([scr[_tt_piece(s, n)] for s in range(SUB)], axis=1)


def _rope_table_kernel(pos_ref, invf_ref, c_ref, s1_ref, s2_ref):
    ang = pos_ref[...] * invf_ref[...]
    d = lax.broadcasted_iota(jnp.int32, ang.shape, 1) % DIFF_QK_DIM
    cos = jnp.cos(ang)
    sin = jnp.sin(ang)
    half = ROT_DIM // 2
    c_ref[...] = jnp.where(d < ROT_DIM, cos, 1.0)
    s1_ref[...] = jnp.where(d < half, -sin, 0.0)
    s2_ref[...] = jnp.where((d >= half) & (d < ROT_DIM), sin, 0.0)


def _rope_tables(pos_b, invf):
    t = pos_b.shape[0]
    tb = 1024
    spec = pl.BlockSpec((tb, LANES), lambda i: (i, 0))
    return pl.pallas_call(
        _rope_table_kernel,
        grid=(t // tb,),
        in_specs=[spec, pl.BlockSpec((1, LANES), lambda i: (0, 0))],
        out_specs=[spec, spec, spec],
        out_shape=[jax.ShapeDtypeStruct((t, LANES), F32)] * 3,
        compiler_params=_params(1),
        name="rope_tables",
    )(pos_b, invf)


def _in_proj_kernel(x_ref, g_ref, wt_ref, wat_ref, proj_ref, ga_ref, h_scr):
    @pl.when(pl.program_id(1) == 0)
    def _():
        def body(c, carry):
            rows = pl.ds(c * 256, 256)
            x = x_ref[rows, :]
            ms = jnp.mean(x * x, axis=-1, keepdims=True)
            h_scr[rows, :] = (x * lax.rsqrt(ms + RMS_EPS) * g_ref[...]).astype(BF16)
            return carry
        lax.fori_loop(0, TM_IN // 256, body, 0)
        ga_ref[...] = _nt_dot(h_scr[...], wat_ref[...].astype(BF16))

    proj_ref[...] = _nt_dot(h_scr[...], wt_ref[...].astype(BF16)).astype(BF16)


def _in_proj(x2, g1, w_in, wa):
    t = x2.shape[0]
    return pl.pallas_call(
        _in_proj_kernel,
        grid=(t // TM_IN, D_MAIN // TN_IN),
        in_specs=[
            pl.BlockSpec((TM_IN, D_MODEL), lambda i, j: (i, 0)),
            pl.BlockSpec((1, D_MODEL), lambda i, j: (0, 0)),
            pl.BlockSpec((TN_IN, D_MODEL), lambda i, j: (j, 0)),
            pl.BlockSpec((LANES, D_MODEL), lambda i, j: (0, 0)),
        ],
        out_specs=[
            pl.BlockSpec((TM_IN, TN_IN), lambda i, j: (i, j)),
            pl.BlockSpec((TM_IN, LANES), lambda i, j: (i, 0)),
        ],
        out_shape=[jax.ShapeDtypeStruct((t, D_MAIN), BF16),
                   jax.ShapeDtypeStruct((t, LANES), F32)],
        scratch_shapes=[pltpu.VMEM((TM_IN, D_MODEL), BF16)],
        compiler_params=_params(2),
        name="in_proj",
    )(x2, g1, w_in, wa)


def _norm_rope(x, g, c, s1, s2):
    lo = lax.broadcasted_iota(jnp.int32, x.shape, 1) < DIFF_QK_DIM
    x2 = x * x
    s_lo = jnp.sum(jnp.where(lo, x2, 0.0), axis=-1, keepdims=True)
    s_hi = jnp.sum(jnp.where(lo, 0.0, x2), axis=-1, keepdims=True)
    ms = jnp.where(lo, s_lo, s_hi) * (1.0 / DIFF_QK_DIM)
    y = x * lax.rsqrt(ms + RMS_EPS) * g
    half = ROT_DIM // 2
    return y * c + pltpu.roll(y, LANES - half, 1) * s1 + pltpu.roll(y, half, 1) * s2


def _attn_kernel(q_ref, k_ref, v_ref, c_ref, s1_ref, s2_ref, qg_ref, kg_ref,
                 lq1_ref, lk1_ref, lq2_ref, lk2_ref, og_ref, o_ref,
                 q1_scr, q2_scr, k_scr, v_scr, *bufs):
    s_len = q_ref.shape[0]
    s_bufs, e_bufs = bufs[0:4], bufs[4:8]
    lam = (jnp.exp(jnp.sum(lq1_ref[...] * lk1_ref[...], axis=-1, keepdims=True))
           - jnp.exp(jnp.sum(lq2_ref[...] * lk2_ref[...], axis=-1, keepdims=True))
           + LAMBDA_INIT)
    lo = lax.broadcasted_iota(jnp.int32, (TQ, LANES), 1) < DIFF_QK_DIM
    for hd in range(ATT_HPS):
        v_scr[hd, :, 0:LANES] = v_ref[:, hd * LANES:(hd + 1) * LANES]
        v_scr[hd, :, LANES:] = jnp.ones((s_len, LANES), BF16)

    n_tiles = s_len // TQ
    tiles = [(hd, i) for hd in range(ATT_HPS) for i in range(n_tiles)]

    def prepare(t):
        hd, i = tiles[t]
        rows = slice(i * TQ, (i + 1) * TQ)
        lanes = slice(hd * LANES, (hd + 1) * LANES)
        c, s1, s2 = c_ref[rows, :], s1_ref[rows, :], s2_ref[rows, :]
        qn = (_norm_rope(q_ref[rows, lanes].astype(F32), qg_ref[...], c, s1, s2)
              * (DIFF_QK_DIM ** -0.5 * math.log2(math.e)))
        q1_scr[hd, rows, :] = jnp.where(lo, qn, 0.0).astype(BF16)
        q2_scr[hd, rows, :] = jnp.where(lo, 0.0, qn).astype(BF16)
        k_scr[hd, rows, :] = _norm_rope(k_ref[rows, lanes].astype(F32), kg_ref[...], c, s1,
                                        s2).astype(BF16)

    diag = (lax.broadcasted_iota(jnp.int32, (TQ, TQ), 1) // CHUNK
            <= lax.broadcasted_iota(jnp.int32, (TQ, TQ), 0) // CHUNK)

    items = [(t, q_scr) for t in range(len(tiles)) for q_scr in (q1_scr, q2_scr)]

    def scores(r):
        t, q_scr = items[r]
        hd, i = tiles[t]
        s_scr = s_bufs[r % 4]
        q = q_scr[hd, i * TQ:(i + 1) * TQ, :]
        n_off = i * TQ
        s_scr[:, n_off:n_off + TQ] = jnp.where(
            diag, _nt_dot(q, k_scr[hd, n_off:n_off + TQ, :]), -jnp.inf)
        if n_off:
            s_scr[:, 0:n_off] = _nt_dot(q, k_scr[hd, 0:n_off, :])

    def softmax_pv(r):
        hd, i = tiles[items[r][0]]
        nk = (i + 1) * TQ
        s_scr, e_scr = s_bufs[r % 4], e_bufs[r % 4]
        m = jnp.max(s_scr[:, 0:nk], axis=-1, keepdims=True)
        e_scr[:, 0:nk] = jnp.exp2(s_scr[:, 0:nk] - m).astype(BF16)
        acc = jnp.dot(e_scr[:, 0:nk], v_scr[hd, 0:nk, :], preferred_element_type=F32)
        return acc[:, 0:LANES] / acc[:, LANES:]

    prepare(0)
    scores(0)
    scores(1)
    for t, (hd, i) in enumerate(tiles):
        if t + 1 < len(tiles):
            prepare(t + 1)
            scores(2 * t + 2)
        o1 = softmax_pv(2 * t)
        if t + 1 < len(tiles):
            scores(2 * t + 3)
        o = o1 - lam * softmax_pv(2 * t + 1)
        ms = jnp.mean(o * o, axis=-1, keepdims=True)
        y = o * lax.rsqrt(ms + RMS_EPS) * og_ref[...] * (1.0 - LAMBDA_INIT)
        o_ref[i * TQ:(i + 1) * TQ, hd * LANES:(hd + 1) * LANES] = y.astype(BF16)


def _diff_attention(proj, tabs, qg, kg, lq1, lk1, lq2, lk2, og, batch, seq):
    c, s1, s2 = tabs
    h = DIFF_HEADS
    hg = h // ATT_HPS
    wide = ATT_HPS * LANES
    blk = lambda off: pl.BlockSpec((seq, wide), lambda b, hh, off=off: (b, off + hh))
    tab = pl.BlockSpec((seq, LANES), lambda b, hh: (b, 0))
    vec = lambda n: pl.BlockSpec((1, n), lambda b, hh: (0, 0))
    return pl.pallas_call(
        _attn_kernel,
        grid=(batch, hg),
        in_specs=[blk(0), blk(hg), blk(2 * hg), tab, tab, tab,
                  vec(LANES), vec(LANES), vec(DIFF_QK_DIM), vec(DIFF_QK_DIM),
                  vec(DIFF_QK_DIM), vec(DIFF_QK_DIM), vec(LANES)],
        out_specs=pl.BlockSpec((seq, wide), lambda b, hh: (b, hh)),
        out_shape=jax.ShapeDtypeStruct((batch * seq, h * DIFF_V_DIM), BF16),
        scratch_shapes=[pltpu.VMEM((ATT_HPS, seq, LANES), BF16)] * 3
        + [pltpu.VMEM((ATT_HPS, seq, 2 * LANES), BF16)]
        + [pltpu.VMEM((TQ, seq), F32)] * 4 + [pltpu.VMEM((TQ, seq), BF16)] * 4,
        compiler_params=_params(2),
        name="diff_attention",
    )(proj, proj, proj, c, s1, s2, qg, kg, lq1, lk1, lq2, lk2, og)


def _split_dot(ones_bf, x):
    hi = x.astype(BF16)
    lo = (x - hi.astype(F32)).astype(BF16)
    return (jnp.dot(ones_bf, hi, preferred_element_type=F32)
            + jnp.dot(ones_bf, lo, preferred_element_type=F32))


def _gla_kernel(q_ref, k_ref, v_ref, r_ref, ga_ref, w2_ref, b2_ref, og_ref, o_ref,
                qin_scr, ut_scr, dec_scr, st_scr, acc_scr):
    s_len = q_ref.shape[0]
    cpg = GB // CHUNK
    ri = lax.broadcasted_iota(jnp.int32, (GB, GB), 0)
    ci = lax.broadcasted_iota(jnp.int32, (GB, GB), 1)
    same = (ri // CHUNK) == (ci // CHUNK)
    blk_ones = jnp.where(same, 1.0, 0.0).astype(BF16)
    tril = same & (ci <= ri)
    tri_ones = jnp.where(tril, 1.0, 0.0).astype(BF16)

    def phase_a(g, carry):
        rows = pl.ds(pl.multiple_of(g * GB, GB), GB)
        pre = jnp.dot(ga_ref[rows, :].astype(BF16), w2_ref[...],
                      preferred_element_type=F32) + b2_ref[...]
        la = -(jnp.maximum(-pre, 0.0) + jnp.log1p(jnp.exp(-jnp.abs(pre)))) * (1.0 / GLA_TAU)
        bc = _split_dot(tri_ones, la)
        bl = _split_dot(blk_ones, la)
        e_neg = jnp.exp(-bc)
        e_last = jnp.exp(bl)
        k = k_ref[rows, :].astype(F32)
        q_in = (q_ref[rows, :].astype(F32) * (GLA_K_DIM ** -0.5) * jnp.exp(bc)).astype(BF16)
        k_in = (k * e_neg).astype(BF16)
        k_dec = (k * (e_last * e_neg)).astype(BF16)
        qin_scr[rows, :] = q_in
        dec_scr[rows, :] = e_last
        v = v_ref[rows, :]
        att = jnp.where(tril, _nt_dot(q_in, k_in), 0.0).astype(BF16)
        acc_scr[rows, :] = jnp.dot(att, v, preferred_element_type=F32)
        for c in range(cpg):
            cr = slice(c * CHUNK, (c + 1) * CHUNK)
            ut_scr[g * cpg + c] = _tn_dot(v[cr, :], k_dec[cr, :])
        return carry

    lax.fori_loop(0, s_len // GB, phase_a, 0, unroll=8)

    def phase_b(c, st):
        st_scr[c] = st.astype(BF16)
        dec = dec_scr[pl.ds(pl.multiple_of(c * CHUNK, CHUNK), 1), :]
        return dec * st + ut_scr[c]

    lax.fori_loop(0, s_len // CHUNK, phase_b, jnp.zeros((GLA_V_DIM, GLA_K_DIM), F32))

    def phase_c(g, carry):
        rows = pl.ds(pl.multiple_of(g * GB, GB), GB)
        inter = [_nt_dot(qin_scr[pl.ds(pl.multiple_of(g * GB + c * CHUNK, CHUNK), CHUNK), :],
                         st_scr[g * cpg + c]) for c in range(cpg)]
        o = acc_scr[rows, :] + jnp.concatenate(inter, axis=0)
        ms = jnp.mean(o * o, axis=-1, keepdims=True)
        y = o * lax.rsqrt(ms + RMS_EPS) * og_ref[...]
        o_ref[rows, :] = (y * _silu(r_ref[rows, :].astype(F32))).astype(BF16)
        return carry

    lax.fori_loop(0, s_len // GB, phase_c, 0, unroll=8)


def _gla(proj, ga, w2p, b2, og, batch, seq):
    hq = 3 * DIFF_HEADS
    kblk = lambda off: pl.BlockSpec((seq, GLA_K_DIM), lambda b, hh, off=off: (b, off + hh))
    vblk = lambda off: pl.BlockSpec((seq, GLA_V_DIM), lambda b, hh, off=off: (b, off + hh))
    return pl.pallas_call(
        _gla_kernel,
        grid=(batch, GLA_HEADS),
        in_specs=[kblk(hq), kblk(hq + GLA_HEADS), vblk(16), vblk(16 + GLA_HEADS),
                  pl.BlockSpec((seq, LANES), lambda b, hh: (b, 0)),
                  pl.BlockSpec((LANES, GLA_K_DIM), lambda b, hh: (0, hh)),
                  pl.BlockSpec((1, GLA_K_DIM), lambda b, hh: (0, hh)),
                  pl.BlockSpec((1, GLA_V_DIM), lambda b, hh: (0, 0))],
        out_specs=pl.BlockSpec((seq, GLA_V_DIM), lambda b, hh: (b, hh)),
        out_shape=jax.ShapeDtypeStruct((batch * seq, GLA_HEADS * GLA_V_DIM), BF16),
        scratch_shapes=[pltpu.VMEM((seq, GLA_K_DIM), BF16),
                        pltpu.VMEM((seq // CHUNK, GLA_V_DIM, GLA_K_DIM), F32),
                        pltpu.VMEM((seq, GLA_K_DIM), F32),
                        pltpu.VMEM((seq // CHUNK, GLA_V_DIM, GLA_K_DIM), BF16),
                        pltpu.VMEM((seq, GLA_V_DIM), F32)],
        compiler_params=_params(2),
        name="gla",
    )(proj, proj, proj, proj, ga, w2p, b2, og)


def _out_proj_kernel(a_ref, g_ref, x_ref, wo_ref, g2_ref, wr_ref, br_ref,
                     x1_ref, h2_ref, lg_ref, tt_scr):
    half = a_ref.shape[1]

    def body(c, carry):
        r0 = pl.multiple_of(c * RB_OUT, RB_OUT)
        rows = pl.ds(r0, RB_OUT)
        mixed = (jnp.dot(a_ref[rows, :], wo_ref[0:half, :], preferred_element_type=F32)
                 + jnp.dot(g_ref[rows, :], wo_ref[half:, :], preferred_element_type=F32))
        x1 = x_ref[rows, :] + mixed
        x1_ref[rows, :] = x1
        ms = jnp.mean(x1 * x1, axis=-1, keepdims=True)
        h2 = x1 * lax.rsqrt(ms + RMS_EPS) * g2_ref[...]
        _store_token_tiles(h2_ref, h2, tt_scr, r0)
        lg = jnp.dot(h2.astype(BF16), wr_ref[...], preferred_element_type=F32) + br_ref[...]
        lg_ref[:, rows] = lg.T[0:N_LOGIT_ROWS, :]
        return carry

    lax.fori_loop(0, TM_OUT // RB_OUT, body, 0)


def _out_proj(a_out, g_out, x2, wo, g2, wr, br):
    t = x2.shape[0]
    half = a_out.shape[1]
    row = lambda n: pl.BlockSpec((TM_OUT, n), lambda i: (i, 0))
    full = lambda r, n: pl.BlockSpec((r, n), lambda i: (0, 0))
    return pl.pallas_call(
        _out_proj_kernel,
        grid=(t // TM_OUT,),
        in_specs=[row(half), row(half), row(D_MODEL), full(D_MODEL, D_MODEL),
                  full(1, D_MODEL), full(D_MODEL, LANES), full(1, LANES)],
        out_specs=[row(D_MODEL), pl.BlockSpec((TM_OUT * SUB, LANES), lambda i: (i, 0)),
                   pl.BlockSpec((N_LOGIT_ROWS, TM_OUT), lambda i: (0, i))],
        out_shape=[jax.ShapeDtypeStruct((t, D_MODEL), F32),
                   jax.ShapeDtypeStruct((t * SUB, LANES), BF16),
                   jax.ShapeDtypeStruct((N_LOGIT_ROWS, t), F32)],
        scratch_shapes=[pltpu.VMEM((RB_OUT * TT_PITCH, LANES), F32)],
        compiler_params=_params(1),
        name="out_proj_router",
    )(a_out, g_out, x2, wo, g2, wr, br)


def _first_argmax(vals, row_f):
    m = jnp.max(vals, axis=0, keepdims=True)
    idx = jnp.min(jnp.where(vals == m, row_f, float(vals.shape[0])), axis=0, keepdims=True)
    return m, idx


def _softmax_rows(v):
    ex = jnp.exp(v - jnp.max(v, axis=0, keepdims=True))
    return ex / jnp.sum(ex, axis=0, keepdims=True)


def _routing_kernel(lg_ref, slots_ref, cw_ref, cnt_ref, tok_scr):
    t = lg_ref.shape[1]
    epg = EXPERTS_PER_GROUP
    row8 = lax.broadcasted_iota(jnp.int32, (8, TB), 0).astype(F32)
    row_e = lax.broadcasted_iota(jnp.int32, (N_EXPERTS, TB), 0).astype(F32)
    ui = lax.broadcasted_iota(jnp.int32, (TB, TB), 0)
    uj = lax.broadcasted_iota(jnp.int32, (TB, TB), 1)
    earlier = jnp.where(ui < uj, 1.0, 0.0).astype(BF16)
    all_ones = jnp.ones((TB, TB), BF16)

    def rows01(r0, r1):
        return jnp.where(row8 == 0.0, r0, jnp.where(row8 == 1.0, r1, jnp.zeros_like(r0)))

    def pass1(b, counts):
        cols = pl.ds(pl.multiple_of(b * TB, TB), TB)
        lg = lg_ref[:, cols]
        lg_groups = jnp.where(row8 < float(N_GROUPS), lg[0:GROUP_ROWS, :], -jnp.inf)
        pg_sel, g_sel = _first_argmax(_softmax_rows(lg_groups), row8)
        le = lg[GROUP_ROWS:GROUP_ROWS + epg, :]
        for g in range(1, N_GROUPS):
            le = jnp.where(g_sel == float(g),
                           lg[GROUP_ROWS + g * epg:GROUP_ROWS + (g + 1) * epg, :], le)
        pe = _softmax_rows(le)
        v1, i1 = _first_argmax(pe, row8)
        v2, i2 = _first_argmax(jnp.where(row8 == i1, -jnp.inf, pe), row8)
        tot = v1 + v2
        e1 = g_sel * epg + i1
        e2 = g_sel * epg + i2
        oh1 = row_e == e1
        oh2 = row_e == e2
        a = jnp.where(oh1 | oh2, 1.0, 0.0).astype(BF16)
        rank = jnp.dot(a, earlier, preferred_element_type=F32) + counts
        r1 = jnp.sum(jnp.where(oh1, rank, 0.0), axis=0, keepdims=True)
        r2 = jnp.sum(jnp.where(oh2, rank, 0.0), axis=0, keepdims=True)
        tok_scr[:, cols] = jnp.where(row8 == 0.0, e1, jnp.where(
            row8 == 1.0, e2, jnp.where(row8 == 2.0, r1, jnp.where(row8 == 3.0, r2, 0.0))))
        cw_ref[:, cols] = rows01((v1 / tot) * pg_sel, (v2 / tot) * pg_sel)
        return counts + jnp.dot(a, all_ones, preferred_element_type=F32)

    counts = lax.fori_loop(0, t // TB, pass1, jnp.zeros((N_EXPERTS, TB), F32))
    cnt_ref[...] = counts[:, 0:LANES]
    n_tiles = jnp.floor((counts + (TM_E - 1)) * (1.0 / TM_E))
    li = lax.broadcasted_iota(jnp.int32, (N_EXPERTS, N_EXPERTS), 0)
    lj = lax.broadcasted_iota(jnp.int32, (N_EXPERTS, N_EXPERTS), 1)
    lower = jnp.where(lj < li, 1.0, 0.0).astype(BF16)
    row_off = jnp.dot(lower, n_tiles.astype(BF16), preferred_element_type=F32) * TM_E

    def pass2(b, carry):
        cols = pl.ds(pl.multiple_of(b * TB, TB), TB)
        tok = tok_scr[:, cols]
        off1 = jnp.sum(jnp.where(row_e == tok[0:1, :], row_off, 0.0), axis=0, keepdims=True)
        off2 = jnp.sum(jnp.where(row_e == tok[1:2, :], row_off, 0.0), axis=0, keepdims=True)
        slots_ref[:, cols] = rows01(off1 + tok[2:3, :], off2 + tok[3:4, :]).astype(jnp.int32)
        return carry

    lax.fori_loop(0, t // TB, pass2, 0)


def _routing(logits_t):
    t = logits_t.shape[1]
    return pl.pallas_call(
        _routing_kernel,
        out_shape=[jax.ShapeDtypeStruct((8, t), jnp.int32),
                   jax.ShapeDtypeStruct((8, t), F32),
                   jax.ShapeDtypeStruct((N_EXPERTS, LANES), F32)],
        scratch_shapes=[pltpu.VMEM((8, t), F32)],
        compiler_params=pltpu.CompilerParams(vmem_limit_bytes=VMEM_LIMIT),
        name="routing",
    )(logits_t)


def _rows_copy(src, dst, sem, src_tok, dst_tok, n):
    first = lambda tok: tok * SUB if isinstance(tok, int) else pl.multiple_of(tok * SUB, SUB)
    s0 = first(src_tok)
    d0 = first(dst_tok)
    return pltpu.make_async_copy(src.at[pl.ds(s0, n * SUB), :], dst.at[pl.ds(d0, n * SUB), :], sem)


def _dispatch_kernel(sl1_ref, sl2_ref, ends_ref, h2_ref, xs_hbm, zero_scr, sem):
    step = pl.program_id(0)

    @pl.when(step == 0)
    def _():
        zero_scr[...] = jnp.zeros_like(zero_scr)
        for e in range(N_EXPERTS):
            start = jnp.maximum(ends_ref[e] - TM_E, 0)
            _rows_copy(zero_scr, xs_hbm, sem, 0, start, TM_E).start()
        for e in range(N_EXPERTS):
            _rows_copy(zero_scr, xs_hbm, sem, 0, 0, TM_E).wait()

    base = step * TB_D

    def issue(r, carry):
        _rows_copy(h2_ref, xs_hbm, sem, r, sl1_ref[base + r], 1).start(priority=0)
        _rows_copy(h2_ref, xs_hbm, sem, r, sl2_ref[base + r], 1).start(priority=1)
        return carry

    lax.fori_loop(0, TB_D, issue, 0, unroll=8)
    _rows_copy(h2_ref, xs_hbm, sem, 0, 0, TB_D).wait()
    _rows_copy(h2_ref, xs_hbm, sem, 0, 0, TB_D).wait()


def _dispatch(h2, sl1, sl2, ends, n_rows):
    t = h2.shape[0] // SUB
    return pl.pallas_call(
        _dispatch_kernel,
        grid_spec=pltpu.PrefetchScalarGridSpec(
            num_scalar_prefetch=3,
            grid=(t // TB_D,),
            in_specs=[pl.BlockSpec((TB_D * SUB, LANES), lambda i, *_: (i, 0))],
            out_specs=pl.BlockSpec(memory_space=pl.ANY),
            scratch_shapes=[pltpu.VMEM((TM_E * SUB, LANES), BF16), pltpu.SemaphoreType.DMA(())],
        ),
        out_shape=jax.ShapeDtypeStruct((n_rows * SUB, LANES), BF16),
        compiler_params=_params(1),
        name="dispatch",
    )(sl1, sl2, ends, h2)


def _expert_kernel(tidx_ref, texp_ref, tfirst_ref, tslot_ref, tnext_ref, nt_ref,
                   xs_ref, wg_hbm, wu_hbm, wd_hbm, ys_ref,
                   wg_buf, wu_buf, wd_buf, wg_bf, wu_bf, wd_bf, tt_scr, sems):
    j = pl.program_id(0)

    def weight_copies(e, slot):
        return [pltpu.make_async_copy(hbm.at[e], buf.at[slot], sems.at[slot, k])
                for k, (hbm, buf) in enumerate(((wg_hbm, wg_buf), (wu_hbm, wu_buf),
                                                (wd_hbm, wd_buf)))]

    @pl.when(j < nt_ref[0])
    def _():
        slot = tslot_ref[j]

        @pl.when(tfirst_ref[j] == 1)
        def _():
            @pl.when(j == 0)
            def _():
                for c in weight_copies(texp_ref[j], slot):
                    c.start()
            for c in weight_copies(texp_ref[j], slot):
                c.wait()

            @pl.when(tnext_ref[j] >= 0)
            def _():
                for c in weight_copies(tnext_ref[j], 1 - slot):
                    c.start()
            wg_bf[...] = wg_buf[slot].astype(BF16)
            wu_bf[...] = wu_buf[slot].astype(BF16)
            wd_bf[...] = wd_buf[slot].astype(BF16)

        xb = _load_token_tiles(xs_ref, tt_scr, TM_E).astype(BF16)
        g = jnp.dot(xb, wg_bf[...], preferred_element_type=F32)
        u = jnp.dot(xb, wu_bf[...], preferred_element_type=F32)
        h = (_silu(g) * u).astype(BF16)
        _store_token_tiles(ys_ref, jnp.dot(h, wd_bf[...], preferred_element_type=F32), tt_scr)


def _expert_mlp(xs, tile_meta, wg, wu, wd):
    n_rows = xs.shape[0] // SUB
    tile = pl.BlockSpec((TM_E * SUB, LANES), lambda j, ti, *_: (ti[j], 0))
    hbm = pl.BlockSpec(memory_space=pl.ANY)
    return pl.pallas_call(
        _expert_kernel,
        grid_spec=pltpu.PrefetchScalarGridSpec(
            num_scalar_prefetch=len(tile_meta),
            grid=(n_rows // TM_E,),
            in_specs=[tile, hbm, hbm, hbm],
            out_specs=tile,
            scratch_shapes=[pltpu.VMEM((2, D_MODEL, D_FF), F32),
                            pltpu.VMEM((2, D_MODEL, D_FF), F32),
                            pltpu.VMEM((2, D_FF, D_MODEL), F32),
                            pltpu.VMEM((D_MODEL, D_FF), BF16),
                            pltpu.VMEM((D_MODEL, D_FF), BF16),
                            pltpu.VMEM((D_FF, D_MODEL), BF16),
                            pltpu.VMEM((TM_E * TT_PITCH, LANES), F32),
                            pltpu.SemaphoreType.DMA((2, 3))],
        ),
        out_shape=jax.ShapeDtypeStruct((n_rows * SUB, LANES), BF16),
        compiler_params=_params(1),
        name="expert_mlp",
    )(*tile_meta, xs, wg, wu, wd)


def _combine_kernel(sl1_ref, sl2_ref, x1_ref, cw_ref, ys_hbm, o_ref, y1_scr, y2_scr, tt1_scr,
                    tt2_scr, sems):
    i = pl.program_id(0)

    def issue(step, slot):
        base = step * TB

        def body(r, carry):
            _rows_copy(ys_hbm, y1_scr.at[slot], sems.at[slot], sl1_ref[base + r], r,
                       1).start(priority=0)
            _rows_copy(ys_hbm, y2_scr.at[slot], sems.at[slot], sl2_ref[base + r], r,
                       1).start(priority=1)
            return carry

        lax.fori_loop(0, TB, body, 0, unroll=8)

    @pl.when(i == 0)
    def _():
        issue(0, 0)

    last = pl.num_programs(0) - 1
    rb = 64
    n_pieces = (TB // rb) * SUB
    per_piece = TB // n_pieces

    def step(slot):
        _rows_copy(ys_hbm, y1_scr.at[slot], sems.at[slot], 0, 0, TB).wait()
        _rows_copy(ys_hbm, y2_scr.at[slot], sems.at[slot], 0, 0, TB).wait()
        nbase = jnp.minimum(i + 1, last) * TB
        cw = jnp.concatenate([cw_ref[...], jnp.zeros((LANES - 8, TB), F32)], axis=0).T
        _tt_fill(tt1_scr, y1_scr.at[slot], TB)
        _tt_fill(tt2_scr, y2_scr.at[slot], TB)
        piece = 0
        for r0 in range(0, TB, rb):
            rows = slice(r0, r0 + rb)
            c1 = jnp.broadcast_to(cw[rows, 0:1], (rb, LANES))
            c2 = jnp.broadcast_to(cw[rows, 1:2], (rb, LANES))
            for s in range(SUB):
                for r in range(piece * per_piece, (piece + 1) * per_piece):
                    _rows_copy(ys_hbm, y1_scr.at[1 - slot], sems.at[1 - slot],
                               sl1_ref[nbase + r], r, 1).start(priority=0)
                    _rows_copy(ys_hbm, y2_scr.at[1 - slot], sems.at[1 - slot],
                               sl2_ref[nbase + r], r, 1).start(priority=1)
                piece += 1
                cols = slice(s * LANES, (s + 1) * LANES)
                tiles = _tt_piece(s, rb, r0)
                o_ref[rows, cols] = (x1_ref[rows, cols] + c1 * tt1_scr[tiles]
                                     + c2 * tt2_scr[tiles])

        @pl.when(i == last)
        def _():
            _rows_copy(ys_hbm, y1_scr.at[1 - slot], sems.at[1 - slot], 0, 0, TB).wait()
            _rows_copy(ys_hbm, y2_scr.at[1 - slot], sems.at[1 - slot], 0, 0, TB).wait()

    for parity in (0, 1):
        @pl.when(i % 2 == parity)
        def _():
            step(parity)


def _combine(x1, cw, ys, sl1, sl2):
    t = x1.shape[0]
    return pl.pallas_call(
        _combine_kernel,
        grid_spec=pltpu.PrefetchScalarGridSpec(
            num_scalar_prefetch=2,
            grid=(t // TB,),
            in_specs=[pl.BlockSpec((TB, D_MODEL), lambda i, *_: (i, 0)),
                      pl.BlockSpec((8, TB), lambda i, *_: (0, i)),
                      pl.BlockSpec(memory_space=pl.ANY)],
            out_specs=pl.BlockSpec((TB, D_MODEL), lambda i, *_: (i, 0)),
            scratch_shapes=[pltpu.VMEM((2, TB * SUB, LANES), BF16),
                            pltpu.VMEM((2, TB * SUB, LANES), BF16),
                            pltpu.VMEM((TB * TT_PITCH, LANES), F32),
                            pltpu.VMEM((TB * TT_PITCH, LANES), F32),
                            pltpu.SemaphoreType.DMA((2,))],
        ),
        out_shape=jax.ShapeDtypeStruct((t, D_MODEL), F32),
        compiler_params=_params(1),
        name="combine",
    )(sl1, sl2, x1, cw, ys)


def _lane_tile(v, reps):
    return jnp.tile(v.reshape(1, -1), (1, reps))


def kernel(x, positions, norm1_g, w_in, q_norm_g, k_norm_g, lambda_q1, lambda_k1, lambda_q2,
           lambda_k2, diff_out_norm_g, gla_w_gate2, gla_b_gate, gla_out_norm_g, w_out, norm2_g,
           w_router_group, b_router_group, w_router_expert, b_router_expert, w_gate_expert,
           w_up_expert, w_down_expert):
    batch, seq, d = x.shape
    t = batch * seq
    x2 = x.reshape(t, d)

    inv = ROPE_THETA ** (-jnp.arange(0, ROT_DIM, 2, dtype=F32) / ROT_DIM)
    lane_d = jnp.arange(LANES) % DIFF_QK_DIM
    invf = jnp.where(lane_d < ROT_DIM, inv[lane_d % (ROT_DIM // 2)], 0.0).reshape(1, LANES)
    pos_b = jnp.broadcast_to(positions.astype(F32).reshape(t, 1), (t, LANES))
    tabs = _rope_tables(pos_b, invf)

    w_in_t = jnp.swapaxes(w_in, 1, 2)[0]
    wa_t = jnp.pad(w_in_t[D_MAIN:, :], ((0, LANES - GLA_GATE_RANK), (0, 0)))
    proj, ga = _in_proj(x2, norm1_g, w_in_t, wa_t)

    a_out = _diff_attention(
        proj, tabs, _lane_tile(q_norm_g[0], 2), _lane_tile(k_norm_g[0], 2),
        lambda_q1, lambda_k1, lambda_q2, lambda_k2, diff_out_norm_g, batch, seq)
    w2p = jnp.pad(gla_w_gate2[0], ((0, LANES - GLA_GATE_RANK), (0, 0))).astype(BF16)
    g_out = _gla(proj, ga, w2p, gla_b_gate, gla_out_norm_g, batch, seq)

    gpad = GROUP_ROWS - N_GROUPS
    lpad = LANES - N_LOGIT_ROWS
    wr = jnp.concatenate(
        [jnp.pad(w_router_group[0], ((0, 0), (0, gpad))),
         jnp.pad(w_router_expert[0].transpose(1, 0, 2).reshape(d, N_EXPERTS),
                 ((0, 0), (0, lpad)))], axis=1)
    br = jnp.concatenate([jnp.pad(b_router_group[0], (0, gpad)),
                          jnp.pad(b_router_expert[0].reshape(-1), (0, lpad))]).reshape(1, LANES)
    x1, h2, logits_t = _out_proj(a_out, g_out, x2, w_out[0].astype(BF16), norm2_g,
                                 wr.astype(BF16), br)

    slots, cw, counts = _routing(logits_t)
    sl1 = slots[0]
    sl2 = slots[1]
    n_rows = 2 * t + N_EXPERTS * TM_E
    max_tiles = n_rows // TM_E
    cnt = counts[:, 0].astype(jnp.int32)
    tile_end = jnp.cumsum((cnt + (TM_E - 1)) // TM_E)
    n_tiles = tile_end[-1:]
    tile_idx = jnp.minimum(jnp.arange(max_tiles, dtype=jnp.int32), n_tiles[0] - 1)
    tile_exp = jnp.sum(tile_idx[:, None] >= tile_end[None, :], axis=1).astype(jnp.int32)
    row_end = (tile_end * TM_E).astype(jnp.int32)
    tile_first = jnp.concatenate(
        [jnp.ones((1,), jnp.int32), (tile_exp[1:] != tile_exp[:-1]).astype(jnp.int32)])
    tile_slot = (jnp.cumsum(tile_first) - 1) % 2
    nxt = tile_end[tile_exp]
    tile_next = jnp.where(nxt < n_tiles[0], tile_exp[jnp.minimum(nxt, max_tiles - 1)], -1)
    tile_meta = [a.astype(jnp.int32)
                 for a in (tile_idx, tile_exp, tile_first, tile_slot, tile_next, n_tiles)]

    xs = _dispatch(h2, sl1, sl2, row_end, n_rows)
    ys = _expert_mlp(xs, tile_meta,
                     w_gate_expert[0].reshape(N_EXPERTS, d, D_FF),
                     w_up_expert[0].reshape(N_EXPERTS, d, D_FF),
                     w_down_expert[0].reshape(N_EXPERTS, D_FF, d))
    out = _combine(x1, cw, ys, sl1, sl2)
    return out.reshape(batch, seq, d)
```

```python
import math

import jax
import jax.numpy as jnp
from jax import lax
from jax.experimental import pallas as pl
from jax.experimental.pallas import tpu as pltpu

D_MODEL = 2048
CHUNK = 64
DIFF_QK_DIM = 64
DIFF_V_DIM = 128
DIFF_HEADS = 8
ROT_DIM = 16
ROPE_THETA = 500000.0
GLA_HEADS = 4
GLA_V_DIM = 256
GLA_K_DIM = 128
GLA_GATE_RANK = 16
GLA_TAU = 16.0
N_GROUPS = 4
EXPERTS_PER_GROUP = 8
N_EXPERTS = N_GROUPS * EXPERTS_PER_GROUP
D_FF = 512
RMS_EPS = 1e-6
LAMBDA_INIT = 0.8 - 0.6 * math.exp(-0.3 * 0)
D_MAIN = 6144
GROUP_ROWS = 8
N_LOGIT_ROWS = GROUP_ROWS + N_EXPERTS

LANES = 128
SUB = D_MODEL // LANES
TT_PITCH = 24
VMEM_LIMIT = 56 * 1024 * 1024

TB_ROPE = 1024
TM_IN = 1024
TN_IN = 1024
RB_IN = 256
TQ = 256
ATT_HPS = 2
GB = 256
TM_OUT = 512
RB_OUT = 256
TB = 256
RB_COMBINE = 64
TB_D = 1024
TM_E = 256

F32 = jnp.float32
BF16 = jnp.bfloat16


def _params(n_axes):
    return pltpu.CompilerParams(dimension_semantics=("arbitrary",) * n_axes,
                                vmem_limit_bytes=VMEM_LIMIT)


def _nt_dot(a, b):
    return lax.dot_general(a, b, (((1,), (1,)), ((), ())), preferred_element_type=F32)


def _tn_dot(a, b):
    return lax.dot_general(a, b, (((0,), (0,)), ((), ())), preferred_element_type=F32)


def _silu(x):
    return x * (1.0 / (1.0 + jnp.exp(-x)))


def _tt_fill(scr, ref, n, tok0=0):
    for t in range(n):
        scr[t * TT_PITCH:t * TT_PITCH + SUB, :] = ref[pl.ds((tok0 + t) * SUB, SUB), :].astype(F32)


def _tt_drain(ref, scr, n, tok0=0):
    for t in range(n):
        ref[pl.ds((tok0 + t) * SUB, SUB), :] = scr[t * TT_PITCH:t * TT_PITCH + SUB, :].astype(BF16)


def _tt_piece(s, n, tok0=0):
    return (pl.ds(tok0 * TT_PITCH + s, n, stride=TT_PITCH), slice(None))


def _store_token_tiles(ref, val, scr, tok0=0):
    n = val.shape[0]
    for s in range(SUB):
        scr[_tt_piece(s, n)] = val[:, s * LANES:(s + 1) * LANES]
    _tt_drain(ref, scr, n, tok0)


def _load_token_tiles(ref, scr, n, tok0=0):
    _tt_fill(scr, ref, n, tok0)
    return jnp.concatenate([scr[_tt_piece(s, n)] for s in range(SUB)], axis=1)


def _rope_table_kernel(pos_ref, invf_ref, c_ref, s1_ref, s2_ref):
    ang = pos_ref[...] * invf_ref[...]
    d = lax.broadcasted_iota(jnp.int32, ang.shape, 1) % DIFF_QK_DIM
    cos = jnp.cos(ang)
    sin = jnp.sin(ang)
    half = ROT_DIM // 2
    c_ref[...] = jnp.where(d < ROT_DIM, cos, 1.0)
    s1_ref[...] = jnp.where(d < half, -sin, 0.0)
    s2_ref[...] = jnp.where((d >= half) & (d < ROT_DIM), sin, 0.0)


def _rope_tables(pos_b, invf):
    t = pos_b.shape[0]
    spec = pl.BlockSpec((TB_ROPE, LANES), lambda i: (i, 0))
    return pl.pallas_call(
        _rope_table_kernel,
        grid=(t // TB_ROPE,),
        in_specs=[spec, pl.BlockSpec((1, LANES), lambda i: (0, 0))],
        out_specs=[spec, spec, spec],
        out_shape=[jax.ShapeDtypeStruct((t, LANES), F32)] * 3,
        compiler_params=_params(1),
        name="rope_tables",
    )(pos_b, invf)


def _in_proj_kernel(x_ref, g_ref, wt_ref, wat_ref, proj_ref, ga_ref, h_scr):
    @pl.when(pl.program_id(1) == 0)
    def _():
        def body(c, carry):
            rows = pl.ds(c * RB_IN, RB_IN)
            x = x_ref[rows, :]
            ms = jnp.mean(x * x, axis=-1, keepdims=True)
            h_scr[rows, :] = (x * lax.rsqrt(ms + RMS_EPS) * g_ref[...]).astype(BF16)
            return carry
        lax.fori_loop(0, TM_IN // RB_IN, body, 0)
        ga_ref[...] = _nt_dot(h_scr[...], wat_ref[...].astype(BF16))

    proj_ref[...] = _nt_dot(h_scr[...], wt_ref[...].astype(BF16)).astype(BF16)


def _in_proj(x2, g1, w_in, wa):
    t = x2.shape[0]
    return pl.pallas_call(
        _in_proj_kernel,
        grid=(t // TM_IN, D_MAIN // TN_IN),
        in_specs=[
            pl.BlockSpec((TM_IN, D_MODEL), lambda i, j: (i, 0)),
            pl.BlockSpec((1, D_MODEL), lambda i, j: (0, 0)),
            pl.BlockSpec((TN_IN, D_MODEL), lambda i, j: (j, 0)),
            pl.BlockSpec((LANES, D_MODEL), lambda i, j: (0, 0)),
        ],
        out_specs=[
            pl.BlockSpec((TM_IN, TN_IN), lambda i, j: (i, j)),
            pl.BlockSpec((TM_IN, LANES), lambda i, j: (i, 0)),
        ],
        out_shape=[jax.ShapeDtypeStruct((t, D_MAIN), BF16),
                   jax.ShapeDtypeStruct((t, LANES), F32)],
        scratch_shapes=[pltpu.VMEM((TM_IN, D_MODEL), BF16)],
        compiler_params=_params(2),
        name="in_proj",
    )(x2, g1, w_in, wa)


def _norm_rope(x, g, c, s1, s2):
    lo = lax.broadcasted_iota(jnp.int32, x.shape, 1) < DIFF_QK_DIM
    x2 = x * x
    s_lo = jnp.sum(jnp.where(lo, x2, 0.0), axis=-1, keepdims=True)
    s_hi = jnp.sum(jnp.where(lo, 0.0, x2), axis=-1, keepdims=True)
    ms = jnp.where(lo, s_lo, s_hi) * (1.0 / DIFF_QK_DIM)
    y = x * lax.rsqrt(ms + RMS_EPS) * g
    half = ROT_DIM // 2
    return y * c + pltpu.roll(y, LANES - half, 1) * s1 + pltpu.roll(y, half, 1) * s2


def _attn_kernel(q_ref, k_ref, v_ref, c_ref, s1_ref, s2_ref, qg_ref, kg_ref,
                 lq1_ref, lk1_ref, lq2_ref, lk2_ref, og_ref, o_ref,
                 q1_scr, q2_scr, k_scr, v_scr, *bufs):
    s_len = q_ref.shape[0]
    s_bufs, e_bufs = bufs[0:4], bufs[4:8]
    lam = (jnp.exp(jnp.sum(lq1_ref[...] * lk1_ref[...], axis=-1, keepdims=True))
           - jnp.exp(jnp.sum(lq2_ref[...] * lk2_ref[...], axis=-1, keepdims=True))
           + LAMBDA_INIT)
    lo = lax.broadcasted_iota(jnp.int32, (TQ, LANES), 1) < DIFF_QK_DIM
    for hd in range(ATT_HPS):
        v_scr[hd, :, 0:LANES] = v_ref[:, hd * LANES:(hd + 1) * LANES]
        v_scr[hd, :, LANES:] = jnp.ones((s_len, LANES), BF16)

    n_tiles = s_len // TQ
    tiles = [(hd, i) for hd in range(ATT_HPS) for i in range(n_tiles)]

    def prepare(t):
        hd, i = tiles[t]
        rows = slice(i * TQ, (i + 1) * TQ)
        lanes = slice(hd * LANES, (hd + 1) * LANES)
        c, s1, s2 = c_ref[rows, :], s1_ref[rows, :], s2_ref[rows, :]
        qn = (_norm_rope(q_ref[rows, lanes].astype(F32), qg_ref[...], c, s1, s2)
              * (DIFF_QK_DIM ** -0.5 * math.log2(math.e)))
        q1_scr[hd, rows, :] = jnp.where(lo, qn, 0.0).astype(BF16)
        q2_scr[hd, rows, :] = jnp.where(lo, 0.0, qn).astype(BF16)
        k_scr[hd, rows, :] = _norm_rope(k_ref[rows, lanes].astype(F32), kg_ref[...], c, s1,
                                        s2).astype(BF16)

    diag = (lax.broadcasted_iota(jnp.int32, (TQ, TQ), 1) // CHUNK
            <= lax.broadcasted_iota(jnp.int32, (TQ, TQ), 0) // CHUNK)

    items = [(t, q_scr) for t in range(len(tiles)) for q_scr in (q1_scr, q2_scr)]

    def scores(r):
        t, q_scr = items[r]
        hd, i = tiles[t]
        s_scr = s_bufs[r % 4]
        q = q_scr[hd, i * TQ:(i + 1) * TQ, :]
        n_off = i * TQ
        s_scr[:, n_off:n_off + TQ] = jnp.where(
            diag, _nt_dot(q, k_scr[hd, n_off:n_off + TQ, :]), -jnp.inf)
        if n_off:
            s_scr[:, 0:n_off] = _nt_dot(q, k_scr[hd, 0:n_off, :])

    def softmax_pv(r):
        hd, i = tiles[items[r][0]]
        nk = (i + 1) * TQ
        s_scr, e_scr = s_bufs[r % 4], e_bufs[r % 4]
        m = jnp.max(s_scr[:, 0:nk], axis=-1, keepdims=True)
        e_scr[:, 0:nk] = jnp.exp2(s_scr[:, 0:nk] - m).astype(BF16)
        acc = jnp.dot(e_scr[:, 0:nk], v_scr[hd, 0:nk, :], preferred_element_type=F32)
        return acc[:, 0:LANES] / acc[:, LANES:]

    prepare(0)
    scores(0)
    scores(1)
    for t, (hd, i) in enumerate(tiles):
        if t + 1 < len(tiles):
            prepare(t + 1)
            scores(2 * t + 2)
        o1 = softmax_pv(2 * t)
        if t + 1 < len(tiles):
            scores(2 * t + 3)
        o = o1 - lam * softmax_pv(2 * t + 1)
        ms = jnp.mean(o * o, axis=-1, keepdims=True)
        y = o * lax.rsqrt(ms + RMS_EPS) * og_ref[...] * (1.0 - LAMBDA_INIT)
        o_ref[i * TQ:(i + 1) * TQ, hd * LANES:(hd + 1) * LANES] = y.astype(BF16)


def _diff_attention(proj, tabs, qg, kg, lq1, lk1, lq2, lk2, og, batch, seq):
    c, s1, s2 = tabs
    h = DIFF_HEADS
    hg = h // ATT_HPS
    wide = ATT_HPS * LANES
    blk = lambda off: pl.BlockSpec((seq, wide), lambda b, hh, off=off: (b, off + hh))
    tab = pl.BlockSpec((seq, LANES), lambda b, hh: (b, 0))
    vec = lambda n: pl.BlockSpec((1, n), lambda b, hh: (0, 0))
    return pl.pallas_call(
        _attn_kernel,
        grid=(batch, hg),
        in_specs=[blk(0), blk(hg), blk(2 * hg), tab, tab, tab,
                  vec(LANES), vec(LANES), vec(DIFF_QK_DIM), vec(DIFF_QK_DIM),
                  vec(DIFF_QK_DIM), vec(DIFF_QK_DIM), vec(LANES)],
        out_specs=pl.BlockSpec((seq, wide), lambda b, hh: (b, hh)),
        out_shape=jax.ShapeDtypeStruct((batch * seq, h * DIFF_V_DIM), BF16),
        scratch_shapes=[pltpu.VMEM((ATT_HPS, seq, LANES), BF16)] * 3
        + [pltpu.VMEM((ATT_HPS, seq, 2 * LANES), BF16)]
        + [pltpu.VMEM((TQ, seq), F32)] * 4 + [pltpu.VMEM((TQ, seq), BF16)] * 4,
        compiler_params=_params(2),
        name="diff_attention",
    )(proj, proj, proj, c, s1, s2, qg, kg, lq1, lk1, lq2, lk2, og)


def _split_dot(ones_bf, x):
    hi = x.astype(BF16)
    lo = (x - hi.astype(F32)).astype(BF16)
    return (jnp.dot(ones_bf, hi, preferred_element_type=F32)
            + jnp.dot(ones_bf, lo, preferred_element_type=F32))


def _gla_kernel(q_ref, k_ref, v_ref, r_ref, ga_ref, w2_ref, b2_ref, og_ref, o_ref,
                qin_scr, ut_scr, dec_scr, st_scr, acc_scr):
    s_len = q_ref.shape[0]
    cpg = GB // CHUNK
    ri = lax.broadcasted_iota(jnp.int32, (GB, GB), 0)
    ci = lax.broadcasted_iota(jnp.int32, (GB, GB), 1)
    same = (ri // CHUNK) == (ci // CHUNK)
    blk_ones = jnp.where(same, 1.0, 0.0).astype(BF16)
    tril = same & (ci <= ri)
    tri_ones = jnp.where(tril, 1.0, 0.0).astype(BF16)

    def phase_a(g, carry):
        rows = pl.ds(pl.multiple_of(g * GB, GB), GB)
        pre = jnp.dot(ga_ref[rows, :].astype(BF16), w2_ref[...],
                      preferred_element_type=F32) + b2_ref[...]
        la = -(jnp.maximum(-pre, 0.0) + jnp.log1p(jnp.exp(-jnp.abs(pre)))) * (1.0 / GLA_TAU)
        bc = _split_dot(tri_ones, la)
        bl = _split_dot(blk_ones, la)
        e_neg = jnp.exp(-bc)
        e_last = jnp.exp(bl)
        k = k_ref[rows, :].astype(F32)
        q_in = (q_ref[rows, :].astype(F32) * (GLA_K_DIM ** -0.5) * jnp.exp(bc)).astype(BF16)
        k_in = (k * e_neg).astype(BF16)
        k_dec = (k * (e_last * e_neg)).astype(BF16)
        qin_scr[rows, :] = q_in
        dec_scr[rows, :] = e_last
        v = v_ref[rows, :]
        att = jnp.where(tril, _nt_dot(q_in, k_in), 0.0).astype(BF16)
        acc_scr[rows, :] = jnp.dot(att, v, preferred_element_type=F32)
        for c in range(cpg):
            cr = slice(c * CHUNK, (c + 1) * CHUNK)
            ut_scr[g * cpg + c] = _tn_dot(v[cr, :], k_dec[cr, :])
        return carry

    lax.fori_loop(0, s_len // GB, phase_a, 0, unroll=8)

    def phase_b(c, st):
        st_scr[c] = st.astype(BF16)
        dec = dec_scr[pl.ds(pl.multiple_of(c * CHUNK, CHUNK), 1), :]
        return dec * st + ut_scr[c]

    lax.fori_loop(0, s_len // CHUNK, phase_b, jnp.zeros((GLA_V_DIM, GLA_K_DIM), F32))

    def phase_c(g, carry):
        rows = pl.ds(pl.multiple_of(g * GB, GB), GB)
        inter = [_nt_dot(qin_scr[pl.ds(pl.multiple_of(g * GB + c * CHUNK, CHUNK), CHUNK), :],
                         st_scr[g * cpg + c]) for c in range(cpg)]
        o = acc_scr[rows, :] + jnp.concatenate(inter, axis=0)
        ms = jnp.mean(o * o, axis=-1, keepdims=True)
        y = o * lax.rsqrt(ms + RMS_EPS) * og_ref[...]
        o_ref[rows, :] = (y * _silu(r_ref[rows, :].astype(F32))).astype(BF16)
        return carry

    lax.fori_loop(0, s_len // GB, phase_c, 0, unroll=8)


def _gla(proj, ga, w2p, b2, og, batch, seq):
    hq = 3 * DIFF_HEADS
    kblk = lambda off: pl.BlockSpec((seq, GLA_K_DIM), lambda b, hh, off=off: (b, off + hh))
    vblk = lambda off: pl.BlockSpec((seq, GLA_V_DIM), lambda b, hh, off=off: (b, off + hh))
    return pl.pallas_call(
        _gla_kernel,
        grid=(batch, GLA_HEADS),
        in_specs=[kblk(hq), kblk(hq + GLA_HEADS), vblk(16), vblk(16 + GLA_HEADS),
                  pl.BlockSpec((seq, LANES), lambda b, hh: (b, 0)),
                  pl.BlockSpec((LANES, GLA_K_DIM), lambda b, hh: (0, hh)),
                  pl.BlockSpec((1, GLA_K_DIM), lambda b, hh: (0, hh)),
                  pl.BlockSpec((1, GLA_V_DIM), lambda b, hh: (0, 0))],
        out_specs=pl.BlockSpec((seq, GLA_V_DIM), lambda b, hh: (b, hh)),
        out_shape=jax.ShapeDtypeStruct((batch * seq, GLA_HEADS * GLA_V_DIM), BF16),
        scratch_shapes=[pltpu.VMEM((seq, GLA_K_DIM), BF16),
                        pltpu.VMEM((seq // CHUNK, GLA_V_DIM, GLA_K_DIM), F32),
                        pltpu.VMEM((seq, GLA_K_DIM), F32),
                        pltpu.VMEM((seq // CHUNK, GLA_V_DIM, GLA_K_DIM), BF16),
                        pltpu.VMEM((seq, GLA_V_DIM), F32)],
        compiler_params=_params(2),
        name="gla",
    )(proj, proj, proj, proj, ga, w2p, b2, og)


def _out_proj_kernel(a_ref, g_ref, x_ref, wo_ref, g2_ref, wr_ref, br_ref,
                     x1_ref, h2_ref, lg_ref, tt_scr):
    half = a_ref.shape[1]

    def body(c, carry):
        r0 = pl.multiple_of(c * RB_OUT, RB_OUT)
        rows = pl.ds(r0, RB_OUT)
        mixed = (jnp.dot(a_ref[rows, :], wo_ref[0:half, :], preferred_element_type=F32)
                 + jnp.dot(g_ref[rows, :], wo_ref[half:, :], preferred_element_type=F32))
        x1 = x_ref[rows, :] + mixed
        x1_ref[rows, :] = x1
        ms = jnp.mean(x1 * x1, axis=-1, keepdims=True)
        h2 = x1 * lax.rsqrt(ms + RMS_EPS) * g2_ref[...]
        _store_token_tiles(h2_ref, h2, tt_scr, r0)
        lg = jnp.dot(h2.astype(BF16), wr_ref[...], preferred_element_type=F32) + br_ref[...]
        lg_ref[:, rows] = lg.T[0:N_LOGIT_ROWS, :]
        return carry

    lax.fori_loop(0, TM_OUT // RB_OUT, body, 0)


def _out_proj(a_out, g_out, x2, wo, g2, wr, br):
    t = x2.shape[0]
    half = a_out.shape[1]
    row = lambda n: pl.BlockSpec((TM_OUT, n), lambda i: (i, 0))
    full = lambda r, n: pl.BlockSpec((r, n), lambda i: (0, 0))
    return pl.pallas_call(
        _out_proj_kernel,
        grid=(t // TM_OUT,),
        in_specs=[row(half), row(half), row(D_MODEL), full(D_MODEL, D_MODEL),
                  full(1, D_MODEL), full(D_MODEL, LANES), full(1, LANES)],
        out_specs=[row(D_MODEL), pl.BlockSpec((TM_OUT * SUB, LANES), lambda i: (i, 0)),
                   pl.BlockSpec((N_LOGIT_ROWS, TM_OUT), lambda i: (0, i))],
        out_shape=[jax.ShapeDtypeStruct((t, D_MODEL), F32),
                   jax.ShapeDtypeStruct((t * SUB, LANES), BF16),
                   jax.ShapeDtypeStruct((N_LOGIT_ROWS, t), F32)],
        scratch_shapes=[pltpu.VMEM((RB_OUT * TT_PITCH, LANES), F32)],
        compiler_params=_params(1),
        name="out_proj_router",
    )(a_out, g_out, x2, wo, g2, wr, br)


def _first_argmax(vals, row_f):
    m = jnp.max(vals, axis=0, keepdims=True)
    idx = jnp.min(jnp.where(vals == m, row_f, float(vals.shape[0])), axis=0, keepdims=True)
    return m, idx


def _softmax_rows(v):
    ex = jnp.exp(v - jnp.max(v, axis=0, keepdims=True))
    return ex / jnp.sum(ex, axis=0, keepdims=True)


def _routing_kernel(lg_ref, slots_ref, cw_ref, cnt_ref, tok_scr):
    t = lg_ref.shape[1]
    epg = EXPERTS_PER_GROUP
    row8 = lax.broadcasted_iota(jnp.int32, (8, TB), 0).astype(F32)
    row_e = lax.broadcasted_iota(jnp.int32, (N_EXPERTS, TB), 0).astype(F32)
    ui = lax.broadcasted_iota(jnp.int32, (TB, TB), 0)
    uj = lax.broadcasted_iota(jnp.int32, (TB, TB), 1)
    earlier = jnp.where(ui < uj, 1.0, 0.0).astype(BF16)
    all_ones = jnp.ones((TB, TB), BF16)

    def rows01(r0, r1):
        return jnp.where(row8 == 0.0, r0, jnp.where(row8 == 1.0, r1, jnp.zeros_like(r0)))

    def pass1(b, counts):
        cols = pl.ds(pl.multiple_of(b * TB, TB), TB)
        lg = lg_ref[:, cols]
        lg_groups = jnp.where(row8 < float(N_GROUPS), lg[0:GROUP_ROWS, :], -jnp.inf)
        pg_sel, g_sel = _first_argmax(_softmax_rows(lg_groups), row8)
        le = lg[GROUP_ROWS:GROUP_ROWS + epg, :]
        for g in range(1, N_GROUPS):
            le = jnp.where(g_sel == float(g),
                           lg[GROUP_ROWS + g * epg:GROUP_ROWS + (g + 1) * epg, :], le)
        pe = _softmax_rows(le)
        v1, i1 = _first_argmax(pe, row8)
        v2, i2 = _first_argmax(jnp.where(row8 == i1, -jnp.inf, pe), row8)
        tot = v1 + v2
        e1 = g_sel * epg + i1
        e2 = g_sel * epg + i2
        oh1 = row_e == e1
        oh2 = row_e == e2
        a = jnp.where(oh1 | oh2, 1.0, 0.0).astype(BF16)
        rank = jnp.dot(a, earlier, preferred_element_type=F32) + counts
        r1 = jnp.sum(jnp.where(oh1, rank, 0.0), axis=0, keepdims=True)
        r2 = jnp.sum(jnp.where(oh2, rank, 0.0), axis=0, keepdims=True)
        tok_scr[:, cols] = jnp.where(row8 == 0.0, e1, jnp.where(
            row8 == 1.0, e2, jnp.where(row8 == 2.0, r1, jnp.where(row8 == 3.0, r2, 0.0))))
        cw_ref[:, cols] = rows01((v1 / tot) * pg_sel, (v2 / tot) * pg_sel)
        return counts + jnp.dot(a, all_ones, preferred_element_type=F32)

    counts = lax.fori_loop(0, t // TB, pass1, jnp.zeros((N_EXPERTS, TB), F32))
    cnt_ref[...] = counts[:, 0:LANES]
    n_tiles = jnp.floor((counts + (TM_E - 1)) * (1.0 / TM_E))
    li = lax.broadcasted_iota(jnp.int32, (N_EXPERTS, N_EXPERTS), 0)
    lj = lax.broadcasted_iota(jnp.int32, (N_EXPERTS, N_EXPERTS), 1)
    lower = jnp.where(lj < li, 1.0, 0.0).astype(BF16)
    row_off = jnp.dot(lower, n_tiles.astype(BF16), preferred_element_type=F32) * TM_E

    def pass2(b, carry):
        cols = pl.ds(pl.multiple_of(b * TB, TB), TB)
        tok = tok_scr[:, cols]
        off1 = jnp.sum(jnp.where(row_e == tok[0:1, :], row_off, 0.0), axis=0, keepdims=True)
        off2 = jnp.sum(jnp.where(row_e == tok[1:2, :], row_off, 0.0), axis=0, keepdims=True)
        slots_ref[:, cols] = rows01(off1 + tok[2:3, :], off2 + tok[3:4, :]).astype(jnp.int32)
        return carry

    lax.fori_loop(0, t // TB, pass2, 0)


def _routing(logits_t):
    t = logits_t.shape[1]
    return pl.pallas_call(
        _routing_kernel,
        out_shape=[jax.ShapeDtypeStruct((8, t), jnp.int32),
                   jax.ShapeDtypeStruct((8, t), F32),
                   jax.ShapeDtypeStruct((N_EXPERTS, LANES), F32)],
        scratch_shapes=[pltpu.VMEM((8, t), F32)],
        compiler_params=pltpu.CompilerParams(vmem_limit_bytes=VMEM_LIMIT),
        name="routing",
    )(logits_t)


def _rows_copy(src, dst, sem, src_tok, dst_tok, n):
    first = lambda tok: tok * SUB if isinstance(tok, int) else pl.multiple_of(tok * SUB, SUB)
    s0 = first(src_tok)
    d0 = first(dst_tok)
    return pltpu.make_async_copy(src.at[pl.ds(s0, n * SUB), :], dst.at[pl.ds(d0, n * SUB), :], sem)


def _dispatch_kernel(sl1_ref, sl2_ref, ends_ref, h2_ref, xs_hbm, zero_scr, sem):
    step = pl.program_id(0)

    @pl.when(step == 0)
    def _():
        zero_scr[...] = jnp.zeros_like(zero_scr)
        for e in range(N_EXPERTS):
            start = jnp.maximum(ends_ref[e] - TM_E, 0)
            _rows_copy(zero_scr, xs_hbm, sem, 0, start, TM_E).start()
        for e in range(N_EXPERTS):
            _rows_copy(zero_scr, xs_hbm, sem, 0, 0, TM_E).wait()

    base = step * TB_D

    def issue(r, carry):
        _rows_copy(h2_ref, xs_hbm, sem, r, sl1_ref[base + r], 1).start(priority=0)
        _rows_copy(h2_ref, xs_hbm, sem, r, sl2_ref[base + r], 1).start(priority=1)
        return carry

    lax.fori_loop(0, TB_D, issue, 0, unroll=8)
    _rows_copy(h2_ref, xs_hbm, sem, 0, 0, TB_D).wait()
    _rows_copy(h2_ref, xs_hbm, sem, 0, 0, TB_D).wait()


def _dispatch(h2, sl1, sl2, ends, n_rows):
    t = h2.shape[0] // SUB
    return pl.pallas_call(
        _dispatch_kernel,
        grid_spec=pltpu.PrefetchScalarGridSpec(
            num_scalar_prefetch=3,
            grid=(t // TB_D,),
            in_specs=[pl.BlockSpec((TB_D * SUB, LANES), lambda i, *_: (i, 0))],
            out_specs=pl.BlockSpec(memory_space=pl.ANY),
            scratch_shapes=[pltpu.VMEM((TM_E * SUB, LANES), BF16), pltpu.SemaphoreType.DMA(())],
        ),
        out_shape=jax.ShapeDtypeStruct((n_rows * SUB, LANES), BF16),
        compiler_params=_params(1),
        name="dispatch",
    )(sl1, sl2, ends, h2)


def _expert_kernel(tidx_ref, texp_ref, tfirst_ref, tslot_ref, tnext_ref, nt_ref,
                   xs_ref, wg_hbm, wu_hbm, wd_hbm, ys_ref,
                   wg_buf, wu_buf, wd_buf, wg_bf, wu_bf, wd_bf, tt_scr, sems):
    j = pl.program_id(0)

    def weight_copies(e, slot):
        return [pltpu.make_async_copy(hbm.at[e], buf.at[slot], sems.at[slot, k])
                for k, (hbm, buf) in enumerate(((wg_hbm, wg_buf), (wu_hbm, wu_buf),
                                                (wd_hbm, wd_buf)))]

    @pl.when(j < nt_ref[0])
    def _():
        slot = tslot_ref[j]

        @pl.when(tfirst_ref[j] == 1)
        def _():
            @pl.when(j == 0)
            def _():
                for c in weight_copies(texp_ref[j], slot):
                    c.start()
            for c in weight_copies(texp_ref[j], slot):
                c.wait()

            @pl.when(tnext_ref[j] >= 0)
            def _():
                for c in weight_copies(tnext_ref[j], 1 - slot):
                    c.start()
            wg_bf[...] = wg_buf[slot].astype(BF16)
            wu_bf[...] = wu_buf[slot].astype(BF16)
            wd_bf[...] = wd_buf[slot].astype(BF16)

        xb = _load_token_tiles(xs_ref, tt_scr, TM_E).astype(BF16)
        g = jnp.dot(xb, wg_bf[...], preferred_element_type=F32)
        u = jnp.dot(xb, wu_bf[...], preferred_element_type=F32)
        h = (_silu(g) * u).astype(BF16)
        _store_token_tiles(ys_ref, jnp.dot(h, wd_bf[...], preferred_element_type=F32), tt_scr)


def _expert_mlp(xs, tile_meta, wg, wu, wd):
    n_rows = xs.shape[0] // SUB
    tile = pl.BlockSpec((TM_E * SUB, LANES), lambda j, ti, *_: (ti[j], 0))
    hbm = pl.BlockSpec(memory_space=pl.ANY)
    return pl.pallas_call(
        _expert_kernel,
        grid_spec=pltpu.PrefetchScalarGridSpec(
            num_scalar_prefetch=len(tile_meta),
            grid=(n_rows // TM_E,),
            in_specs=[tile, hbm, hbm, hbm],
            out_specs=tile,
            scratch_shapes=[pltpu.VMEM((2, D_MODEL, D_FF), F32),
                            pltpu.VMEM((2, D_MODEL, D_FF), F32),
                            pltpu.VMEM((2, D_FF, D_MODEL), F32),
                            pltpu.VMEM((D_MODEL, D_FF), BF16),
                            pltpu.VMEM((D_MODEL, D_FF), BF16),
                            pltpu.VMEM((D_FF, D_MODEL), BF16),
                            pltpu.VMEM((TM_E * TT_PITCH, LANES), F32),
                            pltpu.SemaphoreType.DMA((2, 3))],
        ),
        out_shape=jax.ShapeDtypeStruct((n_rows * SUB, LANES), BF16),
        compiler_params=_params(1),
        name="expert_mlp",
    )(*tile_meta, xs, wg, wu, wd)


def _combine_kernel(sl1_ref, sl2_ref, x1_ref, cw_ref, ys_hbm, o_ref, y1_scr, y2_scr, tt1_scr,
                    tt2_scr, sems):
    i = pl.program_id(0)

    def issue(step, slot):
        base = step * TB

        def body(r, carry):
            _rows_copy(ys_hbm, y1_scr.at[slot], sems.at[slot], sl1_ref[base + r], r,
                       1).start(priority=0)
            _rows_copy(ys_hbm, y2_scr.at[slot], sems.at[slot], sl2_ref[base + r], r,
                       1).start(priority=1)
            return carry

        lax.fori_loop(0, TB, body, 0, unroll=8)

    @pl.when(i == 0)
    def _():
        issue(0, 0)

    last = pl.num_programs(0) - 1
    rb = RB_COMBINE
    n_pieces = (TB // rb) * SUB
    per_piece = TB // n_pieces

    def step(slot):
        _rows_copy(ys_hbm, y1_scr.at[slot], sems.at[slot], 0, 0, TB).wait()
        _rows_copy(ys_hbm, y2_scr.at[slot], sems.at[slot], 0, 0, TB).wait()
        nbase = jnp.minimum(i + 1, last) * TB
        cw = jnp.concatenate([cw_ref[...], jnp.zeros((LANES - 8, TB), F32)], axis=0).T
        _tt_fill(tt1_scr, y1_scr.at[slot], TB)
        _tt_fill(tt2_scr, y2_scr.at[slot], TB)
        piece = 0
        for r0 in range(0, TB, rb):
            rows = slice(r0, r0 + rb)
            c1 = jnp.broadcast_to(cw[rows, 0:1], (rb, LANES))
            c2 = jnp.broadcast_to(cw[rows, 1:2], (rb, LANES))
            for s in range(SUB):
                for r in range(piece * per_piece, (piece + 1) * per_piece):
                    _rows_copy(ys_hbm, y1_scr.at[1 - slot], sems.at[1 - slot],
                               sl1_ref[nbase + r], r, 1).start(priority=0)
                    _rows_copy(ys_hbm, y2_scr.at[1 - slot], sems.at[1 - slot],
                               sl2_ref[nbase + r], r, 1).start(priority=1)
                piece += 1
                cols = slice(s * LANES, (s + 1) * LANES)
                tiles = _tt_piece(s, rb, r0)
                o_ref[rows, cols] = (x1_ref[rows, cols] + c1 * tt1_scr[tiles]
                                     + c2 * tt2_scr[tiles])

        @pl.when(i == last)
        def _():
            _rows_copy(ys_hbm, y1_scr.at[1 - slot], sems.at[1 - slot], 0, 0, TB).wait()
            _rows_copy(ys_hbm, y2_scr.at[1 - slot], sems.at[1 - slot], 0, 0, TB).wait()

    for parity in (0, 1):
        @pl.when(i % 2 == parity)
        def _():
            step(parity)


def _combine(x1, cw, ys, sl1, sl2):
    t = x1.shape[0]
    return pl.pallas_call(
        _combine_kernel,
        grid_spec=pltpu.PrefetchScalarGridSpec(
            num_scalar_prefetch=2,
            grid=(t // TB,),
            in_specs=[pl.BlockSpec((TB, D_MODEL), lambda i, *_: (i, 0)),
                      pl.BlockSpec((8, TB), lambda i, *_: (0, i)),
                      pl.BlockSpec(memory_space=pl.ANY)],
            out_specs=pl.BlockSpec((TB, D_MODEL), lambda i, *_: (i, 0)),
            scratch_shapes=[pltpu.VMEM((2, TB * SUB, LANES), BF16),
                            pltpu.VMEM((2, TB * SUB, LANES), BF16),
                            pltpu.VMEM((TB * TT_PITCH, LANES), F32),
                            pltpu.VMEM((TB * TT_PITCH, LANES), F32),
                            pltpu.SemaphoreType.DMA((2,))],
        ),
        out_shape=jax.ShapeDtypeStruct((t, D_MODEL), F32),
        compiler_params=_params(1),
        name="combine",
    )(sl1, sl2, x1, cw, ys)


def _lane_tile(v, reps):
    return jnp.tile(v.reshape(1, -1), (1, reps))


def kernel(x, positions, norm1_g, w_in, q_norm_g, k_norm_g, lambda_q1, lambda_k1, lambda_q2,
           lambda_k2, diff_out_norm_g, gla_w_gate2, gla_b_gate, gla_out_norm_g, w_out, norm2_g,
           w_router_group, b_router_group, w_router_expert, b_router_expert, w_gate_expert,
           w_up_expert, w_down_expert):
    batch, seq, d = x.shape
    t = batch * seq
    assert d == D_MODEL and w_in.shape == (1, D_MODEL, D_MAIN + GLA_GATE_RANK)
    assert seq % TQ == 0 and seq % GB == 0 and t % TM_IN == 0 and t % TM_OUT == 0 and t % TB_D == 0
    x2 = x.reshape(t, d)

    inv = ROPE_THETA ** (-jnp.arange(0, ROT_DIM, 2, dtype=F32) / ROT_DIM)
    lane_d = jnp.arange(LANES) % DIFF_QK_DIM
    invf = jnp.where(lane_d < ROT_DIM, inv[lane_d % (ROT_DIM // 2)], 0.0).reshape(1, LANES)
    pos_b = jnp.broadcast_to(positions.astype(F32).reshape(t, 1), (t, LANES))
    tabs = _rope_tables(pos_b, invf)

    w_in_t = jnp.swapaxes(w_in, 1, 2)[0]
    wa_t = jnp.pad(w_in_t[D_MAIN:, :], ((0, LANES - GLA_GATE_RANK), (0, 0)))
    proj, ga = _in_proj(x2, norm1_g, w_in_t, wa_t)

    a_out = _diff_attention(
        proj, tabs, _lane_tile(q_norm_g[0], 2), _lane_tile(k_norm_g[0], 2),
        lambda_q1, lambda_k1, lambda_q2, lambda_k2, diff_out_norm_g, batch, seq)
    w2p = jnp.pad(gla_w_gate2[0], ((0, LANES - GLA_GATE_RANK), (0, 0))).astype(BF16)
    g_out = _gla(proj, ga, w2p, gla_b_gate, gla_out_norm_g, batch, seq)

    gpad = GROUP_ROWS - N_GROUPS
    lpad = LANES - N_LOGIT_ROWS
    wr = jnp.concatenate(
        [jnp.pad(w_router_group[0], ((0, 0), (0, gpad))),
         jnp.pad(w_router_expert[0].transpose(1, 0, 2).reshape(d, N_EXPERTS),
                 ((0, 0), (0, lpad)))], axis=1)
    br = jnp.concatenate([jnp.pad(b_router_group[0], (0, gpad)),
                          jnp.pad(b_router_expert[0].reshape(-1), (0, lpad))]).reshape(1, LANES)
    x1, h2, logits_t = _out_proj(a_out, g_out, x2, w_out[0].astype(BF16), norm2_g,
                                 wr.astype(BF16), br)

    slots, cw, counts = _routing(logits_t)
    sl1 = slots[0]
    sl2 = slots[1]
    n_rows = 2 * t + N_EXPERTS * TM_E
    max_tiles = n_rows // TM_E
    cnt = counts[:, 0].astype(jnp.int32)
    tile_end = jnp.cumsum((cnt + (TM_E - 1)) // TM_E)
    n_tiles = tile_end[-1:]
    tile_idx = jnp.minimum(jnp.arange(max_tiles, dtype=jnp.int32), n_tiles[0] - 1)
    tile_exp = jnp.sum(tile_idx[:, None] >= tile_end[None, :], axis=1).astype(jnp.int32)
    row_end = (tile_end * TM_E).astype(jnp.int32)
    tile_first = jnp.concatenate(
        [jnp.ones((1,), jnp.int32), (tile_exp[1:] != tile_exp[:-1]).astype(jnp.int32)])
    tile_slot = (jnp.cumsum(tile_first) - 1) % 2
    nxt = tile_end[tile_exp]
    tile_next = jnp.where(nxt < n_tiles[0], tile_exp[jnp.minimum(nxt, max_tiles - 1)], -1)
    tile_meta = [a.astype(jnp.int32)
                 for a in (tile_idx, tile_exp, tile_first, tile_slot, tile_next, n_tiles)]

    xs = _dispatch(h2, sl1, sl2, row_end, n_rows)
    ys = _expert_mlp(xs, tile_meta,
                     w_gate_expert[0].reshape(N_EXPERTS, d, D_FF),
                     w_up_expert[0].reshape(N_EXPERTS, d, D_FF),
                     w_down_expert[0].reshape(N_EXPERTS, D_FF, d))
    out = _combine(x1, cw, ys, sl1, sl2)
    return out.reshape(batch, seq, d)
```

```python
import math

import jax
import jax.numpy as jnp
from jax import lax
from jax.experimental import pallas as pl
from jax.experimental.pallas import tpu as pltpu

D_MODEL = 2048
CHUNK = 64
DIFF_QK_DIM = 64
DIFF_V_DIM = 128
DIFF_HEADS = 8
ROT_DIM = 16
ROPE_THETA = 500000.0
GLA_HEADS = 4
GLA_V_DIM = 256
GLA_K_DIM = 128
GLA_GATE_RANK = 16
GLA_TAU = 16.0
N_GROUPS = 4
EXPERTS_PER_GROUP = 8
N_EXPERTS = N_GROUPS * EXPERTS_PER_GROUP
D_FF = 512
RMS_EPS = 1e-6
LAMBDA_INIT = 0.8 - 0.6 * math.exp(-0.3 * 0)
D_MAIN = 6144
GROUP_ROWS = 8
N_LOGIT_ROWS = GROUP_ROWS + N_EXPERTS

LANES = 128
SUB = D_MODEL // LANES
TT_PITCH = 24
VMEM_LIMIT = 56 * 1024 * 1024

TB_ROPE = 1024
TM_IN = 1024
TN_IN = 1024
RB_IN = 256
TQ = 256
ATT_HPS = 4
GB = 256
TM_OUT = 512
RB_OUT = 256
TB = 256
RB_COMBINE = 64
TB_D = 1024
TM_E = 256

F32 = jnp.float32
BF16 = jnp.bfloat16


def _params(n_axes):
    return pltpu.CompilerParams(dimension_semantics=("arbitrary",) * n_axes,
                                vmem_limit_bytes=VMEM_LIMIT)


def _nt_dot(a, b):
    return lax.dot_general(a, b, (((1,), (1,)), ((), ())), preferred_element_type=F32)


def _tn_dot(a, b):
    return lax.dot_general(a, b, (((0,), (0,)), ((), ())), preferred_element_type=F32)


def _silu(x):
    return x * (1.0 / (1.0 + jnp.exp(-x)))


def _tt_fill(scr, ref, n, tok0=0):
    for t in range(n):
        scr[t * TT_PITCH:t * TT_PITCH + SUB, :] = ref[pl.ds((tok0 + t) * SUB, SUB), :].astype(F32)


def _tt_drain(ref, scr, n, tok0=0):
    for t in range(n):
        ref[pl.ds((tok0 + t) * SUB, SUB), :] = scr[t * TT_PITCH:t * TT_PITCH + SUB, :].astype(BF16)


def _tt_piece(s, n, tok0=0):
    return (pl.ds(tok0 * TT_PITCH + s, n, stride=TT_PITCH), slice(None))


def _store_token_tiles(ref, val, scr, tok0=0):
    n = val.shape[0]
    for s in range(SUB):
        scr[_tt_piece(s, n)] = val[:, s * LANES:(s + 1) * LANES]
    _tt_drain(ref, scr, n, tok0)


def _load_token_tiles(ref, scr, n, tok0=0):
    _tt_fill(scr, ref, n, tok0)
    return jnp.concatenate([scr[_tt_piece(s, n)] for s in range(SUB)], axis=1)


def _rope_table_kernel(pos_ref, invf_ref, c_ref, s1_ref, s2_ref):
    ang = pos_ref[...] * invf_ref[...]
    d = lax.broadcasted_iota(jnp.int32, ang.shape, 1) % DIFF_QK_DIM
    cos = jnp.cos(ang)
    sin = jnp.sin(ang)
    half = ROT_DIM // 2
    c_ref[...] = jnp.where(d < ROT_DIM, cos, 1.0)
    s1_ref[...] = jnp.where(d < half, -sin, 0.0)
    s2_ref[...] = jnp.where((d >= half) & (d < ROT_DIM), sin, 0.0)


def _rope_tables(pos_b, invf):
    t = pos_b.shape[0]
    spec = pl.BlockSpec((TB_ROPE, LANES), lambda i: (i, 0))
    return pl.pallas_call(
        _rope_table_kernel,
        grid=(t // TB_ROPE,),
        in_specs=[spec, pl.BlockSpec((1, LANES), lambda i: (0, 0))],
        out_specs=[spec, spec, spec],
        out_shape=[jax.ShapeDtypeStruct((t, LANES), F32)] * 3,
        compiler_params=_params(1),
        name="rope_tables",
    )(pos_b, invf)


def _in_proj_kernel(x_ref, g_ref, wt_ref, wat_ref, proj_ref, ga_ref, h_scr):
    @pl.when(pl.program_id(1) == 0)
    def _():
        def body(c, carry):
            rows = pl.ds(c * RB_IN, RB_IN)
            x = x_ref[rows, :]
            ms = jnp.mean(x * x, axis=-1, keepdims=True)
            h_scr[rows, :] = (x * lax.rsqrt(ms + RMS_EPS) * g_ref[...]).astype(BF16)
            return carry
        lax.fori_loop(0, TM_IN // RB_IN, body, 0)
        ga_ref[...] = _nt_dot(h_scr[...], wat_ref[...].astype(BF16))

    proj_ref[...] = _nt_dot(h_scr[...], wt_ref[...].astype(BF16)).astype(BF16)


def _in_proj(x2, g1, w_in, wa):
    t = x2.shape[0]
    return pl.pallas_call(
        _in_proj_kernel,
        grid=(t // TM_IN, D_MAIN // TN_IN),
        in_specs=[
            pl.BlockSpec((TM_IN, D_MODEL), lambda i, j: (i, 0)),
            pl.BlockSpec((1, D_MODEL), lambda i, j: (0, 0)),
            pl.BlockSpec((TN_IN, D_MODEL), lambda i, j: (j, 0)),
            pl.BlockSpec((LANES, D_MODEL), lambda i, j: (0, 0)),
        ],
        out_specs=[
            pl.BlockSpec((TM_IN, TN_IN), lambda i, j: (i, j)),
            pl.BlockSpec((TM_IN, LANES), lambda i, j: (i, 0)),
        ],
        out_shape=[jax.ShapeDtypeStruct((t, D_MAIN), BF16),
                   jax.ShapeDtypeStruct((t, LANES), F32)],
        scratch_shapes=[pltpu.VMEM((TM_IN, D_MODEL), BF16)],
        compiler_params=_params(2),
        name="in_proj",
    )(x2, g1, w_in, wa)


def _norm_rope(x, g, c, s1, s2):
    lo = lax.broadcasted_iota(jnp.int32, x.shape, 1) < DIFF_QK_DIM
    x2 = x * x
    s_lo = jnp.sum(jnp.where(lo, x2, 0.0), axis=-1, keepdims=True)
    s_hi = jnp.sum(jnp.where(lo, 0.0, x2), axis=-1, keepdims=True)
    ms = jnp.where(lo, s_lo, s_hi) * (1.0 / DIFF_QK_DIM)
    y = x * lax.rsqrt(ms + RMS_EPS) * g
    half = ROT_DIM // 2
    return y * c + pltpu.roll(y, LANES - half, 1) * s1 + pltpu.roll(y, half, 1) * s2


def _attn_kernel(q_ref, k_ref, v_ref, c_ref, s1_ref, s2_ref, qg_ref, kg_ref,
                 lq1_ref, lk1_ref, lq2_ref, lk2_ref, og_ref, o_ref,
                 q1_scr, q2_scr, k_scr, v_scr, *bufs):
    s_len = q_ref.shape[0]
    s_bufs, e_bufs = bufs[0:4], bufs[4:8]
    lam = (jnp.exp(jnp.sum(lq1_ref[...] * lk1_ref[...], axis=-1, keepdims=True))
           - jnp.exp(jnp.sum(lq2_ref[...] * lk2_ref[...], axis=-1, keepdims=True))
           + LAMBDA_INIT)
    lo = lax.broadcasted_iota(jnp.int32, (TQ, LANES), 1) < DIFF_QK_DIM
    for hd in range(ATT_HPS):
        v_scr[hd, :, 0:LANES] = v_ref[:, hd * LANES:(hd + 1) * LANES]
        v_scr[hd, :, LANES:] = jnp.ones((s_len, LANES), BF16)

    n_tiles = s_len // TQ
    tiles = [(hd, i) for hd in range(ATT_HPS) for i in range(n_tiles)]

    def prepare(t):
        hd, i = tiles[t]
        rows = slice(i * TQ, (i + 1) * TQ)
        lanes = slice(hd * LANES, (hd + 1) * LANES)
        c, s1, s2 = c_ref[rows, :], s1_ref[rows, :], s2_ref[rows, :]
        qn = (_norm_rope(q_ref[rows, lanes].astype(F32), qg_ref[...], c, s1, s2)
              * (DIFF_QK_DIM ** -0.5 * math.log2(math.e)))
        q1_scr[hd, rows, :] = jnp.where(lo, qn, 0.0).astype(BF16)
        q2_scr[hd, rows, :] = jnp.where(lo, 0.0, qn).astype(BF16)
        k_scr[hd, rows, :] = _norm_rope(k_ref[rows, lanes].astype(F32), kg_ref[...], c, s1,
                                        s2).astype(BF16)

    diag = (lax.broadcasted_iota(jnp.int32, (TQ, TQ), 1) // CHUNK
            <= lax.broadcasted_iota(jnp.int32, (TQ, TQ), 0) // CHUNK)

    items = [(t, q_scr) for t in range(len(tiles)) for q_scr in (q1_scr, q2_scr)]

    def scores(r):
        t, q_scr = items[r]
        hd, i = tiles[t]
        s_scr = s_bufs[r % 4]
        q = q_scr[hd, i * TQ:(i + 1) * TQ, :]
        n_off = i * TQ
        s_scr[:, n_off:n_off + TQ] = jnp.where(
            diag, _nt_dot(q, k_scr[hd, n_off:n_off + TQ, :]), -jnp.inf)
        if n_off:
            s_scr[:, 0:n_off] = _nt_dot(q, k_scr[hd, 0:n_off, :])

    def softmax_pv(r):
        hd, i = tiles[items[r][0]]
        nk = (i + 1) * TQ
        s_scr, e_scr = s_bufs[r % 4], e_bufs[r % 4]
        m = jnp.max(s_scr[:, 0:nk], axis=-1, keepdims=True)
        e_scr[:, 0:nk] = jnp.exp2(s_scr[:, 0:nk] - m).astype(BF16)
        acc = jnp.dot(e_scr[:, 0:nk], v_scr[hd, 0:nk, :], preferred_element_type=F32)
        return acc[:, 0:LANES] / acc[:, LANES:]

    prepare(0)
    scores(0)
    scores(1)
    for t, (hd, i) in enumerate(tiles):
        if t + 1 < len(tiles):
            prepare(t + 1)
            scores(2 * t + 2)
        o1 = softmax_pv(2 * t)
        if t + 1 < len(tiles):
            scores(2 * t + 3)
        o = o1 - lam * softmax_pv(2 * t + 1)
        ms = jnp.mean(o * o, axis=-1, keepdims=True)
        y = o * lax.rsqrt(ms + RMS_EPS) * og_ref[...] * (1.0 - LAMBDA_INIT)
        o_ref[i * TQ:(i + 1) * TQ, hd * LANES:(hd + 1) * LANES] = y.astype(BF16)


def _diff_attention(proj, tabs, qg, kg, lq1, lk1, lq2, lk2, og, batch, seq):
    c, s1, s2 = tabs
    h = DIFF_HEADS
    hg = h // ATT_HPS
    wide = ATT_HPS * LANES
    blk = lambda off: pl.BlockSpec((seq, wide), lambda b, hh, off=off: (b, off + hh))
    tab = pl.BlockSpec((seq, LANES), lambda b, hh: (b, 0))
    vec = lambda n: pl.BlockSpec((1, n), lambda b, hh: (0, 0))
    return pl.pallas_call(
        _attn_kernel,
        grid=(batch, hg),
        in_specs=[blk(0), blk(hg), blk(2 * hg), tab, tab, tab,
                  vec(LANES), vec(LANES), vec(DIFF_QK_DIM), vec(DIFF_QK_DIM),
                  vec(DIFF_QK_DIM), vec(DIFF_QK_DIM), vec(LANES)],
        out_specs=pl.BlockSpec((seq, wide), lambda b, hh: (b, hh)),
        out_shape=jax.ShapeDtypeStruct((batch * seq, h * DIFF_V_DIM), BF16),
        scratch_shapes=[pltpu.VMEM((ATT_HPS, seq, LANES), BF16)] * 3
        + [pltpu.VMEM((ATT_HPS, seq, 2 * LANES), BF16)]
        + [pltpu.VMEM((TQ, seq), F32)] * 4 + [pltpu.VMEM((TQ, seq), BF16)] * 4,
        compiler_params=_params(2),
        name="diff_attention",
    )(proj, proj, proj, c, s1, s2, qg, kg, lq1, lk1, lq2, lk2, og)


def _split_dot(ones_bf, x):
    hi = x.astype(BF16)
    lo = (x - hi.astype(F32)).astype(BF16)
    return (jnp.dot(ones_bf, hi, preferred_element_type=F32)
            + jnp.dot(ones_bf, lo, preferred_element_type=F32))


def _gla_kernel(q_ref, k_ref, v_ref, r_ref, ga_ref, w2_ref, b2_ref, og_ref, o_ref,
                qin_scr, ut_scr, dec_scr, st_scr, acc_scr):
    s_len = q_ref.shape[0]
    cpg = GB // CHUNK
    ri = lax.broadcasted_iota(jnp.int32, (GB, GB), 0)
    ci = lax.broadcasted_iota(jnp.int32, (GB, GB), 1)
    same = (ri // CHUNK) == (ci // CHUNK)
    blk_ones = jnp.where(same, 1.0, 0.0).astype(BF16)
    tril = same & (ci <= ri)
    tri_ones = jnp.where(tril, 1.0, 0.0).astype(BF16)

    def phase_a(g, carry):
        rows = pl.ds(pl.multiple_of(g * GB, GB), GB)
        pre = jnp.dot(ga_ref[rows, :].astype(BF16), w2_ref[...],
                      preferred_element_type=F32) + b2_ref[...]
        la = -(jnp.maximum(-pre, 0.0) + jnp.log1p(jnp.exp(-jnp.abs(pre)))) * (1.0 / GLA_TAU)
        bc = _split_dot(tri_ones, la)
        bl = _split_dot(blk_ones, la)
        e_neg = jnp.exp(-bc)
        e_last = jnp.exp(bl)
        k = k_ref[rows, :].astype(F32)
        q_in = (q_ref[rows, :].astype(F32) * (GLA_K_DIM ** -0.5) * jnp.exp(bc)).astype(BF16)
        k_in = (k * e_neg).astype(BF16)
        k_dec = (k * (e_last * e_neg)).astype(BF16)
        qin_scr[rows, :] = q_in
        dec_scr[rows, :] = e_last
        v = v_ref[rows, :]
        att = jnp.where(tril, _nt_dot(q_in, k_in), 0.0).astype(BF16)
        acc_scr[rows, :] = jnp.dot(att, v, preferred_element_type=F32)
        for c in range(cpg):
            cr = slice(c * CHUNK, (c + 1) * CHUNK)
            ut_scr[g * cpg + c] = _tn_dot(v[cr, :], k_dec[cr, :])
        return carry

    lax.fori_loop(0, s_len // GB, phase_a, 0, unroll=8)

    def phase_b(c, st):
        st_scr[c] = st.astype(BF16)
        dec = dec_scr[pl.ds(pl.multiple_of(c * CHUNK, CHUNK), 1), :]
        return dec * st + ut_scr[c]

    lax.fori_loop(0, s_len // CHUNK, phase_b, jnp.zeros((GLA_V_DIM, GLA_K_DIM), F32))

    def phase_c(g, carry):
        rows = pl.ds(pl.multiple_of(g * GB, GB), GB)
        inter = [_nt_dot(qin_scr[pl.ds(pl.multiple_of(g * GB + c * CHUNK, CHUNK), CHUNK), :],
                         st_scr[g * cpg + c]) for c in range(cpg)]
        o = acc_scr[rows, :] + jnp.concatenate(inter, axis=0)
        ms = jnp.mean(o * o, axis=-1, keepdims=True)
        y = o * lax.rsqrt(ms + RMS_EPS) * og_ref[...]
        o_ref[rows, :] = (y * _silu(r_ref[rows, :].astype(F32))).astype(BF16)
        return carry

    lax.fori_loop(0, s_len // GB, phase_c, 0, unroll=8)


def _gla(proj, ga, w2p, b2, og, batch, seq):
    hq = 3 * DIFF_HEADS
    kblk = lambda off: pl.BlockSpec((seq, GLA_K_DIM), lambda b, hh, off=off: (b, off + hh))
    vblk = lambda off: pl.BlockSpec((seq, GLA_V_DIM), lambda b, hh, off=off: (b, off + hh))
    return pl.pallas_call(
        _gla_kernel,
        grid=(batch, GLA_HEADS),
        in_specs=[kblk(hq), kblk(hq + GLA_HEADS), vblk(16), vblk(16 + GLA_HEADS),
                  pl.BlockSpec((seq, LANES), lambda b, hh: (b, 0)),
                  pl.BlockSpec((LANES, GLA_K_DIM), lambda b, hh: (0, hh)),
                  pl.BlockSpec((1, GLA_K_DIM), lambda b, hh: (0, hh)),
                  pl.BlockSpec((1, GLA_V_DIM), lambda b, hh: (0, 0))],
        out_specs=pl.BlockSpec((seq, GLA_V_DIM), lambda b, hh: (b, hh)),
        out_shape=jax.ShapeDtypeStruct((batch * seq, GLA_HEADS * GLA_V_DIM), BF16),
        scratch_shapes=[pltpu.VMEM((seq, GLA_K_DIM), BF16),
                        pltpu.VMEM((seq // CHUNK, GLA_V_DIM, GLA_K_DIM), F32),
                        pltpu.VMEM((seq, GLA_K_DIM), F32),
                        pltpu.VMEM((seq // CHUNK, GLA_V_DIM, GLA_K_DIM), BF16),
                        pltpu.VMEM((seq, GLA_V_DIM), F32)],
        compiler_params=_params(2),
        name="gla",
    )(proj, proj, proj, proj, ga, w2p, b2, og)


def _out_proj_kernel(a_ref, g_ref, x_ref, wo_ref, g2_ref, wr_ref, br_ref,
                     x1_ref, h2_ref, lg_ref, tt_scr):
    half = a_ref.shape[1]

    def body(c, carry):
        r0 = pl.multiple_of(c * RB_OUT, RB_OUT)
        rows = pl.ds(r0, RB_OUT)
        mixed = (jnp.dot(a_ref[rows, :], wo_ref[0:half, :], preferred_element_type=F32)
                 + jnp.dot(g_ref[rows, :], wo_ref[half:, :], preferred_element_type=F32))
        x1 = x_ref[rows, :] + mixed
        x1_ref[rows, :] = x1
        ms = jnp.mean(x1 * x1, axis=-1, keepdims=True)
        h2 = x1 * lax.rsqrt(ms + RMS_EPS) * g2_ref[...]
        _store_token_tiles(h2_ref, h2, tt_scr, r0)
        lg = jnp.dot(h2.astype(BF16), wr_ref[...], preferred_element_type=F32) + br_ref[...]
        lg_ref[:, rows] = lg.T[0:N_LOGIT_ROWS, :]
        return carry

    lax.fori_loop(0, TM_OUT // RB_OUT, body, 0)


def _out_proj(a_out, g_out, x2, wo, g2, wr, br):
    t = x2.shape[0]
    half = a_out.shape[1]
    row = lambda n: pl.BlockSpec((TM_OUT, n), lambda i: (i, 0))
    full = lambda r, n: pl.BlockSpec((r, n), lambda i: (0, 0))
    return pl.pallas_call(
        _out_proj_kernel,
        grid=(t // TM_OUT,),
        in_specs=[row(half), row(half), row(D_MODEL), full(D_MODEL, D_MODEL),
                  full(1, D_MODEL), full(D_MODEL, LANES), full(1, LANES)],
        out_specs=[row(D_MODEL), pl.BlockSpec((TM_OUT * SUB, LANES), lambda i: (i, 0)),
                   pl.BlockSpec((N_LOGIT_ROWS, TM_OUT), lambda i: (0, i))],
        out_shape=[jax.ShapeDtypeStruct((t, D_MODEL), F32),
                   jax.ShapeDtypeStruct((t * SUB, LANES), BF16),
                   jax.ShapeDtypeStruct((N_LOGIT_ROWS, t), F32)],
        scratch_shapes=[pltpu.VMEM((RB_OUT * TT_PITCH, LANES), F32)],
        compiler_params=_params(1),
        name="out_proj_router",
    )(a_out, g_out, x2, wo, g2, wr, br)


def _first_argmax(vals, row_f):
    m = jnp.max(vals, axis=0, keepdims=True)
    idx = jnp.min(jnp.where(vals == m, row_f, float(vals.shape[0])), axis=0, keepdims=True)
    return m, idx


def _softmax_rows(v):
    ex = jnp.exp(v - jnp.max(v, axis=0, keepdims=True))
    return ex / jnp.sum(ex, axis=0, keepdims=True)


def _routing_kernel(lg_ref, slots_ref, cw_ref, cnt_ref, tok_scr):
    t = lg_ref.shape[1]
    epg = EXPERTS_PER_GROUP
    row8 = lax.broadcasted_iota(jnp.int32, (8, TB), 0).astype(F32)
    row_e = lax.broadcasted_iota(jnp.int32, (N_EXPERTS, TB), 0).astype(F32)
    ui = lax.broadcasted_iota(jnp.int32, (TB, TB), 0)
    uj = lax.broadcasted_iota(jnp.int32, (TB, TB), 1)
    earlier = jnp.where(ui < uj, 1.0, 0.0).astype(BF16)
    all_ones = jnp.ones((TB, TB), BF16)

    def rows01(r0, r1):
        return jnp.where(row8 == 0.0, r0, jnp.where(row8 == 1.0, r1, jnp.zeros_like(r0)))

    def pass1(b, counts):
        cols = pl.ds(pl.multiple_of(b * TB, TB), TB)
        lg = lg_ref[:, cols]
        lg_groups = jnp.where(row8 < float(N_GROUPS), lg[0:GROUP_ROWS, :], -jnp.inf)
        pg_sel, g_sel = _first_argmax(_softmax_rows(lg_groups), row8)
        le = lg[GROUP_ROWS:GROUP_ROWS + epg, :]
        for g in range(1, N_GROUPS):
            le = jnp.where(g_sel == float(g),
                           lg[GROUP_ROWS + g * epg:GROUP_ROWS + (g + 1) * epg, :], le)
        pe = _softmax_rows(le)
        v1, i1 = _first_argmax(pe, row8)
        v2, i2 = _first_argmax(jnp.where(row8 == i1, -jnp.inf, pe), row8)
        tot = v1 + v2
        e1 = g_sel * epg + i1
        e2 = g_sel * epg + i2
        oh1 = row_e == e1
        oh2 = row_e == e2
        a = jnp.where(oh1 | oh2, 1.0, 0.0).astype(BF16)
        rank = jnp.dot(a, earlier, preferred_element_type=F32) + counts
        r1 = jnp.sum(jnp.where(oh1, rank, 0.0), axis=0, keepdims=True)
        r2 = jnp.sum(jnp.where(oh2, rank, 0.0), axis=0, keepdims=True)
        tok_scr[:, cols] = jnp.where(row8 == 0.0, e1, jnp.where(
            row8 == 1.0, e2, jnp.where(row8 == 2.0, r1, jnp.where(row8 == 3.0, r2, 0.0))))
        cw_ref[:, cols] = rows01((v1 / tot) * pg_sel, (v2 / tot) * pg_sel)
        return counts + jnp.dot(a, all_ones, preferred_element_type=F32)

    counts = lax.fori_loop(0, t // TB, pass1, jnp.zeros((N_EXPERTS, TB), F32))
    cnt_ref[...] = counts[:, 0:LANES]
    n_tiles = jnp.floor((counts + (TM_E - 1)) * (1.0 / TM_E))
    li = lax.broadcasted_iota(jnp.int32, (N_EXPERTS, N_EXPERTS), 0)
    lj = lax.broadcasted_iota(jnp.int32, (N_EXPERTS, N_EXPERTS), 1)
    lower = jnp.where(lj < li, 1.0, 0.0).astype(BF16)
    row_off = jnp.dot(lower, n_tiles.astype(BF16), preferred_element_type=F32) * TM_E

    def pass2(b, carry):
        cols = pl.ds(pl.multiple_of(b * TB, TB), TB)
        tok = tok_scr[:, cols]
        off1 = jnp.sum(jnp.where(row_e == tok[0:1, :], row_off, 0.0), axis=0, keepdims=True)
        off2 = jnp.sum(jnp.where(row_e == tok[1:2, :], row_off, 0.0), axis=0, keepdims=True)
        slots_ref[:, cols] = rows01(off1 + tok[2:3, :], off2 + tok[3:4, :]).astype(jnp.int32)
        return carry

    lax.fori_loop(0, t // TB, pass2, 0)


def _routing(logits_t):
    t = logits_t.shape[1]
    return pl.pallas_call(
        _routing_kernel,
        out_shape=[jax.ShapeDtypeStruct((8, t), jnp.int32),
                   jax.ShapeDtypeStruct((8, t), F32),
                   jax.ShapeDtypeStruct((N_EXPERTS, LANES), F32)],
        scratch_shapes=[pltpu.VMEM((8, t), F32)],
        compiler_params=pltpu.CompilerParams(vmem_limit_bytes=VMEM_LIMIT),
        name="routing",
    )(logits_t)


def _rows_copy(src, dst, sem, src_tok, dst_tok, n):
    first = lambda tok: tok * SUB if isinstance(tok, int) else pl.multiple_of(tok * SUB, SUB)
    s0 = first(src_tok)
    d0 = first(dst_tok)
    return pltpu.make_async_copy(src.at[pl.ds(s0, n * SUB), :], dst.at[pl.ds(d0, n * SUB), :], sem)


def _dispatch_kernel(sl1_ref, sl2_ref, ends_ref, h2_ref, xs_hbm, zero_scr, sem):
    step = pl.program_id(0)

    @pl.when(step == 0)
    def _():
        zero_scr[...] = jnp.zeros_like(zero_scr)
        for e in range(N_EXPERTS):
            start = jnp.maximum(ends_ref[e] - TM_E, 0)
            _rows_copy(zero_scr, xs_hbm, sem, 0, start, TM_E).start()
        for e in range(N_EXPERTS):
            _rows_copy(zero_scr, xs_hbm, sem, 0, 0, TM_E).wait()

    base = step * TB_D

    def issue(r, carry):
        _rows_copy(h2_ref, xs_hbm, sem, r, sl1_ref[base + r], 1).start(priority=0)
        _rows_copy(h2_ref, xs_hbm, sem, r, sl2_ref[base + r], 1).start(priority=1)
        return carry

    lax.fori_loop(0, TB_D, issue, 0, unroll=8)
    _rows_copy(h2_ref, xs_hbm, sem, 0, 0, TB_D).wait()
    _rows_copy(h2_ref, xs_hbm, sem, 0, 0, TB_D).wait()


def _dispatch(h2, sl1, sl2, ends, n_rows):
    t = h2.shape[0] // SUB
    return pl.pallas_call(
        _dispatch_kernel,
        grid_spec=pltpu.PrefetchScalarGridSpec(
            num_scalar_prefetch=3,
            grid=(t // TB_D,),
            in_specs=[pl.BlockSpec((TB_D * SUB, LANES), lambda i, *_: (i, 0))],
            out_specs=pl.BlockSpec(memory_space=pl.ANY),
            scratch_shapes=[pltpu.VMEM((TM_E * SUB, LANES), BF16), pltpu.SemaphoreType.DMA(())],
        ),
        out_shape=jax.ShapeDtypeStruct((n_rows * SUB, LANES), BF16),
        compiler_params=_params(1),
        name="dispatch",
    )(sl1, sl2, ends, h2)


def _expert_kernel(tidx_ref, texp_ref, tfirst_ref, tslot_ref, tnext_ref, nt_ref,
                   xs_ref, wg_hbm, wu_hbm, wd_hbm, ys_ref,
                   wg_buf, wu_buf, wd_buf, wg_bf, wu_bf, wd_bf, tt_scr, sems):
    j = pl.program_id(0)

    def weight_copies(e, slot):
        return [pltpu.make_async_copy(hbm.at[e], buf.at[slot], sems.at[slot, k])
                for k, (hbm, buf) in enumerate(((wg_hbm, wg_buf), (wu_hbm, wu_buf),
                                                (wd_hbm, wd_buf)))]

    @pl.when(j < nt_ref[0])
    def _():
        slot = tslot_ref[j]

        @pl.when(tfirst_ref[j] == 1)
        def _():
            @pl.when(j == 0)
            def _():
                for c in weight_copies(texp_ref[j], slot):
                    c.start()
            for c in weight_copies(texp_ref[j], slot):
                c.wait()

            @pl.when(tnext_ref[j] >= 0)
            def _():
                for c in weight_copies(tnext_ref[j], 1 - slot):
                    c.start()
            wg_bf[...] = wg_buf[slot].astype(BF16)
            wu_bf[...] = wu_buf[slot].astype(BF16)
            wd_bf[...] = wd_buf[slot].astype(BF16)

        xb = _load_token_tiles(xs_ref, tt_scr, TM_E).astype(BF16)
        g = jnp.dot(xb, wg_bf[...], preferred_element_type=F32)
        u = jnp.dot(xb, wu_bf[...], preferred_element_type=F32)
        h = (_silu(g) * u).astype(BF16)
        _store_token_tiles(ys_ref, jnp.dot(h, wd_bf[...], preferred_element_type=F32), tt_scr)


def _expert_mlp(xs, tile_meta, wg, wu, wd):
    n_rows = xs.shape[0] // SUB
    tile = pl.BlockSpec((TM_E * SUB, LANES), lambda j, ti, *_: (ti[j], 0))
    hbm = pl.BlockSpec(memory_space=pl.ANY)
    return pl.pallas_call(
        _expert_kernel,
        grid_spec=pltpu.PrefetchScalarGridSpec(
            num_scalar_prefetch=len(tile_meta),
            grid=(n_rows // TM_E,),
            in_specs=[tile, hbm, hbm, hbm],
            out_specs=tile,
            scratch_shapes=[pltpu.VMEM((2, D_MODEL, D_FF), F32),
                            pltpu.VMEM((2, D_MODEL, D_FF), F32),
                            pltpu.VMEM((2, D_FF, D_MODEL), F32),
                            pltpu.VMEM((D_MODEL, D_FF), BF16),
                            pltpu.VMEM((D_MODEL, D_FF), BF16),
                            pltpu.VMEM((D_FF, D_MODEL), BF16),
                            pltpu.VMEM((TM_E * TT_PITCH, LANES), F32),
                            pltpu.SemaphoreType.DMA((2, 3))],
        ),
        out_shape=jax.ShapeDtypeStruct((n_rows * SUB, LANES), BF16),
        compiler_params=_params(1),
        name="expert_mlp",
    )(*tile_meta, xs, wg, wu, wd)


def _combine_kernel(sl1_ref, sl2_ref, x1_ref, cw_ref, ys_hbm, o_ref, y1_scr, y2_scr, tt1_scr,
                    tt2_scr, sems):
    i = pl.program_id(0)

    def issue(step, slot):
        base = step * TB

        def body(r, carry):
            _rows_copy(ys_hbm, y1_scr.at[slot], sems.at[slot], sl1_ref[base + r], r,
                       1).start(priority=0)
            _rows_copy(ys_hbm, y2_scr.at[slot], sems.at[slot], sl2_ref[base + r], r,
                       1).start(priority=1)
            return carry

        lax.fori_loop(0, TB, body, 0, unroll=8)

    @pl.when(i == 0)
    def _():
        issue(0, 0)

    last = pl.num_programs(0) - 1
    rb = RB_COMBINE
    n_pieces = (TB // rb) * SUB
    per_piece = TB // n_pieces

    def step(slot):
        _rows_copy(ys_hbm, y1_scr.at[slot], sems.at[slot], 0, 0, TB).wait()
        _rows_copy(ys_hbm, y2_scr.at[slot], sems.at[slot], 0, 0, TB).wait()
        nbase = jnp.minimum(i + 1, last) * TB
        cw = jnp.concatenate([cw_ref[...], jnp.zeros((LANES - 8, TB), F32)], axis=0).T
        _tt_fill(tt1_scr, y1_scr.at[slot], TB)
        _tt_fill(tt2_scr, y2_scr.at[slot], TB)
        piece = 0
        for r0 in range(0, TB, rb):
            rows = slice(r0, r0 + rb)
            c1 = jnp.broadcast_to(cw[rows, 0:1], (rb, LANES))
            c2 = jnp.broadcast_to(cw[rows, 1:2], (rb, LANES))
            for s in range(SUB):
                for r in range(piece * per_piece, (piece + 1) * per_piece):
                    _rows_copy(ys_hbm, y1_scr.at[1 - slot], sems.at[1 - slot],
                               sl1_ref[nbase + r], r, 1).start(priority=0)
                    _rows_copy(ys_hbm, y2_scr.at[1 - slot], sems.at[1 - slot],
                               sl2_ref[nbase + r], r, 1).start(priority=1)
                piece += 1
                cols = slice(s * LANES, (s + 1) * LANES)
                tiles = _tt_piece(s, rb, r0)
                o_ref[rows, cols] = (x1_ref[rows, cols] + c1 * tt1_scr[tiles]
                                     + c2 * tt2_scr[tiles])

        @pl.when(i == last)
        def _():
            _rows_copy(ys_hbm, y1_scr.at[1 - slot], sems.at[1 - slot], 0, 0, TB).wait()
            _rows_copy(ys_hbm, y2_scr.at[1 - slot], sems.at[1 - slot], 0, 0, TB).wait()

    for parity in (0, 1):
        @pl.when(i % 2 == parity)
        def _():
            step(parity)


def _combine(x1, cw, ys, sl1, sl2):
    t = x1.shape[0]
    return pl.pallas_call(
        _combine_kernel,
        grid_spec=pltpu.PrefetchScalarGridSpec(
            num_scalar_prefetch=2,
            grid=(t // TB,),
            in_specs=[pl.BlockSpec((TB, D_MODEL), lambda i, *_: (i, 0)),
                      pl.BlockSpec((8, TB), lambda i, *_: (0, i)),
                      pl.BlockSpec(memory_space=pl.ANY)],
            out_specs=pl.BlockSpec((TB, D_MODEL), lambda i, *_: (i, 0)),
            scratch_shapes=[pltpu.VMEM((2, TB * SUB, LANES), BF16),
                            pltpu.VMEM((2, TB * SUB, LANES), BF16),
                            pltpu.VMEM((TB * TT_PITCH, LANES), F32),
                            pltpu.VMEM((TB * TT_PITCH, LANES), F32),
                            pltpu.SemaphoreType.DMA((2,))],
        ),
        out_shape=jax.ShapeDtypeStruct((t, D_MODEL), F32),
        compiler_params=_params(1),
        name="combine",
    )(sl1, sl2, x1, cw, ys)


def _lane_tile(v, reps):
    return jnp.tile(v.reshape(1, -1), (1, reps))


def kernel(x, positions, norm1_g, w_in, q_norm_g, k_norm_g, lambda_q1, lambda_k1, lambda_q2,
           lambda_k2, diff_out_norm_g, gla_w_gate2, gla_b_gate, gla_out_norm_g, w_out, norm2_g,
           w_router_group, b_router_group, w_router_expert, b_router_expert, w_gate_expert,
           w_up_expert, w_down_expert):
    batch, seq, d = x.shape
    t = batch * seq
    assert d == D_MODEL and w_in.shape == (1, D_MODEL, D_MAIN + GLA_GATE_RANK)
    assert seq % TQ == 0 and seq % GB == 0 and t % TM_IN == 0 and t % TM_OUT == 0 and t % TB_D == 0
    x2 = x.reshape(t, d)

    inv = ROPE_THETA ** (-jnp.arange(0, ROT_DIM, 2, dtype=F32) / ROT_DIM)
    lane_d = jnp.arange(LANES) % DIFF_QK_DIM
    invf = jnp.where(lane_d < ROT_DIM, inv[lane_d % (ROT_DIM // 2)], 0.0).reshape(1, LANES)
    pos_b = jnp.broadcast_to(positions.astype(F32).reshape(t, 1), (t, LANES))
    tabs = _rope_tables(pos_b, invf)

    w_in_t = jnp.swapaxes(w_in, 1, 2)[0]
    wa_t = jnp.pad(w_in_t[D_MAIN:, :], ((0, LANES - GLA_GATE_RANK), (0, 0)))
    proj, ga = _in_proj(x2, norm1_g, w_in_t, wa_t)

    a_out = _diff_attention(
        proj, tabs, _lane_tile(q_norm_g[0], 2), _lane_tile(k_norm_g[0], 2),
        lambda_q1, lambda_k1, lambda_q2, lambda_k2, diff_out_norm_g, batch, seq)
    w2p = jnp.pad(gla_w_gate2[0], ((0, LANES - GLA_GATE_RANK), (0, 0))).astype(BF16)
    g_out = _gla(proj, ga, w2p, gla_b_gate, gla_out_norm_g, batch, seq)

    gpad = GROUP_ROWS - N_GROUPS
    lpad = LANES - N_LOGIT_ROWS
    wr = jnp.concatenate(
        [jnp.pad(w_router_group[0], ((0, 0), (0, gpad))),
         jnp.pad(w_router_expert[0].transpose(1, 0, 2).reshape(d, N_EXPERTS),
                 ((0, 0), (0, lpad)))], axis=1)
    br = jnp.concatenate([jnp.pad(b_router_group[0], (0, gpad)),
                          jnp.pad(b_router_expert[0].reshape(-1), (0, lpad))]).reshape(1, LANES)
    x1, h2, logits_t = _out_proj(a_out, g_out, x2, w_out[0].astype(BF16), norm2_g,
                                 wr.astype(BF16), br)

    slots, cw, counts = _routing(logits_t)
    sl1 = slots[0]
    sl2 = slots[1]
    n_rows = 2 * t + N_EXPERTS * TM_E
    max_tiles = n_rows // TM_E
    cnt = counts[:, 0].astype(jnp.int32)
    tile_end = jnp.cumsum((cnt + (TM_E - 1)) // TM_E)
    n_tiles = tile_end[-1:]
    tile_idx = jnp.minimum(jnp.arange(max_tiles, dtype=jnp.int32), n_tiles[0] - 1)
    tile_exp = jnp.sum(tile_idx[:, None] >= tile_end[None, :], axis=1).astype(jnp.int32)
    row_end = (tile_end * TM_E).astype(jnp.int32)
    tile_first = jnp.concatenate(
        [jnp.ones((1,), jnp.int32), (tile_exp[1:] != tile_exp[:-1]).astype(jnp.int32)])
    tile_slot = (jnp.cumsum(tile_first) - 1) % 2
    nxt = tile_end[tile_exp]
    tile_next = jnp.where(nxt < n_tiles[0], tile_exp[jnp.minimum(nxt, max_tiles - 1)], -1)
    tile_meta = [a.astype(jnp.int32)
                 for a in (tile_idx, tile_exp, tile_first, tile_slot, tile_next, n_tiles)]

    xs = _dispatch(h2, sl1, sl2, row_end, n_rows)
    ys = _expert_mlp(xs, tile_meta,
                     w_gate_expert[0].reshape(N_EXPERTS, d, D_FF),
                     w_up_expert[0].reshape(N_EXPERTS, d, D_FF),
                     w_down_expert[0].reshape(N_EXPERTS, D_FF, d))
    out = _combine(x1, cw, ys, sl1, sl2)
    return out.reshape(batch, seq, d)
```

```python
import math

import jax
import jax.numpy as jnp
from jax import lax
from jax.experimental import pallas as pl
from jax.experimental.pallas import tpu as pltpu

D_MODEL = 2048
CHUNK = 64
DIFF_QK_DIM = 64
DIFF_V_DIM = 128
DIFF_HEADS = 8
ROT_DIM = 16
ROPE_THETA = 500000.0
GLA_HEADS = 4
GLA_V_DIM = 256
GLA_K_DIM = 128
GLA_GATE_RANK = 16
GLA_TAU = 16.0
N_GROUPS = 4
EXPERTS_PER_GROUP = 8
N_EXPERTS = N_GROUPS * EXPERTS_PER_GROUP
D_FF = 512
RMS_EPS = 1e-6
LAMBDA_INIT = 0.8 - 0.6 * math.exp(-0.3 * 0)
D_MAIN = 6144
GROUP_ROWS = 8
N_LOGIT_ROWS = GROUP_ROWS + N_EXPERTS

LANES = 128
SUB = D_MODEL // LANES
TT_PITCH = 24
VMEM_LIMIT = 56 * 1024 * 1024

TB_ROPE = 1024
TM_IN = 1024
TN_IN = 1024
RB_IN = 256
TQ = 256
ATT_HPS = 2
GB = 256
TM_OUT = 512
RB_OUT = 256
TB = 256
RB_COMBINE = 64
TB_D = 1024
TM_E = 256

F32 = jnp.float32
BF16 = jnp.bfloat16


def _params(n_axes):
    return pltpu.CompilerParams(dimension_semantics=("arbitrary",) * n_axes,
                                vmem_limit_bytes=VMEM_LIMIT)


def _nt_dot(a, b):
    return lax.dot_general(a, b, (((1,), (1,)), ((), ())), preferred_element_type=F32)


def _tn_dot(a, b):
    return lax.dot_general(a, b, (((0,), (0,)), ((), ())), preferred_element_type=F32)


def _silu(x):
    return x * (1.0 / (1.0 + jnp.exp(-x)))


def _tt_fill(scr, ref, n, tok0=0):
    for t in range(n):
        scr[t * TT_PITCH:t * TT_PITCH + SUB, :] = ref[pl.ds((tok0 + t) * SUB, SUB), :].astype(F32)


def _tt_drain(ref, scr, n, tok0=0):
    for t in range(n):
        ref[pl.ds((tok0 + t) * SUB, SUB), :] = scr[t * TT_PITCH:t * TT_PITCH + SUB, :].astype(BF16)


def _tt_piece(s, n, tok0=0):
    return (pl.ds(tok0 * TT_PITCH + s, n, stride=TT_PITCH), slice(None))


def _store_token_tiles(ref, val, scr, tok0=0):
    n = val.shape[0]
    for s in range(SUB):
        scr[_tt_piece(s, n)] = val[:, s * LANES:(s + 1) * LANES]
    _tt_drain(ref, scr, n, tok0)


def _load_token_tiles(ref, scr, n, tok0=0):
    _tt_fill(scr, ref, n, tok0)
    return jnp.concatenate([scr[_tt_piece(s, n)] for s in range(SUB)], axis=1)


def _rope_table_kernel(pos_ref, invf_ref, c_ref, s1_ref, s2_ref):
    ang = pos_ref[...] * invf_ref[...]
    d = lax.broadcasted_iota(jnp.int32, ang.shape, 1) % DIFF_QK_DIM
    cos = jnp.cos(ang)
    sin = jnp.sin(ang)
    half = ROT_DIM // 2
    c_ref[...] = jnp.where(d < ROT_DIM, cos, 1.0)
    s1_ref[...] = jnp.where(d < half, -sin, 0.0)
    s2_ref[...] = jnp.where((d >= half) & (d < ROT_DIM), sin, 0.0)


def _rope_tables(pos_b, invf):
    t = pos_b.shape[0]
    spec = pl.BlockSpec((TB_ROPE, LANES), lambda i: (i, 0))
    return pl.pallas_call(
        _rope_table_kernel,
        grid=(t // TB_ROPE,),
        in_specs=[spec, pl.BlockSpec((1, LANES), lambda i: (0, 0))],
        out_specs=[spec, spec, spec],
        out_shape=[jax.ShapeDtypeStruct((t, LANES), F32)] * 3,
        compiler_params=_params(1),
        name="rope_tables",
    )(pos_b, invf)


def _in_proj_kernel(x_ref, g_ref, wt_ref, wat_ref, proj_ref, ga_ref, h_scr):
    @pl.when(pl.program_id(1) == 0)
    def _():
        def body(c, carry):
            rows = pl.ds(c * RB_IN, RB_IN)
            x = x_ref[rows, :]
            ms = jnp.mean(x * x, axis=-1, keepdims=True)
            h_scr[rows, :] = (x * lax.rsqrt(ms + RMS_EPS) * g_ref[...]).astype(BF16)
            return carry
        lax.fori_loop(0, TM_IN // RB_IN, body, 0)
        ga_ref[...] = _nt_dot(h_scr[...], wat_ref[...].astype(BF16))

    proj_ref[...] = _nt_dot(h_scr[...], wt_ref[...].astype(BF16)).astype(BF16)


def _in_proj(x2, g1, w_in, wa):
    t = x2.shape[0]
    return pl.pallas_call(
        _in_proj_kernel,
        grid=(t // TM_IN, D_MAIN // TN_IN),
        in_specs=[
            pl.BlockSpec((TM_IN, D_MODEL), lambda i, j: (i, 0)),
            pl.BlockSpec((1, D_MODEL), lambda i, j: (0, 0)),
            pl.BlockSpec((TN_IN, D_MODEL), lambda i, j: (j, 0)),
            pl.BlockSpec((LANES, D_MODEL), lambda i, j: (0, 0)),
        ],
        out_specs=[
            pl.BlockSpec((TM_IN, TN_IN), lambda i, j: (i, j)),
            pl.BlockSpec((TM_IN, LANES), lambda i, j: (i, 0)),
        ],
        out_shape=[jax.ShapeDtypeStruct((t, D_MAIN), BF16),
                   jax.ShapeDtypeStruct((t, LANES), F32)],
        scratch_shapes=[pltpu.VMEM((TM_IN, D_MODEL), BF16)],
        compiler_params=_params(2),
        name="in_proj",
    )(x2, g1, w_in, wa)


def _norm_rope(x, g, c, s1, s2):
    lo = lax.broadcasted_iota(jnp.int32, x.shape, 1) < DIFF_QK_DIM
    x2 = x * x
    s_lo = jnp.sum(jnp.where(lo, x2, 0.0), axis=-1, keepdims=True)
    s_hi = jnp.sum(jnp.where(lo, 0.0, x2), axis=-1, keepdims=True)
    ms = jnp.where(lo, s_lo, s_hi) * (1.0 / DIFF_QK_DIM)
    y = x * lax.rsqrt(ms + RMS_EPS) * g
    half = ROT_DIM // 2
    return y * c + pltpu.roll(y, LANES - half, 1) * s1 + pltpu.roll(y, half, 1) * s2


def _attn_kernel(q_ref, k_ref, v_ref, c_ref, s1_ref, s2_ref, qg_ref, kg_ref,
                 lq1_ref, lk1_ref, lq2_ref, lk2_ref, og_ref, o_ref,
                 q1_scr, q2_scr, k_scr, v_scr, *bufs):
    s_len = q_ref.shape[0]
    s_bufs, e_bufs = bufs[0:4], bufs[4:8]
    lam = (jnp.exp(jnp.sum(lq1_ref[...] * lk1_ref[...], axis=-1, keepdims=True))
           - jnp.exp(jnp.sum(lq2_ref[...] * lk2_ref[...], axis=-1, keepdims=True))
           + LAMBDA_INIT)
    lo = lax.broadcasted_iota(jnp.int32, (TQ, LANES), 1) < DIFF_QK_DIM
    for hd in range(ATT_HPS):
        v_scr[hd, :, 0:LANES] = v_ref[:, hd * LANES:(hd + 1) * LANES]
        v_scr[hd, :, LANES:] = jnp.ones((s_len, LANES), BF16)

    n_tiles = s_len // TQ
    tiles = [(hd, i) for hd in range(ATT_HPS) for i in range(n_tiles)]

    def prepare(t):
        hd, i = tiles[t]
        rows = slice(i * TQ, (i + 1) * TQ)
        lanes = slice(hd * LANES, (hd + 1) * LANES)
        c, s1, s2 = c_ref[rows, :], s1_ref[rows, :], s2_ref[rows, :]
        qn = (_norm_rope(q_ref[rows, lanes].astype(F32), qg_ref[...], c, s1, s2)
              * (DIFF_QK_DIM ** -0.5 * math.log2(math.e)))
        q1_scr[hd, rows, :] = jnp.where(lo, qn, 0.0).astype(BF16)
        q2_scr[hd, rows, :] = jnp.where(lo, 0.0, qn).astype(BF16)
        k_scr[hd, rows, :] = _norm_rope(k_ref[rows, lanes].astype(F32), kg_ref[...], c, s1,
                                        s2).astype(BF16)

    diag = (lax.broadcasted_iota(jnp.int32, (TQ, TQ), 1) // CHUNK
            <= lax.broadcasted_iota(jnp.int32, (TQ, TQ), 0) // CHUNK)

    items = [(t, q_scr) for t in range(len(tiles)) for q_scr in (q1_scr, q2_scr)]

    def scores(r):
        t, q_scr = items[r]
        hd, i = tiles[t]
        s_scr = s_bufs[r % 4]
        q = q_scr[hd, i * TQ:(i + 1) * TQ, :]
        n_off = i * TQ
        s_scr[:, n_off:n_off + TQ] = jnp.where(
            diag, _nt_dot(q, k_scr[hd, n_off:n_off + TQ, :]), -jnp.inf)
        if n_off:
            s_scr[:, 0:n_off] = _nt_dot(q, k_scr[hd, 0:n_off, :])

    def softmax_pv(r):
        hd, i = tiles[items[r][0]]
        nk = (i + 1) * TQ
        s_scr, e_scr = s_bufs[r % 4], e_bufs[r % 4]
        m = jnp.max(s_scr[:, 0:nk], axis=-1, keepdims=True)
        e_scr[:, 0:nk] = jnp.exp2(s_scr[:, 0:nk] - m).astype(BF16)
        acc = jnp.dot(e_scr[:, 0:nk], v_scr[hd, 0:nk, :], preferred_element_type=F32)
        return acc[:, 0:LANES] / acc[:, LANES:]

    prepare(0)
    scores(0)
    scores(1)
    for t, (hd, i) in enumerate(tiles):
        if t + 1 < len(tiles):
            prepare(t + 1)
            scores(2 * t + 2)
        o1 = softmax_pv(2 * t)
        if t + 1 < len(tiles):
            scores(2 * t + 3)
        o = o1 - lam * softmax_pv(2 * t + 1)
        ms = jnp.mean(o * o, axis=-1, keepdims=True)
        y = o * lax.rsqrt(ms + RMS_EPS) * og_ref[...] * (1.0 - LAMBDA_INIT)
        o_ref[i * TQ:(i + 1) * TQ, hd * LANES:(hd + 1) * LANES] = y.astype(BF16)


def _diff_attention(proj, tabs, qg, kg, lq1, lk1, lq2, lk2, og, batch, seq):
    c, s1, s2 = tabs
    h = DIFF_HEADS
    hg = h // ATT_HPS
    wide = ATT_HPS * LANES
    blk = lambda off: pl.BlockSpec((seq, wide), lambda b, hh, off=off: (b, off + hh))
    tab = pl.BlockSpec((seq, LANES), lambda b, hh: (b, 0))
    vec = lambda n: pl.BlockSpec((1, n), lambda b, hh: (0, 0))
    return pl.pallas_call(
        _attn_kernel,
        grid=(batch, hg),
        in_specs=[blk(0), blk(hg), blk(2 * hg), tab, tab, tab,
                  vec(LANES), vec(LANES), vec(DIFF_QK_DIM), vec(DIFF_QK_DIM),
                  vec(DIFF_QK_DIM), vec(DIFF_QK_DIM), vec(LANES)],
        out_specs=pl.BlockSpec((seq, wide), lambda b, hh: (b, hh)),
        out_shape=jax.ShapeDtypeStruct((batch * seq, h * DIFF_V_DIM), BF16),
        scratch_shapes=[pltpu.VMEM((ATT_HPS, seq, LANES), BF16)] * 3
        + [pltpu.VMEM((ATT_HPS, seq, 2 * LANES), BF16)]
        + [pltpu.VMEM((TQ, seq), F32)] * 4 + [pltpu.VMEM((TQ, seq), BF16)] * 4,
        compiler_params=_params(2),
        name="diff_attention",
    )(proj, proj, proj, c, s1, s2, qg, kg, lq1, lk1, lq2, lk2, og)


def _split_dot(ones_bf, x):
    hi = x.astype(BF16)
    lo = (x - hi.astype(F32)).astype(BF16)
    return (jnp.dot(ones_bf, hi, preferred_element_type=F32)
            + jnp.dot(ones_bf, lo, preferred_element_type=F32))


def _gla_kernel(q_ref, k_ref, v_ref, r_ref, ga_ref, w2_ref, b2_ref, og_ref, o_ref,
                qin_scr, ut_scr, dec_scr, st_scr, acc_scr):
    s_len = q_ref.shape[0]
    cpg = GB // CHUNK
    ri = lax.broadcasted_iota(jnp.int32, (GB, GB), 0)
    ci = lax.broadcasted_iota(jnp.int32, (GB, GB), 1)
    same = (ri // CHUNK) == (ci // CHUNK)
    blk_ones = jnp.where(same, 1.0, 0.0).astype(BF16)
    tril = same & (ci <= ri)
    tri_ones = jnp.where(tril, 1.0, 0.0).astype(BF16)

    def phase_a(g, carry):
        rows = pl.ds(pl.multiple_of(g * GB, GB), GB)
        pre = jnp.dot(ga_ref[rows, :].astype(BF16), w2_ref[...],
                      preferred_element_type=F32) + b2_ref[...]
        la = -(jnp.maximum(-pre, 0.0) + jnp.log1p(jnp.exp(-jnp.abs(pre)))) * (1.0 / GLA_TAU)
        bc = _split_dot(tri_ones, la)
        bl = _split_dot(blk_ones, la)
        e_neg = jnp.exp(-bc)
        e_last = jnp.exp(bl)
        k = k_ref[rows, :].astype(F32)
        q_in = (q_ref[rows, :].astype(F32) * (GLA_K_DIM ** -0.5) * jnp.exp(bc)).astype(BF16)
        k_in = (k * e_neg).astype(BF16)
        k_dec = (k * (e_last * e_neg)).astype(BF16)
        qin_scr[rows, :] = q_in
        dec_scr[rows, :] = e_last
        v = v_ref[rows, :]
        att = jnp.where(tril, _nt_dot(q_in, k_in), 0.0).astype(BF16)
        acc_scr[rows, :] = jnp.dot(att, v, preferred_element_type=F32)
        for c in range(cpg):
            cr = slice(c * CHUNK, (c + 1) * CHUNK)
            ut_scr[g * cpg + c] = _tn_dot(v[cr, :], k_dec[cr, :])
        return carry

    lax.fori_loop(0, s_len // GB, phase_a, 0, unroll=8)

    def phase_b(c, st):
        st_scr[c] = st.astype(BF16)
        dec = dec_scr[pl.ds(pl.multiple_of(c * CHUNK, CHUNK), 1), :]
        return dec * st + ut_scr[c]

    lax.fori_loop(0, s_len // CHUNK, phase_b, jnp.zeros((GLA_V_DIM, GLA_K_DIM), F32))

    def phase_c(g, carry):
        rows = pl.ds(pl.multiple_of(g * GB, GB), GB)
        inter = [_nt_dot(qin_scr[pl.ds(pl.multiple_of(g * GB + c * CHUNK, CHUNK), CHUNK), :],
                         st_scr[g * cpg + c]) for c in range(cpg)]
        o = acc_scr[rows, :] + jnp.concatenate(inter, axis=0)
        ms = jnp.mean(o * o, axis=-1, keepdims=True)
        y = o * lax.rsqrt(ms + RMS_EPS) * og_ref[...]
        o_ref[rows, :] = (y * _silu(r_ref[rows, :].astype(F32))).astype(BF16)
        return carry

    lax.fori_loop(0, s_len // GB, phase_c, 0, unroll=8)


def _gla(proj, ga, w2p, b2, og, batch, seq):
    hq = 3 * DIFF_HEADS
    kblk = lambda off: pl.BlockSpec((seq, GLA_K_DIM), lambda b, hh, off=off: (b, off + hh))
    vblk = lambda off: pl.BlockSpec((seq, GLA_V_DIM), lambda b, hh, off=off: (b, off + hh))
    return pl.pallas_call(
        _gla_kernel,
        grid=(batch, GLA_HEADS),
        in_specs=[kblk(hq), kblk(hq + GLA_HEADS), vblk(16), vblk(16 + GLA_HEADS),
                  pl.BlockSpec((seq, LANES), lambda b, hh: (b, 0)),
                  pl.BlockSpec((LANES, GLA_K_DIM), lambda b, hh: (0, hh)),
                  pl.BlockSpec((1, GLA_K_DIM), lambda b, hh: (0, hh)),
                  pl.BlockSpec((1, GLA_V_DIM), lambda b, hh: (0, 0))],
        out_specs=pl.BlockSpec((seq, GLA_V_DIM), lambda b, hh: (b, hh)),
        out_shape=jax.ShapeDtypeStruct((batch * seq, GLA_HEADS * GLA_V_DIM), BF16),
        scratch_shapes=[pltpu.VMEM((seq, GLA_K_DIM), BF16),
                        pltpu.VMEM((seq // CHUNK, GLA_V_DIM, GLA_K_DIM), F32),
                        pltpu.VMEM((seq, GLA_K_DIM), F32),
                        pltpu.VMEM((seq // CHUNK, GLA_V_DIM, GLA_K_DIM), BF16),
                        pltpu.VMEM((seq, GLA_V_DIM), F32)],
        compiler_params=_params(2),
        name="gla",
    )(proj, proj, proj, proj, ga, w2p, b2, og)


def _out_proj_kernel(a_ref, g_ref, x_ref, wo_ref, g2_ref, wr_ref, br_ref,
                     x1_ref, h2_ref, lg_ref, tt_scr):
    half = a_ref.shape[1]

    def body(c, carry):
        r0 = pl.multiple_of(c * RB_OUT, RB_OUT)
        rows = pl.ds(r0, RB_OUT)
        mixed = (jnp.dot(a_ref[rows, :], wo_ref[0:half, :], preferred_element_type=F32)
                 + jnp.dot(g_ref[rows, :], wo_ref[half:, :], preferred_element_type=F32))
        x1 = x_ref[rows, :] + mixed
        x1_ref[rows, :] = x1
        ms = jnp.mean(x1 * x1, axis=-1, keepdims=True)
        h2 = x1 * lax.rsqrt(ms + RMS_EPS) * g2_ref[...]
        _store_token_tiles(h2_ref, h2, tt_scr, r0)
        lg = jnp.dot(h2.astype(BF16), wr_ref[...], preferred_element_type=F32) + br_ref[...]
        lg_ref[:, rows] = lg.T[0:N_LOGIT_ROWS, :]
        return carry

    lax.fori_loop(0, TM_OUT // RB_OUT, body, 0)


def _out_proj(a_out, g_out, x2, wo, g2, wr, br):
    t = x2.shape[0]
    half = a_out.shape[1]
    row = lambda n: pl.BlockSpec((TM_OUT, n), lambda i: (i, 0))
    full = lambda r, n: pl.BlockSpec((r, n), lambda i: (0, 0))
    return pl.pallas_call(
        _out_proj_kernel,
        grid=(t // TM_OUT,),
        in_specs=[row(half), row(half), row(D_MODEL), full(D_MODEL, D_MODEL),
                  full(1, D_MODEL), full(D_MODEL, LANES), full(1, LANES)],
        out_specs=[row(D_MODEL), pl.BlockSpec((TM_OUT * SUB, LANES), lambda i: (i, 0)),
                   pl.BlockSpec((N_LOGIT_ROWS, TM_OUT), lambda i: (0, i))],
        out_shape=[jax.ShapeDtypeStruct((t, D_MODEL), F32),
                   jax.ShapeDtypeStruct((t * SUB, LANES), BF16),
                   jax.ShapeDtypeStruct((N_LOGIT_ROWS, t), F32)],
        scratch_shapes=[pltpu.VMEM((RB_OUT * TT_PITCH, LANES), F32)],
        compiler_params=_params(1),
        name="out_proj_router",
    )(a_out, g_out, x2, wo, g2, wr, br)


def _first_argmax(vals, row_f):
    m = jnp.max(vals, axis=0, keepdims=True)
    idx = jnp.min(jnp.where(vals == m, row_f, float(vals.shape[0])), axis=0, keepdims=True)
    return m, idx


def _softmax_rows(v):
    ex = jnp.exp(v - jnp.max(v, axis=0, keepdims=True))
    return ex / jnp.sum(ex, axis=0, keepdims=True)


def _routing_kernel(lg_ref, slots_ref, cw_ref, cnt_ref, tok_scr):
    t = lg_ref.shape[1]
    epg = EXPERTS_PER_GROUP
    row8 = lax.broadcasted_iota(jnp.int32, (8, TB), 0).astype(F32)
    row_e = lax.broadcasted_iota(jnp.int32, (N_EXPERTS, TB), 0).astype(F32)
    ui = lax.broadcasted_iota(jnp.int32, (TB, TB), 0)
    uj = lax.broadcasted_iota(jnp.int32, (TB, TB), 1)
    earlier = jnp.where(ui < uj, 1.0, 0.0).astype(BF16)
    all_ones = jnp.ones((TB, TB), BF16)

    def rows01(r0, r1):
        return jnp.where(row8 == 0.0, r0, jnp.where(row8 == 1.0, r1, jnp.zeros_like(r0)))

    def pass1(b, counts):
        cols = pl.ds(pl.multiple_of(b * TB, TB), TB)
        lg = lg_ref[:, cols]
        lg_groups = jnp.where(row8 < float(N_GROUPS), lg[0:GROUP_ROWS, :], -jnp.inf)
        pg_sel, g_sel = _first_argmax(_softmax_rows(lg_groups), row8)
        le = lg[GROUP_ROWS:GROUP_ROWS + epg, :]
        for g in range(1, N_GROUPS):
            le = jnp.where(g_sel == float(g),
                           lg[GROUP_ROWS + g * epg:GROUP_ROWS + (g + 1) * epg, :], le)
        pe = _softmax_rows(le)
        v1, i1 = _first_argmax(pe, row8)
        v2, i2 = _first_argmax(jnp.where(row8 == i1, -jnp.inf, pe), row8)
        tot = v1 + v2
        e1 = g_sel * epg + i1
        e2 = g_sel * epg + i2
        oh1 = row_e == e1
        oh2 = row_e == e2
        a = jnp.where(oh1 | oh2, 1.0, 0.0).astype(BF16)
        rank = jnp.dot(a, earlier, preferred_element_type=F32) + counts
        r1 = jnp.sum(jnp.where(oh1, rank, 0.0), axis=0, keepdims=True)
        r2 = jnp.sum(jnp.where(oh2, rank, 0.0), axis=0, keepdims=True)
        tok_scr[:, cols] = jnp.where(row8 == 0.0, e1, jnp.where(
            row8 == 1.0, e2, jnp.where(row8 == 2.0, r1, jnp.where(row8 == 3.0, r2, 0.0))))
        cw_ref[:, cols] = rows01((v1 / tot) * pg_sel, (v2 / tot) * pg_sel)
        return counts + jnp.dot(a, all_ones, preferred_element_type=F32)

    counts = lax.fori_loop(0, t // TB, pass1, jnp.zeros((N_EXPERTS, TB), F32))
    cnt_ref[...] = counts[:, 0:LANES]
    n_tiles = jnp.floor((counts + (TM_E - 1)) * (1.0 / TM_E))
    li = lax.broadcasted_iota(jnp.int32, (N_EXPERTS, N_EXPERTS), 0)
    lj = lax.broadcasted_iota(jnp.int32, (N_EXPERTS, N_EXPERTS), 1)
    lower = jnp.where(lj < li, 1.0, 0.0).astype(BF16)
    row_off = jnp.dot(lower, n_tiles.astype(BF16), preferred_element_type=F32) * TM_E

    def pass2(b, carry):
        cols = pl.ds(pl.multiple_of(b * TB, TB), TB)
        tok = tok_scr[:, cols]
        off1 = jnp.sum(jnp.where(row_e == tok[0:1, :], row_off, 0.0), axis=0, keepdims=True)
        off2 = jnp.sum(jnp.where(row_e == tok[1:2, :], row_off, 0.0), axis=0, keepdims=True)
        slots_ref[:, cols] = rows01(off1 + tok[2:3, :], off2 + tok[3:4, :]).astype(jnp.int32)
        return carry

    lax.fori_loop(0, t // TB, pass2, 0)


def _routing(logits_t):
    t = logits_t.shape[1]
    return pl.pallas_call(
        _routing_kernel,
        out_shape=[jax.ShapeDtypeStruct((8, t), jnp.int32),
                   jax.ShapeDtypeStruct((8, t), F32),
                   jax.ShapeDtypeStruct((N_EXPERTS, LANES), F32)],
        scratch_shapes=[pltpu.VMEM((8, t), F32)],
        compiler_params=pltpu.CompilerParams(vmem_limit_bytes=VMEM_LIMIT),
        name="routing",
    )(logits_t)


def _rows_copy(src, dst, sem, src_tok, dst_tok, n):
    first = lambda tok: tok * SUB if isinstance(tok, int) else pl.multiple_of(tok * SUB, SUB)
    s0 = first(src_tok)
    d0 = first(dst_tok)
    return pltpu.make_async_copy(src.at[pl.ds(s0, n * SUB), :], dst.at[pl.ds(d0, n * SUB), :], sem)


def _dispatch_kernel(sl1_ref, sl2_ref, ends_ref, h2_ref, xs_hbm, zero_scr, sem):
    step = pl.program_id(0)

    @pl.when(step == 0)
    def _():
        zero_scr[...] = jnp.zeros_like(zero_scr)
        for e in range(N_EXPERTS):
            start = jnp.maximum(ends_ref[e] - TM_E, 0)
            _rows_copy(zero_scr, xs_hbm, sem, 0, start, TM_E).start()
        for e in range(N_EXPERTS):
            _rows_copy(zero_scr, xs_hbm, sem, 0, 0, TM_E).wait()

    base = step * TB_D

    def issue(r, carry):
        _rows_copy(h2_ref, xs_hbm, sem, r, sl1_ref[base + r], 1).start(priority=0)
        _rows_copy(h2_ref, xs_hbm, sem, r, sl2_ref[base + r], 1).start(priority=1)
        return carry

    lax.fori_loop(0, TB_D, issue, 0, unroll=8)
    _rows_copy(h2_ref, xs_hbm, sem, 0, 0, TB_D).wait()
    _rows_copy(h2_ref, xs_hbm, sem, 0, 0, TB_D).wait()


def _dispatch(h2, sl1, sl2, ends, n_rows):
    t = h2.shape[0] // SUB
    return pl.pallas_call(
        _dispatch_kernel,
        grid_spec=pltpu.PrefetchScalarGridSpec(
            num_scalar_prefetch=3,
            grid=(t // TB_D,),
            in_specs=[pl.BlockSpec((TB_D * SUB, LANES), lambda i, *_: (i, 0))],
            out_specs=pl.BlockSpec(memory_space=pl.ANY),
            scratch_shapes=[pltpu.VMEM((TM_E * SUB, LANES), BF16), pltpu.SemaphoreType.DMA(())],
        ),
        out_shape=jax.ShapeDtypeStruct((n_rows * SUB, LANES), BF16),
        compiler_params=_params(1),
        name="dispatch",
    )(sl1, sl2, ends, h2)


def _expert_kernel(tidx_ref, texp_ref, tfirst_ref, tslot_ref, tnext_ref, nt_ref,
                   xs_ref, wg_hbm, wu_hbm, wd_hbm, ys_ref,
                   wg_buf, wu_buf, wd_buf, wg_bf, wu_bf, wd_bf, tt_scr, sems):
    j = pl.program_id(0)

    def weight_copies(e, slot):
        return [pltpu.make_async_copy(hbm.at[e], buf.at[slot], sems.at[slot, k])
                for k, (hbm, buf) in enumerate(((wg_hbm, wg_buf), (wu_hbm, wu_buf),
                                                (wd_hbm, wd_buf)))]

    @pl.when(j < nt_ref[0])
    def _():
        slot = tslot_ref[j]

        @pl.when(tfirst_ref[j] == 1)
        def _():
            @pl.when(j == 0)
            def _():
                for c in weight_copies(texp_ref[j], slot):
                    c.start()
            for c in weight_copies(texp_ref[j], slot):
                c.wait()

            @pl.when(tnext_ref[j] >= 0)
            def _():
                for c in weight_copies(tnext_ref[j], 1 - slot):
                    c.start(priority=1)
            wg_bf[...] = wg_buf[slot].astype(BF16)
            wu_bf[...] = wu_buf[slot].astype(BF16)
            wd_bf[...] = wd_buf[slot].astype(BF16)

        xb = _load_token_tiles(xs_ref, tt_scr, TM_E).astype(BF16)
        g = jnp.dot(xb, wg_bf[...], preferred_element_type=F32)
        u = jnp.dot(xb, wu_bf[...], preferred_element_type=F32)
        h = (_silu(g) * u).astype(BF16)
        _store_token_tiles(ys_ref, jnp.dot(h, wd_bf[...], preferred_element_type=F32), tt_scr)


def _expert_mlp(xs, tile_meta, wg, wu, wd):
    n_rows = xs.shape[0] // SUB
    tile = pl.BlockSpec((TM_E * SUB, LANES), lambda j, ti, *_: (ti[j], 0))
    hbm = pl.BlockSpec(memory_space=pl.ANY)
    return pl.pallas_call(
        _expert_kernel,
        grid_spec=pltpu.PrefetchScalarGridSpec(
            num_scalar_prefetch=len(tile_meta),
            grid=(n_rows // TM_E,),
            in_specs=[tile, hbm, hbm, hbm],
            out_specs=tile,
            scratch_shapes=[pltpu.VMEM((2, D_MODEL, D_FF), F32),
                            pltpu.VMEM((2, D_MODEL, D_FF), F32),
                            pltpu.VMEM((2, D_FF, D_MODEL), F32),
                            pltpu.VMEM((D_MODEL, D_FF), BF16),
                            pltpu.VMEM((D_MODEL, D_FF), BF16),
                            pltpu.VMEM((D_FF, D_MODEL), BF16),
                            pltpu.VMEM((TM_E * TT_PITCH, LANES), F32),
                            pltpu.SemaphoreType.DMA((2, 3))],
        ),
        out_shape=jax.ShapeDtypeStruct((n_rows * SUB, LANES), BF16),
        compiler_params=_params(1),
        name="expert_mlp",
    )(*tile_meta, xs, wg, wu, wd)


def _combine_kernel(sl1_ref, sl2_ref, x1_ref, cw_ref, ys_hbm, o_ref, y1_scr, y2_scr, tt1_scr,
                    tt2_scr, sems):
    i = pl.program_id(0)

    def issue(step, slot):
        base = step * TB

        def body(r, carry):
            _rows_copy(ys_hbm, y1_scr.at[slot], sems.at[slot], sl1_ref[base + r], r,
                       1).start(priority=0)
            _rows_copy(ys_hbm, y2_scr.at[slot], sems.at[slot], sl2_ref[base + r], r,
                       1).start(priority=1)
            return carry

        lax.fori_loop(0, TB, body, 0, unroll=8)

    @pl.when(i == 0)
    def _():
        issue(0, 0)

    last = pl.num_programs(0) - 1
    rb = RB_COMBINE
    n_pieces = (TB // rb) * SUB
    per_piece = TB // n_pieces

    def step(slot):
        _rows_copy(ys_hbm, y1_scr.at[slot], sems.at[slot], 0, 0, TB).wait()
        _rows_copy(ys_hbm, y2_scr.at[slot], sems.at[slot], 0, 0, TB).wait()
        nbase = jnp.minimum(i + 1, last) * TB
        cw = jnp.concatenate([cw_ref[...], jnp.zeros((LANES - 8, TB), F32)], axis=0).T
        _tt_fill(tt1_scr, y1_scr.at[slot], TB)
        _tt_fill(tt2_scr, y2_scr.at[slot], TB)
        piece = 0
        for r0 in range(0, TB, rb):
            rows = slice(r0, r0 + rb)
            c1 = jnp.broadcast_to(cw[rows, 0:1], (rb, LANES))
            c2 = jnp.broadcast_to(cw[rows, 1:2], (rb, LANES))
            for s in range(SUB):
                for r in range(piece * per_piece, (piece + 1) * per_piece):
                    _rows_copy(ys_hbm, y1_scr.at[1 - slot], sems.at[1 - slot],
                               sl1_ref[nbase + r], r, 1).start(priority=0)
                    _rows_copy(ys_hbm, y2_scr.at[1 - slot], sems.at[1 - slot],
                               sl2_ref[nbase + r], r, 1).start(priority=1)
                piece += 1
                cols = slice(s * LANES, (s + 1) * LANES)
                tiles = _tt_piece(s, rb, r0)
                o_ref[rows, cols] = (x1_ref[rows, cols] + c1 * tt1_scr[tiles]
                                     + c2 * tt2_scr[tiles])

        @pl.when(i == last)
        def _():
            _rows_copy(ys_hbm, y1_scr.at[1 - slot], sems.at[1 - slot], 0, 0, TB).wait()
            _rows_copy(ys_hbm, y2_scr.at[1 - slot], sems.at[1 - slot], 0, 0, TB).wait()

    for parity in (0, 1):
        @pl.when(i % 2 == parity)
        def _():
            step(parity)


def _combine(x1, cw, ys, sl1, sl2):
    t = x1.shape[0]
    return pl.pallas_call(
        _combine_kernel,
        grid_spec=pltpu.PrefetchScalarGridSpec(
            num_scalar_prefetch=2,
            grid=(t // TB,),
            in_specs=[pl.BlockSpec((TB, D_MODEL), lambda i, *_: (i, 0)),
                      pl.BlockSpec((8, TB), lambda i, *_: (0, i)),
                      pl.BlockSpec(memory_space=pl.ANY)],
            out_specs=pl.BlockSpec((TB, D_MODEL), lambda i, *_: (i, 0)),
            scratch_shapes=[pltpu.VMEM((2, TB * SUB, LANES), BF16),
                            pltpu.VMEM((2, TB * SUB, LANES), BF16),
                            pltpu.VMEM((TB * TT_PITCH, LANES), F32),
                            pltpu.VMEM((TB * TT_PITCH, LANES), F32),
                            pltpu.SemaphoreType.DMA((2,))],
        ),
        out_shape=jax.ShapeDtypeStruct((t, D_MODEL), F32),
        compiler_params=_params(1),
        name="combine",
    )(sl1, sl2, x1, cw, ys)


def _lane_tile(v, reps):
    return jnp.tile(v.reshape(1, -1), (1, reps))


def kernel(x, positions, norm1_g, w_in, q_norm_g, k_norm_g, lambda_q1, lambda_k1, lambda_q2,
           lambda_k2, diff_out_norm_g, gla_w_gate2, gla_b_gate, gla_out_norm_g, w_out, norm2_g,
           w_router_group, b_router_group, w_router_expert, b_router_expert, w_gate_expert,
           w_up_expert, w_down_expert):
    batch, seq, d = x.shape
    t = batch * seq
    assert d == D_MODEL and w_in.shape == (1, D_MODEL, D_MAIN + GLA_GATE_RANK)
    assert seq % TQ == 0 and seq % GB == 0 and t % TM_IN == 0 and t % TM_OUT == 0 and t % TB_D == 0
    x2 = x.reshape(t, d)

    inv = ROPE_THETA ** (-jnp.arange(0, ROT_DIM, 2, dtype=F32) / ROT_DIM)
    lane_d = jnp.arange(LANES) % DIFF_QK_DIM
    invf = jnp.where(lane_d < ROT_DIM, inv[lane_d % (ROT_DIM // 2)], 0.0).reshape(1, LANES)
    pos_b = jnp.broadcast_to(positions.astype(F32).reshape(t, 1), (t, LANES))
    tabs = _rope_tables(pos_b, invf)

    w_in_t = jnp.swapaxes(w_in, 1, 2)[0]
    wa_t = jnp.pad(w_in_t[D_MAIN:, :], ((0, LANES - GLA_GATE_RANK), (0, 0)))
    proj, ga = _in_proj(x2, norm1_g, w_in_t, wa_t)

    a_out = _diff_attention(
        proj, tabs, _lane_tile(q_norm_g[0], 2), _lane_tile(k_norm_g[0], 2),
        lambda_q1, lambda_k1, lambda_q2, lambda_k2, diff_out_norm_g, batch, seq)
    w2p = jnp.pad(gla_w_gate2[0], ((0, LANES - GLA_GATE_RANK), (0, 0))).astype(BF16)
    g_out = _gla(proj, ga, w2p, gla_b_gate, gla_out_norm_g, batch, seq)

    gpad = GROUP_ROWS - N_GROUPS
    lpad = LANES - N_LOGIT_ROWS
    wr = jnp.concatenate(
        [jnp.pad(w_router_group[0], ((0, 0), (0, gpad))),
         jnp.pad(w_router_expert[0].transpose(1, 0, 2).reshape(d, N_EXPERTS),
                 ((0, 0), (0, lpad)))], axis=1)
    br = jnp.concatenate([jnp.pad(b_router_group[0], (0, gpad)),
                          jnp.pad(b_router_expert[0].reshape(-1), (0, lpad))]).reshape(1, LANES)
    x1, h2, logits_t = _out_proj(a_out, g_out, x2, w_out[0].astype(BF16), norm2_g,
                                 wr.astype(BF16), br)

    slots, cw, counts = _routing(logits_t)
    sl1 = slots[0]
    sl2 = slots[1]
    n_rows = 2 * t + N_EXPERTS * TM_E
    max_tiles = n_rows // TM_E
    cnt = counts[:, 0].astype(jnp.int32)
    tile_end = jnp.cumsum((cnt + (TM_E - 1)) // TM_E)
    n_tiles = tile_end[-1:]
    tile_idx = jnp.minimum(jnp.arange(max_tiles, dtype=jnp.int32), n_tiles[0] - 1)
    tile_exp = jnp.sum(tile_idx[:, None] >= tile_end[None, :], axis=1).astype(jnp.int32)
    row_end = (tile_end * TM_E).astype(jnp.int32)
    tile_first = jnp.concatenate(
        [jnp.ones((1,), jnp.int32), (tile_exp[1:] != tile_exp[:-1]).astype(jnp.int32)])
    tile_slot = (jnp.cumsum(tile_first) - 1) % 2
    nxt = tile_end[tile_exp]
    tile_next = jnp.where(nxt < n_tiles[0], tile_exp[jnp.minimum(nxt, max_tiles - 1)], -1)
    tile_meta = [a.astype(jnp.int32)
                 for a in (tile_idx, tile_exp, tile_first, tile_slot, tile_next, n_tiles)]

    xs = _dispatch(h2, sl1, sl2, row_end, n_rows)
    ys = _expert_mlp(xs, tile_meta,
                     w_gate_expert[0].reshape(N_EXPERTS, d, D_FF),
                     w_up_expert[0].reshape(N_EXPERTS, d, D_FF),
                     w_down_expert[0].reshape(N_EXPERTS, D_FF, d))
    out = _combine(x1, cw, ys, sl1, sl2)
    return out.reshape(batch, seq, d)
```

```python
import math

import jax
import jax.numpy as jnp
from jax import lax
from jax.experimental import pallas as pl
from jax.experimental.pallas import tpu as pltpu

D_MODEL = 2048
CHUNK = 64
DIFF_QK_DIM = 64
DIFF_V_DIM = 128
DIFF_HEADS = 8
ROT_DIM = 16
ROPE_THETA = 500000.0
GLA_HEADS = 4
GLA_V_DIM = 256
GLA_K_DIM = 128
GLA_GATE_RANK = 16
GLA_TAU = 16.0
N_GROUPS = 4
EXPERTS_PER_GROUP = 8
N_EXPERTS = N_GROUPS * EXPERTS_PER_GROUP
D_FF = 512
RMS_EPS = 1e-6
LAMBDA_INIT = 0.8 - 0.6 * math.exp(-0.3 * 0)
D_MAIN = 6144
GROUP_ROWS = 8
N_LOGIT_ROWS = GROUP_ROWS + N_EXPERTS

LANES = 128
SUB = D_MODEL // LANES
TT_PITCH = 24
VMEM_LIMIT = 56 * 1024 * 1024

TB_ROPE = 1024
TM_IN = 1024
TN_IN = 1024
RB_IN = 256
TQ = 256
ATT_HPS = 2
GB = 256
TM_OUT = 512
RB_OUT = 256
TB = 256
RB_COMBINE = 64
TB_D = 1024
TM_E = 256

F32 = jnp.float32
BF16 = jnp.bfloat16


def _params(n_axes):
    return pltpu.CompilerParams(dimension_semantics=("arbitrary",) * n_axes,
                                vmem_limit_bytes=VMEM_LIMIT)


def _nt_dot(a, b):
    return lax.dot_general(a, b, (((1,), (1,)), ((), ())), preferred_element_type=F32)


def _tn_dot(a, b):
    return lax.dot_general(a, b, (((0,), (0,)), ((), ())), preferred_element_type=F32)


def _silu(x):
    return x * (1.0 / (1.0 + jnp.exp(-x)))


def _tt_fill(scr, ref, n, tok0=0):
    for t in range(n):
        scr[t * TT_PITCH:t * TT_PITCH + SUB, :] = ref[pl.ds((tok0 + t) * SUB, SUB), :].astype(F32)


def _tt_drain(ref, scr, n, tok0=0):
    for t in range(n):
        ref[pl.ds((tok0 + t) * SUB, SUB), :] = scr[t * TT_PITCH:t * TT_PITCH + SUB, :].astype(BF16)


def _tt_piece(s, n, tok0=0):
    return (pl.ds(tok0 * TT_PITCH + s, n, stride=TT_PITCH), slice(None))


def _store_token_tiles(ref, val, scr, tok0=0):
    n = val.shape[0]
    for s in range(SUB):
        scr[_tt_piece(s, n)] = val[:, s * LANES:(s + 1) * LANES]
    _tt_drain(ref, scr, n, tok0)


def _load_token_tiles(ref, scr, n, tok0=0):
    _tt_fill(scr, ref, n, tok0)
    return jnp.concatenate([scr[_tt_piece(s, n)] for s in range(SUB)], axis=1)


def _rope_table_kernel(pos_ref, invf_ref, c_ref, s1_ref, s2_ref):
    ang = pos_ref[...] * invf_ref[...]
    d = lax.broadcasted_iota(jnp.int32, ang.shape, 1) % DIFF_QK_DIM
    cos = jnp.cos(ang)
    sin = jnp.sin(ang)
    half = ROT_DIM // 2
    c_ref[...] = jnp.where(d < ROT_DIM, cos, 1.0)
    s1_ref[...] = jnp.where(d < half, -sin, 0.0)
    s2_ref[...] = jnp.where((d >= half) & (d < ROT_DIM), sin, 0.0)


def _rope_tables(pos_b, invf):
    t = pos_b.shape[0]
    spec = pl.BlockSpec((TB_ROPE, LANES), lambda i: (i, 0))
    return pl.pallas_call(
        _rope_table_kernel,
        grid=(t // TB_ROPE,),
        in_specs=[spec, pl.BlockSpec((1, LANES), lambda i: (0, 0))],
        out_specs=[spec, spec, spec],
        out_shape=[jax.ShapeDtypeStruct((t, LANES), F32)] * 3,
        compiler_params=_params(1),
        name="rope_tables",
    )(pos_b, invf)


def _in_proj_kernel(x_ref, g_ref, wt_ref, wat_ref, proj_ref, ga_ref, h_scr):
    @pl.when(pl.program_id(1) == 0)
    def _():
        def body(c, carry):
            rows = pl.ds(c * RB_IN, RB_IN)
            x = x_ref[rows, :]
            ms = jnp.mean(x * x, axis=-1, keepdims=True)
            h_scr[rows, :] = (x * lax.rsqrt(ms + RMS_EPS) * g_ref[...]).astype(BF16)
            return carry
        lax.fori_loop(0, TM_IN // RB_IN, body, 0)
        ga_ref[...] = _nt_dot(h_scr[...], wat_ref[...].astype(BF16))

    proj_ref[...] = _nt_dot(h_scr[...], wt_ref[...].astype(BF16)).astype(BF16)


def _in_proj(x2, g1, w_in, wa):
    t = x2.shape[0]
    return pl.pallas_call(
        _in_proj_kernel,
        grid=(t // TM_IN, D_MAIN // TN_IN),
        in_specs=[
            pl.BlockSpec((TM_IN, D_MODEL), lambda i, j: (i, 0)),
            pl.BlockSpec((1, D_MODEL), lambda i, j: (0, 0)),
            pl.BlockSpec((TN_IN, D_MODEL), lambda i, j: (j, 0)),
            pl.BlockSpec((LANES, D_MODEL), lambda i, j: (0, 0)),
        ],
        out_specs=[
            pl.BlockSpec((TM_IN, TN_IN), lambda i, j: (i, j)),
            pl.BlockSpec((TM_IN, LANES), lambda i, j: (i, 0)),
        ],
        out_shape=[jax.ShapeDtypeStruct((t, D_MAIN), BF16),
                   jax.ShapeDtypeStruct((t, LANES), F32)],
        scratch_shapes=[pltpu.VMEM((TM_IN, D_MODEL), BF16)],
        compiler_params=_params(2),
        name="in_proj",
    )(x2, g1, w_in, wa)


def _norm_rope(x, g, c, s1, s2):
    lo = lax.broadcasted_iota(jnp.int32, x.shape, 1) < DIFF_QK_DIM
    x2 = x * x
    s_lo = jnp.sum(jnp.where(lo, x2, 0.0), axis=-1, keepdims=True)
    s_hi = jnp.sum(jnp.where(lo, 0.0, x2), axis=-1, keepdims=True)
    ms = jnp.where(lo, s_lo, s_hi) * (1.0 / DIFF_QK_DIM)
    y = x * lax.rsqrt(ms + RMS_EPS) * g
    half = ROT_DIM // 2
    return y * c + pltpu.roll(y, LANES - half, 1) * s1 + pltpu.roll(y, half, 1) * s2


def _attn_kernel(q_ref, k_ref, v_ref, c_ref, s1_ref, s2_ref, qg_ref, kg_ref,
                 lq1_ref, lk1_ref, lq2_ref, lk2_ref, og_ref, o_ref,
                 q1_scr, q2_scr, k_scr, v_scr, *bufs):
    s_len = q_ref.shape[0]
    s_bufs, e_bufs = bufs[0:4], bufs[4:8]
    lam = (jnp.exp(jnp.sum(lq1_ref[...] * lk1_ref[...], axis=-1, keepdims=True))
           - jnp.exp(jnp.sum(lq2_ref[...] * lk2_ref[...], axis=-1, keepdims=True))
           + LAMBDA_INIT)
    lo = lax.broadcasted_iota(jnp.int32, (TQ, LANES), 1) < DIFF_QK_DIM
    for hd in range(ATT_HPS):
        v_scr[hd, :, 0:LANES] = v_ref[:, hd * LANES:(hd + 1) * LANES]
        v_scr[hd, :, LANES:] = jnp.ones((s_len, LANES), BF16)

    n_tiles = s_len // TQ
    tiles = [(hd, i) for hd in range(ATT_HPS) for i in range(n_tiles)]

    def prepare(t):
        hd, i = tiles[t]
        rows = slice(i * TQ, (i + 1) * TQ)
        lanes = slice(hd * LANES, (hd + 1) * LANES)
        c, s1, s2 = c_ref[rows, :], s1_ref[rows, :], s2_ref[rows, :]
        qn = (_norm_rope(q_ref[rows, lanes].astype(F32), qg_ref[...], c, s1, s2)
              * (DIFF_QK_DIM ** -0.5 * math.log2(math.e)))
        q1_scr[hd, rows, :] = jnp.where(lo, qn, 0.0).astype(BF16)
        q2_scr[hd, rows, :] = jnp.where(lo, 0.0, qn).astype(BF16)
        k_scr[hd, rows, :] = _norm_rope(k_ref[rows, lanes].astype(F32), kg_ref[...], c, s1,
                                        s2).astype(BF16)

    diag = (lax.broadcasted_iota(jnp.int32, (TQ, TQ), 1) // CHUNK
            <= lax.broadcasted_iota(jnp.int32, (TQ, TQ), 0) // CHUNK)

    items = [(t, q_scr) for t in range(len(tiles)) for q_scr in (q1_scr, q2_scr)]

    def scores(r):
        t, q_scr = items[r]
        hd, i = tiles[t]
        s_scr = s_bufs[r % 4]
        q = q_scr[hd, i * TQ:(i + 1) * TQ, :]
        n_off = i * TQ
        s_scr[:, n_off:n_off + TQ] = jnp.where(
            diag, _nt_dot(q, k_scr[hd, n_off:n_off + TQ, :]), -jnp.inf)
        if n_off:
            s_scr[:, 0:n_off] = _nt_dot(q, k_scr[hd, 0:n_off, :])

    def softmax_pv(r):
        hd, i = tiles[items[r][0]]
        nk = (i + 1) * TQ
        s_scr, e_scr = s_bufs[r % 4], e_bufs[r % 4]
        m = jnp.max(s_scr[:, 0:nk], axis=-1, keepdims=True)
        e_scr[:, 0:nk] = jnp.exp2(s_scr[:, 0:nk] - m).astype(BF16)
        acc = jnp.dot(e_scr[:, 0:nk], v_scr[hd, 0:nk, :], preferred_element_type=F32)
        return acc[:, 0:LANES] / acc[:, LANES:]

    prepare(0)
    scores(0)
    scores(1)
    for t, (hd, i) in enumerate(tiles):
        if t + 1 < len(tiles):
            prepare(t + 1)
            scores(2 * t + 2)
        o1 = softmax_pv(2 * t)
        if t + 1 < len(tiles):
            scores(2 * t + 3)
        o = o1 - lam * softmax_pv(2 * t + 1)
        ms = jnp.mean(o * o, axis=-1, keepdims=True)
        y = o * lax.rsqrt(ms + RMS_EPS) * og_ref[...] * (1.0 - LAMBDA_INIT)
        o_ref[i * TQ:(i + 1) * TQ, hd * LANES:(hd + 1) * LANES] = y.astype(BF16)


def _diff_attention(proj, tabs, qg, kg, lq1, lk1, lq2, lk2, og, batch, seq):
    c, s1, s2 = tabs
    h = DIFF_HEADS
    hg = h // ATT_HPS
    wide = ATT_HPS * LANES
    blk = lambda off: pl.BlockSpec((seq, wide), lambda b, hh, off=off: (b, off + hh))
    tab = pl.BlockSpec((seq, LANES), lambda b, hh: (b, 0))
    vec = lambda n: pl.BlockSpec((1, n), lambda b, hh: (0, 0))
    return pl.pallas_call(
        _attn_kernel,
        grid=(batch, hg),
        in_specs=[blk(0), blk(hg), blk(2 * hg), tab, tab, tab,
                  vec(LANES), vec(LANES), vec(DIFF_QK_DIM), vec(DIFF_QK_DIM),
                  vec(DIFF_QK_DIM), vec(DIFF_QK_DIM), vec(LANES)],
        out_specs=pl.BlockSpec((seq, wide), lambda b, hh: (b, hh)),
        out_shape=jax.ShapeDtypeStruct((batch * seq, h * DIFF_V_DIM), BF16),
        scratch_shapes=[pltpu.VMEM((ATT_HPS, seq, LANES), BF16)] * 3
        + [pltpu.VMEM((ATT_HPS, seq, 2 * LANES), BF16)]
        + [pltpu.VMEM((TQ, seq), F32)] * 4 + [pltpu.VMEM((TQ, seq), BF16)] * 4,
        compiler_params=_params(2),
        name="diff_attention",
    )(proj, proj, proj, c, s1, s2, qg, kg, lq1, lk1, lq2, lk2, og)


def _split_dot(ones_bf, x):
    hi = x.astype(BF16)
    lo = (x - hi.astype(F32)).astype(BF16)
    return (jnp.dot(ones_bf, hi, preferred_element_type=F32)
            + jnp.dot(ones_bf, lo, preferred_element_type=F32))


def _gla_kernel(q_ref, k_ref, v_ref, r_ref, ga_ref, w2_ref, b2_ref, og_ref, o_ref,
                qin_scr, ut_scr, dec_scr, st_scr, acc_scr):
    s_len = q_ref.shape[0]
    cpg = GB // CHUNK
    ri = lax.broadcasted_iota(jnp.int32, (GB, GB), 0)
    ci = lax.broadcasted_iota(jnp.int32, (GB, GB), 1)
    same = (ri // CHUNK) == (ci // CHUNK)
    blk_ones = jnp.where(same, 1.0, 0.0).astype(BF16)
    tril = same & (ci <= ri)
    tri_ones = jnp.where(tril, 1.0, 0.0).astype(BF16)

    def phase_a(g, carry):
        rows = pl.ds(pl.multiple_of(g * GB, GB), GB)
        pre = jnp.dot(ga_ref[rows, :].astype(BF16), w2_ref[...],
                      preferred_element_type=F32) + b2_ref[...]
        la = -(jnp.maximum(-pre, 0.0) + jnp.log1p(jnp.exp(-jnp.abs(pre)))) * (1.0 / GLA_TAU)
        bc = _split_dot(tri_ones, la)
        bl = _split_dot(blk_ones, la)
        e_neg = jnp.exp(-bc)
        e_last = jnp.exp(bl)
        k = k_ref[rows, :].astype(F32)
        q_in = (q_ref[rows, :].astype(F32) * (GLA_K_DIM ** -0.5) * jnp.exp(bc)).astype(BF16)
        k_in = (k * e_neg).astype(BF16)
        k_dec = (k * (e_last * e_neg)).astype(BF16)
        qin_scr[rows, :] = q_in
        dec_scr[rows, :] = e_last
        v = v_ref[rows, :]
        att = jnp.where(tril, _nt_dot(q_in, k_in), 0.0).astype(BF16)
        acc_scr[rows, :] = jnp.dot(att, v, preferred_element_type=F32)
        for c in range(cpg):
            cr = slice(c * CHUNK, (c + 1) * CHUNK)
            ut_scr[g * cpg + c] = _tn_dot(v[cr, :], k_dec[cr, :])
        return carry

    lax.fori_loop(0, s_len // GB, phase_a, 0, unroll=8)

    def phase_b(c, st):
        st_scr[c] = st.astype(BF16)
        dec = dec_scr[pl.ds(pl.multiple_of(c * CHUNK, CHUNK), 1), :]
        return dec * st + ut_scr[c]

    lax.fori_loop(0, s_len // CHUNK, phase_b, jnp.zeros((GLA_V_DIM, GLA_K_DIM), F32))

    def phase_c(g, carry):
        rows = pl.ds(pl.multiple_of(g * GB, GB), GB)
        inter = [_nt_dot(qin_scr[pl.ds(pl.multiple_of(g * GB + c * CHUNK, CHUNK), CHUNK), :],
                         st_scr[g * cpg + c]) for c in range(cpg)]
        o = acc_scr[rows, :] + jnp.concatenate(inter, axis=0)
        ms = jnp.mean(o * o, axis=-1, keepdims=True)
        y = o * lax.rsqrt(ms + RMS_EPS) * og_ref[...]
        o_ref[rows, :] = (y * _silu(r_ref[rows, :].astype(F32))).astype(BF16)
        return carry

    lax.fori_loop(0, s_len // GB, phase_c, 0, unroll=8)


def _gla(proj, ga, w2p, b2, og, batch, seq):
    hq = 3 * DIFF_HEADS
    kblk = lambda off: pl.BlockSpec((seq, GLA_K_DIM), lambda b, hh, off=off: (b, off + hh))
    vblk = lambda off: pl.BlockSpec((seq, GLA_V_DIM), lambda b, hh, off=off: (b, off + hh))
    return pl.pallas_call(
        _gla_kernel,
        grid=(batch, GLA_HEADS),
        in_specs=[kblk(hq), kblk(hq + GLA_HEADS), vblk(16), vblk(16 + GLA_HEADS),
                  pl.BlockSpec((seq, LANES), lambda b, hh: (b, 0)),
                  pl.BlockSpec((LANES, GLA_K_DIM), lambda b, hh: (0, hh)),
                  pl.BlockSpec((1, GLA_K_DIM), lambda b, hh: (0, hh)),
                  pl.BlockSpec((1, GLA_V_DIM), lambda b, hh: (0, 0))],
        out_specs=pl.BlockSpec((seq, GLA_V_DIM), lambda b, hh: (b, hh)),
        out_shape=jax.ShapeDtypeStruct((batch * seq, GLA_HEADS * GLA_V_DIM), BF16),
        scratch_shapes=[pltpu.VMEM((seq, GLA_K_DIM), BF16),
                        pltpu.VMEM((seq // CHUNK, GLA_V_DIM, GLA_K_DIM), F32),
                        pltpu.VMEM((seq, GLA_K_DIM), F32),
                        pltpu.VMEM((seq // CHUNK, GLA_V_DIM, GLA_K_DIM), BF16),
                        pltpu.VMEM((seq, GLA_V_DIM), F32)],
        compiler_params=_params(2),
        name="gla",
    )(proj, proj, proj, proj, ga, w2p, b2, og)


def _out_proj_kernel(a_ref, g_ref, x_ref, wo_ref, g2_ref, wr_ref, br_ref,
                     x1_ref, h2_ref, lg_ref, tt_scr):
    half = a_ref.shape[1]

    def body(c, carry):
        r0 = pl.multiple_of(c * RB_OUT, RB_OUT)
        rows = pl.ds(r0, RB_OUT)
        mixed = (jnp.dot(a_ref[rows, :], wo_ref[0:half, :], preferred_element_type=F32)
                 + jnp.dot(g_ref[rows, :], wo_ref[half:, :], preferred_element_type=F32))
        x1 = x_ref[rows, :] + mixed
        x1_ref[rows, :] = x1
        ms = jnp.mean(x1 * x1, axis=-1, keepdims=True)
        h2 = x1 * lax.rsqrt(ms + RMS_EPS) * g2_ref[...]
        _store_token_tiles(h2_ref, h2, tt_scr, r0)
        lg = jnp.dot(h2.astype(BF16), wr_ref[...], preferred_element_type=F32) + br_ref[...]
        lg_ref[:, rows] = lg.T[0:N_LOGIT_ROWS, :]
        return carry

    lax.fori_loop(0, TM_OUT // RB_OUT, body, 0)


def _out_proj(a_out, g_out, x2, wo, g2, wr, br):
    t = x2.shape[0]
    half = a_out.shape[1]
    row = lambda n: pl.BlockSpec((TM_OUT, n), lambda i: (i, 0))
    full = lambda r, n: pl.BlockSpec((r, n), lambda i: (0, 0))
    return pl.pallas_call(
        _out_proj_kernel,
        grid=(t // TM_OUT,),
        in_specs=[row(half), row(half), row(D_MODEL), full(D_MODEL, D_MODEL),
                  full(1, D_MODEL), full(D_MODEL, LANES), full(1, LANES)],
        out_specs=[row(D_MODEL), pl.BlockSpec((TM_OUT * SUB, LANES), lambda i: (i, 0)),
                   pl.BlockSpec((N_LOGIT_ROWS, TM_OUT), lambda i: (0, i))],
        out_shape=[jax.ShapeDtypeStruct((t, D_MODEL), F32),
                   jax.ShapeDtypeStruct((t * SUB, LANES), BF16),
                   jax.ShapeDtypeStruct((N_LOGIT_ROWS, t), F32)],
        scratch_shapes=[pltpu.VMEM((RB_OUT * TT_PITCH, LANES), F32)],
        compiler_params=_params(1),
        name="out_proj_router",
    )(a_out, g_out, x2, wo, g2, wr, br)


def _first_argmax(vals, row_f):
    m = jnp.max(vals, axis=0, keepdims=True)
    idx = jnp.min(jnp.where(vals == m, row_f, float(vals.shape[0])), axis=0, keepdims=True)
    return m, idx


def _softmax_rows(v):
    ex = jnp.exp(v - jnp.max(v, axis=0, keepdims=True))
    return ex / jnp.sum(ex, axis=0, keepdims=True)


def _routing_kernel(lg_ref, slots_ref, cw_ref, cnt_ref, tok_scr):
    t = lg_ref.shape[1]
    epg = EXPERTS_PER_GROUP
    row8 = lax.broadcasted_iota(jnp.int32, (8, TB), 0).astype(F32)
    row_e = lax.broadcasted_iota(jnp.int32, (N_EXPERTS, TB), 0).astype(F32)
    ui = lax.broadcasted_iota(jnp.int32, (TB, TB), 0)
    uj = lax.broadcasted_iota(jnp.int32, (TB, TB), 1)
    earlier = jnp.where(ui < uj, 1.0, 0.0).astype(BF16)
    all_ones = jnp.ones((TB, TB), BF16)

    def rows01(r0, r1):
        return jnp.where(row8 == 0.0, r0, jnp.where(row8 == 1.0, r1, jnp.zeros_like(r0)))

    def pass1(b, counts):
        cols = pl.ds(pl.multiple_of(b * TB, TB), TB)
        lg = lg_ref[:, cols]
        lg_groups = jnp.where(row8 < float(N_GROUPS), lg[0:GROUP_ROWS, :], -jnp.inf)
        pg_sel, g_sel = _first_argmax(_softmax_rows(lg_groups), row8)
        le = lg[GROUP_ROWS:GROUP_ROWS + epg, :]
        for g in range(1, N_GROUPS):
            le = jnp.where(g_sel == float(g),
                           lg[GROUP_ROWS + g * epg:GROUP_ROWS + (g + 1) * epg, :], le)
        pe = _softmax_rows(le)
        v1, i1 = _first_argmax(pe, row8)
        v2, i2 = _first_argmax(jnp.where(row8 == i1, -jnp.inf, pe), row8)
        tot = v1 + v2
        e1 = g_sel * epg + i1
        e2 = g_sel * epg + i2
        oh1 = row_e == e1
        oh2 = row_e == e2
        a = jnp.where(oh1 | oh2, 1.0, 0.0).astype(BF16)
        rank = jnp.dot(a, earlier, preferred_element_type=F32) + counts
        r1 = jnp.sum(jnp.where(oh1, rank, 0.0), axis=0, keepdims=True)
        r2 = jnp.sum(jnp.where(oh2, rank, 0.0), axis=0, keepdims=True)
        tok_scr[:, cols] = jnp.where(row8 == 0.0, e1, jnp.where(
            row8 == 1.0, e2, jnp.where(row8 == 2.0, r1, jnp.where(row8 == 3.0, r2, 0.0))))
        cw_ref[:, cols] = rows01((v1 / tot) * pg_sel, (v2 / tot) * pg_sel)
        return counts + jnp.dot(a, all_ones, preferred_element_type=F32)

    counts = lax.fori_loop(0, t // TB, pass1, jnp.zeros((N_EXPERTS, TB), F32))
    cnt_ref[...] = counts[:, 0:LANES]
    n_tiles = jnp.floor((counts + (TM_E - 1)) * (1.0 / TM_E))
    li = lax.broadcasted_iota(jnp.int32, (N_EXPERTS, N_EXPERTS), 0)
    lj = lax.broadcasted_iota(jnp.int32, (N_EXPERTS, N_EXPERTS), 1)
    lower = jnp.where(lj < li, 1.0, 0.0).astype(BF16)
    row_off = jnp.dot(lower, n_tiles.astype(BF16), preferred_element_type=F32) * TM_E

    def pass2(b, carry):
        cols = pl.ds(pl.multiple_of(b * TB, TB), TB)
        tok = tok_scr[:, cols]
        off1 = jnp.sum(jnp.where(row_e == tok[0:1, :], row_off, 0.0), axis=0, keepdims=True)
        off2 = jnp.sum(jnp.where(row_e == tok[1:2, :], row_off, 0.0), axis=0, keepdims=True)
        slots_ref[:, cols] = rows01(off1 + tok[2:3, :], off2 + tok[3:4, :]).astype(jnp.int32)
        return carry

    lax.fori_loop(0, t // TB, pass2, 0)


def _routing(logits_t):
    t = logits_t.shape[1]
    return pl.pallas_call(
        _routing_kernel,
        out_shape=[jax.ShapeDtypeStruct((8, t), jnp.int32),
                   jax.ShapeDtypeStruct((8, t), F32),
                   jax.ShapeDtypeStruct((N_EXPERTS, LANES), F32)],
        scratch_shapes=[pltpu.VMEM((8, t), F32)],
        compiler_params=pltpu.CompilerParams(vmem_limit_bytes=VMEM_LIMIT),
        name="routing",
    )(logits_t)


def _rows_copy(src, dst, sem, src_tok, dst_tok, n):
    first = lambda tok: tok * SUB if isinstance(tok, int) else pl.multiple_of(tok * SUB, SUB)
    s0 = first(src_tok)
    d0 = first(dst_tok)
    return pltpu.make_async_copy(src.at[pl.ds(s0, n * SUB), :], dst.at[pl.ds(d0, n * SUB), :], sem)


def _dispatch_kernel(sl1_ref, sl2_ref, pad0_ref, padn_ref, h2_ref, xs_hbm, zero_scr, sem, zsem):
    step = pl.program_id(0)

    def pad_copies(visit):
        for e in range(N_EXPERTS):
            pos = pad0_ref[e]
            p = TM_E // 2
            while p:
                has = (padn_ref[e] & p) != 0

                @pl.when(has)
                def _(pos=pos, p=p):
                    visit(_rows_copy(zero_scr, xs_hbm, zsem, 0, pos, p))
                pos = pos + jnp.where(has, p, 0)
                p //= 2

    @pl.when(step == 0)
    def _():
        zero_scr[...] = jnp.zeros_like(zero_scr)
        pad_copies(lambda c: c.start())

    @pl.when(step == pl.num_programs(0) - 1)
    def _():
        pad_copies(lambda c: c.wait())

    base = step * TB_D

    def issue(r, carry):
        _rows_copy(h2_ref, xs_hbm, sem, r, sl1_ref[base + r], 1).start(priority=0)
        _rows_copy(h2_ref, xs_hbm, sem, r, sl2_ref[base + r], 1).start(priority=1)
        return carry

    lax.fori_loop(0, TB_D, issue, 0, unroll=8)
    _rows_copy(h2_ref, xs_hbm, sem, 0, 0, TB_D).wait()
    _rows_copy(h2_ref, xs_hbm, sem, 0, 0, TB_D).wait()


def _dispatch(h2, sl1, sl2, pad_start, pad_len, n_rows):
    t = h2.shape[0] // SUB
    return pl.pallas_call(
        _dispatch_kernel,
        grid_spec=pltpu.PrefetchScalarGridSpec(
            num_scalar_prefetch=4,
            grid=(t // TB_D,),
            in_specs=[pl.BlockSpec((TB_D * SUB, LANES), lambda i, *_: (i, 0))],
            out_specs=pl.BlockSpec(memory_space=pl.ANY),
            scratch_shapes=[pltpu.VMEM((TM_E // 2 * SUB, LANES), BF16),
                            pltpu.SemaphoreType.DMA(()), pltpu.SemaphoreType.DMA(())],
        ),
        out_shape=jax.ShapeDtypeStruct((n_rows * SUB, LANES), BF16),
        compiler_params=_params(1),
        name="dispatch",
    )(sl1, sl2, pad_start, pad_len, h2)


def _expert_kernel(tidx_ref, texp_ref, tfirst_ref, tslot_ref, tnext_ref, nt_ref,
                   xs_ref, wg_hbm, wu_hbm, wd_hbm, ys_ref,
                   wg_buf, wu_buf, wd_buf, wg_bf, wu_bf, wd_bf, tt_scr, sems):
    j = pl.program_id(0)

    def weight_copies(e, slot):
        return [pltpu.make_async_copy(hbm.at[e], buf.at[slot], sems.at[slot, k])
                for k, (hbm, buf) in enumerate(((wg_hbm, wg_buf), (wu_hbm, wu_buf),
                                                (wd_hbm, wd_buf)))]

    @pl.when(j < nt_ref[0])
    def _():
        slot = tslot_ref[j]

        @pl.when(tfirst_ref[j] == 1)
        def _():
            @pl.when(j == 0)
            def _():
                for c in weight_copies(texp_ref[j], slot):
                    c.start()
            for c in weight_copies(texp_ref[j], slot):
                c.wait()

            @pl.when(tnext_ref[j] >= 0)
            def _():
                for c in weight_copies(tnext_ref[j], 1 - slot):
                    c.start(priority=1)
            wg_bf[...] = wg_buf[slot].astype(BF16)
            wu_bf[...] = wu_buf[slot].astype(BF16)
            wd_bf[...] = wd_buf[slot].astype(BF16)

        xb = _load_token_tiles(xs_ref, tt_scr, TM_E).astype(BF16)
        g = jnp.dot(xb, wg_bf[...], preferred_element_type=F32)
        u = jnp.dot(xb, wu_bf[...], preferred_element_type=F32)
        h = (_silu(g) * u).astype(BF16)
        _store_token_tiles(ys_ref, jnp.dot(h, wd_bf[...], preferred_element_type=F32), tt_scr)


def _expert_mlp(xs, tile_meta, wg, wu, wd):
    n_rows = xs.shape[0] // SUB
    tile = pl.BlockSpec((TM_E * SUB, LANES), lambda j, ti, *_: (ti[j], 0))
    hbm = pl.BlockSpec(memory_space=pl.ANY)
    return pl.pallas_call(
        _expert_kernel,
        grid_spec=pltpu.PrefetchScalarGridSpec(
            num_scalar_prefetch=len(tile_meta),
            grid=(n_rows // TM_E,),
            in_specs=[tile, hbm, hbm, hbm],
            out_specs=tile,
            scratch_shapes=[pltpu.VMEM((2, D_MODEL, D_FF), F32),
                            pltpu.VMEM((2, D_MODEL, D_FF), F32),
                            pltpu.VMEM((2, D_FF, D_MODEL), F32),
                            pltpu.VMEM((D_MODEL, D_FF), BF16),
                            pltpu.VMEM((D_MODEL, D_FF), BF16),
                            pltpu.VMEM((D_FF, D_MODEL), BF16),
                            pltpu.VMEM((TM_E * TT_PITCH, LANES), F32),
                            pltpu.SemaphoreType.DMA((2, 3))],
        ),
        out_shape=jax.ShapeDtypeStruct((n_rows * SUB, LANES), BF16),
        compiler_params=_params(1),
        name="expert_mlp",
    )(*tile_meta, xs, wg, wu, wd)


def _combine_kernel(sl1_ref, sl2_ref, x1_ref, cw_ref, ys_hbm, o_ref, y1_scr, y2_scr, tt1_scr,
                    tt2_scr, sems):
    i = pl.program_id(0)

    def issue(step, slot):
        base = step * TB

        def body(r, carry):
            _rows_copy(ys_hbm, y1_scr.at[slot], sems.at[slot], sl1_ref[base + r], r,
                       1).start(priority=0)
            _rows_copy(ys_hbm, y2_scr.at[slot], sems.at[slot], sl2_ref[base + r], r,
                       1).start(priority=1)
            return carry

        lax.fori_loop(0, TB, body, 0, unroll=8)

    @pl.when(i == 0)
    def _():
        issue(0, 0)

    last = pl.num_programs(0) - 1
    rb = RB_COMBINE
    n_pieces = (TB // rb) * SUB
    per_piece = TB // n_pieces

    def step(slot):
        _rows_copy(ys_hbm, y1_scr.at[slot], sems.at[slot], 0, 0, TB).wait()
        _rows_copy(ys_hbm, y2_scr.at[slot], sems.at[slot], 0, 0, TB).wait()
        nbase = jnp.minimum(i + 1, last) * TB
        cw = jnp.concatenate([cw_ref[...], jnp.zeros((LANES - 8, TB), F32)], axis=0).T
        _tt_fill(tt1_scr, y1_scr.at[slot], TB)
        _tt_fill(tt2_scr, y2_scr.at[slot], TB)
        piece = 0
        for r0 in range(0, TB, rb):
            rows = slice(r0, r0 + rb)
            c1 = jnp.broadcast_to(cw[rows, 0:1], (rb, LANES))
            c2 = jnp.broadcast_to(cw[rows, 1:2], (rb, LANES))
            for s in range(SUB):
                for r in range(piece * per_piece, (piece + 1) * per_piece):
                    _rows_copy(ys_hbm, y1_scr.at[1 - slot], sems.at[1 - slot],
                               sl1_ref[nbase + r], r, 1).start(priority=0)
                    _rows_copy(ys_hbm, y2_scr.at[1 - slot], sems.at[1 - slot],
                               sl2_ref[nbase + r], r, 1).start(priority=1)
                piece += 1
                cols = slice(s * LANES, (s + 1) * LANES)
                tiles = _tt_piece(s, rb, r0)
                o_ref[rows, cols] = (x1_ref[rows, cols] + c1 * tt1_scr[tiles]
                                     + c2 * tt2_scr[tiles])

        @pl.when(i == last)
        def _():
            _rows_copy(ys_hbm, y1_scr.at[1 - slot], sems.at[1 - slot], 0, 0, TB).wait()
            _rows_copy(ys_hbm, y2_scr.at[1 - slot], sems.at[1 - slot], 0, 0, TB).wait()

    for parity in (0, 1):
        @pl.when(i % 2 == parity)
        def _():
            step(parity)


def _combine(x1, cw, ys, sl1, sl2):
    t = x1.shape[0]
    return pl.pallas_call(
        _combine_kernel,
        grid_spec=pltpu.PrefetchScalarGridSpec(
            num_scalar_prefetch=2,
            grid=(t // TB,),
            in_specs=[pl.BlockSpec((TB, D_MODEL), lambda i, *_: (i, 0)),
                      pl.BlockSpec((8, TB), lambda i, *_: (0, i)),
                      pl.BlockSpec(memory_space=pl.ANY)],
            out_specs=pl.BlockSpec((TB, D_MODEL), lambda i, *_: (i, 0)),
            scratch_shapes=[pltpu.VMEM((2, TB * SUB, LANES), BF16),
                            pltpu.VMEM((2, TB * SUB, LANES), BF16),
                            pltpu.VMEM((TB * TT_PITCH, LANES), F32),
                            pltpu.VMEM((TB * TT_PITCH, LANES), F32),
                            pltpu.SemaphoreType.DMA((2,))],
        ),
        out_shape=jax.ShapeDtypeStruct((t, D_MODEL), F32),
        compiler_params=_params(1),
        name="combine",
    )(sl1, sl2, x1, cw, ys)


def _lane_tile(v, reps):
    return jnp.tile(v.reshape(1, -1), (1, reps))


def kernel(x, positions, norm1_g, w_in, q_norm_g, k_norm_g, lambda_q1, lambda_k1, lambda_q2,
           lambda_k2, diff_out_norm_g, gla_w_gate2, gla_b_gate, gla_out_norm_g, w_out, norm2_g,
           w_router_group, b_router_group, w_router_expert, b_router_expert, w_gate_expert,
           w_up_expert, w_down_expert):
    batch, seq, d = x.shape
    t = batch * seq
    assert d == D_MODEL and w_in.shape == (1, D_MODEL, D_MAIN + GLA_GATE_RANK)
    assert seq % TQ == 0 and seq % GB == 0 and t % TM_IN == 0 and t % TM_OUT == 0 and t % TB_D == 0
    x2 = x.reshape(t, d)

    inv = ROPE_THETA ** (-jnp.arange(0, ROT_DIM, 2, dtype=F32) / ROT_DIM)
    lane_d = jnp.arange(LANES) % DIFF_QK_DIM
    invf = jnp.where(lane_d < ROT_DIM, inv[lane_d % (ROT_DIM // 2)], 0.0).reshape(1, LANES)
    pos_b = jnp.broadcast_to(positions.astype(F32).reshape(t, 1), (t, LANES))
    tabs = _rope_tables(pos_b, invf)

    w_in_t = jnp.swapaxes(w_in, 1, 2)[0]
    wa_t = jnp.pad(w_in_t[D_MAIN:, :], ((0, LANES - GLA_GATE_RANK), (0, 0)))
    proj, ga = _in_proj(x2, norm1_g, w_in_t, wa_t)

    a_out = _diff_attention(
        proj, tabs, _lane_tile(q_norm_g[0], 2), _lane_tile(k_norm_g[0], 2),
        lambda_q1, lambda_k1, lambda_q2, lambda_k2, diff_out_norm_g, batch, seq)
    w2p = jnp.pad(gla_w_gate2[0], ((0, LANES - GLA_GATE_RANK), (0, 0))).astype(BF16)
    g_out = _gla(proj, ga, w2p, gla_b_gate, gla_out_norm_g, batch, seq)

    gpad = GROUP_ROWS - N_GROUPS
    lpad = LANES - N_LOGIT_ROWS
    wr = jnp.concatenate(
        [jnp.pad(w_router_group[0], ((0, 0), (0, gpad))),
         jnp.pad(w_router_expert[0].transpose(1, 0, 2).reshape(d, N_EXPERTS),
                 ((0, 0), (0, lpad)))], axis=1)
    br = jnp.concatenate([jnp.pad(b_router_group[0], (0, gpad)),
                          jnp.pad(b_router_expert[0].reshape(-1), (0, lpad))]).reshape(1, LANES)
    x1, h2, logits_t = _out_proj(a_out, g_out, x2, w_out[0].astype(BF16), norm2_g,
                                 wr.astype(BF16), br)

    slots, cw, counts = _routing(logits_t)
    sl1 = slots[0]
    sl2 = slots[1]
    n_rows = 2 * t + N_EXPERTS * TM_E
    max_tiles = n_rows // TM_E
    cnt = counts[:, 0].astype(jnp.int32)
    tile_end = jnp.cumsum((cnt + (TM_E - 1)) // TM_E)
    n_tiles = tile_end[-1:]
    tile_idx = jnp.minimum(jnp.arange(max_tiles, dtype=jnp.int32), n_tiles[0] - 1)
    tile_exp = jnp.sum(tile_idx[:, None] >= tile_end[None, :], axis=1).astype(jnp.int32)
    row_end = (tile_end * TM_E).astype(jnp.int32)
    n_tile_e = (cnt + (TM_E - 1)) // TM_E
    pad_start = (row_end - n_tile_e * TM_E + cnt).astype(jnp.int32)
    pad_len = (row_end - pad_start).astype(jnp.int32)
    tile_first = jnp.concatenate(
        [jnp.ones((1,), jnp.int32), (tile_exp[1:] != tile_exp[:-1]).astype(jnp.int32)])
    tile_slot = (jnp.cumsum(tile_first) - 1) % 2
    nxt = tile_end[tile_exp]
    tile_next = jnp.where(nxt < n_tiles[0], tile_exp[jnp.minimum(nxt, max_tiles - 1)], -1)
    tile_meta = [a.astype(jnp.int32)
                 for a in (tile_idx, tile_exp, tile_first, tile_slot, tile_next, n_tiles)]

    xs = _dispatch(h2, sl1, sl2, pad_start, pad_len, n_rows)
    ys = _expert_mlp(xs, tile_meta,
                     w_gate_expert[0].reshape(N_EXPERTS, d, D_FF),
                     w_up_expert[0].reshape(N_EXPERTS, d, D_FF),
                     w_down_expert[0].reshape(N_EXPERTS, D_FF, d))
    out = _combine(x1, cw, ys, sl1, sl2)
    return out.reshape(batch, seq, d)
```

```python
import math

import jax
import jax.numpy as jnp
from jax import lax
from jax.experimental import pallas as pl
from jax.experimental.pallas import tpu as pltpu

D_MODEL = 2048
CHUNK = 64
DIFF_QK_DIM = 64
DIFF_V_DIM = 128
DIFF_HEADS = 8
ROT_DIM = 16
ROPE_THETA = 500000.0
GLA_HEADS = 4
GLA_V_DIM = 256
GLA_K_DIM = 128
GLA_GATE_RANK = 16
GLA_TAU = 16.0
N_GROUPS = 4
EXPERTS_PER_GROUP = 8
N_EXPERTS = N_GROUPS * EXPERTS_PER_GROUP
D_FF = 512
RMS_EPS = 1e-6
LAMBDA_INIT = 0.8 - 0.6 * math.exp(-0.3 * 0)
D_MAIN = 6144
GROUP_ROWS = 8
N_LOGIT_ROWS = GROUP_ROWS + N_EXPERTS

LANES = 128
SUB = D_MODEL // LANES
TT_PITCH = 24
VMEM_LIMIT = 56 * 1024 * 1024

TB_ROPE = 1024
TM_IN = 1024
TN_IN = 1024
RB_IN = 256
TQ = 256
ATT_HPS = 2
GB = 256
TM_OUT = 512
RB_OUT = 256
TB = 256
RB_COMBINE = 64
TB_D = 1024
TM_E = 256

F32 = jnp.float32
BF16 = jnp.bfloat16


def _params(n_axes):
    return pltpu.CompilerParams(dimension_semantics=("arbitrary",) * n_axes,
                                vmem_limit_bytes=VMEM_LIMIT)


def _nt_dot(a, b):
    return lax.dot_general(a, b, (((1,), (1,)), ((), ())), preferred_element_type=F32)


def _tn_dot(a, b):
    return lax.dot_general(a, b, (((0,), (0,)), ((), ())), preferred_element_type=F32)


def _silu(x):
    return x * (1.0 / (1.0 + jnp.exp(-x)))


def _tt_fill(scr, ref, n, tok0=0):
    for t in range(n):
        scr[t * TT_PITCH:t * TT_PITCH + SUB, :] = ref[pl.ds((tok0 + t) * SUB, SUB), :].astype(F32)


def _tt_drain(ref, scr, n, tok0=0):
    for t in range(n):
        ref[pl.ds((tok0 + t) * SUB, SUB), :] = scr[t * TT_PITCH:t * TT_PITCH + SUB, :].astype(BF16)


def _tt_piece(s, n, tok0=0):
    return (pl.ds(tok0 * TT_PITCH + s, n, stride=TT_PITCH), slice(None))


def _store_token_tiles(ref, val, scr, tok0=0):
    n = val.shape[0]
    for s in range(SUB):
        scr[_tt_piece(s, n)] = val[:, s * LANES:(s + 1) * LANES]
    _tt_drain(ref, scr, n, tok0)


def _load_token_tiles(ref, scr, n, tok0=0):
    _tt_fill(scr, ref, n, tok0)
    return jnp.concatenate([scr[_tt_piece(s, n)] for s in range(SUB)], axis=1)


def _rope_table_kernel(pos_ref, invf_ref, c_ref, s1_ref, s2_ref):
    ang = pos_ref[...] * invf_ref[...]
    d = lax.broadcasted_iota(jnp.int32, ang.shape, 1) % DIFF_QK_DIM
    cos = jnp.cos(ang)
    sin = jnp.sin(ang)
    half = ROT_DIM // 2
    c_ref[...] = jnp.where(d < ROT_DIM, cos, 1.0)
    s1_ref[...] = jnp.where(d < half, -sin, 0.0)
    s2_ref[...] = jnp.where((d >= half) & (d < ROT_DIM), sin, 0.0)


def _rope_tables(pos_b, invf):
    t = pos_b.shape[0]
    spec = pl.BlockSpec((TB_ROPE, LANES), lambda i: (i, 0))
    return pl.pallas_call(
        _rope_table_kernel,
        grid=(t // TB_ROPE,),
        in_specs=[spec, pl.BlockSpec((1, LANES), lambda i: (0, 0))],
        out_specs=[spec, spec, spec],
        out_shape=[jax.ShapeDtypeStruct((t, LANES), F32)] * 3,
        compiler_params=_params(1),
        name="rope_tables",
    )(pos_b, invf)


def _in_proj_kernel(x_ref, g_ref, wt_ref, wat_ref, proj_ref, ga_ref, h_scr):
    @pl.when(pl.program_id(1) == 0)
    def _():
        def body(c, carry):
            rows = pl.ds(c * RB_IN, RB_IN)
            x = x_ref[rows, :]
            ms = jnp.mean(x * x, axis=-1, keepdims=True)
            h_scr[rows, :] = (x * lax.rsqrt(ms + RMS_EPS) * g_ref[...]).astype(BF16)
            return carry
        lax.fori_loop(0, TM_IN // RB_IN, body, 0)
        ga_ref[...] = _nt_dot(h_scr[...], wat_ref[...].astype(BF16))

    proj_ref[...] = _nt_dot(h_scr[...], wt_ref[...].astype(BF16)).astype(BF16)


def _in_proj(x2, g1, w_in, wa):
    t = x2.shape[0]
    return pl.pallas_call(
        _in_proj_kernel,
        grid=(t // TM_IN, D_MAIN // TN_IN),
        in_specs=[
            pl.BlockSpec((TM_IN, D_MODEL), lambda i, j: (i, 0)),
            pl.BlockSpec((1, D_MODEL), lambda i, j: (0, 0)),
            pl.BlockSpec((TN_IN, D_MODEL), lambda i, j: (j, 0)),
            pl.BlockSpec((LANES, D_MODEL), lambda i, j: (0, 0)),
        ],
        out_specs=[
            pl.BlockSpec((TM_IN, TN_IN), lambda i, j: (i, j)),
            pl.BlockSpec((TM_IN, LANES), lambda i, j: (i, 0)),
        ],
        out_shape=[jax.ShapeDtypeStruct((t, D_MAIN), BF16),
                   jax.ShapeDtypeStruct((t, LANES), F32)],
        scratch_shapes=[pltpu.VMEM((TM_IN, D_MODEL), BF16)],
        compiler_params=_params(2),
        name="in_proj",
    )(x2, g1, w_in, wa)


def _norm_rope(x, g, c, s1, s2):
    lo = lax.broadcasted_iota(jnp.int32, x.shape, 1) < DIFF_QK_DIM
    x2 = x * x
    s_lo = jnp.sum(jnp.where(lo, x2, 0.0), axis=-1, keepdims=True)
    s_hi = jnp.sum(jnp.where(lo, 0.0, x2), axis=-1, keepdims=True)
    ms = jnp.where(lo, s_lo, s_hi) * (1.0 / DIFF_QK_DIM)
    y = x * lax.rsqrt(ms + RMS_EPS) * g
    half = ROT_DIM // 2
    return y * c + pltpu.roll(y, LANES - half, 1) * s1 + pltpu.roll(y, half, 1) * s2


def _attn_kernel(q_ref, k_ref, v_ref, c_ref, s1_ref, s2_ref, qg_ref, kg_ref,
                 lq1_ref, lk1_ref, lq2_ref, lk2_ref, og_ref, o_ref,
                 q1_scr, q2_scr, k_scr, v_scr, *bufs):
    s_len = q_ref.shape[0]
    s_bufs, e_bufs = bufs[0:4], bufs[4:8]
    lam = (jnp.exp(jnp.sum(lq1_ref[...] * lk1_ref[...], axis=-1, keepdims=True))
           - jnp.exp(jnp.sum(lq2_ref[...] * lk2_ref[...], axis=-1, keepdims=True))
           + LAMBDA_INIT)
    lo = lax.broadcasted_iota(jnp.int32, (TQ, LANES), 1) < DIFF_QK_DIM
    for hd in range(ATT_HPS):
        v_scr[hd, :, 0:LANES] = v_ref[:, hd * LANES:(hd + 1) * LANES]
        v_scr[hd, :, LANES:] = jnp.ones((s_len, LANES), BF16)

    n_tiles = s_len // TQ
    tiles = [(hd, i) for hd in range(ATT_HPS) for i in range(n_tiles)]

    def prepare(t):
        hd, i = tiles[t]
        rows = slice(i * TQ, (i + 1) * TQ)
        lanes = slice(hd * LANES, (hd + 1) * LANES)
        c, s1, s2 = c_ref[rows, :], s1_ref[rows, :], s2_ref[rows, :]
        qn = (_norm_rope(q_ref[rows, lanes].astype(F32), qg_ref[...], c, s1, s2)
              * (DIFF_QK_DIM ** -0.5 * math.log2(math.e)))
        q1_scr[hd, rows, :] = jnp.where(lo, qn, 0.0).astype(BF16)
        q2_scr[hd, rows, :] = jnp.where(lo, 0.0, qn).astype(BF16)
        k_scr[hd, rows, :] = _norm_rope(k_ref[rows, lanes].astype(F32), kg_ref[...], c, s1,
                                        s2).astype(BF16)

    diag = (lax.broadcasted_iota(jnp.int32, (TQ, TQ), 1) // CHUNK
            <= lax.broadcasted_iota(jnp.int32, (TQ, TQ), 0) // CHUNK)

    items = [(t, q_scr) for t in range(len(tiles)) for q_scr in (q1_scr, q2_scr)]

    def scores(r):
        t, q_scr = items[r]
        hd, i = tiles[t]
        s_scr = s_bufs[r % 4]
        q = q_scr[hd, i * TQ:(i + 1) * TQ, :]
        n_off = i * TQ
        s_scr[:, n_off:n_off + TQ] = jnp.where(
            diag, _nt_dot(q, k_scr[hd, n_off:n_off + TQ, :]), -jnp.inf)
        if n_off:
            s_scr[:, 0:n_off] = _nt_dot(q, k_scr[hd, 0:n_off, :])

    def softmax_pv(r):
        hd, i = tiles[items[r][0]]
        nk = (i + 1) * TQ
        s_scr, e_scr = s_bufs[r % 4], e_bufs[r % 4]
        m = jnp.max(s_scr[:, 0:nk], axis=-1, keepdims=True)
        e_scr[:, 0:nk] = jnp.exp2(s_scr[:, 0:nk] - m).astype(BF16)
        acc = jnp.dot(e_scr[:, 0:nk], v_scr[hd, 0:nk, :], preferred_element_type=F32)
        return acc[:, 0:LANES] / acc[:, LANES:]

    prepare(0)
    scores(0)
    scores(1)
    for t, (hd, i) in enumerate(tiles):
        if t + 1 < len(tiles):
            prepare(t + 1)
            scores(2 * t + 2)
        o1 = softmax_pv(2 * t)
        if t + 1 < len(tiles):
            scores(2 * t + 3)
        o = o1 - lam * softmax_pv(2 * t + 1)
        ms = jnp.mean(o * o, axis=-1, keepdims=True)
        y = o * lax.rsqrt(ms + RMS_EPS) * og_ref[...] * (1.0 - LAMBDA_INIT)
        o_ref[i * TQ:(i + 1) * TQ, hd * LANES:(hd + 1) * LANES] = y.astype(BF16)


def _diff_attention(proj, tabs, qg, kg, lq1, lk1, lq2, lk2, og, batch, seq):
    c, s1, s2 = tabs
    h = DIFF_HEADS
    hg = h // ATT_HPS
    wide = ATT_HPS * LANES
    blk = lambda off: pl.BlockSpec((seq, wide), lambda b, hh, off=off: (b, off + hh))
    tab = pl.BlockSpec((seq, LANES), lambda b, hh: (b, 0))
    vec = lambda n: pl.BlockSpec((1, n), lambda b, hh: (0, 0))
    return pl.pallas_call(
        _attn_kernel,
        grid=(batch, hg),
        in_specs=[blk(0), blk(hg), blk(2 * hg), tab, tab, tab,
                  vec(LANES), vec(LANES), vec(DIFF_QK_DIM), vec(DIFF_QK_DIM),
                  vec(DIFF_QK_DIM), vec(DIFF_QK_DIM), vec(LANES)],
        out_specs=pl.BlockSpec((seq, wide), lambda b, hh: (b, hh)),
        out_shape=jax.ShapeDtypeStruct((batch * seq, h * DIFF_V_DIM), BF16),
        scratch_shapes=[pltpu.VMEM((ATT_HPS, seq, LANES), BF16)] * 3
        + [pltpu.VMEM((ATT_HPS, seq, 2 * LANES), BF16)]
        + [pltpu.VMEM((TQ, seq), F32)] * 4 + [pltpu.VMEM((TQ, seq), BF16)] * 4,
        compiler_params=_params(2),
        name="diff_attention",
    )(proj, proj, proj, c, s1, s2, qg, kg, lq1, lk1, lq2, lk2, og)


def _split_dot(ones_bf, x):
    hi = x.astype(BF16)
    lo = (x - hi.astype(F32)).astype(BF16)
    return (jnp.dot(ones_bf, hi, preferred_element_type=F32)
            + jnp.dot(ones_bf, lo, preferred_element_type=F32))


def _gla_kernel(q_ref, k_ref, v_ref, r_ref, ga_ref, w2_ref, b2_ref, og_ref, o_ref,
                qin_scr, ut_scr, dec_scr, st_scr, acc_scr):
    s_len = q_ref.shape[0]
    cpg = GB // CHUNK
    ri = lax.broadcasted_iota(jnp.int32, (GB, GB), 0)
    ci = lax.broadcasted_iota(jnp.int32, (GB, GB), 1)
    same = (ri // CHUNK) == (ci // CHUNK)
    blk_ones = jnp.where(same, 1.0, 0.0).astype(BF16)
    tril = same & (ci <= ri)
    tri_ones = jnp.where(tril, 1.0, 0.0).astype(BF16)

    def phase_a(g, carry):
        rows = pl.ds(pl.multiple_of(g * GB, GB), GB)
        pre = jnp.dot(ga_ref[rows, :].astype(BF16), w2_ref[...],
                      preferred_element_type=F32) + b2_ref[...]
        la = -(jnp.maximum(-pre, 0.0) + jnp.log1p(jnp.exp(-jnp.abs(pre)))) * (1.0 / GLA_TAU)
        bc = _split_dot(tri_ones, la)
        bl = _split_dot(blk_ones, la)
        e_neg = jnp.exp(-bc)
        e_last = jnp.exp(bl)
        k = k_ref[rows, :].astype(F32)
        q_in = (q_ref[rows, :].astype(F32) * (GLA_K_DIM ** -0.5) * jnp.exp(bc)).astype(BF16)
        k_in = (k * e_neg).astype(BF16)
        k_dec = (k * (e_last * e_neg)).astype(BF16)
        qin_scr[rows, :] = q_in
        dec_scr[rows, :] = e_last
        v = v_ref[rows, :]
        att = jnp.where(tril, _nt_dot(q_in, k_in), 0.0).astype(BF16)
        acc_scr[rows, :] = jnp.dot(att, v, preferred_element_type=F32)
        for c in range(cpg):
            cr = slice(c * CHUNK, (c + 1) * CHUNK)
            ut_scr[g * cpg + c] = _tn_dot(v[cr, :], k_dec[cr, :])
        return carry

    lax.fori_loop(0, s_len // GB, phase_a, 0, unroll=8)

    def phase_b(c, st):
        st_scr[c] = st.astype(BF16)
        dec = dec_scr[pl.ds(pl.multiple_of(c * CHUNK, CHUNK), 1), :]
        return dec * st + ut_scr[c]

    lax.fori_loop(0, s_len // CHUNK, phase_b, jnp.zeros((GLA_V_DIM, GLA_K_DIM), F32))

    def phase_c(g, carry):
        rows = pl.ds(pl.multiple_of(g * GB, GB), GB)
        inter = [_nt_dot(qin_scr[pl.ds(pl.multiple_of(g * GB + c * CHUNK, CHUNK), CHUNK), :],
                         st_scr[g * cpg + c]) for c in range(cpg)]
        o = acc_scr[rows, :] + jnp.concatenate(inter, axis=0)
        ms = jnp.mean(o * o, axis=-1, keepdims=True)
        y = o * lax.rsqrt(ms + RMS_EPS) * og_ref[...]
        o_ref[rows, :] = (y * _silu(r_ref[rows, :].astype(F32))).astype(BF16)
        return carry

    lax.fori_loop(0, s_len // GB, phase_c, 0, unroll=8)


def _gla(proj, ga, w2p, b2, og, batch, seq):
    hq = 3 * DIFF_HEADS
    kblk = lambda off: pl.BlockSpec((seq, GLA_K_DIM), lambda b, hh, off=off: (b, off + hh))
    vblk = lambda off: pl.BlockSpec((seq, GLA_V_DIM), lambda b, hh, off=off: (b, off + hh))
    return pl.pallas_call(
        _gla_kernel,
        grid=(batch, GLA_HEADS),
        in_specs=[kblk(hq), kblk(hq + GLA_HEADS), vblk(16), vblk(16 + GLA_HEADS),
                  pl.BlockSpec((seq, LANES), lambda b, hh: (b, 0)),
                  pl.BlockSpec((LANES, GLA_K_DIM), lambda b, hh: (0, hh)),
                  pl.BlockSpec((1, GLA_K_DIM), lambda b, hh: (0, hh)),
                  pl.BlockSpec((1, GLA_V_DIM), lambda b, hh: (0, 0))],
        out_specs=pl.BlockSpec((seq, GLA_V_DIM), lambda b, hh: (b, hh)),
        out_shape=jax.ShapeDtypeStruct((batch * seq, GLA_HEADS * GLA_V_DIM), BF16),
        scratch_shapes=[pltpu.VMEM((seq, GLA_K_DIM), BF16),
                        pltpu.VMEM((seq // CHUNK, GLA_V_DIM, GLA_K_DIM), F32),
                        pltpu.VMEM((seq, GLA_K_DIM), F32),
                        pltpu.VMEM((seq // CHUNK, GLA_V_DIM, GLA_K_DIM), BF16),
                        pltpu.VMEM((seq, GLA_V_DIM), F32)],
        compiler_params=_params(2),
        name="gla",
    )(proj, proj, proj, proj, ga, w2p, b2, og)


def _out_proj_kernel(a_ref, g_ref, x_ref, wo_ref, g2_ref, wr_ref, br_ref,
                     x1_ref, h2_ref, lg_ref, tt_scr):
    half = a_ref.shape[1]

    def body(c, carry):
        r0 = pl.multiple_of(c * RB_OUT, RB_OUT)
        rows = pl.ds(r0, RB_OUT)
        mixed = (jnp.dot(a_ref[rows, :], wo_ref[0:half, :], preferred_element_type=F32)
                 + jnp.dot(g_ref[rows, :], wo_ref[half:, :], preferred_element_type=F32))
        x1 = x_ref[rows, :] + mixed
        x1_ref[rows, :] = x1
        ms = jnp.mean(x1 * x1, axis=-1, keepdims=True)
        h2 = x1 * lax.rsqrt(ms + RMS_EPS) * g2_ref[...]
        _store_token_tiles(h2_ref, h2, tt_scr, r0)
        lg = jnp.dot(h2.astype(BF16), wr_ref[...], preferred_element_type=F32) + br_ref[...]
        lg_ref[:, rows] = lg.T[0:N_LOGIT_ROWS, :]
        return carry

    lax.fori_loop(0, TM_OUT // RB_OUT, body, 0)


def _out_proj(a_out, g_out, x2, wo, g2, wr, br):
    t = x2.shape[0]
    half = a_out.shape[1]
    row = lambda n: pl.BlockSpec((TM_OUT, n), lambda i: (i, 0))
    full = lambda r, n: pl.BlockSpec((r, n), lambda i: (0, 0))
    return pl.pallas_call(
        _out_proj_kernel,
        grid=(t // TM_OUT,),
        in_specs=[row(half), row(half), row(D_MODEL), full(D_MODEL, D_MODEL),
                  full(1, D_MODEL), full(D_MODEL, LANES), full(1, LANES)],
        out_specs=[row(D_MODEL), pl.BlockSpec((TM_OUT * SUB, LANES), lambda i: (i, 0)),
                   pl.BlockSpec((N_LOGIT_ROWS, TM_OUT), lambda i: (0, i))],
        out_shape=[jax.ShapeDtypeStruct((t, D_MODEL), F32),
                   jax.ShapeDtypeStruct((t * SUB, LANES), BF16),
                   jax.ShapeDtypeStruct((N_LOGIT_ROWS, t), F32)],
        scratch_shapes=[pltpu.VMEM((RB_OUT * TT_PITCH, LANES), F32)],
        compiler_params=_params(1),
        name="out_proj_router",
    )(a_out, g_out, x2, wo, g2, wr, br)


def _first_argmax(vals, row_f):
    m = jnp.max(vals, axis=0, keepdims=True)
    idx = jnp.min(jnp.where(vals == m, row_f, float(vals.shape[0])), axis=0, keepdims=True)
    return m, idx


def _softmax_rows(v):
    ex = jnp.exp(v - jnp.max(v, axis=0, keepdims=True))
    return ex / jnp.sum(ex, axis=0, keepdims=True)


def _routing_kernel(lg_ref, slots_ref, cw_ref, cnt_ref, tok_scr):
    t = lg_ref.shape[1]
    epg = EXPERTS_PER_GROUP
    row8 = lax.broadcasted_iota(jnp.int32, (8, TB), 0).astype(F32)
    row_e = lax.broadcasted_iota(jnp.int32, (N_EXPERTS, TB), 0).astype(F32)
    ui = lax.broadcasted_iota(jnp.int32, (TB, TB), 0)
    uj = lax.broadcasted_iota(jnp.int32, (TB, TB), 1)
    earlier = jnp.where(ui < uj, 1.0, 0.0).astype(BF16)
    all_ones = jnp.ones((TB, TB), BF16)

    def rows01(r0, r1):
        return jnp.where(row8 == 0.0, r0, jnp.where(row8 == 1.0, r1, jnp.zeros_like(r0)))

    def pass1(b, counts):
        cols = pl.ds(pl.multiple_of(b * TB, TB), TB)
        lg = lg_ref[:, cols]
        lg_groups = jnp.where(row8 < float(N_GROUPS), lg[0:GROUP_ROWS, :], -jnp.inf)
        pg_sel, g_sel = _first_argmax(_softmax_rows(lg_groups), row8)
        le = lg[GROUP_ROWS:GROUP_ROWS + epg, :]
        for g in range(1, N_GROUPS):
            le = jnp.where(g_sel == float(g),
                           lg[GROUP_ROWS + g * epg:GROUP_ROWS + (g + 1) * epg, :], le)
        pe = _softmax_rows(le)
        v1, i1 = _first_argmax(pe, row8)
        v2, i2 = _first_argmax(jnp.where(row8 == i1, -jnp.inf, pe), row8)
        tot = v1 + v2
        e1 = g_sel * epg + i1
        e2 = g_sel * epg + i2
        oh1 = row_e == e1
        oh2 = row_e == e2
        a = jnp.where(oh1 | oh2, 1.0, 0.0).astype(BF16)
        rank = jnp.dot(a, earlier, preferred_element_type=F32) + counts
        r1 = jnp.sum(jnp.where(oh1, rank, 0.0), axis=0, keepdims=True)
        r2 = jnp.sum(jnp.where(oh2, rank, 0.0), axis=0, keepdims=True)
        tok_scr[:, cols] = jnp.where(row8 == 0.0, e1, jnp.where(
            row8 == 1.0, e2, jnp.where(row8 == 2.0, r1, jnp.where(row8 == 3.0, r2, 0.0))))
        cw_ref[:, cols] = rows01((v1 / tot) * pg_sel, (v2 / tot) * pg_sel)
        return counts + jnp.dot(a, all_ones, preferred_element_type=F32)

    counts = lax.fori_loop(0, t // TB, pass1, jnp.zeros((N_EXPERTS, TB), F32))
    cnt_ref[...] = counts[:, 0:LANES]
    n_tiles = jnp.floor((counts + (TM_E - 1)) * (1.0 / TM_E))
    li = lax.broadcasted_iota(jnp.int32, (N_EXPERTS, N_EXPERTS), 0)
    lj = lax.broadcasted_iota(jnp.int32, (N_EXPERTS, N_EXPERTS), 1)
    lower = jnp.where(lj < li, 1.0, 0.0).astype(BF16)
    row_off = jnp.dot(lower, n_tiles.astype(BF16), preferred_element_type=F32) * TM_E

    def pass2(b, carry):
        cols = pl.ds(pl.multiple_of(b * TB, TB), TB)
        tok = tok_scr[:, cols]
        off1 = jnp.sum(jnp.where(row_e == tok[0:1, :], row_off, 0.0), axis=0, keepdims=True)
        off2 = jnp.sum(jnp.where(row_e == tok[1:2, :], row_off, 0.0), axis=0, keepdims=True)
        slots_ref[:, cols] = rows01(off1 + tok[2:3, :], off2 + tok[3:4, :]).astype(jnp.int32)
        return carry

    lax.fori_loop(0, t // TB, pass2, 0)


def _routing(logits_t):
    t = logits_t.shape[1]
    return pl.pallas_call(
        _routing_kernel,
        out_shape=[jax.ShapeDtypeStruct((8, t), jnp.int32),
                   jax.ShapeDtypeStruct((8, t), F32),
                   jax.ShapeDtypeStruct((N_EXPERTS, LANES), F32)],
        scratch_shapes=[pltpu.VMEM((8, t), F32)],
        compiler_params=pltpu.CompilerParams(vmem_limit_bytes=VMEM_LIMIT),
        name="routing",
    )(logits_t)


def _rows_copy(src, dst, sem, src_tok, dst_tok, n):
    first = lambda tok: tok * SUB if isinstance(tok, int) else pl.multiple_of(tok * SUB, SUB)
    s0 = first(src_tok)
    d0 = first(dst_tok)
    return pltpu.make_async_copy(src.at[pl.ds(s0, n * SUB), :], dst.at[pl.ds(d0, n * SUB), :], sem)


def _dispatch_kernel(sl1_ref, sl2_ref, pad0_ref, padn_ref, h2_hbm, xs_hbm, zero_scr, bufs, sems,
                     bsems, zsem):
    step = pl.program_id(0)

    def pad_copies(visit):
        for e in range(N_EXPERTS):
            pos = pad0_ref[e]
            p = TM_E // 2
            while p:
                has = (padn_ref[e] & p) != 0

                @pl.when(has)
                def _(pos=pos, p=p):
                    visit(_rows_copy(zero_scr, xs_hbm, zsem, 0, pos, p))
                pos = pos + jnp.where(has, p, 0)
                p //= 2

    @pl.when(step == 0)
    def _():
        zero_scr[...] = jnp.zeros_like(zero_scr)
        pad_copies(lambda c: c.start())

    @pl.when(step == pl.num_programs(0) - 1)
    def _():
        pad_copies(lambda c: c.wait())

    base = step * TB_D

    def block_copy(b):
        rows = pl.ds(pl.multiple_of(b * (TB_D * SUB), TB_D * SUB), TB_D * SUB)
        return pltpu.make_async_copy(h2_hbm.at[rows, :], bufs.at[b % 3], bsems.at[b % 3])

    @pl.when(step == 0)
    def _():
        block_copy(0).start()

    @pl.when(step + 1 < pl.num_programs(0))
    def _():
        block_copy(step + 1).start()

    block_copy(step).wait()
    h2_ref = bufs.at[step % 3]

    def wait_step(par):
        _rows_copy(h2_ref, xs_hbm, sems.at[par], 0, 0, TB_D).wait()
        _rows_copy(h2_ref, xs_hbm, sems.at[par], 0, 0, TB_D).wait()

    for par in (0, 1):
        @pl.when(step % 2 == par)
        def _(par=par):
            def issue(r, carry):
                _rows_copy(h2_ref, xs_hbm, sems.at[par], r, sl1_ref[base + r], 1).start(priority=0)
                _rows_copy(h2_ref, xs_hbm, sems.at[par], r, sl2_ref[base + r], 1).start(priority=1)
                return carry

            lax.fori_loop(0, TB_D, issue, 0, unroll=8)

            @pl.when(step > 0)
            def _():
                wait_step(1 - par)

            @pl.when(step == pl.num_programs(0) - 1)
            def _():
                wait_step(par)


def _dispatch(h2, sl1, sl2, pad_start, pad_len, n_rows):
    t = h2.shape[0] // SUB
    return pl.pallas_call(
        _dispatch_kernel,
        grid_spec=pltpu.PrefetchScalarGridSpec(
            num_scalar_prefetch=4,
            grid=(t // TB_D,),
            in_specs=[pl.BlockSpec(memory_space=pl.ANY)],
            out_specs=pl.BlockSpec(memory_space=pl.ANY),
            scratch_shapes=[pltpu.VMEM((TM_E // 2 * SUB, LANES), BF16),
                            pltpu.VMEM((3, TB_D * SUB, LANES), BF16),
                            pltpu.SemaphoreType.DMA((2,)), pltpu.SemaphoreType.DMA((3,)),
                            pltpu.SemaphoreType.DMA(())],
        ),
        out_shape=jax.ShapeDtypeStruct((n_rows * SUB, LANES), BF16),
        compiler_params=_params(1),
        name="dispatch",
    )(sl1, sl2, pad_start, pad_len, h2)


def _expert_kernel(tidx_ref, texp_ref, tfirst_ref, tslot_ref, tnext_ref, nt_ref,
                   xs_ref, wg_hbm, wu_hbm, wd_hbm, ys_ref,
                   wg_buf, wu_buf, wd_buf, wg_bf, wu_bf, wd_bf, tt_scr, sems):
    j = pl.program_id(0)

    def weight_copies(e, slot):
        return [pltpu.make_async_copy(hbm.at[e], buf.at[slot], sems.at[slot, k])
                for k, (hbm, buf) in enumerate(((wg_hbm, wg_buf), (wu_hbm, wu_buf),
                                                (wd_hbm, wd_buf)))]

    @pl.when(j < nt_ref[0])
    def _():
        slot = tslot_ref[j]

        @pl.when(tfirst_ref[j] == 1)
        def _():
            @pl.when(j == 0)
            def _():
                for c in weight_copies(texp_ref[j], slot):
                    c.start()
            for c in weight_copies(texp_ref[j], slot):
                c.wait()

            @pl.when(tnext_ref[j] >= 0)
            def _():
                for c in weight_copies(tnext_ref[j], 1 - slot):
                    c.start(priority=1)
            wg_bf[...] = wg_buf[slot].astype(BF16)
            wu_bf[...] = wu_buf[slot].astype(BF16)
            wd_bf[...] = wd_buf[slot].astype(BF16)

        xb = _load_token_tiles(xs_ref, tt_scr, TM_E).astype(BF16)
        g = jnp.dot(xb, wg_bf[...], preferred_element_type=F32)
        u = jnp.dot(xb, wu_bf[...], preferred_element_type=F32)
        h = (_silu(g) * u).astype(BF16)
        _store_token_tiles(ys_ref, jnp.dot(h, wd_bf[...], preferred_element_type=F32), tt_scr)


def _expert_mlp(xs, tile_meta, wg, wu, wd):
    n_rows = xs.shape[0] // SUB
    tile = pl.BlockSpec((TM_E * SUB, LANES), lambda j, ti, *_: (ti[j], 0))
    hbm = pl.BlockSpec(memory_space=pl.ANY)
    return pl.pallas_call(
        _expert_kernel,
        grid_spec=pltpu.PrefetchScalarGridSpec(
            num_scalar_prefetch=len(tile_meta),
            grid=(n_rows // TM_E,),
            in_specs=[tile, hbm, hbm, hbm],
            out_specs=tile,
            scratch_shapes=[pltpu.VMEM((2, D_MODEL, D_FF), F32),
                            pltpu.VMEM((2, D_MODEL, D_FF), F32),
                            pltpu.VMEM((2, D_FF, D_MODEL), F32),
                            pltpu.VMEM((D_MODEL, D_FF), BF16),
                            pltpu.VMEM((D_MODEL, D_FF), BF16),
                            pltpu.VMEM((D_FF, D_MODEL), BF16),
                            pltpu.VMEM((TM_E * TT_PITCH, LANES), F32),
                            pltpu.SemaphoreType.DMA((2, 3))],
        ),
        out_shape=jax.ShapeDtypeStruct((n_rows * SUB, LANES), BF16),
        compiler_params=_params(1),
        name="expert_mlp",
    )(*tile_meta, xs, wg, wu, wd)


def _combine_kernel(sl1_ref, sl2_ref, x1_ref, cw_ref, ys_hbm, o_ref, y1_scr, y2_scr, tt1_scr,
                    tt2_scr, sems):
    i = pl.program_id(0)

    def issue(step, slot):
        base = step * TB

        def body(r, carry):
            _rows_copy(ys_hbm, y1_scr.at[slot], sems.at[slot], sl1_ref[base + r], r,
                       1).start(priority=0)
            _rows_copy(ys_hbm, y2_scr.at[slot], sems.at[slot], sl2_ref[base + r], r,
                       1).start(priority=1)
            return carry

        lax.fori_loop(0, TB, body, 0, unroll=8)

    @pl.when(i == 0)
    def _():
        issue(0, 0)

    last = pl.num_programs(0) - 1
    rb = RB_COMBINE
    n_pieces = (TB // rb) * SUB
    per_piece = TB // n_pieces

    def step(slot):
        _rows_copy(ys_hbm, y1_scr.at[slot], sems.at[slot], 0, 0, TB).wait()
        _rows_copy(ys_hbm, y2_scr.at[slot], sems.at[slot], 0, 0, TB).wait()
        nbase = jnp.minimum(i + 1, last) * TB
        cw = jnp.concatenate([cw_ref[...], jnp.zeros((LANES - 8, TB), F32)], axis=0).T
        _tt_fill(tt1_scr, y1_scr.at[slot], TB)
        _tt_fill(tt2_scr, y2_scr.at[slot], TB)
        piece = 0
        for r0 in range(0, TB, rb):
            rows = slice(r0, r0 + rb)
            c1 = jnp.broadcast_to(cw[rows, 0:1], (rb, LANES))
            c2 = jnp.broadcast_to(cw[rows, 1:2], (rb, LANES))
            for s in range(SUB):
                for r in range(piece * per_piece, (piece + 1) * per_piece):
                    _rows_copy(ys_hbm, y1_scr.at[1 - slot], sems.at[1 - slot],
                               sl1_ref[nbase + r], r, 1).start(priority=0)
                    _rows_copy(ys_hbm, y2_scr.at[1 - slot], sems.at[1 - slot],
                               sl2_ref[nbase + r], r, 1).start(priority=1)
                piece += 1
                cols = slice(s * LANES, (s + 1) * LANES)
                tiles = _tt_piece(s, rb, r0)
                o_ref[rows, cols] = (x1_ref[rows, cols] + c1 * tt1_scr[tiles]
                                     + c2 * tt2_scr[tiles])

        @pl.when(i == last)
        def _():
            _rows_copy(ys_hbm, y1_scr.at[1 - slot], sems.at[1 - slot], 0, 0, TB).wait()
            _rows_copy(ys_hbm, y2_scr.at[1 - slot], sems.at[1 - slot], 0, 0, TB).wait()

    for parity in (0, 1):
        @pl.when(i % 2 == parity)
        def _():
            step(parity)


def _combine(x1, cw, ys, sl1, sl2):
    t = x1.shape[0]
    return pl.pallas_call(
        _combine_kernel,
        grid_spec=pltpu.PrefetchScalarGridSpec(
            num_scalar_prefetch=2,
            grid=(t // TB,),
            in_specs=[pl.BlockSpec((TB, D_MODEL), lambda i, *_: (i, 0)),
                      pl.BlockSpec((8, TB), lambda i, *_: (0, i)),
                      pl.BlockSpec(memory_space=pl.ANY)],
            out_specs=pl.BlockSpec((TB, D_MODEL), lambda i, *_: (i, 0)),
            scratch_shapes=[pltpu.VMEM((2, TB * SUB, LANES), BF16),
                            pltpu.VMEM((2, TB * SUB, LANES), BF16),
                            pltpu.VMEM((TB * TT_PITCH, LANES), F32),
                            pltpu.VMEM((TB * TT_PITCH, LANES), F32),
                            pltpu.SemaphoreType.DMA((2,))],
        ),
        out_shape=jax.ShapeDtypeStruct((t, D_MODEL), F32),
        compiler_params=_params(1),
        name="combine",
    )(sl1, sl2, x1, cw, ys)


def _lane_tile(v, reps):
    return jnp.tile(v.reshape(1, -1), (1, reps))


def kernel(x, positions, norm1_g, w_in, q_norm_g, k_norm_g, lambda_q1, lambda_k1, lambda_q2,
           lambda_k2, diff_out_norm_g, gla_w_gate2, gla_b_gate, gla_out_norm_g, w_out, norm2_g,
           w_router_group, b_router_group, w_router_expert, b_router_expert, w_gate_expert,
           w_up_expert, w_down_expert):
    batch, seq, d = x.shape
    t = batch * seq
    assert d == D_MODEL and w_in.shape == (1, D_MODEL, D_MAIN + GLA_GATE_RANK)
    assert seq % TQ == 0 and seq % GB == 0 and t % TM_IN == 0 and t % TM_OUT == 0 and t % TB_D == 0
    x2 = x.reshape(t, d)

    inv = ROPE_THETA ** (-jnp.arange(0, ROT_DIM, 2, dtype=F32) / ROT_DIM)
    lane_d = jnp.arange(LANES) % DIFF_QK_DIM
    invf = jnp.where(lane_d < ROT_DIM, inv[lane_d % (ROT_DIM // 2)], 0.0).reshape(1, LANES)
    pos_b = jnp.broadcast_to(positions.astype(F32).reshape(t, 1), (t, LANES))
    tabs = _rope_tables(pos_b, invf)

    w_in_t = jnp.swapaxes(w_in, 1, 2)[0]
    wa_t = jnp.pad(w_in_t[D_MAIN:, :], ((0, LANES - GLA_GATE_RANK), (0, 0)))
    proj, ga = _in_proj(x2, norm1_g, w_in_t, wa_t)

    a_out = _diff_attention(
        proj, tabs, _lane_tile(q_norm_g[0], 2), _lane_tile(k_norm_g[0], 2),
        lambda_q1, lambda_k1, lambda_q2, lambda_k2, diff_out_norm_g, batch, seq)
    w2p = jnp.pad(gla_w_gate2[0], ((0, LANES - GLA_GATE_RANK), (0, 0))).astype(BF16)
    g_out = _gla(proj, ga, w2p, gla_b_gate, gla_out_norm_g, batch, seq)

    gpad = GROUP_ROWS - N_GROUPS
    lpad = LANES - N_LOGIT_ROWS
    wr = jnp.concatenate(
        [jnp.pad(w_router_group[0], ((0, 0), (0, gpad))),
         jnp.pad(w_router_expert[0].transpose(1, 0, 2).reshape(d, N_EXPERTS),
                 ((0, 0), (0, lpad)))], axis=1)
    br = jnp.concatenate([jnp.pad(b_router_group[0], (0, gpad)),
                          jnp.pad(b_router_expert[0].reshape(-1), (0, lpad))]).reshape(1, LANES)
    x1, h2, logits_t = _out_proj(a_out, g_out, x2, w_out[0].astype(BF16), norm2_g,
                                 wr.astype(BF16), br)

    slots, cw, counts = _routing(logits_t)
    sl1 = slots[0]
    sl2 = slots[1]
    n_rows = 2 * t + N_EXPERTS * TM_E
    max_tiles = n_rows // TM_E
    cnt = counts[:, 0].astype(jnp.int32)
    tile_end = jnp.cumsum((cnt + (TM_E - 1)) // TM_E)
    n_tiles = tile_end[-1:]
    tile_idx = jnp.minimum(jnp.arange(max_tiles, dtype=jnp.int32), n_tiles[0] - 1)
    tile_exp = jnp.sum(tile_idx[:, None] >= tile_end[None, :], axis=1).astype(jnp.int32)
    row_end = (tile_end * TM_E).astype(jnp.int32)
    n_tile_e = (cnt + (TM_E - 1)) // TM_E
    pad_start = (row_end - n_tile_e * TM_E + cnt).astype(jnp.int32)
    pad_len = (row_end - pad_start).astype(jnp.int32)
    tile_first = jnp.concatenate(
        [jnp.ones((1,), jnp.int32), (tile_exp[1:] != tile_exp[:-1]).astype(jnp.int32)])
    tile_slot = (jnp.cumsum(tile_first) - 1) % 2
    nxt = tile_end[tile_exp]
    tile_next = jnp.where(nxt < n_tiles[0], tile_exp[jnp.minimum(nxt, max_tiles - 1)], -1)
    tile_meta = [a.astype(jnp.int32)
                 for a in (tile_idx, tile_exp, tile_first, tile_slot, tile_next, n_tiles)]

    xs = _dispatch(h2, sl1, sl2, pad_start, pad_len, n_rows)
    ys = _expert_mlp(xs, tile_meta,
                     w_gate_expert[0].reshape(N_EXPERTS, d, D_FF),
                     w_up_expert[0].reshape(N_EXPERTS, d, D_FF),
                     w_down_expert[0].reshape(N_EXPERTS, D_FF, d))
    out = _combine(x1, cw, ys, sl1, sl2)
    return out.reshape(batch, seq, d)
```
